```python
import math
import jax, jax.numpy as jnp
from jax import lax
import numpy as np

D_MODEL = 1024
BATCH = 8
SEQ = 2048
DEPTH = 2
DEC_BATCH = 32
DEC_SEQ = 32
PAST_LEN = 2048

CHUNK = 64
N_EVEN = (DEPTH + 1) // 2
N_ODD = DEPTH // 2
NORM_EPS = 1e-6
CONV_W = 4
DA = 512
NB_A = 8
DBLK_A = DA // NB_A
LRU_C = 8.0
HB = 8
DHB = 64
DB = HB * DHB
N_PREV_CHUNKS = 8
BAND_PAST = N_PREV_CHUNKS * CHUNK
MAX_REL = 128
HC = 8
DK = 128
DV = 128
DCK = HC * DK
DCV = HC * DV
N_QKV = 2 * DCK + DCV
DELTA_CHUNK = 64
N_EXPERTS = 64
TOP_K = 8
N_GROUPS = 8
TOPK_GROUPS = 4
D_EXPERT = 256
D_SHARED = 256
ROUTE_SCALE = 2.5
IN0 = 2 * DA + 3 * DB
IN1 = N_QKV + DCV + 2 * HC

kernel_name = 'hybrid_streaming_encoder_step'


def rmsnorm(x, g):
    xf = x.astype(jnp.float32)
    y = xf * lax.rsqrt(jnp.mean(xf * xf, axis=-1, keepdims=True) + NORM_EPS)
    return (y * g.astype(jnp.float32)).astype(x.dtype)


def l2norm(x):
    xf = x.astype(jnp.float32)
    return xf * lax.rsqrt(jnp.sum(xf * xf, axis=-1, keepdims=True) + NORM_EPS)


def causal_dwconv(x, w, buf):
    T = x.shape[1]
    xp = jnp.concatenate([buf.astype(x.dtype), x], axis=1)
    y = sum(xp[:, j:j + T] * w[j] for j in range(CONV_W))
    return y, xp[:, -(CONV_W - 1):]


def linear_scan(a, b, h0):
    b = b.at[:, 0].add(a[:, 0] * h0)
    def combine(l, r):
        return l[0] * r[0], r[0] * l[1] + r[1]
    _, h = lax.associative_scan(combine, (a, b), axis=1)
    return h


def rglru_branch(xa, ga, conv_w, conv_b, w_r, b_r, w_i, b_i, lam, conv_buf, h0):
    B, T, _ = xa.shape
    xc, new_buf = causal_dwconv(xa, conv_w, conv_buf)
    xc = xc + conv_b
    xb = xc.reshape(B, T, NB_A, DBLK_A)
    gate_r = jnp.einsum('btnd,nde->btne', xb, w_r).reshape(B, T, DA) + b_r
    gate_i = jnp.einsum('btnd,nde->btne', xb, w_i).reshape(B, T, DA) + b_i
    r = jax.nn.sigmoid(gate_r.astype(jnp.float32))
    i = jax.nn.sigmoid(gate_i.astype(jnp.float32))
    log_a = -LRU_C * r * jax.nn.softplus(-lam.astype(jnp.float32))
    a = jnp.exp(log_a)
    u = jnp.sqrt(-jnp.expm1(2.0 * log_a)) * (i * xc.astype(jnp.float32))
    h = linear_scan(a, u, h0.astype(jnp.float32))
    y = h.astype(xa.dtype) * jax.nn.gelu(ga)
    return y, new_buf, h[:, -1].astype(xa.dtype)


def rel_bias(table, q_pos, k_pos):
    rel = jnp.clip(q_pos[:, None] - k_pos[None, :], -MAX_REL, MAX_REL) + MAX_REL
    return jnp.take(table.astype(jnp.float32), rel, axis=1)


def band_attention_prompt(q, k, v, table):
    B, T = q.shape[:2]
    nC = T // CHUNK
    band = (N_PREV_CHUNKS + 1) * CHUNK
    pad = jnp.zeros((B, BAND_PAST, HB, DHB), k.dtype)
    kc = jnp.concatenate([pad, k], axis=1).reshape(B, nC + N_PREV_CHUNKS, CHUNK, HB, DHB)
    vc = jnp.concatenate([pad, v], axis=1).reshape(B, nC + N_PREV_CHUNKS, CHUNK, HB, DHB)
    idx = jnp.arange(nC)[:, None] + jnp.arange(N_PREV_CHUNKS + 1)[None, :]
    kb = kc[:, idx].reshape(B, nC, band, HB, DHB)
    vb = vc[:, idx].reshape(B, nC, band, HB, DHB)
    qc = q.reshape(B, nC, CHUNK, HB, DHB)
    s = jnp.einsum('bcqhd,bckhd->bhcqk', qc, kb).astype(jnp.float32) * DHB ** -0.5
    k_off = jnp.arange(band)
    bias = rel_bias(table, BAND_PAST + jnp.arange(CHUNK), k_off)
    valid = (jnp.arange(nC)[:, None] - N_PREV_CHUNKS) * CHUNK + k_off[None, :] >= 0
    s = jnp.where(valid[None, None, :, None, :], s + bias[None, :, None], -jnp.inf)
    p = jax.nn.softmax(s, axis=-1).astype(v.dtype)
    return jnp.einsum('bhcqk,bckhd->bcqhd', p, vb).reshape(B, T, HB, DHB)


def band_attention_sample(q, k, v, k_cache, v_cache, table):
    P_, T = k_cache.shape[1], q.shape[1]
    kk = jnp.concatenate([k_cache.astype(k.dtype), k], axis=1)
    vv = jnp.concatenate([v_cache.astype(v.dtype), v], axis=1)
    s = jnp.einsum('bqhd,bkhd->bhqk', q, kk).astype(jnp.float32) * DHB ** -0.5
    s = s + rel_bias(table, P_ + jnp.arange(T), jnp.arange(P_ + T))[None]
    p = jax.nn.softmax(s, axis=-1).astype(v.dtype)
    return jnp.einsum('bhqk,bkhd->bqhd', p, vv)


def chunk_gated_delta(q, k, v, g, beta, S0):
    B, T, H, _ = q.shape
    C = DELTA_CHUNK
    n = -(-T // C)
    pad = n * C - T
    def blocks(x):
        x = jnp.pad(x.astype(jnp.float32), [(0, 0), (0, pad)] + [(0, 0)] * (x.ndim - 2))
        x = x.reshape((B, n, C) + x.shape[2:])
        return jnp.moveaxis(jnp.moveaxis(x, 1, 0), 2, 3)
    q, k, v, g, beta = blocks(q), blocks(k), blocks(v), blocks(g), blocks(beta)
    G = jnp.cumsum(g, axis=-1)
    causal = jnp.tril(jnp.ones((C, C), bool))
    strict = jnp.tril(jnp.ones((C, C), bool), -1)
    diff = G[..., :, None] - G[..., None, :]
    L = jnp.where(causal, jnp.exp(jnp.where(causal, diff, 0.0)), 0.0)
    kb = k * beta[..., None]
    M = jnp.where(strict, jnp.einsum('nbhid,nbhjd->nbhij', kb, k) * L, 0.0) + jnp.eye(C, dtype=jnp.float32)
    rhs = jnp.concatenate([v * beta[..., None], kb * jnp.exp(G)[..., None]], axis=-1)
    X = lax.linalg.triangular_solve(M, rhs, left_side=True, lower=True, unit_diagonal=True)
    u, w = X[..., :DV], X[..., DV:]
    A_qk = jnp.einsum('nbhid,nbhjd->nbhij', q, k) * L
    q_dec = q * jnp.exp(G)[..., None]
    k_dec = k * jnp.exp(G[..., -1:] - G)[..., None]
    s_dec = jnp.exp(G[..., -1])
    def step(S, xs):
        u_c, w_c, q_c, a_c, k_c, d_c = xs
        v_new = u_c - jnp.einsum('bhck,bhkv->bhcv', w_c, S)
        o_c = jnp.einsum('bhck,bhkv->bhcv', q_c, S) + jnp.einsum('bhij,bhjv->bhiv', a_c, v_new)
        S = S * d_c[..., None, None] + jnp.einsum('bhck,bhcv->bhkv', k_c, v_new)
        return S, o_c
    S, o = lax.scan(step, S0.astype(jnp.float32), (u, w, q_dec, A_qk, k_dec, s_dec))
    o = jnp.moveaxis(jnp.moveaxis(o, 3, 2), 0, 1).reshape(B, n * C, H, DV)[:, :T]
    return o, S


def even_mixer(h, P, j, a_buf, a_h0, k_cache, v_cache):
    B, T, _ = h.shape
    proj = h @ P['w_in0'][j]
    xa, ga, q, k, v = jnp.split(proj, [DA, 2 * DA, 2 * DA + DB, 2 * DA + 2 * DB], axis=-1)
    ya, a_buf_new, a_h_new = rglru_branch(xa, ga, P['a_conv_w'][j], P['a_conv_b'][j], P['a_w_r'][j], P['a_b_r'][j],
                                          P['a_w_i'][j], P['a_b_i'][j], P['a_lambda'][j], a_buf, a_h0)
    q = q.reshape(B, T, HB, DHB)
    k = k.reshape(B, T, HB, DHB)
    v = v.reshape(B, T, HB, DHB)
    table = P['b_rel_bias'][j]
    if k_cache is None:
        yb = band_attention_prompt(q, k, v, table)
        keep = min(BAND_PAST, T)
        k_new, v_new = k[:, T - keep:], v[:, T - keep:]
    else:
        yb = band_attention_sample(q, k, v, k_cache, v_cache, table)
        k_new, v_new = k, v
    y = jnp.concatenate([ya, yb.reshape(B, T, DB)], axis=-1) @ P['w_out0'][j]
    return y, a_buf_new, a_h_new, k_new, v_new


def odd_mixer(h, P, j, c_buf, S0):
    B, T, _ = h.shape
    f32 = jnp.float32
    proj = h @ P['w_in1'][j]
    qkv, z, a_in, b_in = jnp.split(proj, [N_QKV, N_QKV + DCV, N_QKV + DCV + HC], axis=-1)
    qkv, new_buf = causal_dwconv(qkv, P['c_conv_w'][j], c_buf)
    qkv = jax.nn.silu(qkv)
    q, k, v = jnp.split(qkv, [DCK, 2 * DCK], axis=-1)
    q = l2norm(q.reshape(B, T, HC, DK)) * DK ** -0.5
    k = l2norm(k.reshape(B, T, HC, DK))
    v = v.reshape(B, T, HC, DV)
    g = -jnp.exp(P['c_A_log'][j].astype(f32)) * jax.nn.softplus(a_in.astype(f32) + P['c_dt_bias'][j].astype(f32))
    beta = jax.nn.sigmoid(b_in.astype(f32))
    o, S = chunk_gated_delta(q, k, v, g, beta, S0)
    o = rmsnorm(o, P['c_norm_g'][j]) * jax.nn.silu(z.reshape(B, T, HC, DV).astype(f32))
    y = o.reshape(B, T, DCV).astype(h.dtype) @ P['w_out1'][j]
    return y, new_buf, S.astype(h.dtype)


def moe(h, rw, rb, wg, wu, wd, sg, su, sd):
    B, T, D = h.shape
    xt = h.reshape(B * T, D)
    scores = jax.nn.sigmoid((xt @ rw).astype(jnp.float32))
    sel = scores + rb.astype(jnp.float32)
    grp = sel.reshape(-1, N_GROUPS, N_EXPERTS // N_GROUPS)
    grp_score = jnp.sum(lax.top_k(grp, 2)[0], axis=-1)
    _, gidx = lax.top_k(grp_score, TOPK_GROUPS)
    gmask = jnp.any(gidx[..., None] == jnp.arange(N_GROUPS), axis=-2)
    emask = jnp.repeat(gmask, N_EXPERTS // N_GROUPS, axis=-1)
    _, eidx = lax.top_k(jnp.where(emask, sel, -jnp.inf), TOP_K)
    wsel = jnp.take_along_axis(scores, eidx, axis=-1)
    wsel = wsel / jnp.sum(wsel, axis=-1, keepdims=True) * ROUTE_SCALE
    gates = jnp.sum(jax.nn.one_hot(eidx, N_EXPERTS, dtype=jnp.float32) * wsel[..., None], axis=1)
    def expert_step(acc, xs):
        g_w, u_w, d_w, gate = xs
        hid = jax.nn.silu(xt @ g_w) * (xt @ u_w)
        return acc + gate[:, None].astype(xt.dtype) * (hid @ d_w), None
    acc, _ = lax.scan(expert_step, jnp.zeros_like(xt), (wg, wu, wd, gates.T))
    shared = (jax.nn.silu(xt @ sg) * (xt @ su)) @ sd
    return (acc + shared).reshape(B, T, D)


def run_group(x, c, P, a_conv, a_h, b_k, b_v, c_conv, c_S):
    cond = jax.nn.silu(c)
    nb_k, nb_v, na_c, na_h, nc_c, nc_s = [], [], [], [], [], []
    for l in range(DEPTH):
        mod = cond @ P['w_mod'][l] + P['b_mod'][l]
        sh1, sc1, gt1, sh2, sc2, gt2 = jnp.split(mod[:, None, :], 6, axis=-1)
        h = rmsnorm(x, P['norm1_g'][l]) * (1 + sc1) + sh1
        j = l // 2
        if l % 2 == 0:
            y, ac, ah, bk, bv = even_mixer(h, P, j, a_conv[j], a_h[j],
                                           None if b_k is None else b_k[j], None if b_v is None else b_v[j])
            na_c.append(ac)
            na_h.append(ah)
            nb_k.append(bk)
            nb_v.append(bv)
        else:
            y, cc, cs = odd_mixer(h, P, j, c_conv[j], c_S[j])
            nc_c.append(cc)
            nc_s.append(cs)
        x = x + gt1 * y
        h = rmsnorm(x, P['norm2_g'][l]) * (1 + sc2) + sh2
        x = x + gt2 * moe(h, P['router_w'][l], P['router_bias'][l], P['e_w_gate'][l], P['e_w_up'][l],
                          P['e_w_down'][l], P['s_w_gate'][l], P['s_w_up'][l], P['s_w_down'][l])
    y = rmsnorm(x, P['final_g'])
    return y, (jnp.stack(nb_k), jnp.stack(nb_v), jnp.stack(na_c), jnp.stack(na_h), jnp.stack(nc_c), jnp.stack(nc_s))


def setup_inputs(seed: int = 0) -> dict:
    key = jax.random.key(seed)
    keys = iter(jax.random.split(key, 48))
    f32 = jnp.float32
    def nrm(shape, scale):
        return scale * jax.random.normal(next(keys), shape, f32)
    def gain(shape):
        return 1.0 + nrm(shape, 0.05)
    d = D_MODEL
    past_b = min(BAND_PAST, PAST_LEN)
    u = jax.random.uniform(next(keys), (N_EVEN, DA), f32, 0.9, 0.999)
    a0 = u ** (1.0 / LRU_C)
    a_lambda = jnp.log(a0) - jnp.log1p(-a0)
    dt = jnp.exp(jax.random.uniform(next(keys), (N_ODD, HC), f32, math.log(1e-3), math.log(1e-1)))
    c_dt_bias = dt + jnp.log(-jnp.expm1(-dt))
    c_A_log = jnp.log(jax.random.uniform(next(keys), (N_ODD, HC), f32, 1.0, 16.0))
    return {
        'x_prompt': nrm((BATCH, SEQ, d), 1.0),
        'x_sample': nrm((DEC_BATCH, DEC_SEQ, d), 1.0),
        'c_prompt': nrm((BATCH, d), 1.0),
        'c_sample': nrm((DEC_BATCH, d), 1.0),
        'cache_b_k': nrm((N_EVEN, DEC_BATCH, past_b, HB, DHB), 1.0),
        'cache_b_v': nrm((N_EVEN, DEC_BATCH, past_b, HB, DHB), 1.0),
        'state_a_conv': nrm((N_EVEN, DEC_BATCH, CONV_W - 1, DA), 1.0),
        'state_a_h': nrm((N_EVEN, DEC_BATCH, DA), 0.5),
        'state_c_conv': nrm((N_ODD, DEC_BATCH, CONV_W - 1, N_QKV), 1.0),
        'state_c_S': nrm((N_ODD, DEC_BATCH, HC, DK, DV), 0.1),
        'norm1_g': gain((DEPTH, d)),
        'norm2_g': gain((DEPTH, d)),
        'final_g': gain((d,)),
        'w_mod': nrm((DEPTH, d, 6 * d), 0.5 * d ** -0.5),
        'b_mod': nrm((DEPTH, 6 * d), 0.01),
        'w_in0': nrm((N_EVEN, d, IN0), d ** -0.5),
        'w_out0': nrm((N_EVEN, DA + DB, d), (DA + DB) ** -0.5),
        'a_conv_w': nrm((N_EVEN, CONV_W, DA), 0.5),
        'a_conv_b': nrm((N_EVEN, DA), 0.01),
        'a_w_r': nrm((N_EVEN, NB_A, DBLK_A, DBLK_A), DBLK_A ** -0.5),
        'a_b_r': nrm((N_EVEN, DA), 0.01),
        'a_w_i': nrm((N_EVEN, NB_A, DBLK_A, DBLK_A), DBLK_A ** -0.5),
        'a_b_i': nrm((N_EVEN, DA), 0.01),
        'a_lambda': a_lambda,
        'b_rel_bias': nrm((N_EVEN, HB, 2 * MAX_REL + 1), 0.2),
        'w_in1': nrm((N_ODD, d, IN1), d ** -0.5),
        'w_out1': nrm((N_ODD, DCV, d), DCV ** -0.5),
        'c_conv_w': nrm((N_ODD, CONV_W, N_QKV), 0.5),
        'c_A_log': c_A_log,
        'c_dt_bias': c_dt_bias,
        'c_norm_g': gain((N_ODD, DV)),
        'router_w': nrm((DEPTH, d, N_EXPERTS), d ** -0.5),
        'router_bias': nrm((DEPTH, N_EXPERTS), 0.01),
        'e_w_gate': nrm((DEPTH, N_EXPERTS, d, D_EXPERT), d ** -0.5),
        'e_w_up': nrm((DEPTH, N_EXPERTS, d, D_EXPERT), d ** -0.5),
        'e_w_down': nrm((DEPTH, N_EXPERTS, D_EXPERT, d), D_EXPERT ** -0.5),
        's_w_gate': nrm((DEPTH, d, D_SHARED), d ** -0.5),
        's_w_up': nrm((DEPTH, d, D_SHARED), d ** -0.5),
        's_w_down': nrm((DEPTH, D_SHARED, d), D_SHARED ** -0.5),
    }


def reference(x_prompt, x_sample, c_prompt, c_sample, cache_b_k, cache_b_v, state_a_conv, state_a_h,
              state_c_conv, state_c_S, norm1_g, norm2_g, final_g, w_mod, b_mod, w_in0, w_out0, a_conv_w,
              a_conv_b, a_w_r, a_b_r, a_w_i, a_b_i, a_lambda, b_rel_bias, w_in1, w_out1, c_conv_w, c_A_log,
              c_dt_bias, c_norm_g, router_w, router_bias, e_w_gate, e_w_up, e_w_down, s_w_gate, s_w_up, s_w_down):
    P = dict(norm1_g=norm1_g, norm2_g=norm2_g, final_g=final_g, w_mod=w_mod, b_mod=b_mod, w_in0=w_in0,
             w_out0=w_out0, a_conv_w=a_conv_w, a_conv_b=a_conv_b, a_w_r=a_w_r, a_b_r=a_b_r, a_w_i=a_w_i,
             a_b_i=a_b_i, a_lambda=a_lambda, b_rel_bias=b_rel_bias, w_in1=w_in1, w_out1=w_out1,
             c_conv_w=c_conv_w, c_A_log=c_A_log, c_dt_bias=c_dt_bias, c_norm_g=c_norm_g, router_w=router_w,
             router_bias=router_bias, e_w_gate=e_w_gate, e_w_up=e_w_up, e_w_down=e_w_down,
             s_w_gate=s_w_gate, s_w_up=s_w_up, s_w_down=s_w_down)
    dt = x_prompt.dtype
    bp = x_prompt.shape[0]
    y_prompt, (bk_p, bv_p, ac_p, ah_p, cc_p, cs_p) = run_group(
        x_prompt, c_prompt, P,
        jnp.zeros((N_EVEN, bp, CONV_W - 1, DA), dt), jnp.zeros((N_EVEN, bp, DA), dt), None, None,
        jnp.zeros((N_ODD, bp, CONV_W - 1, N_QKV), dt), jnp.zeros((N_ODD, bp, HC, DK, DV), dt))
    y_sample, (bk_s, bv_s, ac_s, ah_s, cc_s, cs_s) = run_group(
        x_sample, c_sample, P, state_a_conv, state_a_h, cache_b_k, cache_b_v, state_c_conv, state_c_S)
    return (y_prompt, y_sample, bk_p, bk_s, bv_p, bv_s, ac_p, ac_s, ah_p, ah_s, cc_p, cc_s, cs_p, cs_s)
```

```python
import functools

import jax
import jax.numpy as jnp
from jax import lax
from jax.experimental import pallas as pl
from jax.experimental.pallas import tpu as pltpu

F32 = jnp.float32
BF16 = jnp.bfloat16
I32 = jnp.int32

D_MODEL = 1024
CHUNK = 64
NORM_EPS = 1e-6
CONV_W = 4
DA = 512
NB_A = 8
LRU_C = 8.0
HB = 8
DHB = 64
DB = HB * DHB
BAND_PAST = 8 * CHUNK
MAX_REL = 128
HC = 8
DK = 128
DV = 128
DCK = HC * DK
DCV = HC * DV
N_QKV = 2 * DCK + DCV
DELTA_CHUNK = 64
N_EXPERTS = 64
TOP_K = 8
N_GROUPS = 8
TOPK_GROUPS = 4
D_EXPERT = 256
D_SHARED = 256
ROUTE_SCALE = 2.5
IN0 = 2 * DA + 3 * DB
IN1 = N_QKV + DCV + 2 * HC

LANES = 128
SUBLANES = 8
PB = 32
IN1_PAD = N_QKV + DCV + LANES
VMEM_LIMIT = 48 * 1024 * 1024
HI = lax.Precision.HIGHEST


def _cparams(sem):
    return pltpu.CompilerParams(dimension_semantics=sem, vmem_limit_bytes=VMEM_LIMIT)


def _softplus(x):
    return jnp.maximum(x, 0.0) + jnp.log1p(jnp.exp(-jnp.abs(x)))


def _dot(a, b, precision=None):
    return jnp.dot(a, b, preferred_element_type=F32, precision=precision)


def _dot_nt(a, b, precision=None):
    return lax.dot_general(a, b, (((1,), (1,)), ((), ())), preferred_element_type=F32, precision=precision)


def _dot_tn(a, b, precision=None):
    return lax.dot_general(a, b, (((0,), (0,)), ((), ())), preferred_element_type=F32, precision=precision)


def _mod_kernel(c_ref, w_ref, b_ref, o_ref):
    c = c_ref[...]
    cond = c * jax.nn.sigmoid(c)
    o_ref[0] = _dot(cond.astype(BF16), w_ref[0].astype(BF16)) + b_ref[0]


def _modulation(c_all, w_mod, b_mod):
    depth, d, n6 = w_mod.shape
    nb = c_all.shape[0]
    tn = 1536
    return pl.pallas_call(
        _mod_kernel,
        grid=(depth, n6 // tn),
        in_specs=[pl.BlockSpec((nb, d), lambda l, j: (0, 0)),
                  pl.BlockSpec((1, d, tn), lambda l, j: (l, 0, j)),
                  pl.BlockSpec((1, 1, tn), lambda l, j: (l, 0, j))],
        out_specs=pl.BlockSpec((1, nb, tn), lambda l, j: (l, 0, j)),
        out_shape=jax.ShapeDtypeStruct((depth, nb, n6), F32),
        compiler_params=_cparams(("parallel", "parallel")),
        name="modulation",
    )(c_all, w_mod, b_mod.reshape(depth, 1, n6))


def _norm_mod(x3, g, sc, sh):
    ms = jnp.mean(x3 * x3, axis=-1, keepdims=True)
    y = x3 * lax.rsqrt(ms + NORM_EPS) * g
    h = y * (1.0 + sc) + sh
    return h.reshape(x3.shape[0] * x3.shape[1], x3.shape[2])


def _inproj_kernel(x_ref, g_ref, sc_ref, sh_ref, w_ref, o_ref, h_scr):
    @pl.when(pl.program_id(1) == 0)
    def _():
        h_scr[...] = _norm_mod(x_ref[...], g_ref[...], sc_ref[...], sh_ref[...]).astype(BF16)
    o_ref[...] = _dot(h_scr[...], w_ref[...])


def _in_projection(x_all, g, mod_pb, sc_col, sh_col, w_bf16, tn):
    npb, pb, d = x_all.shape
    nout = w_bf16.shape[1]
    bB = 32
    tm = bB * pb
    return pl.pallas_call(
        _inproj_kernel,
        grid=(npb // bB, nout // tn),
        in_specs=[pl.BlockSpec((bB, pb, d), lambda i, j: (i, 0, 0)),
                  pl.BlockSpec((1, d), lambda i, j: (0, 0)),
                  pl.BlockSpec((bB, 1, d), lambda i, j: (i, 0, sc_col)),
                  pl.BlockSpec((bB, 1, d), lambda i, j: (i, 0, sh_col)),
                  pl.BlockSpec((d, tn), lambda i, j: (0, j))],
        out_specs=pl.BlockSpec((tm, tn), lambda i, j: (i, j)),
        out_shape=jax.ShapeDtypeStruct((npb * pb, nout), F32),
        scratch_shapes=[pltpu.VMEM((tm, d), BF16)],
        compiler_params=_cparams(("parallel", "arbitrary")),
        name="in_projection",
    )(x_all, g.reshape(1, d), mod_pb, mod_pb, w_bf16)


def _rglru_kernel(xa_ref, ga_ref, cb_ref, h0_ref, cw_ref, cbias_ref, wr_ref, br_ref, wi_ref, bi_ref, lam_ref,
                  y_ref, cbo_ref, ho_ref, xbuf, hcar):
    tT = xa_ref.shape[0]

    @pl.when(pl.program_id(1) == 0)
    def _():
        xbuf[0:SUBLANES, :] = jnp.zeros((SUBLANES, DA), F32)
        xbuf[SUBLANES - (CONV_W - 1):SUBLANES, :] = cb_ref[...]
        hcar[...] = h0_ref[...]

    xa = xa_ref[...]
    xbuf[SUBLANES:SUBLANES + tT, :] = xa
    xc = cw_ref[3:4, :] * xa + cbias_ref[...]
    for j in range(CONV_W - 1):
        xc = xc + cw_ref[j:j + 1, :] * xbuf[SUBLANES - 3 + j:SUBLANES - 3 + j + tT, :]
    tail = xbuf[tT:tT + SUBLANES, :]
    xbuf[0:SUBLANES, :] = tail
    cbo_ref[...] = tail[SUBLANES - (CONV_W - 1):, :]

    xcb = xc.astype(BF16)
    r = jax.nn.sigmoid(_dot(xcb, wr_ref[...]) + br_ref[...])
    i = jax.nn.sigmoid(_dot(xcb, wi_ref[...]) + bi_ref[...])
    log_a = -LRU_C * r * _softplus(-lam_ref[...])
    a = jnp.exp(log_a)
    th = jnp.tanh(log_a)
    u = jnp.sqrt(-2.0 * th / (1.0 - th)) * (i * xc)

    rows = lax.broadcasted_iota(I32, (tT, DA), 0)
    s = 1
    while s < tT:
        a_sh = pltpu.roll(a, s, 0)
        u_sh = pltpu.roll(u, s, 0)
        m = rows >= s
        u = jnp.where(m, a * u_sh + u, u)
        a = jnp.where(m, a * a_sh, a)
        s *= 2
    h = a * hcar[...] + u
    hlast = h[tT - 1:tT, :]
    hcar[...] = hlast
    ho_ref[...] = hlast
    y_ref[...] = (h * jax.nn.gelu(ga_ref[...])).astype(BF16)


def _rglru(proj, row0, B, T, tT, conv_buf, h0, cw, cbias, wr, br, wi, bi, lam):
    nT = T // tT
    rb0 = row0 // tT
    vec = lambda: pl.BlockSpec((1, DA), lambda b, t: (0, 0))
    return pl.pallas_call(
        _rglru_kernel,
        grid=(B, nT),
        in_specs=[pl.BlockSpec((tT, DA), lambda b, t: (rb0 + b * nT + t, 0)),
                  pl.BlockSpec((tT, DA), lambda b, t: (rb0 + b * nT + t, 1)),
                  pl.BlockSpec((None, CONV_W - 1, DA), lambda b, t: (b, 0, 0)),
                  pl.BlockSpec((None, 1, DA), lambda b, t: (b, 0, 0)),
                  pl.BlockSpec((CONV_W, DA), lambda b, t: (0, 0)),
                  vec(),
                  pl.BlockSpec((DA, DA), lambda b, t: (0, 0)),
                  vec(),
                  pl.BlockSpec((DA, DA), lambda b, t: (0, 0)),
                  vec(), vec()],
        out_specs=[pl.BlockSpec((tT, DA), lambda b, t: (b * nT + t, 0)),
                   pl.BlockSpec((None, CONV_W - 1, DA), lambda b, t: (b, 0, 0)),
                   pl.BlockSpec((None, 1, DA), lambda b, t: (b, 0, 0))],
        out_shape=[jax.ShapeDtypeStruct((B * T, DA), BF16),
                   jax.ShapeDtypeStruct((B, CONV_W - 1, DA), F32),
                   jax.ShapeDtypeStruct((B, 1, DA), F32)],
        scratch_shapes=[pltpu.VMEM((tT + SUBLANES, DA), F32), pltpu.VMEM((1, DA), F32)],
        compiler_params=_cparams(("parallel", "arbitrary")),
        name="rglru",
    )(proj, proj, conv_buf, h0.reshape(B, 1, DA), cw, cbias.reshape(1, DA), wr, br.reshape(1, DA),
      wi, bi.reshape(1, DA), lam.reshape(1, DA))


def _attn_kernel(*refs, chq, has_hist):
    if has_hist:
        q_ref, k_ref, v_ref, hk_ref, hv_ref, bias_ref, o_ref, kbuf, vbuf = refs
    else:
        q_ref, k_ref, v_ref, bias_ref, o_ref, kbuf, vbuf = refs
    c = pl.program_id(1)
    T = k_ref.shape[0]
    W = BAND_PAST + chq

    @pl.when(c == 0)
    def _():
        if has_hist:
            kbuf[0:BAND_PAST, :] = hk_ref[...].astype(BF16)
            vbuf[0:BAND_PAST, :] = hv_ref[...].astype(BF16)
        else:
            kbuf[0:BAND_PAST, :] = jnp.zeros((BAND_PAST, DB), BF16)
            vbuf[0:BAND_PAST, :] = jnp.zeros((BAND_PAST, DB), BF16)
        kbuf[BAND_PAST:BAND_PAST + T, :] = k_ref[...].astype(BF16)
        vbuf[BAND_PAST:BAND_PAST + T, :] = v_ref[...].astype(BF16)

    start = pl.multiple_of(c * chq, chq)
    kw = kbuf[pl.ds(start, W), :]
    vw = vbuf[pl.ds(start, W), :]
    q = q_ref[...].astype(BF16)
    if not has_hist:
        kpos = start - BAND_PAST + lax.broadcasted_iota(I32, (chq, W), 1)
        valid = kpos >= 0
    outs = []
    for h in range(HB):
        sl = slice(h * DHB, (h + 1) * DHB)
        s = _dot_nt(q[:, sl], kw[:, sl]) * (DHB ** -0.5) + bias_ref[h]
        if not has_hist:
            s = jnp.where(valid, s, -jnp.inf)
        m = jnp.max(s, axis=-1, keepdims=True)
        e = jnp.exp(s - m)
        p = e / jnp.sum(e, axis=-1, keepdims=True)
        outs.append(_dot(p.astype(BF16), vw[:, sl]))
    o_ref[...] = jnp.concatenate(outs, axis=-1).astype(BF16)


def _band_attention(proj, row0, B, T, chq, bias, hist_k=None, hist_v=None):
    nC = T // chq
    rbq = row0 // chq
    rbk = row0 // T
    has_hist = hist_k is not None
    in_specs = [pl.BlockSpec((chq, DB), lambda b, c: (rbq + b * nC + c, 2)),
                pl.BlockSpec((T, DB), lambda b, c: (rbk + b, 3)),
                pl.BlockSpec((T, DB), lambda b, c: (rbk + b, 4))]
    args = [proj, proj, proj]
    if has_hist:
        in_specs += [pl.BlockSpec((None, BAND_PAST, DB), lambda b, c: (b, 0, 0))] * 2
        args += [hist_k, hist_v]
    in_specs.append(pl.BlockSpec((HB, chq, BAND_PAST + chq), lambda b, c: (0, 0, 0)))
    args.append(bias)
    return pl.pallas_call(
        functools.partial(_attn_kernel, chq=chq, has_hist=has_hist),
        grid=(B, nC),
        in_specs=in_specs,
        out_specs=pl.BlockSpec((chq, DB), lambda b, c: (b * nC + c, 0)),
        out_shape=jax.ShapeDtypeStruct((B * T, DB), BF16),
        scratch_shapes=[pltpu.VMEM((BAND_PAST + T, DB), BF16), pltpu.VMEM((BAND_PAST + T, DB), BF16)],
        compiler_params=_cparams(("parallel", "arbitrary")),
        name="band_attention",
    )(*args)


def _rel_bias(table, chq):
    q_pos = BAND_PAST + jnp.arange(chq)
    k_pos = jnp.arange(BAND_PAST + chq)
    rel = jnp.clip(q_pos[:, None] - k_pos[None, :], -MAX_REL, MAX_REL) + MAX_REL
    return jnp.take(table, rel, axis=1)


def _outproj_kernel(*refs, n_in):
    ys, ws = refs[:n_in], refs[n_in:2 * n_in]
    x_ref, gt_ref, o_ref = refs[2 * n_in:]
    acc = _dot(ys[0][...], ws[0][...])
    for y_ref, w_ref in zip(ys[1:], ws[1:]):
        acc = acc + _dot(y_ref[...], w_ref[...])
    bB, pb, d = x_ref.shape
    o_ref[...] = x_ref[...] + gt_ref[...] * acc.reshape(bB, pb, d)


def _out_projection(ys, ws, x_all, mod_pb, gt_col):
    npb, pb, d = x_all.shape
    bB = 32
    tm = bB * pb
    n_in = len(ys)
    in_specs = [pl.BlockSpec((tm, y.shape[1]), lambda i: (i, 0)) for y in ys]
    in_specs += [pl.BlockSpec(w.shape, lambda i: (0, 0)) for w in ws]
    in_specs += [pl.BlockSpec((bB, pb, d), lambda i: (i, 0, 0)),
                 pl.BlockSpec((bB, 1, d), lambda i: (i, 0, gt_col))]
    return pl.pallas_call(
        functools.partial(_outproj_kernel, n_in=n_in),
        grid=(npb // bB,),
        in_specs=in_specs,
        out_specs=pl.BlockSpec((bB, pb, d), lambda i: (i, 0, 0)),
        out_shape=jax.ShapeDtypeStruct(x_all.shape, F32),
        compiler_params=_cparams(("parallel",)),
        name="out_projection",
    )(*ys, *ws, x_all, mod_pb)


def _gdnpre_kernel(qkv_ref, ab_ref, cb_ref, cw_ref, alog_ref, dtb_ref,
                   q_ref, k_ref, v_ref, gb_ref, cbo_ref, xbuf):
    tT = qkv_ref.shape[0]

    @pl.when(pl.program_id(1) == 0)
    def _():
        xbuf[0:SUBLANES, :] = jnp.zeros((SUBLANES, N_QKV), F32)
        xbuf[SUBLANES - (CONV_W - 1):SUBLANES, :] = cb_ref[...]

    x = qkv_ref[...]
    xbuf[SUBLANES:SUBLANES + tT, :] = x
    xc = cw_ref[3:4, :] * x
    for j in range(CONV_W - 1):
        xc = xc + cw_ref[j:j + 1, :] * xbuf[SUBLANES - 3 + j:SUBLANES - 3 + j + tT, :]
    tail = xbuf[tT:tT + SUBLANES, :]
    xbuf[0:SUBLANES, :] = tail
    cbo_ref[...] = tail[SUBLANES - (CONV_W - 1):, :]

    xs = xc * jax.nn.sigmoid(xc)
    for h in range(HC):
        qh = xs[:, h * DK:(h + 1) * DK]
        q_ref[:, h * DK:(h + 1) * DK] = (qh * lax.rsqrt(jnp.sum(qh * qh, axis=-1, keepdims=True) + NORM_EPS)
                                         * (DK ** -0.5))
        kh = xs[:, DCK + h * DK:DCK + (h + 1) * DK]
        k_ref[:, h * DK:(h + 1) * DK] = kh * lax.rsqrt(jnp.sum(kh * kh, axis=-1, keepdims=True) + NORM_EPS)
    v_ref[...] = xs[:, 2 * DCK:]

    ab = ab_ref[...]
    g = -jnp.exp(alog_ref[...]) * _softplus(ab + dtb_ref[...])
    beta = jax.nn.sigmoid(ab)
    lane = lax.broadcasted_iota(I32, ab.shape, 1)
    gb_ref[...] = jnp.where(lane < HC, g, beta)


def _gdn_pre(proj, row0, B, T, tT, conv_buf, cw, alog_pad, dtb_pad):
    nT = T // tT
    rb0 = row0 // tT
    return pl.pallas_call(
        _gdnpre_kernel,
        grid=(B, nT),
        in_specs=[pl.BlockSpec((tT, N_QKV), lambda b, t: (rb0 + b * nT + t, 0)),
                  pl.BlockSpec((tT, LANES), lambda b, t: (rb0 + b * nT + t, (N_QKV + DCV) // LANES)),
                  pl.BlockSpec((None, CONV_W - 1, N_QKV), lambda b, t: (b, 0, 0)),
                  pl.BlockSpec((CONV_W, N_QKV), lambda b, t: (0, 0)),
                  pl.BlockSpec((1, LANES), lambda b, t: (0, 0)),
                  pl.BlockSpec((1, LANES), lambda b, t: (0, 0))],
        out_specs=[pl.BlockSpec((tT, DCK), lambda b, t: (b * nT + t, 0)),
                   pl.BlockSpec((tT, DCK), lambda b, t: (b * nT + t, 0)),
                   pl.BlockSpec((tT, DCV), lambda b, t: (b * nT + t, 0)),
                   pl.BlockSpec((tT, LANES), lambda b, t: (b * nT + t, 0)),
                   pl.BlockSpec((None, CONV_W - 1, N_QKV), lambda b, t: (b, 0, 0))],
        out_shape=[jax.ShapeDtypeStruct((B * T, DCK), F32),
                   jax.ShapeDtypeStruct((B * T, DCK), F32),
                   jax.ShapeDtypeStruct((B * T, DCV), F32),
                   jax.ShapeDtypeStruct((B * T, LANES), F32),
                   jax.ShapeDtypeStruct((B, CONV_W - 1, N_QKV), F32)],
        scratch_shapes=[pltpu.VMEM((tT + SUBLANES, N_QKV), F32)],
        compiler_params=_cparams(("parallel", "arbitrary")),
        name="gdn_pre",
    )(proj, proj, conv_buf, cw, alog_pad, dtb_pad)


def _delta_kernel(q_ref, k_ref, v_ref, gb_ref, z_ref, s0_ref, ng_ref, o_ref, so_ref, s_scr, *, C, hps):
    T = q_ref.shape[0]
    n_chunks = T // C
    s_scr[...] = s0_ref[...]
    ri = lax.broadcasted_iota(I32, (C, C), 0)
    ci = lax.broadcasted_iota(I32, (C, C), 1)
    causal = ri >= ci
    strict = ri > ci
    eye = (ri == ci).astype(F32)
    tril = causal.astype(F32)
    ones_cc = jnp.ones((C, C), F32)
    head0 = pl.program_id(1) * hps

    def chunk(n, carry):
        r0 = pl.multiple_of(n * C, C)
        gbc = gb_ref[pl.ds(r0, C), :]
        lane = lax.broadcasted_iota(I32, gbc.shape, 1)
        for j in range(hps):
            cs = slice(j * DK, (j + 1) * DK)
            g = jnp.sum(jnp.where(lane == head0 + j, gbc, 0.0), axis=1, keepdims=True)
            beta = jnp.sum(jnp.where(lane == head0 + j + HC, gbc, 0.0), axis=1, keepdims=True)
            q = q_ref[pl.ds(r0, C), cs]
            k = k_ref[pl.ds(r0, C), cs]
            v = v_ref[pl.ds(r0, C), cs]
            gcol = _dot(tril, jnp.broadcast_to(g, (C, LANES)), HI)
            G = gcol[:, 0:1]
            grow = _dot(ones_cc, eye * gcol[:, :C], HI)
            L = jnp.where(causal, jnp.exp(jnp.where(causal, gcol[:, :C] - grow, 0.0)), 0.0)
            kb = k * beta
            kbf = k.astype(BF16)
            A = jnp.where(strict, _dot_nt(kb.astype(BF16), kbf) * L, 0.0)
            Bm = -A
            P = eye + Bm
            p2 = 2
            while p2 < C:
                Bm = _dot(Bm, Bm, HI)
                P = P + _dot(P, Bm, HI)
                p2 *= 2
            eG = jnp.exp(G)
            X = _dot(P, jnp.concatenate([v * beta, kb * eG], axis=1), HI)
            u = X[:, :DV]
            w = X[:, DV:]
            a_qk = _dot_nt(q.astype(BF16), kbf) * L
            g_last = gcol[C - 1:C, 0:1]
            q_dec = q * eG
            k_dec = k * jnp.exp(g_last - G)
            S = s_scr[j]
            Sb = S.astype(BF16)
            v_new = u - _dot(w.astype(BF16), Sb)
            vnb = v_new.astype(BF16)
            o = _dot(q_dec.astype(BF16), Sb) + _dot(a_qk.astype(BF16), vnb)
            s_scr[j] = S * jnp.exp(g_last) + _dot_tn(k_dec.astype(BF16), vnb)
            on = o * lax.rsqrt(jnp.mean(o * o, axis=-1, keepdims=True) + NORM_EPS) * ng_ref[...]
            z = z_ref[pl.ds(r0, C), cs]
            o_ref[pl.ds(r0, C), cs] = (on * (z * jax.nn.sigmoid(z))).astype(BF16)
        return carry

    lax.fori_loop(0, n_chunks, chunk, 0)
    so_ref[...] = s_scr[...]


def _delta_rule(q, k, v, gb, proj, row0, B, T, S0, norm_g, hps):
    C = min(DELTA_CHUNK, T)
    rb = row0 // T
    wb = hps * DK
    zcol0 = N_QKV // wb
    qkv_spec = lambda: pl.BlockSpec((T, wb), lambda b, h: (b, h))
    return pl.pallas_call(
        functools.partial(_delta_kernel, C=C, hps=hps),
        grid=(B, HC // hps),
        in_specs=[qkv_spec(), qkv_spec(), qkv_spec(),
                  pl.BlockSpec((T, LANES), lambda b, h: (b, 0)),
                  pl.BlockSpec((T, wb), lambda b, h: (rb + b, zcol0 + h)),
                  pl.BlockSpec((None, hps, DK, DV), lambda b, h: (b, h, 0, 0)),
                  pl.BlockSpec((1, DV), lambda b, h: (0, 0))],
        out_specs=[pl.BlockSpec((T, wb), lambda b, h: (b, h)),
                   pl.BlockSpec((None, hps, DK, DV), lambda b, h: (b, h, 0, 0))],
        out_shape=[jax.ShapeDtypeStruct((B * T, DCV), BF16),
                   jax.ShapeDtypeStruct((B, HC, DK, DV), F32)],
        scratch_shapes=[pltpu.VMEM((hps, DK, DV), F32)],
        compiler_params=_cparams(("parallel", "parallel")),
        name="delta_rule",
    )(q, k, v, gb, proj, S0, norm_g.reshape(1, DV))


def _first_argmax(x, idx, size):
    m = jnp.max(x, axis=0, keepdims=True)
    am = jnp.min(jnp.where(x == m, idx, size), axis=0, keepdims=True)
    return m, am


def _router_kernel(x_ref, g_ref, sc_ref, sh_ref, rwt_ref, rb_ref,
                   h_ref, e_ref, w_ref, r_ref, cnt_ref, run_scr):
    tn = h_ref.shape[0]
    gsz = N_EXPERTS // N_GROUPS

    @pl.when(pl.program_id(0) == 0)
    def _():
        run_scr[...] = jnp.zeros_like(run_scr)

    h = _norm_mod(x_ref[...], g_ref[...], sc_ref[...], sh_ref[...])
    h_ref[...] = h
    scores = jax.nn.sigmoid(_dot_nt(rwt_ref[...], h, HI))
    sel = scores + rb_ref[...]

    gidx = lax.broadcasted_iota(I32, (gsz, tn), 0)
    gs_rows = []
    for gi in range(N_GROUPS):
        blk = sel[gi * gsz:(gi + 1) * gsz, :]
        m1, a1 = _first_argmax(blk, gidx, gsz)
        m2 = jnp.max(jnp.where(gidx == a1, -jnp.inf, blk), axis=0, keepdims=True)
        gs_rows.append(m1 + m2)
    gscore = jnp.concatenate(gs_rows, axis=0)
    gi8 = lax.broadcasted_iota(I32, (N_GROUPS, tn), 0)
    gmask = jnp.zeros((N_GROUPS, tn), jnp.bool_)
    gwork = gscore
    for _ in range(TOPK_GROUPS):
        _, a = _first_argmax(gwork, gi8, N_GROUPS)
        hit = gi8 == a
        gmask = jnp.logical_or(gmask, hit)
        gwork = jnp.where(hit, -jnp.inf, gwork)
    emask = jnp.concatenate([jnp.broadcast_to(gmask[gi:gi + 1, :], (gsz, tn)) for gi in range(N_GROUPS)], axis=0)

    eidx = lax.broadcasted_iota(I32, (N_EXPERTS, tn), 0)
    work = jnp.where(emask, sel, -jnp.inf)
    hits = []
    chosen = jnp.zeros((N_EXPERTS, tn), jnp.bool_)
    for _ in range(TOP_K):
        _, a = _first_argmax(work, eidx, N_EXPERTS)
        hit = eidx == a
        hits.append((a, hit))
        chosen = jnp.logical_or(chosen, hit)
        work = jnp.where(hit, -jnp.inf, work)
    chosen_f = chosen.astype(F32)
    denom = jnp.sum(scores * chosen_f, axis=0, keepdims=True)

    ti = lax.broadcasted_iota(I32, (tn, tn), 0)
    tj = lax.broadcasted_iota(I32, (tn, tn), 1)
    before = (ti < tj).astype(BF16)
    rank = _dot(chosen_f.astype(BF16), before) + run_scr[...]
    run_scr[...] = run_scr[...] + jnp.sum(chosen_f, axis=1, keepdims=True)
    cnt_ref[...] = jnp.broadcast_to(run_scr[...], cnt_ref.shape).astype(I32)

    e_rows, w_rows, r_rows = [], [], []
    for a, hit in hits:
        hf = hit.astype(F32)
        e_rows.append(a)
        w_rows.append(jnp.sum(scores * hf, axis=0, keepdims=True) / denom * ROUTE_SCALE)
        r_rows.append(jnp.sum(rank * hf, axis=0, keepdims=True))
    e_ref[...] = jnp.concatenate(e_rows, axis=0)
    w_ref[...] = jnp.concatenate(w_rows, axis=0)
    r_ref[...] = jnp.concatenate(r_rows, axis=0).astype(I32)


def _router(x_all, g, mod_pb, sc_col, sh_col, rwt, rb):
    npb, pb, d = x_all.shape
    n = npb * pb
    bB = 16
    tn = bB * pb
    return pl.pallas_call(
        _router_kernel,
        grid=(npb // bB,),
        in_specs=[pl.BlockSpec((bB, pb, d), lambda i: (i, 0, 0)),
                  pl.BlockSpec((1, d), lambda i: (0, 0)),
                  pl.BlockSpec((bB, 1, d), lambda i: (i, 0, sc_col)),
                  pl.BlockSpec((bB, 1, d), lambda i: (i, 0, sh_col)),
                  pl.BlockSpec((N_EXPERTS, d), lambda i: (0, 0)),
                  pl.BlockSpec((N_EXPERTS, 1), lambda i: (0, 0))],
        out_specs=[pl.BlockSpec((tn, d), lambda i: (i, 0)),
                   pl.BlockSpec((TOP_K, tn), lambda i: (0, i)),
                   pl.BlockSpec((TOP_K, tn), lambda i: (0, i)),
                   pl.BlockSpec((TOP_K, tn), lambda i: (0, i)),
                   pl.BlockSpec((N_EXPERTS, LANES), lambda i: (0, 0))],
        out_shape=[jax.ShapeDtypeStruct((n, d), F32),
                   jax.ShapeDtypeStruct((TOP_K, n), I32),
                   jax.ShapeDtypeStruct((TOP_K, n), F32),
                   jax.ShapeDtypeStruct((TOP_K, n), I32),
                   jax.ShapeDtypeStruct((N_EXPERTS, LANES), I32)],
        scratch_shapes=[pltpu.VMEM((N_EXPERTS, 1), F32)],
        compiler_params=_cparams(("arbitrary",)),
        name="moe_router",
    )(x_all, g.reshape(1, d), mod_pb, mod_pb, rwt, rb.reshape(N_EXPERTS, 1))


def _gather_rows(idx_ref, n_rows, src_hbm, dst, sem):
    def body(r, carry):
        tok = idx_ref[0, r]
        pltpu.make_async_copy(src_hbm.at[pl.ds(tok, 1), :], dst.at[pl.ds(r, 1), :], sem).start()
        return carry
    lax.fori_loop(0, n_rows, body, 0, unroll=8)


def _experts_kernel(te_ref, nu_ref, src_ref, nxt_ref, h_hbm, wgu_ref, wd_ref, ys_ref, buf, sem):
    i = pl.program_id(0)
    tm = ys_ref.shape[0]
    slot = lax.rem(i, 2)
    nu = nu_ref[0]

    @pl.when(i == 0)
    def _():
        _gather_rows(src_ref, tm, h_hbm, buf.at[0], sem.at[0])

    @pl.when(i + 1 < nu)
    def _():
        _gather_rows(nxt_ref, tm, h_hbm, buf.at[1 - slot], sem.at[1 - slot])

    @pl.when(i < nu)
    def _():
        pltpu.make_async_copy(h_hbm.at[pl.ds(0, tm), :], buf.at[slot], sem.at[slot]).wait()
        x = buf[slot].astype(BF16)
        gu = _dot(x, wgu_ref[0])
        hid = jax.nn.silu(gu[:, :D_EXPERT]) * gu[:, D_EXPERT:]
        ys_ref[...] = _dot(hid.astype(BF16), wd_ref[0])

    @pl.when(i >= nu)
    def _():
        ys_ref[...] = jnp.zeros_like(ys_ref)


def _routed_experts(h2, src_tok, tile_expert, n_used, wgu, wd, tm):
    n_tiles = src_tok.shape[0]
    d = h2.shape[1]
    last = n_tiles - 1
    grid_spec = pltpu.PrefetchScalarGridSpec(
        num_scalar_prefetch=2,
        grid=(n_tiles,),
        in_specs=[pl.BlockSpec((None, 1, tm), lambda i, te, nu: (i, 0, 0), memory_space=pltpu.SMEM),
                  pl.BlockSpec((None, 1, tm), lambda i, te, nu: (jnp.minimum(i + 1, last), 0, 0),
                               memory_space=pltpu.SMEM),
                  pl.BlockSpec(memory_space=pl.ANY),
                  pl.BlockSpec((1, d, 2 * D_EXPERT), lambda i, te, nu: (te[i], 0, 0)),
                  pl.BlockSpec((1, D_EXPERT, d), lambda i, te, nu: (te[i], 0, 0))],
        out_specs=pl.BlockSpec((tm, d), lambda i, te, nu: (i, 0)),
        scratch_shapes=[pltpu.VMEM((2, tm, d), F32), pltpu.SemaphoreType.DMA((2,))],
    )
    return pl.pallas_call(
        _experts_kernel,
        grid_spec=grid_spec,
        out_shape=jax.ShapeDtypeStruct((n_tiles * tm, d), F32),
        compiler_params=_cparams(("arbitrary",)),
        name="moe_experts",
    )(tile_expert, n_used, src_tok, src_tok, h2, wgu, wd)


def _combine_kernel(pos_ref, nxt_ref, ys_hbm, w_ref, h_ref, x_ref, gt_ref, sgu_ref, sd_ref, o_ref, gbuf, sem):
    i = pl.program_id(0)
    tb = w_ref.shape[0]
    slot = lax.rem(i, 2)

    @pl.when(i == 0)
    def _():
        _gather_rows(pos_ref, TOP_K * tb, ys_hbm, gbuf.at[0], sem.at[0])

    @pl.when(i + 1 < pl.num_programs(0))
    def _():
        _gather_rows(nxt_ref, TOP_K * tb, ys_hbm, gbuf.at[1 - slot], sem.at[1 - slot])

    hb = h_ref[...].astype(BF16)
    gu = _dot(hb, sgu_ref[...])
    hid = jax.nn.silu(gu[:, :D_SHARED]) * gu[:, D_SHARED:]
    acc = _dot(hid.astype(BF16), sd_ref[...])
    pltpu.make_async_copy(ys_hbm.at[pl.ds(0, TOP_K * tb), :], gbuf.at[slot], sem.at[slot]).wait()
    w = w_ref[...]
    for k in range(TOP_K):
        acc = acc + w[:, k:k + 1] * gbuf[slot, k * tb:(k + 1) * tb, :]
    bB, pb, d = x_ref.shape
    o_ref[...] = x_ref[...] + gt_ref[...] * acc.reshape(bB, pb, d)


def _moe_combine(ys, pos_tiles, w_tok, h2, x_all, mod_pb, gt_col, sgu, sd, tb):
    npb, pb, d = x_all.shape
    n_tb = pos_tiles.shape[0]
    bB = tb // pb
    last = n_tb - 1
    return pl.pallas_call(
        _combine_kernel,
        grid=(n_tb,),
        in_specs=[pl.BlockSpec((None, 1, TOP_K * tb), lambda i: (i, 0, 0), memory_space=pltpu.SMEM),
                  pl.BlockSpec((None, 1, TOP_K * tb), lambda i: (jnp.minimum(i + 1, last), 0, 0),
                               memory_space=pltpu.SMEM),
                  pl.BlockSpec(memory_space=pl.ANY),
                  pl.BlockSpec((tb, TOP_K), lambda i: (i, 0)),
                  pl.BlockSpec((tb, d), lambda i: (i, 0)),
                  pl.BlockSpec((bB, pb, d), lambda i: (i, 0, 0)),
                  pl.BlockSpec((bB, 1, d), lambda i: (i, 0, gt_col)),
                  pl.BlockSpec(sgu.shape, lambda i: (0, 0)),
                  pl.BlockSpec(sd.shape, lambda i: (0, 0))],
        out_specs=pl.BlockSpec((bB, pb, d), lambda i: (i, 0, 0)),
        out_shape=jax.ShapeDtypeStruct(x_all.shape, F32),
        scratch_shapes=[pltpu.VMEM((2, TOP_K * tb, d), F32), pltpu.SemaphoreType.DMA((2,))],
        compiler_params=_cparams(("arbitrary",)),
        name="moe_combine",
    )(pos_tiles, pos_tiles, ys, w_tok, h2, x_all, mod_pb, sgu, sd)


def _moe(x_all, g, mod_pb, rw, rb, wg, wu, wd, sg, su, sd):
    npb, pb, d = x_all.shape
    n = npb * pb
    tm = 256
    tb = 128
    h2, e_t, w_t, r_t, cnt = _router(x_all, g, mod_pb, 4, 3, rw.T, rb)

    counts = cnt[:, 0]
    padded = (counts + tm - 1) // tm * tm
    ends = jnp.cumsum(padded)
    starts = ends - padded
    n_tiles = (n * TOP_K + N_EXPERTS * tm) // tm
    pos = starts[e_t] + r_t
    n_used = (ends[-1] // tm).astype(I32)
    tile_start = jnp.minimum(jnp.arange(n_tiles, dtype=I32), n_used - 1) * tm
    tile_expert = jnp.sum(ends[None, :] <= tile_start[:, None], axis=1).astype(I32)
    tok = jnp.broadcast_to(jnp.arange(n, dtype=I32)[None, :], (TOP_K, n))
    src_tok = jnp.zeros((n_tiles * tm,), I32).at[pos.reshape(-1)].set(tok.reshape(-1))
    src_tok = src_tok.reshape(n_tiles, 1, tm)
    pos_tiles = pos.reshape(TOP_K, n // tb, tb).transpose(1, 0, 2).reshape(n // tb, 1, TOP_K * tb)

    wgu = jnp.concatenate([wg, wu], axis=-1).astype(BF16)
    ys = _routed_experts(h2, src_tok, tile_expert, n_used.reshape(1), wgu, wd.astype(BF16), tm)
    sgu = jnp.concatenate([sg, su], axis=-1).astype(BF16)
    return _moe_combine(ys, pos_tiles, w_t.T, h2, x_all, mod_pb, 5, sgu, sd.astype(BF16), tb)


def _final_kernel(x_ref, g_ref, o_ref):
    x = x_ref[...]
    ms = jnp.mean(x * x, axis=-1, keepdims=True)
    o_ref[...] = x * lax.rsqrt(ms + NORM_EPS) * g_ref[...]


def _final_norm(x_all, g, pb0, n_pb):
    npb, pb, d = x_all.shape
    bB = 32
    return pl.pallas_call(
        _final_kernel,
        grid=(n_pb // bB,),
        in_specs=[pl.BlockSpec((bB, pb, d), lambda i: (pb0 // bB + i, 0, 0)),
                  pl.BlockSpec((1, d), lambda i: (0, 0))],
        out_specs=pl.BlockSpec((bB, pb, d), lambda i: (i, 0, 0)),
        out_shape=jax.ShapeDtypeStruct((n_pb, pb, d), F32),
        compiler_params=_cparams(("parallel",)),
        name="final_norm",
    )(x_all, g.reshape(1, d))


def _block_diag(w):
    nb, di, do = w.shape
    return jnp.einsum("nde,nm->ndme", w, jnp.eye(nb, dtype=w.dtype)).reshape(nb * di, nb * do)


def kernel(x_prompt, x_sample, c_prompt, c_sample, cache_b_k, cache_b_v, state_a_conv, state_a_h, state_c_conv, state_c_S, norm1_g, norm2_g, final_g, w_mod, b_mod, w_in0, w_out0, a_conv_w, a_conv_b, a_w_r, a_b_r, a_w_i, a_b_i, a_lambda, b_rel_bias, w_in1, w_out1, c_conv_w, c_A_log, c_dt_bias, c_norm_g, router_w, router_bias, e_w_gate, e_w_up, e_w_down, s_w_gate, s_w_up, s_w_down):
    Bp, Tp, d = x_prompt.shape
    Bs, Ts, _ = x_sample.shape
    np_tok, ns_tok = Bp * Tp, Bs * Ts
    npb_p, npb_s = np_tok // PB, ns_tok // PB
    depth = w_mod.shape[0]

    x_all = jnp.concatenate([x_prompt.reshape(npb_p, PB, d), x_sample.reshape(npb_s, PB, d)], axis=0)
    mod = _modulation(jnp.concatenate([c_prompt, c_sample], axis=0), w_mod, b_mod)
    row_of_pb = jnp.concatenate([jnp.repeat(jnp.arange(Bp), Tp // PB), Bp + jnp.repeat(jnp.arange(Bs), Ts // PB)])

    outs = {}
    for l in range(depth):
        mod_pb = jnp.take(mod[l], row_of_pb, axis=0)[:, None, :]
        j = l // 2
        if l % 2 == 0:
            proj = _in_projection(x_all, norm1_g[l], mod_pb, 1, 0, w_in0[j].astype(BF16), 1280)
            wr, wi = _block_diag(a_w_r[j]).astype(BF16), _block_diag(a_w_i[j]).astype(BF16)
            a_args = (a_conv_w[j], a_conv_b[j], wr, a_b_r[j], wi, a_b_i[j], a_lambda[j])
            ya_p, ac_p, ah_p = _rglru(proj, 0, Bp, Tp, 512, jnp.zeros((Bp, CONV_W - 1, DA), F32),
                                      jnp.zeros((Bp, DA), F32), *a_args)
            ya_s, ac_s, ah_s = _rglru(proj, np_tok, Bs, Ts, Ts, state_a_conv[j], state_a_h[j], *a_args)
            yb_p = _band_attention(proj, 0, Bp, Tp, CHUNK, _rel_bias(b_rel_bias[j], CHUNK))
            yb_s = _band_attention(proj, np_tok, Bs, Ts, Ts, _rel_bias(b_rel_bias[j], Ts),
                                   cache_b_k[j].reshape(Bs, BAND_PAST, DB), cache_b_v[j].reshape(Bs, BAND_PAST, DB))
            ya = jnp.concatenate([ya_p, ya_s], axis=0)
            yb = jnp.concatenate([yb_p, yb_s], axis=0)
            w_out = w_out0[j].astype(BF16)
            x_all = _out_projection([ya, yb], [w_out[:DA], w_out[DA:]], x_all, mod_pb, 2)
            keep = min(BAND_PAST, Tp)
            kv_p = proj[:np_tok].reshape(Bp, Tp, IN0)[:, Tp - keep:, 2 * DA + DB:]
            kv_s = proj[np_tok:].reshape(Bs, Ts, IN0)[:, :, 2 * DA + DB:]
            outs.setdefault("bk_p", []).append(kv_p[..., :DB].reshape(Bp, keep, HB, DHB))
            outs.setdefault("bv_p", []).append(kv_p[..., DB:].reshape(Bp, keep, HB, DHB))
            outs.setdefault("bk_s", []).append(kv_s[..., :DB].reshape(Bs, Ts, HB, DHB))
            outs.setdefault("bv_s", []).append(kv_s[..., DB:].reshape(Bs, Ts, HB, DHB))
            outs.setdefault("ac_p", []).append(ac_p)
            outs.setdefault("ac_s", []).append(ac_s)
            outs.setdefault("ah_p", []).append(ah_p.reshape(Bp, DA))
            outs.setdefault("ah_s", []).append(ah_s.reshape(Bs, DA))
        else:
            w_in = jnp.pad(w_in1[j], ((0, 0), (0, IN1_PAD - IN1))).astype(BF16)
            proj = _in_projection(x_all, norm1_g[l], mod_pb, 1, 0, w_in, 1408)
            alog = jnp.pad(c_A_log[j], (0, LANES - HC)).reshape(1, LANES)
            dtb = jnp.pad(c_dt_bias[j], (0, LANES - HC)).reshape(1, LANES)
            q_p, k_p, v_p, gb_p, cc_p = _gdn_pre(proj, 0, Bp, Tp, 256, jnp.zeros((Bp, CONV_W - 1, N_QKV), F32),
                                                 c_conv_w[j], alog, dtb)
            q_s, k_s, v_s, gb_s, cc_s = _gdn_pre(proj, np_tok, Bs, Ts, Ts, state_c_conv[j], c_conv_w[j], alog, dtb)
            o_p, cs_p = _delta_rule(q_p, k_p, v_p, gb_p, proj, 0, Bp, Tp, jnp.zeros((Bp, HC, DK, DV), F32),
                                    c_norm_g[j], 2)
            o_s, cs_s = _delta_rule(q_s, k_s, v_s, gb_s, proj, np_tok, Bs, Ts, state_c_S[j], c_norm_g[j], 2)
            o = jnp.concatenate([o_p, o_s], axis=0)
            x_all = _out_projection([o], [w_out1[j].astype(BF16)], x_all, mod_pb, 2)
            outs.setdefault("cc_p", []).append(cc_p)
            outs.setdefault("cc_s", []).append(cc_s)
            outs.setdefault("cs_p", []).append(cs_p)
            outs.setdefault("cs_s", []).append(cs_s)
        x_all = _moe(x_all, norm2_g[l], mod_pb, router_w[l], router_bias[l], e_w_gate[l], e_w_up[l],
                     e_w_down[l], s_w_gate[l], s_w_up[l], s_w_down[l])

    y_prompt = _final_norm(x_all, final_g, 0, npb_p).reshape(Bp, Tp, d)
    y_sample = _final_norm(x_all, final_g, npb_p, npb_s).reshape(Bs, Ts, d)
    st = {k: jnp.stack(v) for k, v in outs.items()}
    return (y_prompt, y_sample, st["bk_p"], st["bk_s"], st["bv_p"], st["bv_s"], st["ac_p"], st["ac_s"],
            st["ah_p"], st["ah_s"], st["cc_p"], st["cc_s"], st["cs_p"], st["cs_s"])
```

```python
import functools

import jax
import jax.numpy as jnp
from jax import lax
from jax.experimental import pallas as pl
from jax.experimental.pallas import tpu as pltpu

F32 = jnp.float32
BF16 = jnp.bfloat16
I32 = jnp.int32

D_MODEL = 1024
CHUNK = 64
NORM_EPS = 1e-6
CONV_W = 4
DA = 512
NB_A = 8
LRU_C = 8.0
HB = 8
DHB = 64
DB = HB * DHB
BAND_PAST = 8 * CHUNK
MAX_REL = 128
HC = 8
DK = 128
DV = 128
DCK = HC * DK
DCV = HC * DV
N_QKV = 2 * DCK + DCV
DELTA_CHUNK = 64
N_EXPERTS = 64
TOP_K = 8
N_GROUPS = 8
TOPK_GROUPS = 4
D_EXPERT = 256
D_SHARED = 256
ROUTE_SCALE = 2.5
IN0 = 2 * DA + 3 * DB
IN1 = N_QKV + DCV + 2 * HC

LANES = 128
SUBLANES = 8
PB = 32
IN1_PAD = N_QKV + DCV + LANES
VMEM_LIMIT = 48 * 1024 * 1024
HI = lax.Precision.HIGHEST


def _cparams(sem):
    return pltpu.CompilerParams(dimension_semantics=sem, vmem_limit_bytes=VMEM_LIMIT)


def _softplus(x):
    return jnp.maximum(x, 0.0) + jnp.log1p(jnp.exp(-jnp.abs(x)))


def _dot(a, b, precision=None):
    return jnp.dot(a, b, preferred_element_type=F32, precision=precision)


def _dot_nt(a, b, precision=None):
    return lax.dot_general(a, b, (((1,), (1,)), ((), ())), preferred_element_type=F32, precision=precision)


def _dot_tn(a, b, precision=None):
    return lax.dot_general(a, b, (((0,), (0,)), ((), ())), preferred_element_type=F32, precision=precision)


def _mod_kernel(c_ref, w_ref, b_ref, o_ref):
    c = c_ref[...]
    cond = c * jax.nn.sigmoid(c)
    o_ref[0] = _dot(cond.astype(BF16), w_ref[0].astype(BF16)) + b_ref[0]


def _modulation(c_all, w_mod, b_mod):
    depth, d, n6 = w_mod.shape
    nb = c_all.shape[0]
    tn = 1536
    return pl.pallas_call(
        _mod_kernel,
        grid=(depth, n6 // tn),
        in_specs=[pl.BlockSpec((nb, d), lambda l, j: (0, 0)),
                  pl.BlockSpec((1, d, tn), lambda l, j: (l, 0, j)),
                  pl.BlockSpec((1, 1, tn), lambda l, j: (l, 0, j))],
        out_specs=pl.BlockSpec((1, nb, tn), lambda l, j: (l, 0, j)),
        out_shape=jax.ShapeDtypeStruct((depth, nb, n6), F32),
        compiler_params=_cparams(("parallel", "parallel")),
        name="modulation",
    )(c_all, w_mod, b_mod.reshape(depth, 1, n6))


def _norm_mod(x3, g, sc, sh):
    ms = jnp.mean(x3 * x3, axis=-1, keepdims=True)
    y = x3 * lax.rsqrt(ms + NORM_EPS) * g
    h = y * (1.0 + sc) + sh
    return h.reshape(x3.shape[0] * x3.shape[1], x3.shape[2])


def _inproj_kernel(x_ref, g_ref, sc_ref, sh_ref, w_ref, o_ref, h_scr):
    @pl.when(pl.program_id(1) == 0)
    def _():
        h_scr[...] = _norm_mod(x_ref[...], g_ref[...], sc_ref[...], sh_ref[...]).astype(BF16)
    o_ref[...] = _dot(h_scr[...], w_ref[...])


def _in_projection(x_all, g, mod_pb, sc_col, sh_col, w_bf16, tn):
    npb, pb, d = x_all.shape
    nout = w_bf16.shape[1]
    bB = 32
    tm = bB * pb
    return pl.pallas_call(
        _inproj_kernel,
        grid=(npb // bB, nout // tn),
        in_specs=[pl.BlockSpec((bB, pb, d), lambda i, j: (i, 0, 0)),
                  pl.BlockSpec((1, d), lambda i, j: (0, 0)),
                  pl.BlockSpec((bB, 1, d), lambda i, j: (i, 0, sc_col)),
                  pl.BlockSpec((bB, 1, d), lambda i, j: (i, 0, sh_col)),
                  pl.BlockSpec((d, tn), lambda i, j: (0, j))],
        out_specs=pl.BlockSpec((tm, tn), lambda i, j: (i, j)),
        out_shape=jax.ShapeDtypeStruct((npb * pb, nout), F32),
        scratch_shapes=[pltpu.VMEM((tm, d), BF16)],
        compiler_params=_cparams(("parallel", "arbitrary")),
        name="in_projection",
    )(x_all, g.reshape(1, d), mod_pb, mod_pb, w_bf16)


def _rglru_kernel(xa_ref, ga_ref, cb_ref, h0_ref, cw_ref, cbias_ref, wr_ref, br_ref, wi_ref, bi_ref, lam_ref,
                  y_ref, cbo_ref, ho_ref, xbuf, hcar):
    tT = xa_ref.shape[0]

    @pl.when(pl.program_id(1) == 0)
    def _():
        xbuf[0:SUBLANES, :] = jnp.zeros((SUBLANES, DA), F32)
        xbuf[SUBLANES - (CONV_W - 1):SUBLANES, :] = cb_ref[...]
        hcar[...] = h0_ref[...]

    xa = xa_ref[...]
    xbuf[SUBLANES:SUBLANES + tT, :] = xa
    xc = cw_ref[3:4, :] * xa + cbias_ref[...]
    for j in range(CONV_W - 1):
        xc = xc + cw_ref[j:j + 1, :] * xbuf[SUBLANES - 3 + j:SUBLANES - 3 + j + tT, :]
    tail = xbuf[tT:tT + SUBLANES, :]
    xbuf[0:SUBLANES, :] = tail
    cbo_ref[...] = tail[SUBLANES - (CONV_W - 1):, :]

    xcb = xc.astype(BF16)
    r = jax.nn.sigmoid(_dot(xcb, wr_ref[...]) + br_ref[...])
    i = jax.nn.sigmoid(_dot(xcb, wi_ref[...]) + bi_ref[...])
    log_a = -LRU_C * r * _softplus(-lam_ref[...])
    a = jnp.exp(log_a)
    th = jnp.tanh(log_a)
    u = jnp.sqrt(-2.0 * th / (1.0 - th)) * (i * xc)

    rows = lax.broadcasted_iota(I32, (tT, DA), 0)
    s = 1
    while s < tT:
        a_sh = pltpu.roll(a, s, 0)
        u_sh = pltpu.roll(u, s, 0)
        m = rows >= s
        u = jnp.where(m, a * u_sh + u, u)
        a = jnp.where(m, a * a_sh, a)
        s *= 2
    h = a * hcar[...] + u
    hlast = h[tT - 1:tT, :]
    hcar[...] = hlast
    ho_ref[...] = hlast
    y_ref[...] = (h * jax.nn.gelu(ga_ref[...])).astype(BF16)


def _rglru(proj, row0, B, T, tT, conv_buf, h0, cw, cbias, wr, br, wi, bi, lam):
    nT = T // tT
    rb0 = row0 // tT
    vec = lambda: pl.BlockSpec((1, DA), lambda b, t: (0, 0))
    return pl.pallas_call(
        _rglru_kernel,
        grid=(B, nT),
        in_specs=[pl.BlockSpec((tT, DA), lambda b, t: (rb0 + b * nT + t, 0)),
                  pl.BlockSpec((tT, DA), lambda b, t: (rb0 + b * nT + t, 1)),
                  pl.BlockSpec((None, CONV_W - 1, DA), lambda b, t: (b, 0, 0)),
                  pl.BlockSpec((None, 1, DA), lambda b, t: (b, 0, 0)),
                  pl.BlockSpec((CONV_W, DA), lambda b, t: (0, 0)),
                  vec(),
                  pl.BlockSpec((DA, DA), lambda b, t: (0, 0)),
                  vec(),
                  pl.BlockSpec((DA, DA), lambda b, t: (0, 0)),
                  vec(), vec()],
        out_specs=[pl.BlockSpec((tT, DA), lambda b, t: (b * nT + t, 0)),
                   pl.BlockSpec((None, CONV_W - 1, DA), lambda b, t: (b, 0, 0)),
                   pl.BlockSpec((None, 1, DA), lambda b, t: (b, 0, 0))],
        out_shape=[jax.ShapeDtypeStruct((B * T, DA), BF16),
                   jax.ShapeDtypeStruct((B, CONV_W - 1, DA), F32),
                   jax.ShapeDtypeStruct((B, 1, DA), F32)],
        scratch_shapes=[pltpu.VMEM((tT + SUBLANES, DA), F32), pltpu.VMEM((1, DA), F32)],
        compiler_params=_cparams(("parallel", "arbitrary")),
        name="rglru",
    )(proj, proj, conv_buf, h0.reshape(B, 1, DA), cw, cbias.reshape(1, DA), wr, br.reshape(1, DA),
      wi, bi.reshape(1, DA), lam.reshape(1, DA))


def _attn_kernel(*refs, chq, has_hist):
    if has_hist:
        q_ref, k_ref, v_ref, hk_ref, hv_ref, bias_ref, o_ref, kbuf, vbuf = refs
    else:
        q_ref, k_ref, v_ref, bias_ref, o_ref, kbuf, vbuf = refs
    c = pl.program_id(1)
    T = k_ref.shape[0]
    W = BAND_PAST + chq

    @pl.when(c == 0)
    def _():
        if has_hist:
            kbuf[0:BAND_PAST, :] = hk_ref[...].astype(BF16)
            vbuf[0:BAND_PAST, :] = hv_ref[...].astype(BF16)
        else:
            kbuf[0:BAND_PAST, :] = jnp.zeros((BAND_PAST, DB), BF16)
            vbuf[0:BAND_PAST, :] = jnp.zeros((BAND_PAST, DB), BF16)
        kbuf[BAND_PAST:BAND_PAST + T, :] = k_ref[...].astype(BF16)
        vbuf[BAND_PAST:BAND_PAST + T, :] = v_ref[...].astype(BF16)

    start = pl.multiple_of(c * chq, chq)
    kw = kbuf[pl.ds(start, W), :]
    vw = vbuf[pl.ds(start, W), :]
    q = q_ref[...].astype(BF16)
    if not has_hist:
        kpos = start - BAND_PAST + lax.broadcasted_iota(I32, (chq, W), 1)
        valid = kpos >= 0
    outs = []
    for h in range(HB):
        sl = slice(h * DHB, (h + 1) * DHB)
        s = _dot_nt(q[:, sl], kw[:, sl]) * (DHB ** -0.5) + bias_ref[h]
        if not has_hist:
            s = jnp.where(valid, s, -jnp.inf)
        m = jnp.max(s, axis=-1, keepdims=True)
        e = jnp.exp(s - m)
        p = e / jnp.sum(e, axis=-1, keepdims=True)
        outs.append(_dot(p.astype(BF16), vw[:, sl]))
    o_ref[...] = jnp.concatenate(outs, axis=-1).astype(BF16)


def _band_attention(proj, row0, B, T, chq, bias, hist_k=None, hist_v=None):
    nC = T // chq
    rbq = row0 // chq
    rbk = row0 // T
    has_hist = hist_k is not None
    in_specs = [pl.BlockSpec((chq, DB), lambda b, c: (rbq + b * nC + c, 2)),
                pl.BlockSpec((T, DB), lambda b, c: (rbk + b, 3)),
                pl.BlockSpec((T, DB), lambda b, c: (rbk + b, 4))]
    args = [proj, proj, proj]
    if has_hist:
        in_specs += [pl.BlockSpec((None, BAND_PAST, DB), lambda b, c: (b, 0, 0))] * 2
        args += [hist_k, hist_v]
    in_specs.append(pl.BlockSpec((HB, chq, BAND_PAST + chq), lambda b, c: (0, 0, 0)))
    args.append(bias)
    return pl.pallas_call(
        functools.partial(_attn_kernel, chq=chq, has_hist=has_hist),
        grid=(B, nC),
        in_specs=in_specs,
        out_specs=pl.BlockSpec((chq, DB), lambda b, c: (b * nC + c, 0)),
        out_shape=jax.ShapeDtypeStruct((B * T, DB), BF16),
        scratch_shapes=[pltpu.VMEM((BAND_PAST + T, DB), BF16), pltpu.VMEM((BAND_PAST + T, DB), BF16)],
        compiler_params=_cparams(("parallel", "arbitrary")),
        name="band_attention",
    )(*args)


def _rel_bias(table, chq):
    q_pos = BAND_PAST + jnp.arange(chq)
    k_pos = jnp.arange(BAND_PAST + chq)
    rel = jnp.clip(q_pos[:, None] - k_pos[None, :], -MAX_REL, MAX_REL) + MAX_REL
    return jnp.take(table, rel, axis=1)


def _outproj_kernel(*refs, n_in):
    ys, ws = refs[:n_in], refs[n_in:2 * n_in]
    x_ref, gt_ref, o_ref = refs[2 * n_in:]
    acc = _dot(ys[0][...], ws[0][...])
    for y_ref, w_ref in zip(ys[1:], ws[1:]):
        acc = acc + _dot(y_ref[...], w_ref[...])
    bB, pb, d = x_ref.shape
    o_ref[...] = x_ref[...] + gt_ref[...] * acc.reshape(bB, pb, d)


def _out_projection(ys, ws, x_all, mod_pb, gt_col):
    npb, pb, d = x_all.shape
    bB = 32
    tm = bB * pb
    n_in = len(ys)
    in_specs = [pl.BlockSpec((tm, y.shape[1]), lambda i: (i, 0)) for y in ys]
    in_specs += [pl.BlockSpec(w.shape, lambda i: (0, 0)) for w in ws]
    in_specs += [pl.BlockSpec((bB, pb, d), lambda i: (i, 0, 0)),
                 pl.BlockSpec((bB, 1, d), lambda i: (i, 0, gt_col))]
    return pl.pallas_call(
        functools.partial(_outproj_kernel, n_in=n_in),
        grid=(npb // bB,),
        in_specs=in_specs,
        out_specs=pl.BlockSpec((bB, pb, d), lambda i: (i, 0, 0)),
        out_shape=jax.ShapeDtypeStruct(x_all.shape, F32),
        compiler_params=_cparams(("parallel",)),
        name="out_projection",
    )(*ys, *ws, x_all, mod_pb)


def _gdnpre_kernel(qkv_ref, ab_ref, cb_ref, cw_ref, alog_ref, dtb_ref,
                   q_ref, k_ref, v_ref, gb_ref, cbo_ref, xbuf):
    tT = qkv_ref.shape[0]

    @pl.when(pl.program_id(1) == 0)
    def _():
        xbuf[0:SUBLANES, :] = jnp.zeros((SUBLANES, N_QKV), F32)
        xbuf[SUBLANES - (CONV_W - 1):SUBLANES, :] = cb_ref[...]

    x = qkv_ref[...]
    xbuf[SUBLANES:SUBLANES + tT, :] = x
    xc = cw_ref[3:4, :] * x
    for j in range(CONV_W - 1):
        xc = xc + cw_ref[j:j + 1, :] * xbuf[SUBLANES - 3 + j:SUBLANES - 3 + j + tT, :]
    tail = xbuf[tT:tT + SUBLANES, :]
    xbuf[0:SUBLANES, :] = tail
    cbo_ref[...] = tail[SUBLANES - (CONV_W - 1):, :]

    xs = xc * jax.nn.sigmoid(xc)
    for h in range(HC):
        qh = xs[:, h * DK:(h + 1) * DK]
        q_ref[:, h * DK:(h + 1) * DK] = (qh * lax.rsqrt(jnp.sum(qh * qh, axis=-1, keepdims=True) + NORM_EPS)
                                         * (DK ** -0.5))
        kh = xs[:, DCK + h * DK:DCK + (h + 1) * DK]
        k_ref[:, h * DK:(h + 1) * DK] = kh * lax.rsqrt(jnp.sum(kh * kh, axis=-1, keepdims=True) + NORM_EPS)
    v_ref[...] = xs[:, 2 * DCK:]

    ab = ab_ref[...]
    g = -jnp.exp(alog_ref[...]) * _softplus(ab + dtb_ref[...])
    beta = jax.nn.sigmoid(ab)
    lane = lax.broadcasted_iota(I32, ab.shape, 1)
    gb_ref[...] = jnp.where(lane < HC, g, beta)


def _gdn_pre(proj, row0, B, T, tT, conv_buf, cw, alog_pad, dtb_pad):
    nT = T // tT
    rb0 = row0 // tT
    return pl.pallas_call(
        _gdnpre_kernel,
        grid=(B, nT),
        in_specs=[pl.BlockSpec((tT, N_QKV), lambda b, t: (rb0 + b * nT + t, 0)),
                  pl.BlockSpec((tT, LANES), lambda b, t: (rb0 + b * nT + t, (N_QKV + DCV) // LANES)),
                  pl.BlockSpec((None, CONV_W - 1, N_QKV), lambda b, t: (b, 0, 0)),
                  pl.BlockSpec((CONV_W, N_QKV), lambda b, t: (0, 0)),
                  pl.BlockSpec((1, LANES), lambda b, t: (0, 0)),
                  pl.BlockSpec((1, LANES), lambda b, t: (0, 0))],
        out_specs=[pl.BlockSpec((tT, DCK), lambda b, t: (b * nT + t, 0)),
                   pl.BlockSpec((tT, DCK), lambda b, t: (b * nT + t, 0)),
                   pl.BlockSpec((tT, DCV), lambda b, t: (b * nT + t, 0)),
                   pl.BlockSpec((tT, LANES), lambda b, t: (b * nT + t, 0)),
                   pl.BlockSpec((None, CONV_W - 1, N_QKV), lambda b, t: (b, 0, 0))],
        out_shape=[jax.ShapeDtypeStruct((B * T, DCK), F32),
                   jax.ShapeDtypeStruct((B * T, DCK), F32),
                   jax.ShapeDtypeStruct((B * T, DCV), F32),
                   jax.ShapeDtypeStruct((B * T, LANES), F32),
                   jax.ShapeDtypeStruct((B, CONV_W - 1, N_QKV), F32)],
        scratch_shapes=[pltpu.VMEM((tT + SUBLANES, N_QKV), F32)],
        compiler_params=_cparams(("parallel", "arbitrary")),
        name="gdn_pre",
    )(proj, proj, conv_buf, cw, alog_pad, dtb_pad)


def _delta_kernel(q_ref, k_ref, v_ref, gb_ref, z_ref, s0_ref, ng_ref, o_ref, so_ref, s_scr, *, C, hps):
    T = q_ref.shape[0]
    n_chunks = T // C
    s_scr[...] = s0_ref[...]
    ri = lax.broadcasted_iota(I32, (C, C), 0)
    ci = lax.broadcasted_iota(I32, (C, C), 1)
    causal = ri >= ci
    strict = ri > ci
    eye = (ri == ci).astype(F32)
    tril = causal.astype(F32)
    ones_cc = jnp.ones((C, C), F32)
    head0 = pl.program_id(1) * hps

    def chunk(n, carry):
        r0 = pl.multiple_of(n * C, C)
        gbc = gb_ref[pl.ds(r0, C), :]
        lane = lax.broadcasted_iota(I32, gbc.shape, 1)
        for j in range(hps):
            cs = slice(j * DK, (j + 1) * DK)
            g = jnp.sum(jnp.where(lane == head0 + j, gbc, 0.0), axis=1, keepdims=True)
            beta = jnp.sum(jnp.where(lane == head0 + j + HC, gbc, 0.0), axis=1, keepdims=True)
            q = q_ref[pl.ds(r0, C), cs]
            k = k_ref[pl.ds(r0, C), cs]
            v = v_ref[pl.ds(r0, C), cs]
            gcol = _dot(tril, jnp.broadcast_to(g, (C, LANES)), HI)
            G = gcol[:, 0:1]
            grow = _dot(ones_cc, eye * gcol[:, :C], HI)
            L = jnp.where(causal, jnp.exp(jnp.where(causal, gcol[:, :C] - grow, 0.0)), 0.0)
            kb = k * beta
            kbf = k.astype(BF16)
            A = jnp.where(strict, _dot_nt(kb.astype(BF16), kbf) * L, 0.0)
            Bm = -A
            P = eye + Bm
            p2 = 2
            while p2 < C:
                Bm = _dot(Bm, Bm, HI)
                P = P + _dot(P, Bm, HI)
                p2 *= 2
            eG = jnp.exp(G)
            X = _dot(P, jnp.concatenate([v * beta, kb * eG], axis=1), HI)
            u = X[:, :DV]
            w = X[:, DV:]
            a_qk = _dot_nt(q.astype(BF16), kbf) * L
            g_last = gcol[C - 1:C, 0:1]
            q_dec = q * eG
            k_dec = k * jnp.exp(g_last - G)
            S = s_scr[j]
            Sb = S.astype(BF16)
            v_new = u - _dot(w.astype(BF16), Sb)
            vnb = v_new.astype(BF16)
            o = _dot(q_dec.astype(BF16), Sb) + _dot(a_qk.astype(BF16), vnb)
            s_scr[j] = S * jnp.exp(g_last) + _dot_tn(k_dec.astype(BF16), vnb)
            on = o * lax.rsqrt(jnp.mean(o * o, axis=-1, keepdims=True) + NORM_EPS) * ng_ref[...]
            z = z_ref[pl.ds(r0, C), cs]
            o_ref[pl.ds(r0, C), cs] = (on * (z * jax.nn.sigmoid(z))).astype(BF16)
        return carry

    lax.fori_loop(0, n_chunks, chunk, 0)
    so_ref[...] = s_scr[...]


def _delta_rule(q, k, v, gb, proj, row0, B, T, S0, norm_g, hps):
    C = min(DELTA_CHUNK, T)
    rb = row0 // T
    wb = hps * DK
    zcol0 = N_QKV // wb
    qkv_spec = lambda: pl.BlockSpec((T, wb), lambda b, h: (b, h))
    return pl.pallas_call(
        functools.partial(_delta_kernel, C=C, hps=hps),
        grid=(B, HC // hps),
        in_specs=[qkv_spec(), qkv_spec(), qkv_spec(),
                  pl.BlockSpec((T, LANES), lambda b, h: (b, 0)),
                  pl.BlockSpec((T, wb), lambda b, h: (rb + b, zcol0 + h)),
                  pl.BlockSpec((None, hps, DK, DV), lambda b, h: (b, h, 0, 0)),
                  pl.BlockSpec((1, DV), lambda b, h: (0, 0))],
        out_specs=[pl.BlockSpec((T, wb), lambda b, h: (b, h)),
                   pl.BlockSpec((None, hps, DK, DV), lambda b, h: (b, h, 0, 0))],
        out_shape=[jax.ShapeDtypeStruct((B * T, DCV), BF16),
                   jax.ShapeDtypeStruct((B, HC, DK, DV), F32)],
        scratch_shapes=[pltpu.VMEM((hps, DK, DV), F32)],
        compiler_params=_cparams(("parallel", "parallel")),
        name="delta_rule",
    )(q, k, v, gb, proj, S0, norm_g.reshape(1, DV))


def _first_argmax(x, idx, size):
    m = jnp.max(x, axis=0, keepdims=True)
    am = jnp.min(jnp.where(x == m, idx, size), axis=0, keepdims=True)
    return m, am


def _router_kernel(x_ref, g_ref, sc_ref, sh_ref, rwt_ref, rb_ref,
                   h_ref, e_ref, w_ref, r_ref, cnt_ref, run_scr):
    tn = h_ref.shape[0]
    gsz = N_EXPERTS // N_GROUPS

    @pl.when(pl.program_id(0) == 0)
    def _():
        run_scr[...] = jnp.zeros_like(run_scr)

    h = _norm_mod(x_ref[...], g_ref[...], sc_ref[...], sh_ref[...])
    h_ref[...] = h
    scores = jax.nn.sigmoid(_dot_nt(rwt_ref[...], h, HI))
    sel = scores + rb_ref[...]

    gidx = lax.broadcasted_iota(I32, (gsz, tn), 0)
    gs_rows = []
    for gi in range(N_GROUPS):
        blk = sel[gi * gsz:(gi + 1) * gsz, :]
        m1, a1 = _first_argmax(blk, gidx, gsz)
        m2 = jnp.max(jnp.where(gidx == a1, -jnp.inf, blk), axis=0, keepdims=True)
        gs_rows.append(m1 + m2)
    gscore = jnp.concatenate(gs_rows, axis=0)
    gi8 = lax.broadcasted_iota(I32, (N_GROUPS, tn), 0)
    gmask = jnp.zeros((N_GROUPS, tn), jnp.bool_)
    gwork = gscore
    for _ in range(TOPK_GROUPS):
        _, a = _first_argmax(gwork, gi8, N_GROUPS)
        hit = gi8 == a
        gmask = jnp.logical_or(gmask, hit)
        gwork = jnp.where(hit, -jnp.inf, gwork)
    emask = jnp.concatenate([jnp.broadcast_to(gmask[gi:gi + 1, :], (gsz, tn)) for gi in range(N_GROUPS)], axis=0)

    eidx = lax.broadcasted_iota(I32, (N_EXPERTS, tn), 0)
    work = jnp.where(emask, sel, -jnp.inf)
    hits = []
    chosen = jnp.zeros((N_EXPERTS, tn), jnp.bool_)
    for _ in range(TOP_K):
        _, a = _first_argmax(work, eidx, N_EXPERTS)
        hit = eidx == a
        hits.append((a, hit))
        chosen = jnp.logical_or(chosen, hit)
        work = jnp.where(hit, -jnp.inf, work)
    chosen_f = chosen.astype(F32)
    denom = jnp.sum(scores * chosen_f, axis=0, keepdims=True)

    ti = lax.broadcasted_iota(I32, (tn, tn), 0)
    tj = lax.broadcasted_iota(I32, (tn, tn), 1)
    before = (ti < tj).astype(BF16)
    rank = _dot(chosen_f.astype(BF16), before) + run_scr[...]
    run_scr[...] = run_scr[...] + jnp.sum(chosen_f, axis=1, keepdims=True)
    cnt_ref[...] = jnp.broadcast_to(run_scr[...], cnt_ref.shape).astype(I32)

    e_rows, w_rows, r_rows = [], [], []
    for a, hit in hits:
        hf = hit.astype(F32)
        e_rows.append(a)
        w_rows.append(jnp.sum(scores * hf, axis=0, keepdims=True) / denom * ROUTE_SCALE)
        r_rows.append(jnp.sum(rank * hf, axis=0, keepdims=True))
    e_ref[...] = jnp.concatenate(e_rows, axis=0)
    w_ref[...] = jnp.concatenate(w_rows, axis=0)
    r_ref[...] = jnp.concatenate(r_rows, axis=0).astype(I32)


def _router(x_all, g, mod_pb, sc_col, sh_col, rwt, rb):
    npb, pb, d = x_all.shape
    n = npb * pb
    bB = 16
    tn = bB * pb
    return pl.pallas_call(
        _router_kernel,
        grid=(npb // bB,),
        in_specs=[pl.BlockSpec((bB, pb, d), lambda i: (i, 0, 0)),
                  pl.BlockSpec((1, d), lambda i: (0, 0)),
                  pl.BlockSpec((bB, 1, d), lambda i: (i, 0, sc_col)),
                  pl.BlockSpec((bB, 1, d), lambda i: (i, 0, sh_col)),
                  pl.BlockSpec((N_EXPERTS, d), lambda i: (0, 0)),
                  pl.BlockSpec((N_EXPERTS, 1), lambda i: (0, 0))],
        out_specs=[pl.BlockSpec((tn, d), lambda i: (i, 0)),
                   pl.BlockSpec((TOP_K, tn), lambda i: (0, i)),
                   pl.BlockSpec((TOP_K, tn), lambda i: (0, i)),
                   pl.BlockSpec((TOP_K, tn), lambda i: (0, i)),
                   pl.BlockSpec((N_EXPERTS, LANES), lambda i: (0, 0))],
        out_shape=[jax.ShapeDtypeStruct((n, d), F32),
                   jax.ShapeDtypeStruct((TOP_K, n), I32),
                   jax.ShapeDtypeStruct((TOP_K, n), F32),
                   jax.ShapeDtypeStruct((TOP_K, n), I32),
                   jax.ShapeDtypeStruct((N_EXPERTS, LANES), I32)],
        scratch_shapes=[pltpu.VMEM((N_EXPERTS, 1), F32)],
        compiler_params=_cparams(("arbitrary",)),
        name="moe_router",
    )(x_all, g.reshape(1, d), mod_pb, mod_pb, rwt, rb.reshape(N_EXPERTS, 1))


def _row_slab(ref, r):
    return ref.at[pl.ds(pl.multiple_of(r * SUBLANES, SUBLANES), SUBLANES), :]


def _to_row_tiles(dst_ref, x, row0=0):
    rows = x.shape[0]
    for c in range(SUBLANES):
        dst_ref[pl.ds(row0 * SUBLANES + c, rows, stride=SUBLANES), :] = x[:, c * LANES:(c + 1) * LANES]


def _lane_tile(src_ref, c, rows, row0=0):
    return src_ref[pl.ds(row0 * SUBLANES + c, rows, stride=SUBLANES), :]


def _from_row_tiles(src_ref, rows):
    return jnp.concatenate([_lane_tile(src_ref, c, rows) for c in range(SUBLANES)], axis=1)


def _dispatch_kernel(zf_ref, pos_ref, h_ref, xs_hbm, sbuf, zbuf, sem, zsem, *, tm):
    i = pl.program_id(0)
    n_steps = pl.num_programs(0)
    td = h_ref.shape[0]
    n_tiles = zf_ref.shape[0]
    slot = lax.rem(i, 2)

    @pl.when(i == 0)
    def _():
        zbuf[...] = jnp.zeros_like(zbuf)

        def zero_copy(t):
            r0 = pl.multiple_of(t * (tm * SUBLANES), tm * SUBLANES)
            return pltpu.make_async_copy(zbuf, xs_hbm.at[pl.ds(r0, tm * SUBLANES), :], zsem)

        def start(t, carry):
            @pl.when(zf_ref[t] != 0)
            def _():
                zero_copy(t).start()
            return carry

        def wait(t, carry):
            @pl.when(zf_ref[t] != 0)
            def _():
                zero_copy(t).wait()
            return carry

        lax.fori_loop(0, n_tiles, start, 0)
        lax.fori_loop(0, n_tiles, wait, 0)

    def drain(s):
        for _ in range(TOP_K):
            pltpu.make_async_copy(sbuf.at[s], xs_hbm.at[pl.ds(0, td * SUBLANES), :], sem.at[s]).wait()

    @pl.when(i >= 2)
    def _():
        drain(slot)

    _to_row_tiles(sbuf.at[slot], h_ref[...])

    def body(j, carry):
        t0 = pl.multiple_of(j * SUBLANES, SUBLANES)
        for u in range(SUBLANES):
            src = _row_slab(sbuf.at[slot], t0 + u)
            for k in range(TOP_K):
                p = pos_ref[0, k * td + t0 + u]
                pltpu.make_async_copy(src, _row_slab(xs_hbm, p), sem.at[slot]).start()
        return carry

    lax.fori_loop(0, td // SUBLANES, body, 0)

    @pl.when(i == n_steps - 1)
    def _():
        drain(slot)
        drain(1 - slot)


def _moe_dispatch(h2, pos_tiles, zero_flag, n_tiles, tm, td):
    n, d = h2.shape
    grid_spec = pltpu.PrefetchScalarGridSpec(
        num_scalar_prefetch=1,
        grid=(n // td,),
        in_specs=[pl.BlockSpec((None, 1, TOP_K * td), lambda i, zf: (i, 0, 0), memory_space=pltpu.SMEM),
                  pl.BlockSpec((td, d), lambda i, zf: (i, 0))],
        out_specs=pl.BlockSpec(memory_space=pl.ANY),
        scratch_shapes=[pltpu.VMEM((2, td * SUBLANES, LANES), F32), pltpu.VMEM((tm * SUBLANES, LANES), F32),
                        pltpu.SemaphoreType.DMA((2,)), pltpu.SemaphoreType.DMA(())],
    )
    return pl.pallas_call(
        functools.partial(_dispatch_kernel, tm=tm),
        grid_spec=grid_spec,
        out_shape=jax.ShapeDtypeStruct((n_tiles * tm * SUBLANES, LANES), F32),
        compiler_params=_cparams(("arbitrary",)),
        name="moe_dispatch",
    )(zero_flag, pos_tiles, h2)


def _experts_kernel(te_ref, nu_ref, xs_ref, wgu_ref, wd_ref, ys_ref):
    i = pl.program_id(0)
    nu = nu_ref[0]

    @pl.when(i < nu)
    def _():
        x = _from_row_tiles(xs_ref, xs_ref.shape[0] // SUBLANES).astype(BF16)
        gu = _dot(x, wgu_ref[0])
        hid = jax.nn.silu(gu[:, :D_EXPERT]) * gu[:, D_EXPERT:]
        _to_row_tiles(ys_ref, _dot(hid.astype(BF16), wd_ref[0]))

    @pl.when(i >= nu)
    def _():
        ys_ref[...] = jnp.zeros_like(ys_ref)


def _routed_experts(xs, tile_expert, n_used, wgu, wd, tm):
    n_tiles = xs.shape[0] // (tm * SUBLANES)
    d = wgu.shape[1]
    grid_spec = pltpu.PrefetchScalarGridSpec(
        num_scalar_prefetch=2,
        grid=(n_tiles,),
        in_specs=[pl.BlockSpec((tm * SUBLANES, LANES), lambda i, te, nu: (jnp.minimum(i, nu[0] - 1), 0)),
                  pl.BlockSpec((1, d, 2 * D_EXPERT), lambda i, te, nu: (te[i], 0, 0)),
                  pl.BlockSpec((1, D_EXPERT, d), lambda i, te, nu: (te[i], 0, 0))],
        out_specs=pl.BlockSpec((tm * SUBLANES, LANES), lambda i, te, nu: (i, 0)),
    )
    return pl.pallas_call(
        _experts_kernel,
        grid_spec=grid_spec,
        out_shape=jax.ShapeDtypeStruct(xs.shape, F32),
        compiler_params=_cparams(("arbitrary",)),
        name="moe_experts",
    )(tile_expert, n_used, xs, wgu, wd)


def _gather_rows(idx_ref, n_rows, src_hbm, dst, sem):
    def body(j, carry):
        r0 = pl.multiple_of(j * SUBLANES, SUBLANES)
        for u in range(SUBLANES):
            pltpu.make_async_copy(_row_slab(src_hbm, idx_ref[0, r0 + u]), _row_slab(dst, r0 + u), sem).start()
        return carry
    lax.fori_loop(0, n_rows // SUBLANES, body, 0)


def _combine_kernel(pos_ref, nxt_ref, ys_hbm, w_ref, h_ref, x_ref, gt_ref, sgu_ref, sd_ref, o_ref, gbuf, sem):
    i = pl.program_id(0)
    tb = w_ref.shape[0]
    slot = lax.rem(i, 2)

    @pl.when(i == 0)
    def _():
        _gather_rows(pos_ref, TOP_K * tb, ys_hbm, gbuf.at[0], sem.at[0])

    @pl.when(i + 1 < pl.num_programs(0))
    def _():
        _gather_rows(nxt_ref, TOP_K * tb, ys_hbm, gbuf.at[1 - slot], sem.at[1 - slot])

    hb = h_ref[...].astype(BF16)
    gu = _dot(hb, sgu_ref[...])
    hid = jax.nn.silu(gu[:, :D_SHARED]) * gu[:, D_SHARED:]
    shared = _dot(hid.astype(BF16), sd_ref[...])
    pltpu.make_async_copy(ys_hbm.at[pl.ds(0, TOP_K * tb * SUBLANES), :], gbuf.at[slot], sem.at[slot]).wait()
    w = w_ref[...]
    wk = [jnp.broadcast_to(w[:, k:k + 1], (tb, LANES)) for k in range(TOP_K)]
    gs = gbuf.at[slot]
    cols = []
    for c in range(SUBLANES):
        a = wk[0] * _lane_tile(gs, c, tb)
        for k in range(1, TOP_K):
            a = a + wk[k] * _lane_tile(gs, c, tb, k * tb)
        cols.append(a)
    acc = shared + jnp.concatenate(cols, axis=1)
    bB, pb, d = x_ref.shape
    o_ref[...] = x_ref[...] + gt_ref[...] * acc.reshape(bB, pb, d)


def _moe_combine(ys, pos_tiles, w_tok, h2, x_all, mod_pb, gt_col, sgu, sd, tb):
    npb, pb, d = x_all.shape
    n_tb = pos_tiles.shape[0]
    bB = tb // pb
    last = n_tb - 1
    return pl.pallas_call(
        _combine_kernel,
        grid=(n_tb,),
        in_specs=[pl.BlockSpec((None, 1, TOP_K * tb), lambda i: (i, 0, 0), memory_space=pltpu.SMEM),
                  pl.BlockSpec((None, 1, TOP_K * tb), lambda i: (jnp.minimum(i + 1, last), 0, 0),
                               memory_space=pltpu.SMEM),
                  pl.BlockSpec(memory_space=pl.ANY),
                  pl.BlockSpec((tb, TOP_K), lambda i: (i, 0)),
                  pl.BlockSpec((tb, d), lambda i: (i, 0)),
                  pl.BlockSpec((bB, pb, d), lambda i: (i, 0, 0)),
                  pl.BlockSpec((bB, 1, d), lambda i: (i, 0, gt_col)),
                  pl.BlockSpec(sgu.shape, lambda i: (0, 0)),
                  pl.BlockSpec(sd.shape, lambda i: (0, 0))],
        out_specs=pl.BlockSpec((bB, pb, d), lambda i: (i, 0, 0)),
        out_shape=jax.ShapeDtypeStruct(x_all.shape, F32),
        scratch_shapes=[pltpu.VMEM((2, TOP_K * tb * SUBLANES, LANES), F32), pltpu.SemaphoreType.DMA((2,))],
        compiler_params=_cparams(("arbitrary",)),
        name="moe_combine",
    )(pos_tiles, pos_tiles, ys, w_tok, h2, x_all, mod_pb, sgu, sd)


def _moe(x_all, g, mod_pb, rw, rb, wg, wu, wd, sg, su, sd):
    npb, pb, d = x_all.shape
    n = npb * pb
    tm = 256
    tb = 128
    h2, e_t, w_t, r_t, cnt = _router(x_all, g, mod_pb, 4, 3, rw.T, rb)

    counts = cnt[:, 0]
    padded = (counts + tm - 1) // tm * tm
    ends = jnp.cumsum(padded)
    starts = ends - padded
    n_tiles = (n * TOP_K + N_EXPERTS * (tm - 1)) // tm
    onehot = e_t[:, :, None] == jnp.arange(N_EXPERTS, dtype=I32)
    pos = jnp.sum(jnp.where(onehot, starts, 0), axis=-1) + r_t
    n_used = (ends[-1] // tm).astype(I32)
    tile_idx = jnp.arange(n_tiles, dtype=I32)
    tile_start = jnp.minimum(tile_idx, n_used - 1) * tm
    tile_expert = jnp.sum(ends[None, :] <= tile_start[:, None], axis=1).astype(I32)
    is_seg_end = jnp.any((ends[None, :] == (tile_idx[:, None] + 1) * tm) & (padded[None, :] > 0), axis=1)
    zero_flag = jnp.logical_or(is_seg_end, tile_idx >= n_used).astype(I32)
    pos_tiles = pos.reshape(TOP_K, n // tb, tb).transpose(1, 0, 2).reshape(n // tb, 1, TOP_K * tb)

    xs = _moe_dispatch(h2, pos_tiles, zero_flag, n_tiles, tm, tb)
    wgu = jnp.concatenate([wg, wu], axis=-1).astype(BF16)
    ys = _routed_experts(xs, tile_expert, n_used.reshape(1), wgu, wd.astype(BF16), tm)
    sgu = jnp.concatenate([sg, su], axis=-1).astype(BF16)
    return _moe_combine(ys, pos_tiles, w_t.T, h2, x_all, mod_pb, 5, sgu, sd.astype(BF16), tb)


def _final_kernel(x_ref, g_ref, o_ref):
    x = x_ref[...]
    ms = jnp.mean(x * x, axis=-1, keepdims=True)
    o_ref[...] = x * lax.rsqrt(ms + NORM_EPS) * g_ref[...]


def _final_norm(x_all, g, pb0, n_pb):
    npb, pb, d = x_all.shape
    bB = 32
    return pl.pallas_call(
        _final_kernel,
        grid=(n_pb // bB,),
        in_specs=[pl.BlockSpec((bB, pb, d), lambda i: (pb0 // bB + i, 0, 0)),
                  pl.BlockSpec((1, d), lambda i: (0, 0))],
        out_specs=pl.BlockSpec((bB, pb, d), lambda i: (i, 0, 0)),
        out_shape=jax.ShapeDtypeStruct((n_pb, pb, d), F32),
        compiler_params=_cparams(("parallel",)),
        name="final_norm",
    )(x_all, g.reshape(1, d))


def _block_diag(w):
    nb, di, do = w.shape
    return jnp.einsum("nde,nm->ndme", w, jnp.eye(nb, dtype=w.dtype)).reshape(nb * di, nb * do)


def kernel(x_prompt, x_sample, c_prompt, c_sample, cache_b_k, cache_b_v, state_a_conv, state_a_h, state_c_conv, state_c_S, norm1_g, norm2_g, final_g, w_mod, b_mod, w_in0, w_out0, a_conv_w, a_conv_b, a_w_r, a_b_r, a_w_i, a_b_i, a_lambda, b_rel_bias, w_in1, w_out1, c_conv_w, c_A_log, c_dt_bias, c_norm_g, router_w, router_bias, e_w_gate, e_w_up, e_w_down, s_w_gate, s_w_up, s_w_down):
    Bp, Tp, d = x_prompt.shape
    Bs, Ts, _ = x_sample.shape
    np_tok, ns_tok = Bp * Tp, Bs * Ts
    npb_p, npb_s = np_tok // PB, ns_tok // PB
    depth = w_mod.shape[0]

    x_all = jnp.concatenate([x_prompt.reshape(npb_p, PB, d), x_sample.reshape(npb_s, PB, d)], axis=0)
    mod = _modulation(jnp.concatenate([c_prompt, c_sample], axis=0), w_mod, b_mod)

    outs = {}
    for l in range(depth):
        mod_pb = jnp.concatenate([jnp.repeat(mod[l, :Bp], Tp // PB, axis=0),
                                  jnp.repeat(mod[l, Bp:], Ts // PB, axis=0)], axis=0)[:, None, :]
        j = l // 2
        if l % 2 == 0:
            proj = _in_projection(x_all, norm1_g[l], mod_pb, 1, 0, w_in0[j].astype(BF16), 1280)
            wr, wi = _block_diag(a_w_r[j]).astype(BF16), _block_diag(a_w_i[j]).astype(BF16)
            a_args = (a_conv_w[j], a_conv_b[j], wr, a_b_r[j], wi, a_b_i[j], a_lambda[j])
            ya_p, ac_p, ah_p = _rglru(proj, 0, Bp, Tp, 512, jnp.zeros((Bp, CONV_W - 1, DA), F32),
                                      jnp.zeros((Bp, DA), F32), *a_args)
            ya_s, ac_s, ah_s = _rglru(proj, np_tok, Bs, Ts, Ts, state_a_conv[j], state_a_h[j], *a_args)
            yb_p = _band_attention(proj, 0, Bp, Tp, CHUNK, _rel_bias(b_rel_bias[j], CHUNK))
            yb_s = _band_attention(proj, np_tok, Bs, Ts, Ts, _rel_bias(b_rel_bias[j], Ts),
                                   cache_b_k[j].reshape(Bs, BAND_PAST, DB), cache_b_v[j].reshape(Bs, BAND_PAST, DB))
            ya = jnp.concatenate([ya_p, ya_s], axis=0)
            yb = jnp.concatenate([yb_p, yb_s], axis=0)
            w_out = w_out0[j].astype(BF16)
            x_all = _out_projection([ya, yb], [w_out[:DA], w_out[DA:]], x_all, mod_pb, 2)
            keep = min(BAND_PAST, Tp)
            kv_p = proj[:np_tok].reshape(Bp, Tp, IN0)[:, Tp - keep:, 2 * DA + DB:]
            kv_s = proj[np_tok:].reshape(Bs, Ts, IN0)[:, :, 2 * DA + DB:]
            outs.setdefault("bk_p", []).append(kv_p[..., :DB].reshape(Bp, keep, HB, DHB))
            outs.setdefault("bv_p", []).append(kv_p[..., DB:].reshape(Bp, keep, HB, DHB))
            outs.setdefault("bk_s", []).append(kv_s[..., :DB].reshape(Bs, Ts, HB, DHB))
            outs.setdefault("bv_s", []).append(kv_s[..., DB:].reshape(Bs, Ts, HB, DHB))
            outs.setdefault("ac_p", []).append(ac_p)
            outs.setdefault("ac_s", []).append(ac_s)
            outs.setdefault("ah_p", []).append(ah_p.reshape(Bp, DA))
            outs.setdefault("ah_s", []).append(ah_s.reshape(Bs, DA))
        else:
            w_in = jnp.pad(w_in1[j], ((0, 0), (0, IN1_PAD - IN1))).astype(BF16)
            proj = _in_projection(x_all, norm1_g[l], mod_pb, 1, 0, w_in, 1408)
            alog = jnp.pad(c_A_log[j], (0, LANES - HC)).reshape(1, LANES)
            dtb = jnp.pad(c_dt_bias[j], (0, LANES - HC)).reshape(1, LANES)
            q_p, k_p, v_p, gb_p, cc_p = _gdn_pre(proj, 0, Bp, Tp, 256, jnp.zeros((Bp, CONV_W - 1, N_QKV), F32),
                                                 c_conv_w[j], alog, dtb)
            q_s, k_s, v_s, gb_s, cc_s = _gdn_pre(proj, np_tok, Bs, Ts, Ts, state_c_conv[j], c_conv_w[j], alog, dtb)
            o_p, cs_p = _delta_rule(q_p, k_p, v_p, gb_p, proj, 0, Bp, Tp, jnp.zeros((Bp, HC, DK, DV), F32),
                                    c_norm_g[j], 2)
            o_s, cs_s = _delta_rule(q_s, k_s, v_s, gb_s, proj, np_tok, Bs, Ts, state_c_S[j], c_norm_g[j], 2)
            o = jnp.concatenate([o_p, o_s], axis=0)
            x_all = _out_projection([o], [w_out1[j].astype(BF16)], x_all, mod_pb, 2)
            outs.setdefault("cc_p", []).append(cc_p)
            outs.setdefault("cc_s", []).append(cc_s)
            outs.setdefault("cs_p", []).append(cs_p)
            outs.setdefault("cs_s", []).append(cs_s)
        x_all = _moe(x_all, norm2_g[l], mod_pb, router_w[l], router_bias[l], e_w_gate[l], e_w_up[l],
                     e_w_down[l], s_w_gate[l], s_w_up[l], s_w_down[l])

    y_prompt = _final_norm(x_all, final_g, 0, npb_p).reshape(Bp, Tp, d)
    y_sample = _final_norm(x_all, final_g, npb_p, npb_s).reshape(Bs, Ts, d)
    st = {k: jnp.stack(v) for k, v in outs.items()}
    return (y_prompt, y_sample, st["bk_p"], st["bk_s"], st["bv_p"], st["bv_s"], st["ac_p"], st["ac_s"],
            st["ah_p"], st["ah_s"], st["cc_p"], st["cc_s"], st["cs_p"], st["cs_s"])
```

```python
import functools

import jax
import jax.numpy as jnp
from jax import lax
from jax.experimental import pallas as pl
from jax.experimental.pallas import tpu as pltpu

F32 = jnp.float32
BF16 = jnp.bfloat16
I32 = jnp.int32

D_MODEL = 1024
CHUNK = 64
NORM_EPS = 1e-6
CONV_W = 4
DA = 512
NB_A = 8
LRU_C = 8.0
HB = 8
DHB = 64
DB = HB * DHB
BAND_PAST = 8 * CHUNK
MAX_REL = 128
HC = 8
DK = 128
DV = 128
DCK = HC * DK
DCV = HC * DV
N_QKV = 2 * DCK + DCV
DELTA_BLOCK = 128
N_EXPERTS = 64
TOP_K = 8
N_GROUPS = 8
TOPK_GROUPS = 4
D_EXPERT = 256
D_SHARED = 256
ROUTE_SCALE = 2.5
IN0 = 2 * DA + 3 * DB
IN1 = N_QKV + DCV + 2 * HC

LANES = 128
SUBLANES = 8
PB = 32
IN1_PAD = N_QKV + DCV + LANES
VMEM_LIMIT = 48 * 1024 * 1024
HI = lax.Precision.HIGHEST


def _cparams(sem):
    return pltpu.CompilerParams(dimension_semantics=sem, vmem_limit_bytes=VMEM_LIMIT)


def _softplus(x):
    return jnp.maximum(x, 0.0) + jnp.log1p(jnp.exp(-jnp.abs(x)))


def _dot(a, b, precision=None):
    return jnp.dot(a, b, preferred_element_type=F32, precision=precision)


def _dot_nt(a, b, precision=None):
    return lax.dot_general(a, b, (((1,), (1,)), ((), ())), preferred_element_type=F32, precision=precision)


def _dot_tn(a, b, precision=None):
    return lax.dot_general(a, b, (((0,), (0,)), ((), ())), preferred_element_type=F32, precision=precision)


def _mod_kernel(c_ref, w_ref, b_ref, o_ref):
    c = c_ref[...]
    cond = c * jax.nn.sigmoid(c)
    o_ref[0] = _dot(cond.astype(BF16), w_ref[0].astype(BF16)) + b_ref[0]


def _modulation(c_all, w_mod, b_mod):
    depth, d, n6 = w_mod.shape
    nb = c_all.shape[0]
    tn = 1536
    return pl.pallas_call(
        _mod_kernel,
        grid=(depth, n6 // tn),
        in_specs=[pl.BlockSpec((nb, d), lambda l, j: (0, 0)),
                  pl.BlockSpec((1, d, tn), lambda l, j: (l, 0, j)),
                  pl.BlockSpec((1, 1, tn), lambda l, j: (l, 0, j))],
        out_specs=pl.BlockSpec((1, nb, tn), lambda l, j: (l, 0, j)),
        out_shape=jax.ShapeDtypeStruct((depth, nb, n6), F32),
        compiler_params=_cparams(("parallel", "parallel")),
        name="modulation",
    )(c_all, w_mod, b_mod.reshape(depth, 1, n6))


def _norm_mod(x3, g, sc, sh):
    ms = jnp.mean(x3 * x3, axis=-1, keepdims=True)
    y = x3 * lax.rsqrt(ms + NORM_EPS) * g
    h = y * (1.0 + sc) + sh
    return h.reshape(x3.shape[0] * x3.shape[1], x3.shape[2])


def _inproj_kernel(x_ref, g_ref, sc_ref, sh_ref, w_ref, o_ref, h_scr):
    @pl.when(pl.program_id(1) == 0)
    def _():
        h_scr[...] = _norm_mod(x_ref[...], g_ref[...], sc_ref[...], sh_ref[...]).astype(BF16)
    o_ref[...] = _dot(h_scr[...], w_ref[...])


def _in_projection(x_all, g, mod_pb, sc_col, sh_col, w_bf16, tn):
    npb, pb, d = x_all.shape
    nout = w_bf16.shape[1]
    bB = 32
    tm = bB * pb
    return pl.pallas_call(
        _inproj_kernel,
        grid=(npb // bB, nout // tn),
        in_specs=[pl.BlockSpec((bB, pb, d), lambda i, j: (i, 0, 0)),
                  pl.BlockSpec((1, d), lambda i, j: (0, 0)),
                  pl.BlockSpec((bB, 1, d), lambda i, j: (i, 0, sc_col)),
                  pl.BlockSpec((bB, 1, d), lambda i, j: (i, 0, sh_col)),
                  pl.BlockSpec((d, tn), lambda i, j: (0, j))],
        out_specs=pl.BlockSpec((tm, tn), lambda i, j: (i, j)),
        out_shape=jax.ShapeDtypeStruct((npb * pb, nout), F32),
        scratch_shapes=[pltpu.VMEM((tm, d), BF16)],
        compiler_params=_cparams(("parallel", "arbitrary")),
        name="in_projection",
    )(x_all, g.reshape(1, d), mod_pb, mod_pb, w_bf16)


def _rglru_kernel(xa_ref, ga_ref, cb_ref, h0_ref, cw_ref, cbias_ref, wr_ref, br_ref, wi_ref, bi_ref, lam_ref,
                  y_ref, cbo_ref, ho_ref, xbuf, hcar):
    tT = xa_ref.shape[0]

    @pl.when(pl.program_id(1) == 0)
    def _():
        xbuf[0:SUBLANES, :] = jnp.zeros((SUBLANES, DA), F32)
        xbuf[SUBLANES - (CONV_W - 1):SUBLANES, :] = cb_ref[...]
        hcar[...] = h0_ref[...]

    xa = xa_ref[...]
    xbuf[SUBLANES:SUBLANES + tT, :] = xa
    xc = cw_ref[3:4, :] * xa + cbias_ref[...]
    for j in range(CONV_W - 1):
        xc = xc + cw_ref[j:j + 1, :] * xbuf[SUBLANES - 3 + j:SUBLANES - 3 + j + tT, :]
    tail = xbuf[tT:tT + SUBLANES, :]
    xbuf[0:SUBLANES, :] = tail
    cbo_ref[...] = tail[SUBLANES - (CONV_W - 1):, :]

    xcb = xc.astype(BF16)
    r = jax.nn.sigmoid(_dot(xcb, wr_ref[...]) + br_ref[...])
    i = jax.nn.sigmoid(_dot(xcb, wi_ref[...]) + bi_ref[...])
    log_a = -LRU_C * r * _softplus(-lam_ref[...])
    a = jnp.exp(log_a)
    th = jnp.tanh(log_a)
    u = jnp.sqrt(-2.0 * th / (1.0 - th)) * (i * xc)

    rows = lax.broadcasted_iota(I32, (tT, DA), 0)
    s = 1
    while s < tT:
        a_sh = pltpu.roll(a, s, 0)
        u_sh = pltpu.roll(u, s, 0)
        m = rows >= s
        u = jnp.where(m, a * u_sh + u, u)
        a = jnp.where(m, a * a_sh, a)
        s *= 2
    h = a * hcar[...] + u
    hlast = h[tT - 1:tT, :]
    hcar[...] = hlast
    ho_ref[...] = hlast
    y_ref[...] = (h * jax.nn.gelu(ga_ref[...])).astype(BF16)


def _rglru(proj, row0, B, T, tT, conv_buf, h0, cw, cbias, wr, br, wi, bi, lam):
    nT = T // tT
    rb0 = row0 // tT
    vec = lambda: pl.BlockSpec((1, DA), lambda b, t: (0, 0))
    return pl.pallas_call(
        _rglru_kernel,
        grid=(B, nT),
        in_specs=[pl.BlockSpec((tT, DA), lambda b, t: (rb0 + b * nT + t, 0)),
                  pl.BlockSpec((tT, DA), lambda b, t: (rb0 + b * nT + t, 1)),
                  pl.BlockSpec((None, CONV_W - 1, DA), lambda b, t: (b, 0, 0)),
                  pl.BlockSpec((None, 1, DA), lambda b, t: (b, 0, 0)),
                  pl.BlockSpec((CONV_W, DA), lambda b, t: (0, 0)),
                  vec(),
                  pl.BlockSpec((DA, DA), lambda b, t: (0, 0)),
                  vec(),
                  pl.BlockSpec((DA, DA), lambda b, t: (0, 0)),
                  vec(), vec()],
        out_specs=[pl.BlockSpec((tT, DA), lambda b, t: (b * nT + t, 0)),
                   pl.BlockSpec((None, CONV_W - 1, DA), lambda b, t: (b, 0, 0)),
                   pl.BlockSpec((None, 1, DA), lambda b, t: (b, 0, 0))],
        out_shape=[jax.ShapeDtypeStruct((B * T, DA), BF16),
                   jax.ShapeDtypeStruct((B, CONV_W - 1, DA), F32),
                   jax.ShapeDtypeStruct((B, 1, DA), F32)],
        scratch_shapes=[pltpu.VMEM((tT + SUBLANES, DA), F32), pltpu.VMEM((1, DA), F32)],
        compiler_params=_cparams(("parallel", "arbitrary")),
        name="rglru",
    )(proj, proj, conv_buf, h0.reshape(B, 1, DA), cw, cbias.reshape(1, DA), wr, br.reshape(1, DA),
      wi, bi.reshape(1, DA), lam.reshape(1, DA))


def _attn_kernel(*refs, chq, has_hist):
    if has_hist:
        q_ref, k_ref, v_ref, hk_ref, hv_ref, bias_ref, o_ref, kbuf, vbuf = refs
    else:
        q_ref, k_ref, v_ref, bias_ref, o_ref, kbuf, vbuf = refs
    c = pl.program_id(1)
    T = k_ref.shape[0]
    W = BAND_PAST + chq

    @pl.when(c == 0)
    def _():
        if has_hist:
            kbuf[0:BAND_PAST, :] = hk_ref[...].astype(BF16)
            vbuf[0:BAND_PAST, :] = hv_ref[...].astype(BF16)
        else:
            kbuf[0:BAND_PAST, :] = jnp.zeros((BAND_PAST, DB), BF16)
            vbuf[0:BAND_PAST, :] = jnp.zeros((BAND_PAST, DB), BF16)
        kbuf[BAND_PAST:BAND_PAST + T, :] = k_ref[...].astype(BF16)
        vbuf[BAND_PAST:BAND_PAST + T, :] = v_ref[...].astype(BF16)

    start = pl.multiple_of(c * chq, chq)
    kw = kbuf[pl.ds(start, W), :]
    vw = vbuf[pl.ds(start, W), :]
    q = q_ref[...].astype(BF16)
    if not has_hist:
        kpos = start - BAND_PAST + lax.broadcasted_iota(I32, (chq, W), 1)
        valid = kpos >= 0
    outs = []
    for h in range(HB):
        sl = slice(h * DHB, (h + 1) * DHB)
        s = _dot_nt(q[:, sl], kw[:, sl]) * (DHB ** -0.5) + bias_ref[h]
        if not has_hist:
            s = jnp.where(valid, s, -jnp.inf)
        m = jnp.max(s, axis=-1, keepdims=True)
        e = jnp.exp(s - m)
        p = e / jnp.sum(e, axis=-1, keepdims=True)
        outs.append(_dot(p.astype(BF16), vw[:, sl]))
    o_ref[...] = jnp.concatenate(outs, axis=-1).astype(BF16)


def _band_attention(proj, row0, B, T, chq, bias, hist_k=None, hist_v=None):
    nC = T // chq
    rbq = row0 // chq
    rbk = row0 // T
    has_hist = hist_k is not None
    in_specs = [pl.BlockSpec((chq, DB), lambda b, c: (rbq + b * nC + c, 2)),
                pl.BlockSpec((T, DB), lambda b, c: (rbk + b, 3)),
                pl.BlockSpec((T, DB), lambda b, c: (rbk + b, 4))]
    args = [proj, proj, proj]
    if has_hist:
        in_specs += [pl.BlockSpec((None, BAND_PAST, DB), lambda b, c: (b, 0, 0))] * 2
        args += [hist_k, hist_v]
    in_specs.append(pl.BlockSpec((HB, chq, BAND_PAST + chq), lambda b, c: (0, 0, 0)))
    args.append(bias)
    return pl.pallas_call(
        functools.partial(_attn_kernel, chq=chq, has_hist=has_hist),
        grid=(B, nC),
        in_specs=in_specs,
        out_specs=pl.BlockSpec((chq, DB), lambda b, c: (b * nC + c, 0)),
        out_shape=jax.ShapeDtypeStruct((B * T, DB), BF16),
        scratch_shapes=[pltpu.VMEM((BAND_PAST + T, DB), BF16), pltpu.VMEM((BAND_PAST + T, DB), BF16)],
        compiler_params=_cparams(("parallel", "arbitrary")),
        name="band_attention",
    )(*args)


def _rel_bias(table, chq):
    W = BAND_PAST + chq
    lw = W + chq - 1
    n_clipped = BAND_PAST - MAX_REL + chq
    w = jnp.concatenate([jnp.broadcast_to(table[:, 2 * MAX_REL:], (HB, n_clipped)),
                         table[:, MAX_REL + 1 - chq:2 * MAX_REL][:, ::-1]], axis=1)
    flat = jnp.tile(jnp.pad(w, ((0, 0), (0, 1))), (1, chq))[:, :chq * lw]
    return flat.reshape(HB, chq, lw)[:, :, chq - 1:chq - 1 + W]


def _outproj_kernel(*refs, n_in):
    ys, ws = refs[:n_in], refs[n_in:2 * n_in]
    x_ref, gt_ref, o_ref = refs[2 * n_in:]
    acc = _dot(ys[0][...], ws[0][...])
    for y_ref, w_ref in zip(ys[1:], ws[1:]):
        acc = acc + _dot(y_ref[...], w_ref[...])
    bB, pb, d = x_ref.shape
    o_ref[...] = x_ref[...] + gt_ref[...] * acc.reshape(bB, pb, d)


def _out_projection(ys, ws, x_all, mod_pb, gt_col):
    npb, pb, d = x_all.shape
    bB = 32
    tm = bB * pb
    n_in = len(ys)
    in_specs = [pl.BlockSpec((tm, y.shape[1]), lambda i: (i, 0)) for y in ys]
    in_specs += [pl.BlockSpec(w.shape, lambda i: (0, 0)) for w in ws]
    in_specs += [pl.BlockSpec((bB, pb, d), lambda i: (i, 0, 0)),
                 pl.BlockSpec((bB, 1, d), lambda i: (i, 0, gt_col))]
    return pl.pallas_call(
        functools.partial(_outproj_kernel, n_in=n_in),
        grid=(npb // bB,),
        in_specs=in_specs,
        out_specs=pl.BlockSpec((bB, pb, d), lambda i: (i, 0, 0)),
        out_shape=jax.ShapeDtypeStruct(x_all.shape, F32),
        compiler_params=_cparams(("parallel",)),
        name="out_projection",
    )(*ys, *ws, x_all, mod_pb)


def _gdnpre_kernel(qkv_ref, ab_ref, cb_ref, cw_ref, alog_ref, dtb_ref,
                   q_ref, k_ref, v_ref, gb_ref, gt_ref, cbo_ref, xbuf, *, C):
    tT = qkv_ref.shape[0]

    @pl.when(pl.program_id(1) == 0)
    def _():
        xbuf[0:SUBLANES, :] = jnp.zeros((SUBLANES, N_QKV), F32)
        xbuf[SUBLANES - (CONV_W - 1):SUBLANES, :] = cb_ref[...]

    x = qkv_ref[...]
    xbuf[SUBLANES:SUBLANES + tT, :] = x
    xc = cw_ref[3:4, :] * x
    for j in range(CONV_W - 1):
        xc = xc + cw_ref[j:j + 1, :] * xbuf[SUBLANES - 3 + j:SUBLANES - 3 + j + tT, :]
    tail = xbuf[tT:tT + SUBLANES, :]
    xbuf[0:SUBLANES, :] = tail
    cbo_ref[...] = tail[SUBLANES - (CONV_W - 1):, :]

    xs = xc * jax.nn.sigmoid(xc)
    for h in range(HC):
        qh = xs[:, h * DK:(h + 1) * DK]
        q_ref[:, h * DK:(h + 1) * DK] = (qh * lax.rsqrt(jnp.sum(qh * qh, axis=-1, keepdims=True) + NORM_EPS)
                                         * (DK ** -0.5))
        kh = xs[:, DCK + h * DK:DCK + (h + 1) * DK]
        k_ref[:, h * DK:(h + 1) * DK] = kh * lax.rsqrt(jnp.sum(kh * kh, axis=-1, keepdims=True) + NORM_EPS)
    v_ref[...] = xs[:, 2 * DCK:]

    ab = ab_ref[...]
    g = -jnp.exp(alog_ref[...]) * _softplus(ab + dtb_ref[...])
    beta = jax.nn.sigmoid(ab)
    lane = lax.broadcasted_iota(I32, ab.shape, 1)
    g = jnp.where(lane < HC, g, 0.0)
    ri = lax.broadcasted_iota(I32, (C, C), 0)
    ci = lax.broadcasted_iota(I32, (C, C), 1)
    tril = (ri >= ci).astype(F32)
    triu = (ri <= ci).astype(F32)
    for c in range(tT // C):
        gc = g[c * C:(c + 1) * C, :]
        G = _dot(tril, gc, HI)
        lane_c = lax.broadcasted_iota(I32, (C, LANES), 1)
        gb_ref[c * C:(c + 1) * C, :] = jnp.where(lane_c < HC, G, beta[c * C:(c + 1) * C, :])
        gt_ref[c * 2 * HC:(c + 1) * 2 * HC, :] = _dot_tn(gc, triu, HI)[:2 * HC, :]


def _gdn_pre(proj, row0, B, T, tT, C, conv_buf, cw, alog_pad, dtb_pad):
    nT = T // tT
    rb0 = row0 // tT
    cpt = tT // C
    return pl.pallas_call(
        functools.partial(_gdnpre_kernel, C=C),
        grid=(B, nT),
        in_specs=[pl.BlockSpec((tT, N_QKV), lambda b, t: (rb0 + b * nT + t, 0)),
                  pl.BlockSpec((tT, LANES), lambda b, t: (rb0 + b * nT + t, (N_QKV + DCV) // LANES)),
                  pl.BlockSpec((None, CONV_W - 1, N_QKV), lambda b, t: (b, 0, 0)),
                  pl.BlockSpec((CONV_W, N_QKV), lambda b, t: (0, 0)),
                  pl.BlockSpec((1, LANES), lambda b, t: (0, 0)),
                  pl.BlockSpec((1, LANES), lambda b, t: (0, 0))],
        out_specs=[pl.BlockSpec((tT, DCK), lambda b, t: (b * nT + t, 0)),
                   pl.BlockSpec((tT, DCK), lambda b, t: (b * nT + t, 0)),
                   pl.BlockSpec((tT, DCV), lambda b, t: (b * nT + t, 0)),
                   pl.BlockSpec((tT, LANES), lambda b, t: (b * nT + t, 0)),
                   pl.BlockSpec((cpt * 2 * HC, C), lambda b, t: (b * nT + t, 0)),
                   pl.BlockSpec((None, CONV_W - 1, N_QKV), lambda b, t: (b, 0, 0))],
        out_shape=[jax.ShapeDtypeStruct((B * T, DCK), F32),
                   jax.ShapeDtypeStruct((B * T, DCK), F32),
                   jax.ShapeDtypeStruct((B * T, DCV), F32),
                   jax.ShapeDtypeStruct((B * T, LANES), F32),
                   jax.ShapeDtypeStruct((B * T // C * 2 * HC, C), F32),
                   jax.ShapeDtypeStruct((B, CONV_W - 1, N_QKV), F32)],
        scratch_shapes=[pltpu.VMEM((tT + SUBLANES, N_QKV), F32)],
        compiler_params=_cparams(("parallel", "arbitrary")),
        name="gdn_pre",
    )(proj, proj, conv_buf, cw, alog_pad, dtb_pad)


def _split_bf16(a):
    hi = a.astype(BF16)
    return hi, (a - hi.astype(F32)).astype(BF16)


def _dot3(a, b):
    return _dot(a[0], b[0]) + (_dot(a[0], b[1]) + _dot(a[1], b[0]))


def _delta_kernel(q_ref, k_ref, v_ref, gb_ref, gt_ref, z_ref, s0_ref, ng_ref, o_ref, so_ref,
                  s_scr, u_scr, w_scr, qd_scr, kd_scr, a_scr, *, C, hps, unroll):
    T = q_ref.shape[0]
    n_chunks = T // C
    ri = lax.broadcasted_iota(I32, (C, C), 0)
    ci = lax.broadcasted_iota(I32, (C, C), 1)
    causal = ri >= ci
    strict = ri > ci
    eye = (ri == ci).astype(F32)
    head0 = pl.program_id(1) * hps
    lane = lax.broadcasted_iota(I32, (C, LANES), 1)

    def pick(x, col):
        lane_x = lax.broadcasted_iota(I32, x.shape, 1)
        return jnp.sum(jnp.where(lane_x == col, x, 0.0), axis=1, keepdims=True)

    def operands(n, j):
        rows = pl.ds(pl.multiple_of(n * C, C), C)
        cs = slice(j * DK, (j + 1) * DK)
        gbc = gb_ref[rows, :]
        G = pick(gbc, head0 + j)
        beta = pick(gbc, head0 + j + HC)
        return rows, cs, G, beta

    def prepare(streams):
        def lower_left(m):
            s = m.bit_length() - 1
            return jnp.logical_and(((ri ^ ci) >> s) == 1, ((ri >> s) & 1) == 1)

        Bs, Ps = [], []
        for n, j in streams:
            rows, cs, G, beta = operands(n, j)
            g_row = gt_ref[pl.ds(n * (2 * HC) + head0 + j, 1), :]
            k = k_ref[rows, cs]
            L = jnp.where(causal, jnp.exp(jnp.where(causal, G - g_row, 0.0)), 0.0)
            qk = _dot_nt(jnp.concatenate([k * beta, q_ref[rows, cs]], axis=0).astype(BF16), k.astype(BF16))
            a_scr[j, rows, :] = (qk[C:] * L).astype(BF16)
            Bm = jnp.where(strict, -(qk[:C] * L), 0.0)
            Bs.append(Bm)
            Ps.append(eye + jnp.where(lower_left(1), Bm, 0.0))
        m = 2
        while m < C:
            Ds = [_split_bf16(p) for p in Ps]
            Ws = [_split_bf16(_dot3(_split_bf16(jnp.where(lower_left(m), b, 0.0)), d)) for b, d in zip(Bs, Ds)]
            Ps = [p + _dot3(d, w) for p, d, w in zip(Ps, Ds, Ws)]
            m *= 2
        for (n, j), P in zip(streams, Ps):
            rows, cs, G, beta = operands(n, j)
            k = k_ref[rows, cs]
            eG = jnp.exp(G)
            rhs = jnp.concatenate([v_ref[rows, cs] * beta, (k * beta) * eG], axis=1)
            X = _dot3(_split_bf16(P), _split_bf16(rhs))
            u_scr[j, rows, :] = X[:, :DV]
            w_scr[j, rows, :] = X[:, DV:].astype(BF16)
            qd_scr[j, rows, :] = (q_ref[rows, cs] * eG).astype(BF16)
            kd_scr[j, rows, :] = (k * jnp.exp(G[C - 1:C, :] - G)).astype(BF16)

    def prepare_body(i, carry):
        prepare([(i * unroll + m, j) for m in range(unroll) for j in range(hps)])
        return carry

    lax.fori_loop(0, n_chunks // unroll, prepare_body, 0)

    s_scr[...] = s0_ref[...]

    def advance(n, carry):
        r0 = pl.multiple_of(n * C, C)
        rows = pl.ds(r0, C)
        g_end = gb_ref[pl.ds(r0 + C - 1, 1), :]
        heads = range(hps)
        S = [s_scr[j] for j in heads]
        Sb = [s.astype(BF16) for s in S]
        wS = [_dot(w_scr[j, rows, :], Sb[j]) for j in heads]
        qS = [_dot(qd_scr[j, rows, :], Sb[j]) for j in heads]
        vnb = [(u_scr[j, rows, :] - wS[j]).astype(BF16) for j in heads]
        kv = [_dot_tn(kd_scr[j, rows, :], vnb[j]) for j in heads]
        av = [_dot(a_scr[j, rows, :], vnb[j]) for j in heads]
        for j in heads:
            cs = slice(j * DK, (j + 1) * DK)
            s_scr[j] = S[j] * jnp.exp(pick(g_end, head0 + j)) + kv[j]
            o = qS[j] + av[j]
            on = o * lax.rsqrt(jnp.mean(o * o, axis=-1, keepdims=True) + NORM_EPS) * ng_ref[...]
            z = z_ref[rows, cs]
            o_ref[rows, cs] = (on * (z * jax.nn.sigmoid(z))).astype(BF16)
        return carry

    lax.fori_loop(0, n_chunks, advance, 0)
    so_ref[...] = s_scr[...]


def _delta_rule(q, k, v, gb, gt, proj, row0, B, T, C, S0, norm_g, hps):
    rb = row0 // T
    wb = hps * DK
    zcol0 = N_QKV // wb
    n_chunks = T // C
    qkv_spec = lambda: pl.BlockSpec((T, wb), lambda b, h: (b, h))
    return pl.pallas_call(
        functools.partial(_delta_kernel, C=C, hps=hps, unroll=2 if n_chunks % 2 == 0 else 1),
        grid=(B, HC // hps),
        in_specs=[qkv_spec(), qkv_spec(), qkv_spec(),
                  pl.BlockSpec((T, LANES), lambda b, h: (b, 0)),
                  pl.BlockSpec((n_chunks * 2 * HC, C), lambda b, h: (b, 0)),
                  pl.BlockSpec((T, wb), lambda b, h: (rb + b, zcol0 + h)),
                  pl.BlockSpec((None, hps, DK, DV), lambda b, h: (b, h, 0, 0)),
                  pl.BlockSpec((1, DV), lambda b, h: (0, 0))],
        out_specs=[pl.BlockSpec((T, wb), lambda b, h: (b, h)),
                   pl.BlockSpec((None, hps, DK, DV), lambda b, h: (b, h, 0, 0))],
        out_shape=[jax.ShapeDtypeStruct((B * T, DCV), BF16),
                   jax.ShapeDtypeStruct((B, HC, DK, DV), F32)],
        scratch_shapes=[pltpu.VMEM((hps, DK, DV), F32),
                        pltpu.VMEM((hps, T, DV), F32),
                        pltpu.VMEM((hps, T, DK), BF16),
                        pltpu.VMEM((hps, T, DK), BF16),
                        pltpu.VMEM((hps, T, DK), BF16),
                        pltpu.VMEM((hps, T, C), BF16)],
        compiler_params=_cparams(("parallel", "parallel")),
        name="delta_rule",
    )(q, k, v, gb, gt, proj, S0, norm_g.reshape(1, DV))


def _first_argmax(x, idx, size):
    m = jnp.max(x, axis=0, keepdims=True)
    am = jnp.min(jnp.where(x == m, idx, size), axis=0, keepdims=True)
    return m, am


def _router_kernel(x_ref, g_ref, sc_ref, sh_ref, rwt_ref, rb_ref,
                   h_ref, e_ref, w_ref, r_ref, cnt_ref, run_scr):
    tn = h_ref.shape[0]
    gsz = N_EXPERTS // N_GROUPS

    @pl.when(pl.program_id(0) == 0)
    def _():
        run_scr[...] = jnp.zeros_like(run_scr)

    h = _norm_mod(x_ref[...], g_ref[...], sc_ref[...], sh_ref[...])
    h_ref[...] = h
    scores = jax.nn.sigmoid(_dot_nt(rwt_ref[...], h, HI))
    sel = scores + rb_ref[...]

    gidx = lax.broadcasted_iota(I32, (gsz, tn), 0)
    gs_rows = []
    for gi in range(N_GROUPS):
        blk = sel[gi * gsz:(gi + 1) * gsz, :]
        m1, a1 = _first_argmax(blk, gidx, gsz)
        m2 = jnp.max(jnp.where(gidx == a1, -jnp.inf, blk), axis=0, keepdims=True)
        gs_rows.append(m1 + m2)
    gscore = jnp.concatenate(gs_rows, axis=0)
    gi8 = lax.broadcasted_iota(I32, (N_GROUPS, tn), 0)
    gmask = jnp.zeros((N_GROUPS, tn), jnp.bool_)
    gwork = gscore
    for _ in range(TOPK_GROUPS):
        _, a = _first_argmax(gwork, gi8, N_GROUPS)
        hit = gi8 == a
        gmask = jnp.logical_or(gmask, hit)
        gwork = jnp.where(hit, -jnp.inf, gwork)
    emask = jnp.concatenate([jnp.broadcast_to(gmask[gi:gi + 1, :], (gsz, tn)) for gi in range(N_GROUPS)], axis=0)

    eidx = lax.broadcasted_iota(I32, (N_EXPERTS, tn), 0)
    work = jnp.where(emask, sel, -jnp.inf)
    hits = []
    chosen = jnp.zeros((N_EXPERTS, tn), jnp.bool_)
    for _ in range(TOP_K):
        _, a = _first_argmax(work, eidx, N_EXPERTS)
        hit = eidx == a
        hits.append((a, hit))
        chosen = jnp.logical_or(chosen, hit)
        work = jnp.where(hit, -jnp.inf, work)
    chosen_f = chosen.astype(F32)
    denom = jnp.sum(scores * chosen_f, axis=0, keepdims=True)

    ti = lax.broadcasted_iota(I32, (tn, tn), 0)
    tj = lax.broadcasted_iota(I32, (tn, tn), 1)
    before = (ti < tj).astype(BF16)
    rank = _dot(chosen_f.astype(BF16), before) + run_scr[...]
    run_scr[...] = run_scr[...] + jnp.sum(chosen_f, axis=1, keepdims=True)
    cnt_ref[...] = jnp.broadcast_to(run_scr[...], cnt_ref.shape).astype(I32)

    e_rows, w_rows, r_rows = [], [], []
    for a, hit in hits:
        hf = hit.astype(F32)
        e_rows.append(a)
        w_rows.append(jnp.sum(scores * hf, axis=0, keepdims=True) / denom * ROUTE_SCALE)
        r_rows.append(jnp.sum(rank * hf, axis=0, keepdims=True))
    e_ref[...] = jnp.concatenate(e_rows, axis=0)
    w_ref[...] = jnp.concatenate(w_rows, axis=0)
    r_ref[...] = jnp.concatenate(r_rows, axis=0).astype(I32)


def _router(x_all, g, mod_pb, sc_col, sh_col, rwt, rb):
    npb, pb, d = x_all.shape
    n = npb * pb
    bB = 16
    tn = bB * pb
    return pl.pallas_call(
        _router_kernel,
        grid=(npb // bB,),
        in_specs=[pl.BlockSpec((bB, pb, d), lambda i: (i, 0, 0)),
                  pl.BlockSpec((1, d), lambda i: (0, 0)),
                  pl.BlockSpec((bB, 1, d), lambda i: (i, 0, sc_col)),
                  pl.BlockSpec((bB, 1, d), lambda i: (i, 0, sh_col)),
                  pl.BlockSpec((N_EXPERTS, d), lambda i: (0, 0)),
                  pl.BlockSpec((N_EXPERTS, 1), lambda i: (0, 0))],
        out_specs=[pl.BlockSpec((tn, d), lambda i: (i, 0)),
                   pl.BlockSpec((TOP_K, tn), lambda i: (0, i)),
                   pl.BlockSpec((TOP_K, tn), lambda i: (0, i)),
                   pl.BlockSpec((TOP_K, tn), lambda i: (0, i)),
                   pl.BlockSpec((N_EXPERTS, LANES), lambda i: (0, 0))],
        out_shape=[jax.ShapeDtypeStruct((n, d), F32),
                   jax.ShapeDtypeStruct((TOP_K, n), I32),
                   jax.ShapeDtypeStruct((TOP_K, n), F32),
                   jax.ShapeDtypeStruct((TOP_K, n), I32),
                   jax.ShapeDtypeStruct((N_EXPERTS, LANES), I32)],
        scratch_shapes=[pltpu.VMEM((N_EXPERTS, 1), F32)],
        compiler_params=_cparams(("arbitrary",)),
        name="moe_router",
    )(x_all, g.reshape(1, d), mod_pb, mod_pb, rwt, rb.reshape(N_EXPERTS, 1))


def _row_slab(ref, r):
    return ref.at[pl.ds(pl.multiple_of(r * SUBLANES, SUBLANES), SUBLANES), :]


def _to_row_tiles(dst_ref, x, row0=0):
    rows = x.shape[0]
    for c in range(SUBLANES):
        dst_ref[pl.ds(row0 * SUBLANES + c, rows, stride=SUBLANES), :] = x[:, c * LANES:(c + 1) * LANES]


def _lane_tile(src_ref, c, rows, row0=0):
    return src_ref[pl.ds(row0 * SUBLANES + c, rows, stride=SUBLANES), :]


def _from_row_tiles(src_ref, rows):
    return jnp.concatenate([_lane_tile(src_ref, c, rows) for c in range(SUBLANES)], axis=1)


def _dispatch_kernel(zf_ref, pos_ref, h_ref, xs_hbm, sbuf, zbuf, sem, zsem, *, tm):
    i = pl.program_id(0)
    n_steps = pl.num_programs(0)
    td = h_ref.shape[0]
    n_tiles = zf_ref.shape[0]
    slot = lax.rem(i, 2)

    @pl.when(i == 0)
    def _():
        zbuf[...] = jnp.zeros_like(zbuf)

        def zero_copy(t):
            r0 = pl.multiple_of(t * (tm * SUBLANES), tm * SUBLANES)
            return pltpu.make_async_copy(zbuf, xs_hbm.at[pl.ds(r0, tm * SUBLANES), :], zsem)

        def start(t, carry):
            @pl.when(zf_ref[t] != 0)
            def _():
                zero_copy(t).start()
            return carry

        def wait(t, carry):
            @pl.when(zf_ref[t] != 0)
            def _():
                zero_copy(t).wait()
            return carry

        lax.fori_loop(0, n_tiles, start, 0)
        lax.fori_loop(0, n_tiles, wait, 0)

    def drain(s):
        for _ in range(TOP_K):
            pltpu.make_async_copy(sbuf.at[s], xs_hbm.at[pl.ds(0, td * SUBLANES), :], sem.at[s]).wait()

    @pl.when(i >= 2)
    def _():
        drain(slot)

    _to_row_tiles(sbuf.at[slot], h_ref[...])

    def body(j, carry):
        t0 = pl.multiple_of(j * SUBLANES, SUBLANES)
        for u in range(SUBLANES):
            src = _row_slab(sbuf.at[slot], t0 + u)
            for k in range(TOP_K):
                p = pos_ref[0, k * td + t0 + u]
                pltpu.make_async_copy(src, _row_slab(xs_hbm, p), sem.at[slot]).start()
        return carry

    lax.fori_loop(0, td // SUBLANES, body, 0)

    @pl.when(i == n_steps - 1)
    def _():
        drain(slot)
        drain(1 - slot)


def _moe_dispatch(h2, pos_tiles, zero_flag, n_tiles, tm, td):
    n, d = h2.shape
    grid_spec = pltpu.PrefetchScalarGridSpec(
        num_scalar_prefetch=1,
        grid=(n // td,),
        in_specs=[pl.BlockSpec((None, 1, TOP_K * td), lambda i, zf: (i, 0, 0), memory_space=pltpu.SMEM),
                  pl.BlockSpec((td, d), lambda i, zf: (i, 0))],
        out_specs=pl.BlockSpec(memory_space=pl.ANY),
        scratch_shapes=[pltpu.VMEM((2, td * SUBLANES, LANES), F32), pltpu.VMEM((tm * SUBLANES, LANES), F32),
                        pltpu.SemaphoreType.DMA((2,)), pltpu.SemaphoreType.DMA(())],
    )
    return pl.pallas_call(
        functools.partial(_dispatch_kernel, tm=tm),
        grid_spec=grid_spec,
        out_shape=jax.ShapeDtypeStruct((n_tiles * tm * SUBLANES, LANES), F32),
        compiler_params=_cparams(("arbitrary",)),
        name="moe_dispatch",
    )(zero_flag, pos_tiles, h2)


def _experts_kernel(te_ref, nu_ref, xs_ref, wgu_ref, wd_ref, ys_ref):
    i = pl.program_id(0)
    nu = nu_ref[0]

    @pl.when(i < nu)
    def _():
        x = _from_row_tiles(xs_ref, xs_ref.shape[0] // SUBLANES).astype(BF16)
        gu = _dot(x, wgu_ref[0])
        hid = jax.nn.silu(gu[:, :D_EXPERT]) * gu[:, D_EXPERT:]
        _to_row_tiles(ys_ref, _dot(hid.astype(BF16), wd_ref[0]))

    @pl.when(i >= nu)
    def _():
        ys_ref[...] = jnp.zeros_like(ys_ref)


def _routed_experts(xs, tile_expert, n_used, wgu, wd, tm):
    n_tiles = xs.shape[0] // (tm * SUBLANES)
    d = wgu.shape[1]
    grid_spec = pltpu.PrefetchScalarGridSpec(
        num_scalar_prefetch=2,
        grid=(n_tiles,),
        in_specs=[pl.BlockSpec((tm * SUBLANES, LANES), lambda i, te, nu: (jnp.minimum(i, nu[0] - 1), 0)),
                  pl.BlockSpec((1, d, 2 * D_EXPERT), lambda i, te, nu: (te[i], 0, 0)),
                  pl.BlockSpec((1, D_EXPERT, d), lambda i, te, nu: (te[i], 0, 0))],
        out_specs=pl.BlockSpec((tm * SUBLANES, LANES), lambda i, te, nu: (i, 0)),
    )
    return pl.pallas_call(
        _experts_kernel,
        grid_spec=grid_spec,
        out_shape=jax.ShapeDtypeStruct(xs.shape, F32),
        compiler_params=_cparams(("arbitrary",)),
        name="moe_experts",
    )(tile_expert, n_used, xs, wgu, wd)


def _gather_rows(idx_ref, n_rows, src_hbm, dst, sem):
    def body(j, carry):
        r0 = pl.multiple_of(j * SUBLANES, SUBLANES)
        for u in range(SUBLANES):
            pltpu.make_async_copy(_row_slab(src_hbm, idx_ref[0, r0 + u]), _row_slab(dst, r0 + u), sem).start()
        return carry
    lax.fori_loop(0, n_rows // SUBLANES, body, 0)


def _combine_kernel(pos_ref, nxt_ref, ys_hbm, w_ref, h_ref, x_ref, gt_ref, sgu_ref, sd_ref, o_ref, gbuf, sem):
    i = pl.program_id(0)
    tb = w_ref.shape[0]
    slot = lax.rem(i, 2)

    @pl.when(i == 0)
    def _():
        _gather_rows(pos_ref, TOP_K * tb, ys_hbm, gbuf.at[0], sem.at[0])

    @pl.when(i + 1 < pl.num_programs(0))
    def _():
        _gather_rows(nxt_ref, TOP_K * tb, ys_hbm, gbuf.at[1 - slot], sem.at[1 - slot])

    hb = h_ref[...].astype(BF16)
    gu = _dot(hb, sgu_ref[...])
    hid = jax.nn.silu(gu[:, :D_SHARED]) * gu[:, D_SHARED:]
    shared = _dot(hid.astype(BF16), sd_ref[...])
    pltpu.make_async_copy(ys_hbm.at[pl.ds(0, TOP_K * tb * SUBLANES), :], gbuf.at[slot], sem.at[slot]).wait()
    w = w_ref[...]
    wk = [jnp.broadcast_to(w[:, k:k + 1], (tb, LANES)) for k in range(TOP_K)]
    gs = gbuf.at[slot]
    cols = []
    for c in range(SUBLANES):
        a = wk[0] * _lane_tile(gs, c, tb)
        for k in range(1, TOP_K):
            a = a + wk[k] * _lane_tile(gs, c, tb, k * tb)
        cols.append(a)
    acc = shared + jnp.concatenate(cols, axis=1)
    bB, pb, d = x_ref.shape
    o_ref[...] = x_ref[...] + gt_ref[...] * acc.reshape(bB, pb, d)


def _moe_combine(ys, pos_tiles, w_tok, h2, x_all, mod_pb, gt_col, sgu, sd, tb):
    npb, pb, d = x_all.shape
    n_tb = pos_tiles.shape[0]
    bB = tb // pb
    last = n_tb - 1
    return pl.pallas_call(
        _combine_kernel,
        grid=(n_tb,),
        in_specs=[pl.BlockSpec((None, 1, TOP_K * tb), lambda i: (i, 0, 0), memory_space=pltpu.SMEM),
                  pl.BlockSpec((None, 1, TOP_K * tb), lambda i: (jnp.minimum(i + 1, last), 0, 0),
                               memory_space=pltpu.SMEM),
                  pl.BlockSpec(memory_space=pl.ANY),
                  pl.BlockSpec((tb, TOP_K), lambda i: (i, 0)),
                  pl.BlockSpec((tb, d), lambda i: (i, 0)),
                  pl.BlockSpec((bB, pb, d), lambda i: (i, 0, 0)),
                  pl.BlockSpec((bB, 1, d), lambda i: (i, 0, gt_col)),
                  pl.BlockSpec(sgu.shape, lambda i: (0, 0)),
                  pl.BlockSpec(sd.shape, lambda i: (0, 0))],
        out_specs=pl.BlockSpec((bB, pb, d), lambda i: (i, 0, 0)),
        out_shape=jax.ShapeDtypeStruct(x_all.shape, F32),
        scratch_shapes=[pltpu.VMEM((2, TOP_K * tb * SUBLANES, LANES), F32), pltpu.SemaphoreType.DMA((2,))],
        compiler_params=_cparams(("arbitrary",)),
        name="moe_combine",
    )(pos_tiles, pos_tiles, ys, w_tok, h2, x_all, mod_pb, sgu, sd)


def _moe(x_all, g, mod_pb, rw, rb, wg, wu, wd, sg, su, sd):
    npb, pb, d = x_all.shape
    n = npb * pb
    tm = 256
    tb = 128
    h2, e_t, w_t, r_t, cnt = _router(x_all, g, mod_pb, 4, 3, rw.T, rb)

    counts = cnt[:, 0]
    padded = (counts + tm - 1) // tm * tm
    ends = jnp.cumsum(padded)
    starts = ends - padded
    n_tiles = (n * TOP_K + N_EXPERTS * (tm - 1)) // tm
    onehot = e_t[:, :, None] == jnp.arange(N_EXPERTS, dtype=I32)
    pos = jnp.sum(jnp.where(onehot, starts, 0), axis=-1) + r_t
    n_used = (ends[-1] // tm).astype(I32)
    tile_idx = jnp.arange(n_tiles, dtype=I32)
    tile_start = jnp.minimum(tile_idx, n_used - 1) * tm
    tile_expert = jnp.sum(ends[None, :] <= tile_start[:, None], axis=1).astype(I32)
    is_seg_end = jnp.any((ends[None, :] == (tile_idx[:, None] + 1) * tm) & (padded[None, :] > 0), axis=1)
    zero_flag = jnp.logical_or(is_seg_end, tile_idx >= n_used).astype(I32)
    pos_tiles = pos.reshape(TOP_K, n // tb, tb).transpose(1, 0, 2).reshape(n // tb, 1, TOP_K * tb)

    xs = _moe_dispatch(h2, pos_tiles, zero_flag, n_tiles, tm, tb)
    wgu = jnp.concatenate([wg, wu], axis=-1).astype(BF16)
    ys = _routed_experts(xs, tile_expert, n_used.reshape(1), wgu, wd.astype(BF16), tm)
    sgu = jnp.concatenate([sg, su], axis=-1).astype(BF16)
    return _moe_combine(ys, pos_tiles, w_t.T, h2, x_all, mod_pb, 5, sgu, sd.astype(BF16), tb)


def _final_kernel(x_ref, g_ref, o_ref):
    x = x_ref[...]
    ms = jnp.mean(x * x, axis=-1, keepdims=True)
    o_ref[...] = x * lax.rsqrt(ms + NORM_EPS) * g_ref[...]


def _final_norm(x_all, g, pb0, n_pb):
    npb, pb, d = x_all.shape
    bB = 32
    return pl.pallas_call(
        _final_kernel,
        grid=(n_pb // bB,),
        in_specs=[pl.BlockSpec((bB, pb, d), lambda i: (pb0 // bB + i, 0, 0)),
                  pl.BlockSpec((1, d), lambda i: (0, 0))],
        out_specs=pl.BlockSpec((bB, pb, d), lambda i: (i, 0, 0)),
        out_shape=jax.ShapeDtypeStruct((n_pb, pb, d), F32),
        compiler_params=_cparams(("parallel",)),
        name="final_norm",
    )(x_all, g.reshape(1, d))


def _block_diag(w):
    nb, di, do = w.shape
    return jnp.einsum("nde,nm->ndme", w, jnp.eye(nb, dtype=w.dtype)).reshape(nb * di, nb * do)


def kernel(x_prompt, x_sample, c_prompt, c_sample, cache_b_k, cache_b_v, state_a_conv, state_a_h, state_c_conv, state_c_S, norm1_g, norm2_g, final_g, w_mod, b_mod, w_in0, w_out0, a_conv_w, a_conv_b, a_w_r, a_b_r, a_w_i, a_b_i, a_lambda, b_rel_bias, w_in1, w_out1, c_conv_w, c_A_log, c_dt_bias, c_norm_g, router_w, router_bias, e_w_gate, e_w_up, e_w_down, s_w_gate, s_w_up, s_w_down):
    Bp, Tp, d = x_prompt.shape
    Bs, Ts, _ = x_sample.shape
    np_tok, ns_tok = Bp * Tp, Bs * Ts
    npb_p, npb_s = np_tok // PB, ns_tok // PB
    depth = w_mod.shape[0]

    x_all = jnp.concatenate([x_prompt.reshape(npb_p, PB, d), x_sample.reshape(npb_s, PB, d)], axis=0)
    mod = _modulation(jnp.concatenate([c_prompt, c_sample], axis=0), w_mod, b_mod)

    outs = {}
    for l in range(depth):
        mod_pb = jnp.concatenate([jnp.repeat(mod[l, :Bp], Tp // PB, axis=0),
                                  jnp.repeat(mod[l, Bp:], Ts // PB, axis=0)], axis=0)[:, None, :]
        j = l // 2
        if l % 2 == 0:
            proj = _in_projection(x_all, norm1_g[l], mod_pb, 1, 0, w_in0[j].astype(BF16), 1280)
            wr, wi = _block_diag(a_w_r[j]).astype(BF16), _block_diag(a_w_i[j]).astype(BF16)
            a_args = (a_conv_w[j], a_conv_b[j], wr, a_b_r[j], wi, a_b_i[j], a_lambda[j])
            ya_p, ac_p, ah_p = _rglru(proj, 0, Bp, Tp, 512, jnp.zeros((Bp, CONV_W - 1, DA), F32),
                                      jnp.zeros((Bp, DA), F32), *a_args)
            ya_s, ac_s, ah_s = _rglru(proj, np_tok, Bs, Ts, Ts, state_a_conv[j], state_a_h[j], *a_args)
            yb_p = _band_attention(proj, 0, Bp, Tp, CHUNK, _rel_bias(b_rel_bias[j], CHUNK))
            yb_s = _band_attention(proj, np_tok, Bs, Ts, Ts, _rel_bias(b_rel_bias[j], Ts),
                                   cache_b_k[j].reshape(Bs, BAND_PAST, DB), cache_b_v[j].reshape(Bs, BAND_PAST, DB))
            ya = jnp.concatenate([ya_p, ya_s], axis=0)
            yb = jnp.concatenate([yb_p, yb_s], axis=0)
            w_out = w_out0[j].astype(BF16)
            x_all = _out_projection([ya, yb], [w_out[:DA], w_out[DA:]], x_all, mod_pb, 2)
            keep = min(BAND_PAST, Tp)
            kv_p = proj[:np_tok].reshape(Bp, Tp, IN0)[:, Tp - keep:, 2 * DA + DB:]
            kv_s = proj[np_tok:].reshape(Bs, Ts, IN0)[:, :, 2 * DA + DB:]
            outs.setdefault("bk_p", []).append(kv_p[..., :DB].reshape(Bp, keep, HB, DHB))
            outs.setdefault("bv_p", []).append(kv_p[..., DB:].reshape(Bp, keep, HB, DHB))
            outs.setdefault("bk_s", []).append(kv_s[..., :DB].reshape(Bs, Ts, HB, DHB))
            outs.setdefault("bv_s", []).append(kv_s[..., DB:].reshape(Bs, Ts, HB, DHB))
            outs.setdefault("ac_p", []).append(ac_p)
            outs.setdefault("ac_s", []).append(ac_s)
            outs.setdefault("ah_p", []).append(ah_p.reshape(Bp, DA))
            outs.setdefault("ah_s", []).append(ah_s.reshape(Bs, DA))
        else:
            w_in = jnp.pad(w_in1[j], ((0, 0), (0, IN1_PAD - IN1))).astype(BF16)
            proj = _in_projection(x_all, norm1_g[l], mod_pb, 1, 0, w_in, 1408)
            alog = jnp.pad(c_A_log[j], (0, LANES - HC)).reshape(1, LANES)
            dtb = jnp.pad(c_dt_bias[j], (0, LANES - HC)).reshape(1, LANES)
            cp, cs = min(DELTA_BLOCK, Tp), min(DELTA_BLOCK, Ts)
            q_p, k_p, v_p, gb_p, gt_p, cc_p = _gdn_pre(proj, 0, Bp, Tp, 256, cp,
                                                       jnp.zeros((Bp, CONV_W - 1, N_QKV), F32), c_conv_w[j], alog, dtb)
            q_s, k_s, v_s, gb_s, gt_s, cc_s = _gdn_pre(proj, np_tok, Bs, Ts, Ts, cs, state_c_conv[j], c_conv_w[j],
                                                       alog, dtb)
            o_p, cs_p = _delta_rule(q_p, k_p, v_p, gb_p, gt_p, proj, 0, Bp, Tp, cp,
                                    jnp.zeros((Bp, HC, DK, DV), F32), c_norm_g[j], 2)
            o_s, cs_s = _delta_rule(q_s, k_s, v_s, gb_s, gt_s, proj, np_tok, Bs, Ts, cs, state_c_S[j],
                                    c_norm_g[j], 2)
            o = jnp.concatenate([o_p, o_s], axis=0)
            x_all = _out_projection([o], [w_out1[j].astype(BF16)], x_all, mod_pb, 2)
            outs.setdefault("cc_p", []).append(cc_p)
            outs.setdefault("cc_s", []).append(cc_s)
            outs.setdefault("cs_p", []).append(cs_p)
            outs.setdefault("cs_s", []).append(cs_s)
        x_all = _moe(x_all, norm2_g[l], mod_pb, router_w[l], router_bias[l], e_w_gate[l], e_w_up[l],
                     e_w_down[l], s_w_gate[l], s_w_up[l], s_w_down[l])

    y_prompt = _final_norm(x_all, final_g, 0, npb_p).reshape(Bp, Tp, d)
    y_sample = _final_norm(x_all, final_g, npb_p, npb_s).reshape(Bs, Ts, d)
    st = {k: jnp.stack(v) for k, v in outs.items()}
    return (y_prompt, y_sample, st["bk_p"], st["bk_s"], st["bv_p"], st["bv_s"], st["ac_p"], st["ac_s"],
            st["ah_p"], st["ah_s"], st["cc_p"], st["cc_s"], st["cs_p"], st["cs_s"])
```

```python
import functools

import jax
import jax.numpy as jnp
from jax import lax
from jax.experimental import pallas as pl
from jax.experimental.pallas import tpu as pltpu

F32 = jnp.float32
BF16 = jnp.bfloat16
I32 = jnp.int32

D_MODEL = 1024
CHUNK = 64
NORM_EPS = 1e-6
CONV_W = 4
DA = 512
NB_A = 8
LRU_C = 8.0
HB = 8
DHB = 64
DB = HB * DHB
BAND_PAST = 8 * CHUNK
MAX_REL = 128
HC = 8
DK = 128
DV = 128
DCK = HC * DK
DCV = HC * DV
N_QKV = 2 * DCK + DCV
DELTA_BLOCK = 128
N_EXPERTS = 64
TOP_K = 8
N_GROUPS = 8
TOPK_GROUPS = 4
D_EXPERT = 256
D_SHARED = 256
ROUTE_SCALE = 2.5
IN0 = 2 * DA + 3 * DB
IN1 = N_QKV + DCV + 2 * HC

LANES = 128
SUBLANES = 8
PB = 32
IN1_PAD = N_QKV + DCV + LANES
VMEM_LIMIT = 48 * 1024 * 1024
HI = lax.Precision.HIGHEST


def _cparams(sem):
    return pltpu.CompilerParams(dimension_semantics=sem, vmem_limit_bytes=VMEM_LIMIT)


def _softplus(x):
    return jnp.maximum(x, 0.0) + jnp.log1p(jnp.exp(-jnp.abs(x)))


def _dot(a, b, precision=None):
    return jnp.dot(a, b, preferred_element_type=F32, precision=precision)


def _dot_nt(a, b, precision=None):
    return lax.dot_general(a, b, (((1,), (1,)), ((), ())), preferred_element_type=F32, precision=precision)


def _dot_tn(a, b, precision=None):
    return lax.dot_general(a, b, (((0,), (0,)), ((), ())), preferred_element_type=F32, precision=precision)


def _mod_kernel(c_ref, w_ref, b_ref, o_ref):
    c = c_ref[...]
    cond = c * jax.nn.sigmoid(c)
    o_ref[0] = _dot(cond.astype(BF16), w_ref[0].astype(BF16)) + b_ref[0]


def _modulation(c_all, w_mod, b_mod):
    depth, d, n6 = w_mod.shape
    nb = c_all.shape[0]
    tn = 1536
    return pl.pallas_call(
        _mod_kernel,
        grid=(depth, n6 // tn),
        in_specs=[pl.BlockSpec((nb, d), lambda l, j: (0, 0)),
                  pl.BlockSpec((1, d, tn), lambda l, j: (l, 0, j)),
                  pl.BlockSpec((1, 1, tn), lambda l, j: (l, 0, j))],
        out_specs=pl.BlockSpec((1, nb, tn), lambda l, j: (l, 0, j)),
        out_shape=jax.ShapeDtypeStruct((depth, nb, n6), F32),
        compiler_params=_cparams(("parallel", "parallel")),
        name="modulation",
    )(c_all, w_mod, b_mod.reshape(depth, 1, n6))


def _norm_mod(x3, g, sc, sh):
    ms = jnp.mean(x3 * x3, axis=-1, keepdims=True)
    y = x3 * lax.rsqrt(ms + NORM_EPS) * g
    h = y * (1.0 + sc) + sh
    return h.reshape(x3.shape[0] * x3.shape[1], x3.shape[2])


def _inproj_kernel(x_ref, g_ref, sc_ref, sh_ref, w_ref, o_ref, h_scr):
    @pl.when(pl.program_id(1) == 0)
    def _():
        h_scr[...] = _norm_mod(x_ref[...], g_ref[...], sc_ref[...], sh_ref[...]).astype(BF16)
    o_ref[...] = _dot(h_scr[...], w_ref[...])


def _in_projection(x_all, g, mod_pb, sc_col, sh_col, w_bf16, tn):
    npb, pb, d = x_all.shape
    nout = w_bf16.shape[1]
    bB = 32
    tm = bB * pb
    return pl.pallas_call(
        _inproj_kernel,
        grid=(npb // bB, nout // tn),
        in_specs=[pl.BlockSpec((bB, pb, d), lambda i, j: (i, 0, 0)),
                  pl.BlockSpec((1, d), lambda i, j: (0, 0)),
                  pl.BlockSpec((bB, 1, d), lambda i, j: (i, 0, sc_col)),
                  pl.BlockSpec((bB, 1, d), lambda i, j: (i, 0, sh_col)),
                  pl.BlockSpec((d, tn), lambda i, j: (0, j))],
        out_specs=pl.BlockSpec((tm, tn), lambda i, j: (i, j)),
        out_shape=jax.ShapeDtypeStruct((npb * pb, nout), F32),
        scratch_shapes=[pltpu.VMEM((tm, d), BF16)],
        compiler_params=_cparams(("parallel", "arbitrary")),
        name="in_projection",
    )(x_all, g.reshape(1, d), mod_pb, mod_pb, w_bf16)


def _rglru_kernel(xa_ref, ga_ref, cb_ref, h0_ref, cw_ref, cbias_ref, wr_ref, br_ref, wi_ref, bi_ref, lam_ref,
                  y_ref, cbo_ref, ho_ref, xbuf, hcar):
    tT = xa_ref.shape[0]

    @pl.when(pl.program_id(1) == 0)
    def _():
        xbuf[0:SUBLANES, :] = jnp.zeros((SUBLANES, DA), F32)
        xbuf[SUBLANES - (CONV_W - 1):SUBLANES, :] = cb_ref[...]
        hcar[...] = h0_ref[...]

    xa = xa_ref[...]
    xbuf[SUBLANES:SUBLANES + tT, :] = xa
    xc = cw_ref[3:4, :] * xa + cbias_ref[...]
    for j in range(CONV_W - 1):
        xc = xc + cw_ref[j:j + 1, :] * xbuf[SUBLANES - 3 + j:SUBLANES - 3 + j + tT, :]
    tail = xbuf[tT:tT + SUBLANES, :]
    xbuf[0:SUBLANES, :] = tail
    cbo_ref[...] = tail[SUBLANES - (CONV_W - 1):, :]

    xcb = xc.astype(BF16)
    r = jax.nn.sigmoid(_dot(xcb, wr_ref[...]) + br_ref[...])
    i = jax.nn.sigmoid(_dot(xcb, wi_ref[...]) + bi_ref[...])
    log_a = -LRU_C * r * _softplus(-lam_ref[...])
    a = jnp.exp(log_a)
    th = jnp.tanh(log_a)
    u = jnp.sqrt(-2.0 * th / (1.0 - th)) * (i * xc)

    rows = lax.broadcasted_iota(I32, (tT, DA), 0)
    s = 1
    while s < tT:
        a_sh = pltpu.roll(a, s, 0)
        u_sh = pltpu.roll(u, s, 0)
        m = rows >= s
        u = jnp.where(m, a * u_sh + u, u)
        a = jnp.where(m, a * a_sh, a)
        s *= 2
    h = a * hcar[...] + u
    hlast = h[tT - 1:tT, :]
    hcar[...] = hlast
    ho_ref[...] = hlast
    y_ref[...] = (h * jax.nn.gelu(ga_ref[...])).astype(BF16)


def _rglru(proj, row0, B, T, tT, conv_buf, h0, cw, cbias, wr, br, wi, bi, lam):
    nT = T // tT
    rb0 = row0 // tT
    vec = lambda: pl.BlockSpec((1, DA), lambda b, t: (0, 0))
    return pl.pallas_call(
        _rglru_kernel,
        grid=(B, nT),
        in_specs=[pl.BlockSpec((tT, DA), lambda b, t: (rb0 + b * nT + t, 0)),
                  pl.BlockSpec((tT, DA), lambda b, t: (rb0 + b * nT + t, 1)),
                  pl.BlockSpec((None, CONV_W - 1, DA), lambda b, t: (b, 0, 0)),
                  pl.BlockSpec((None, 1, DA), lambda b, t: (b, 0, 0)),
                  pl.BlockSpec((CONV_W, DA), lambda b, t: (0, 0)),
                  vec(),
                  pl.BlockSpec((DA, DA), lambda b, t: (0, 0)),
                  vec(),
                  pl.BlockSpec((DA, DA), lambda b, t: (0, 0)),
                  vec(), vec()],
        out_specs=[pl.BlockSpec((tT, DA), lambda b, t: (b * nT + t, 0)),
                   pl.BlockSpec((None, CONV_W - 1, DA), lambda b, t: (b, 0, 0)),
                   pl.BlockSpec((None, 1, DA), lambda b, t: (b, 0, 0))],
        out_shape=[jax.ShapeDtypeStruct((B * T, DA), BF16),
                   jax.ShapeDtypeStruct((B, CONV_W - 1, DA), F32),
                   jax.ShapeDtypeStruct((B, 1, DA), F32)],
        scratch_shapes=[pltpu.VMEM((tT + SUBLANES, DA), F32), pltpu.VMEM((1, DA), F32)],
        compiler_params=_cparams(("parallel", "arbitrary")),
        name="rglru",
    )(proj, proj, conv_buf, h0.reshape(B, 1, DA), cw, cbias.reshape(1, DA), wr, br.reshape(1, DA),
      wi, bi.reshape(1, DA), lam.reshape(1, DA))


def _attn_kernel(*refs, chq, has_hist):
    if has_hist:
        q_ref, k_ref, v_ref, hk_ref, hv_ref, bias_ref, o_ref, ko_ref, vo_ref, kbuf, vbuf = refs
    else:
        q_ref, k_ref, v_ref, bias_ref, o_ref, ko_ref, vo_ref, kbuf, vbuf = refs
    c = pl.program_id(1)
    T = k_ref.shape[0]
    W = BAND_PAST + chq
    keep = ko_ref.shape[0]

    heads = range(HB)

    @pl.when(c == 0)
    def _():
        ko_ref[...] = k_ref[T - keep:T, :]
        vo_ref[...] = v_ref[T - keep:T, :]
        for h in heads:
            sl = slice(h * DHB, (h + 1) * DHB)
            if has_hist:
                kbuf[h, 0:BAND_PAST, :] = hk_ref[:, sl].astype(BF16)
                vbuf[h, 0:BAND_PAST, :] = hv_ref[:, sl].astype(BF16)
            else:
                kbuf[h, 0:BAND_PAST, :] = jnp.zeros((BAND_PAST, DHB), BF16)
                vbuf[h, 0:BAND_PAST, :] = jnp.zeros((BAND_PAST, DHB), BF16)
            kbuf[h, BAND_PAST:BAND_PAST + T, :] = k_ref[:, sl].astype(BF16)
            vbuf[h, BAND_PAST:BAND_PAST + T, :] = v_ref[:, sl].astype(BF16)

    start = pl.multiple_of(c * chq, chq)
    win = pl.ds(start, W)
    q = q_ref[...].astype(BF16)
    s = [_dot_nt(q[:, h * DHB:(h + 1) * DHB], kbuf[h, win, :]) * (DHB ** -0.5) + bias_ref[h] for h in heads]
    if not has_hist:
        valid = start - BAND_PAST + lax.broadcasted_iota(I32, (chq, W), 1) >= 0
        s = [jnp.where(valid, x, -jnp.inf) for x in s]
    m = [jnp.max(x, axis=-1, keepdims=True) for x in s]
    e = [jnp.exp(x - mx) for x, mx in zip(s, m)]
    den = [jnp.sum(x, axis=-1, keepdims=True) for x in e]
    o = [_dot(e[h].astype(BF16), vbuf[h, win, :]) / den[h] for h in heads]
    o_ref[...] = jnp.concatenate(o, axis=-1).astype(BF16)


def _band_attention(proj, row0, B, T, chq, bias, hist_k=None, hist_v=None):
    nC = T // chq
    rbq = row0 // chq
    rbk = row0 // T
    has_hist = hist_k is not None
    keep = min(BAND_PAST, T)
    in_specs = [pl.BlockSpec((chq, DB), lambda b, c: (rbq + b * nC + c, 2)),
                pl.BlockSpec((T, DB), lambda b, c: (rbk + b, 3)),
                pl.BlockSpec((T, DB), lambda b, c: (rbk + b, 4))]
    args = [proj, proj, proj]
    if has_hist:
        in_specs += [pl.BlockSpec((None, BAND_PAST, DB), lambda b, c: (b, 0, 0))] * 2
        args += [hist_k, hist_v]
    in_specs.append(pl.BlockSpec((HB, chq, BAND_PAST + chq), lambda b, c: (0, 0, 0)))
    args.append(bias)
    return pl.pallas_call(
        functools.partial(_attn_kernel, chq=chq, has_hist=has_hist),
        grid=(B, nC),
        in_specs=in_specs,
        out_specs=[pl.BlockSpec((chq, DB), lambda b, c: (b * nC + c, 0)),
                   pl.BlockSpec((None, keep, DB), lambda b, c: (b, 0, 0)),
                   pl.BlockSpec((None, keep, DB), lambda b, c: (b, 0, 0))],
        out_shape=[jax.ShapeDtypeStruct((B * T, DB), BF16),
                   jax.ShapeDtypeStruct((B, keep, DB), F32),
                   jax.ShapeDtypeStruct((B, keep, DB), F32)],
        scratch_shapes=[pltpu.VMEM((HB, BAND_PAST + T, DHB), BF16), pltpu.VMEM((HB, BAND_PAST + T, DHB), BF16)],
        compiler_params=_cparams(("parallel", "arbitrary")),
        name="band_attention",
    )(*args)


def _rel_bias(table, chq):
    W = BAND_PAST + chq
    lw = W + chq - 1
    n_clipped = BAND_PAST - MAX_REL + chq
    w = jnp.concatenate([jnp.broadcast_to(table[:, 2 * MAX_REL:], (HB, n_clipped)),
                         table[:, MAX_REL + 1 - chq:2 * MAX_REL][:, ::-1]], axis=1)
    flat = jnp.tile(jnp.pad(w, ((0, 0), (0, 1))), (1, chq))[:, :chq * lw]
    return flat.reshape(HB, chq, lw)[:, :, chq - 1:chq - 1 + W]


def _outproj_kernel(*refs, n_in):
    ys, ws = refs[:n_in], refs[n_in:2 * n_in]
    x_ref, gt_ref, o_ref = refs[2 * n_in:]
    acc = _dot(ys[0][...], ws[0][...])
    for y_ref, w_ref in zip(ys[1:], ws[1:]):
        acc = acc + _dot(y_ref[...], w_ref[...])
    bB, pb, d = x_ref.shape
    o_ref[...] = x_ref[...] + gt_ref[...] * acc.reshape(bB, pb, d)


def _out_projection(ys, ws, x_all, mod_pb, gt_col):
    npb, pb, d = x_all.shape
    bB = 32
    tm = bB * pb
    n_in = len(ys)
    in_specs = [pl.BlockSpec((tm, y.shape[1]), lambda i: (i, 0)) for y in ys]
    in_specs += [pl.BlockSpec(w.shape, lambda i: (0, 0)) for w in ws]
    in_specs += [pl.BlockSpec((bB, pb, d), lambda i: (i, 0, 0)),
                 pl.BlockSpec((bB, 1, d), lambda i: (i, 0, gt_col))]
    return pl.pallas_call(
        functools.partial(_outproj_kernel, n_in=n_in),
        grid=(npb // bB,),
        in_specs=in_specs,
        out_specs=pl.BlockSpec((bB, pb, d), lambda i: (i, 0, 0)),
        out_shape=jax.ShapeDtypeStruct(x_all.shape, F32),
        compiler_params=_cparams(("parallel",)),
        name="out_projection",
    )(*ys, *ws, x_all, mod_pb)


def _gdnpre_kernel(qkv_ref, ab_ref, cb_ref, cw_ref, alog_ref, dtb_ref,
                   q_ref, k_ref, v_ref, gb_ref, gt_ref, cbo_ref, xbuf, *, C):
    tT = qkv_ref.shape[0]

    @pl.when(pl.program_id(1) == 0)
    def _():
        xbuf[0:SUBLANES, :] = jnp.zeros((SUBLANES, N_QKV), F32)
        xbuf[SUBLANES - (CONV_W - 1):SUBLANES, :] = cb_ref[...]

    x = qkv_ref[...]
    xbuf[SUBLANES:SUBLANES + tT, :] = x
    xc = cw_ref[3:4, :] * x
    for j in range(CONV_W - 1):
        xc = xc + cw_ref[j:j + 1, :] * xbuf[SUBLANES - 3 + j:SUBLANES - 3 + j + tT, :]
    tail = xbuf[tT:tT + SUBLANES, :]
    xbuf[0:SUBLANES, :] = tail
    cbo_ref[...] = tail[SUBLANES - (CONV_W - 1):, :]

    xs = xc * jax.nn.sigmoid(xc)
    for h in range(HC):
        qh = xs[:, h * DK:(h + 1) * DK]
        q_ref[:, h * DK:(h + 1) * DK] = (qh * lax.rsqrt(jnp.sum(qh * qh, axis=-1, keepdims=True) + NORM_EPS)
                                         * (DK ** -0.5))
        kh = xs[:, DCK + h * DK:DCK + (h + 1) * DK]
        k_ref[:, h * DK:(h + 1) * DK] = kh * lax.rsqrt(jnp.sum(kh * kh, axis=-1, keepdims=True) + NORM_EPS)
    v_ref[...] = xs[:, 2 * DCK:]

    ab = ab_ref[...]
    g = -jnp.exp(alog_ref[...]) * _softplus(ab + dtb_ref[...])
    beta = jax.nn.sigmoid(ab)
    lane = lax.broadcasted_iota(I32, ab.shape, 1)
    g = jnp.where(lane < HC, g, 0.0)
    ri = lax.broadcasted_iota(I32, (C, C), 0)
    ci = lax.broadcasted_iota(I32, (C, C), 1)
    tril = (ri >= ci).astype(F32)
    triu = (ri <= ci).astype(F32)
    for c in range(tT // C):
        gc = g[c * C:(c + 1) * C, :]
        G = _dot(tril, gc, HI)
        lane_c = lax.broadcasted_iota(I32, (C, LANES), 1)
        gb_ref[c * C:(c + 1) * C, :] = jnp.where(lane_c < HC, G, beta[c * C:(c + 1) * C, :])
        gt_ref[c * 2 * HC:(c + 1) * 2 * HC, :] = _dot_tn(gc, triu, HI)[:2 * HC, :]


def _gdn_pre(proj, row0, B, T, tT, C, conv_buf, cw, alog_pad, dtb_pad):
    nT = T // tT
    rb0 = row0 // tT
    cpt = tT // C
    return pl.pallas_call(
        functools.partial(_gdnpre_kernel, C=C),
        grid=(B, nT),
        in_specs=[pl.BlockSpec((tT, N_QKV), lambda b, t: (rb0 + b * nT + t, 0)),
                  pl.BlockSpec((tT, LANES), lambda b, t: (rb0 + b * nT + t, (N_QKV + DCV) // LANES)),
                  pl.BlockSpec((None, CONV_W - 1, N_QKV), lambda b, t: (b, 0, 0)),
                  pl.BlockSpec((CONV_W, N_QKV), lambda b, t: (0, 0)),
                  pl.BlockSpec((1, LANES), lambda b, t: (0, 0)),
                  pl.BlockSpec((1, LANES), lambda b, t: (0, 0))],
        out_specs=[pl.BlockSpec((tT, DCK), lambda b, t: (b * nT + t, 0)),
                   pl.BlockSpec((tT, DCK), lambda b, t: (b * nT + t, 0)),
                   pl.BlockSpec((tT, DCV), lambda b, t: (b * nT + t, 0)),
                   pl.BlockSpec((tT, LANES), lambda b, t: (b * nT + t, 0)),
                   pl.BlockSpec((cpt * 2 * HC, C), lambda b, t: (b * nT + t, 0)),
                   pl.BlockSpec((None, CONV_W - 1, N_QKV), lambda b, t: (b, 0, 0))],
        out_shape=[jax.ShapeDtypeStruct((B * T, DCK), F32),
                   jax.ShapeDtypeStruct((B * T, DCK), F32),
                   jax.ShapeDtypeStruct((B * T, DCV), F32),
                   jax.ShapeDtypeStruct((B * T, LANES), F32),
                   jax.ShapeDtypeStruct((B * T // C * 2 * HC, C), F32),
                   jax.ShapeDtypeStruct((B, CONV_W - 1, N_QKV), F32)],
        scratch_shapes=[pltpu.VMEM((tT + SUBLANES, N_QKV), F32)],
        compiler_params=_cparams(("parallel", "arbitrary")),
        name="gdn_pre",
    )(proj, proj, conv_buf, cw, alog_pad, dtb_pad)


def _split_bf16(a):
    hi = a.astype(BF16)
    return hi, (a - hi.astype(F32)).astype(BF16)


def _dot3(a, b):
    return _dot(a[0], b[0]) + (_dot(a[0], b[1]) + _dot(a[1], b[0]))


def _delta_kernel(q_ref, k_ref, v_ref, gb_ref, gt_ref, z_ref, s0_ref, ng_ref, o_ref, so_ref,
                  s_scr, u_scr, w_scr, qd_scr, kd_scr, a_scr, *, C, hps, unroll):
    T = q_ref.shape[0]
    n_chunks = T // C
    ri = lax.broadcasted_iota(I32, (C, C), 0)
    ci = lax.broadcasted_iota(I32, (C, C), 1)
    causal = ri >= ci
    strict = ri > ci
    eye = (ri == ci).astype(F32)
    head0 = pl.program_id(1) * hps
    lane = lax.broadcasted_iota(I32, (C, LANES), 1)

    def pick(x, col):
        lane_x = lax.broadcasted_iota(I32, x.shape, 1)
        return jnp.sum(jnp.where(lane_x == col, x, 0.0), axis=1, keepdims=True)

    def operands(n, j):
        rows = pl.ds(pl.multiple_of(n * C, C), C)
        cs = slice(j * DK, (j + 1) * DK)
        gbc = gb_ref[rows, :]
        G = pick(gbc, head0 + j)
        beta = pick(gbc, head0 + j + HC)
        return rows, cs, G, beta

    def prepare(streams):
        def lower_left(m):
            s = m.bit_length() - 1
            return jnp.logical_and(((ri ^ ci) >> s) == 1, ((ri >> s) & 1) == 1)

        Bs, Ps = [], []
        for n, j in streams:
            rows, cs, G, beta = operands(n, j)
            g_row = gt_ref[pl.ds(n * (2 * HC) + head0 + j, 1), :]
            k = k_ref[rows, cs]
            L = jnp.where(causal, jnp.exp(jnp.where(causal, G - g_row, 0.0)), 0.0)
            qk = _dot_nt(jnp.concatenate([k * beta, q_ref[rows, cs]], axis=0).astype(BF16), k.astype(BF16))
            a_scr[j, rows, :] = (qk[C:] * L).astype(BF16)
            Bm = jnp.where(strict, -(qk[:C] * L), 0.0)
            Bs.append(Bm)
            Ps.append(eye + jnp.where(lower_left(1), Bm, 0.0))
        m = 2
        while m < C:
            Ds = [_split_bf16(p) for p in Ps]
            Ws = [_split_bf16(_dot3(_split_bf16(jnp.where(lower_left(m), b, 0.0)), d)) for b, d in zip(Bs, Ds)]
            Ps = [p + _dot3(d, w) for p, d, w in zip(Ps, Ds, Ws)]
            m *= 2
        for (n, j), P in zip(streams, Ps):
            rows, cs, G, beta = operands(n, j)
            k = k_ref[rows, cs]
            eG = jnp.exp(G)
            rhs = jnp.concatenate([v_ref[rows, cs] * beta, (k * beta) * eG], axis=1)
            X = _dot3(_split_bf16(P), _split_bf16(rhs))
            u_scr[j, rows, :] = X[:, :DV]
            w_scr[j, rows, :] = X[:, DV:].astype(BF16)
            qd_scr[j, rows, :] = (q_ref[rows, cs] * eG).astype(BF16)
            kd_scr[j, rows, :] = (k * jnp.exp(G[C - 1:C, :] - G)).astype(BF16)

    def prepare_body(i, carry):
        prepare([(i * unroll + m, j) for m in range(unroll) for j in range(hps)])
        return carry

    lax.fori_loop(0, n_chunks // unroll, prepare_body, 0)

    s_scr[...] = s0_ref[...]

    def advance(n, carry):
        r0 = pl.multiple_of(n * C, C)
        rows = pl.ds(r0, C)
        g_end = gb_ref[pl.ds(r0 + C - 1, 1), :]
        heads = range(hps)
        S = [s_scr[j] for j in heads]
        Sb = [s.astype(BF16) for s in S]
        wS = [_dot(w_scr[j, rows, :], Sb[j]) for j in heads]
        qS = [_dot(qd_scr[j, rows, :], Sb[j]) for j in heads]
        vnb = [(u_scr[j, rows, :] - wS[j]).astype(BF16) for j in heads]
        kv = [_dot_tn(kd_scr[j, rows, :], vnb[j]) for j in heads]
        av = [_dot(a_scr[j, rows, :], vnb[j]) for j in heads]
        for j in heads:
            cs = slice(j * DK, (j + 1) * DK)
            s_scr[j] = S[j] * jnp.exp(pick(g_end, head0 + j)) + kv[j]
            o = qS[j] + av[j]
            on = o * lax.rsqrt(jnp.mean(o * o, axis=-1, keepdims=True) + NORM_EPS) * ng_ref[...]
            z = z_ref[rows, cs]
            o_ref[rows, cs] = (on * (z * jax.nn.sigmoid(z))).astype(BF16)
        return carry

    lax.fori_loop(0, n_chunks, advance, 0)
    so_ref[...] = s_scr[...]


def _delta_rule(q, k, v, gb, gt, proj, row0, B, T, C, S0, norm_g, hps):
    rb = row0 // T
    wb = hps * DK
    zcol0 = N_QKV // wb
    n_chunks = T // C
    qkv_spec = lambda: pl.BlockSpec((T, wb), lambda b, h: (b, h))
    return pl.pallas_call(
        functools.partial(_delta_kernel, C=C, hps=hps, unroll=2 if n_chunks % 2 == 0 else 1),
        grid=(B, HC // hps),
        in_specs=[qkv_spec(), qkv_spec(), qkv_spec(),
                  pl.BlockSpec((T, LANES), lambda b, h: (b, 0)),
                  pl.BlockSpec((n_chunks * 2 * HC, C), lambda b, h: (b, 0)),
                  pl.BlockSpec((T, wb), lambda b, h: (rb + b, zcol0 + h)),
                  pl.BlockSpec((None, hps, DK, DV), lambda b, h: (b, h, 0, 0)),
                  pl.BlockSpec((1, DV), lambda b, h: (0, 0))],
        out_specs=[pl.BlockSpec((T, wb), lambda b, h: (b, h)),
                   pl.BlockSpec((None, hps, DK, DV), lambda b, h: (b, h, 0, 0))],
        out_shape=[jax.ShapeDtypeStruct((B * T, DCV), BF16),
                   jax.ShapeDtypeStruct((B, HC, DK, DV), F32)],
        scratch_shapes=[pltpu.VMEM((hps, DK, DV), F32),
                        pltpu.VMEM((hps, T, DV), F32),
                        pltpu.VMEM((hps, T, DK), BF16),
                        pltpu.VMEM((hps, T, DK), BF16),
                        pltpu.VMEM((hps, T, DK), BF16),
                        pltpu.VMEM((hps, T, C), BF16)],
        compiler_params=_cparams(("parallel", "parallel")),
        name="delta_rule",
    )(q, k, v, gb, gt, proj, S0, norm_g.reshape(1, DV))


def _first_argmax(x, idx, size):
    m = jnp.max(x, axis=0, keepdims=True)
    am = jnp.min(jnp.where(x == m, idx, size), axis=0, keepdims=True)
    return m, am


def _router_kernel(x_ref, g_ref, sc_ref, sh_ref, rwt_ref, rb_ref,
                   h_ref, e_ref, w_ref, r_ref, cnt_ref, run_scr):
    tn = h_ref.shape[0]
    gsz = N_EXPERTS // N_GROUPS

    @pl.when(pl.program_id(0) == 0)
    def _():
        run_scr[...] = jnp.zeros_like(run_scr)

    h = _norm_mod(x_ref[...], g_ref[...], sc_ref[...], sh_ref[...])
    h_ref[...] = h
    scores = jax.nn.sigmoid(_dot_nt(rwt_ref[...], h, HI))
    sel = scores + rb_ref[...]

    gidx = lax.broadcasted_iota(I32, (gsz, tn), 0)
    gs_rows = []
    for gi in range(N_GROUPS):
        blk = sel[gi * gsz:(gi + 1) * gsz, :]
        m1, a1 = _first_argmax(blk, gidx, gsz)
        m2 = jnp.max(jnp.where(gidx == a1, -jnp.inf, blk), axis=0, keepdims=True)
        gs_rows.append(m1 + m2)
    gscore = jnp.concatenate(gs_rows, axis=0)
    gi8 = lax.broadcasted_iota(I32, (N_GROUPS, tn), 0)
    gmask = jnp.zeros((N_GROUPS, tn), jnp.bool_)
    gwork = gscore
    for _ in range(TOPK_GROUPS):
        _, a = _first_argmax(gwork, gi8, N_GROUPS)
        hit = gi8 == a
        gmask = jnp.logical_or(gmask, hit)
        gwork = jnp.where(hit, -jnp.inf, gwork)
    emask = jnp.concatenate([jnp.broadcast_to(gmask[gi:gi + 1, :], (gsz, tn)) for gi in range(N_GROUPS)], axis=0)

    eidx = lax.broadcasted_iota(I32, (N_EXPERTS, tn), 0)
    work = jnp.where(emask, sel, -jnp.inf)
    hits = []
    chosen = jnp.zeros((N_EXPERTS, tn), jnp.bool_)
    for _ in range(TOP_K):
        _, a = _first_argmax(work, eidx, N_EXPERTS)
        hit = eidx == a
        hits.append((a, hit))
        chosen = jnp.logical_or(chosen, hit)
        work = jnp.where(hit, -jnp.inf, work)
    chosen_f = chosen.astype(F32)
    denom = jnp.sum(scores * chosen_f, axis=0, keepdims=True)

    ti = lax.broadcasted_iota(I32, (tn, tn), 0)
    tj = lax.broadcasted_iota(I32, (tn, tn), 1)
    before = (ti < tj).astype(BF16)
    rank = _dot(chosen_f.astype(BF16), before) + run_scr[...]
    run_scr[...] = run_scr[...] + jnp.sum(chosen_f, axis=1, keepdims=True)
    cnt_ref[...] = jnp.broadcast_to(run_scr[...], cnt_ref.shape).astype(I32)

    e_rows, w_rows, r_rows = [], [], []
    for a, hit in hits:
        hf = hit.astype(F32)
        e_rows.append(a)
        w_rows.append(jnp.sum(scores * hf, axis=0, keepdims=True) / denom * ROUTE_SCALE)
        r_rows.append(jnp.sum(rank * hf, axis=0, keepdims=True))
    e_ref[...] = jnp.concatenate(e_rows, axis=0)
    w_ref[...] = jnp.concatenate(w_rows, axis=0)
    r_ref[...] = jnp.concatenate(r_rows, axis=0).astype(I32)


def _router(x_all, g, mod_pb, sc_col, sh_col, rwt, rb):
    npb, pb, d = x_all.shape
    n = npb * pb
    bB = 16
    tn = bB * pb
    return pl.pallas_call(
        _router_kernel,
        grid=(npb // bB,),
        in_specs=[pl.BlockSpec((bB, pb, d), lambda i: (i, 0, 0)),
                  pl.BlockSpec((1, d), lambda i: (0, 0)),
                  pl.BlockSpec((bB, 1, d), lambda i: (i, 0, sc_col)),
                  pl.BlockSpec((bB, 1, d), lambda i: (i, 0, sh_col)),
                  pl.BlockSpec((N_EXPERTS, d), lambda i: (0, 0)),
                  pl.BlockSpec((N_EXPERTS, 1), lambda i: (0, 0))],
        out_specs=[pl.BlockSpec((tn, d), lambda i: (i, 0)),
                   pl.BlockSpec((TOP_K, tn), lambda i: (0, i)),
                   pl.BlockSpec((TOP_K, tn), lambda i: (0, i)),
                   pl.BlockSpec((TOP_K, tn), lambda i: (0, i)),
                   pl.BlockSpec((N_EXPERTS, LANES), lambda i: (0, 0))],
        out_shape=[jax.ShapeDtypeStruct((n, d), F32),
                   jax.ShapeDtypeStruct((TOP_K, n), I32),
                   jax.ShapeDtypeStruct((TOP_K, n), F32),
                   jax.ShapeDtypeStruct((TOP_K, n), I32),
                   jax.ShapeDtypeStruct((N_EXPERTS, LANES), I32)],
        scratch_shapes=[pltpu.VMEM((N_EXPERTS, 1), F32)],
        compiler_params=_cparams(("arbitrary",)),
        name="moe_router",
    )(x_all, g.reshape(1, d), mod_pb, mod_pb, rwt, rb.reshape(N_EXPERTS, 1))


def _row_slab(ref, r):
    return ref.at[pl.ds(pl.multiple_of(r * SUBLANES, SUBLANES), SUBLANES), :]


def _to_row_tiles(dst_ref, x, row0=0):
    rows = x.shape[0]
    for c in range(SUBLANES):
        dst_ref[pl.ds(row0 * SUBLANES + c, rows, stride=SUBLANES), :] = x[:, c * LANES:(c + 1) * LANES]


def _lane_tile(src_ref, c, rows, row0=0):
    return src_ref[pl.ds(row0 * SUBLANES + c, rows, stride=SUBLANES), :]


def _from_row_tiles(src_ref, rows):
    return jnp.concatenate([_lane_tile(src_ref, c, rows) for c in range(SUBLANES)], axis=1)


def _dispatch_kernel(zf_ref, pos_ref, h_ref, xs_hbm, sbuf, zbuf, sem, zsem, *, tm):
    i = pl.program_id(0)
    n_steps = pl.num_programs(0)
    td = h_ref.shape[0]
    n_tiles = zf_ref.shape[0]
    slot = lax.rem(i, 2)

    @pl.when(i == 0)
    def _():
        zbuf[...] = jnp.zeros_like(zbuf)

        def zero_copy(t):
            r0 = pl.multiple_of(t * (tm * SUBLANES), tm * SUBLANES)
            return pltpu.make_async_copy(zbuf, xs_hbm.at[pl.ds(r0, tm * SUBLANES), :], zsem)

        def start(t, carry):
            @pl.when(zf_ref[t] != 0)
            def _():
                zero_copy(t).start()
            return carry

        def wait(t, carry):
            @pl.when(zf_ref[t] != 0)
            def _():
                zero_copy(t).wait()
            return carry

        lax.fori_loop(0, n_tiles, start, 0)
        lax.fori_loop(0, n_tiles, wait, 0)

    def drain(s):
        for _ in range(TOP_K):
            pltpu.make_async_copy(sbuf.at[s], xs_hbm.at[pl.ds(0, td * SUBLANES), :], sem.at[s]).wait()

    @pl.when(i >= 2)
    def _():
        drain(slot)

    _to_row_tiles(sbuf.at[slot], h_ref[...])

    def body(j, carry):
        t0 = pl.multiple_of(j * SUBLANES, SUBLANES)
        for u in range(SUBLANES):
            src = _row_slab(sbuf.at[slot], t0 + u)
            for k in range(TOP_K):
                p = pos_ref[0, k * td + t0 + u]
                pltpu.make_async_copy(src, _row_slab(xs_hbm, p), sem.at[slot]).start()
        return carry

    lax.fori_loop(0, td // SUBLANES, body, 0)

    @pl.when(i == n_steps - 1)
    def _():
        drain(slot)
        drain(1 - slot)


def _moe_dispatch(h2, pos_tiles, zero_flag, n_tiles, tm, td):
    n, d = h2.shape
    grid_spec = pltpu.PrefetchScalarGridSpec(
        num_scalar_prefetch=1,
        grid=(n // td,),
        in_specs=[pl.BlockSpec((None, 1, TOP_K * td), lambda i, zf: (i, 0, 0), memory_space=pltpu.SMEM),
                  pl.BlockSpec((td, d), lambda i, zf: (i, 0))],
        out_specs=pl.BlockSpec(memory_space=pl.ANY),
        scratch_shapes=[pltpu.VMEM((2, td * SUBLANES, LANES), F32), pltpu.VMEM((tm * SUBLANES, LANES), F32),
                        pltpu.SemaphoreType.DMA((2,)), pltpu.SemaphoreType.DMA(())],
    )
    return pl.pallas_call(
        functools.partial(_dispatch_kernel, tm=tm),
        grid_spec=grid_spec,
        out_shape=jax.ShapeDtypeStruct((n_tiles * tm * SUBLANES, LANES), F32),
        compiler_params=_cparams(("arbitrary",)),
        name="moe_dispatch",
    )(zero_flag, pos_tiles, h2)


def _experts_kernel(te_ref, nu_ref, xs_ref, wg_ref, wu_ref, wd_ref, ys_ref, wgu_s, wd_s):
    i = pl.program_id(0)
    nu = nu_ref[0]

    @pl.when(jnp.logical_or(i == 0, te_ref[i] != te_ref[jnp.maximum(i - 1, 0)]))
    def _():
        wgu_s[:, :D_EXPERT] = wg_ref[0].astype(BF16)
        wgu_s[:, D_EXPERT:] = wu_ref[0].astype(BF16)
        wd_s[...] = wd_ref[0].astype(BF16)

    @pl.when(i < nu)
    def _():
        x = _from_row_tiles(xs_ref, xs_ref.shape[0] // SUBLANES).astype(BF16)
        gu = _dot(x, wgu_s[...])
        hid = jax.nn.silu(gu[:, :D_EXPERT]) * gu[:, D_EXPERT:]
        _to_row_tiles(ys_ref, _dot(hid.astype(BF16), wd_s[...]))

    @pl.when(i >= nu)
    def _():
        ys_ref[...] = jnp.zeros_like(ys_ref)


def _routed_experts(xs, tile_expert, n_used, wg, wu, wd, tm):
    n_tiles = xs.shape[0] // (tm * SUBLANES)
    d = wg.shape[1]
    grid_spec = pltpu.PrefetchScalarGridSpec(
        num_scalar_prefetch=2,
        grid=(n_tiles,),
        in_specs=[pl.BlockSpec((tm * SUBLANES, LANES), lambda i, te, nu: (jnp.minimum(i, nu[0] - 1), 0)),
                  pl.BlockSpec((1, d, D_EXPERT), lambda i, te, nu: (te[i], 0, 0)),
                  pl.BlockSpec((1, d, D_EXPERT), lambda i, te, nu: (te[i], 0, 0)),
                  pl.BlockSpec((1, D_EXPERT, d), lambda i, te, nu: (te[i], 0, 0))],
        out_specs=pl.BlockSpec((tm * SUBLANES, LANES), lambda i, te, nu: (i, 0)),
        scratch_shapes=[pltpu.VMEM((d, 2 * D_EXPERT), BF16), pltpu.VMEM((D_EXPERT, d), BF16)],
    )
    return pl.pallas_call(
        _experts_kernel,
        grid_spec=grid_spec,
        out_shape=jax.ShapeDtypeStruct(xs.shape, F32),
        compiler_params=_cparams(("arbitrary",)),
        name="moe_experts",
    )(tile_expert, n_used, xs, wg, wu, wd)


def _gather_rows(idx_ref, n_rows, src_hbm, dst, sem):
    def body(j, carry):
        r0 = pl.multiple_of(j * SUBLANES, SUBLANES)
        for u in range(SUBLANES):
            pltpu.make_async_copy(_row_slab(src_hbm, idx_ref[0, r0 + u]), _row_slab(dst, r0 + u), sem).start()
        return carry
    lax.fori_loop(0, n_rows // SUBLANES, body, 0)


def _combine_kernel(pos_ref, nxt_ref, ys_hbm, w_ref, h_ref, x_ref, gt_ref, sgu_ref, sd_ref, o_ref, gbuf, sem):
    i = pl.program_id(0)
    tb = w_ref.shape[0]
    slot = lax.rem(i, 2)

    @pl.when(i == 0)
    def _():
        _gather_rows(pos_ref, TOP_K * tb, ys_hbm, gbuf.at[0], sem.at[0])

    @pl.when(i + 1 < pl.num_programs(0))
    def _():
        _gather_rows(nxt_ref, TOP_K * tb, ys_hbm, gbuf.at[1 - slot], sem.at[1 - slot])

    hb = h_ref[...].astype(BF16)
    gu = _dot(hb, sgu_ref[...])
    hid = jax.nn.silu(gu[:, :D_SHARED]) * gu[:, D_SHARED:]
    shared = _dot(hid.astype(BF16), sd_ref[...])
    pltpu.make_async_copy(ys_hbm.at[pl.ds(0, TOP_K * tb * SUBLANES), :], gbuf.at[slot], sem.at[slot]).wait()
    w = w_ref[...]
    wk = [jnp.broadcast_to(w[:, k:k + 1], (tb, LANES)) for k in range(TOP_K)]
    gs = gbuf.at[slot]
    cols = []
    for c in range(SUBLANES):
        a = wk[0] * _lane_tile(gs, c, tb)
        for k in range(1, TOP_K):
            a = a + wk[k] * _lane_tile(gs, c, tb, k * tb)
        cols.append(a)
    acc = shared + jnp.concatenate(cols, axis=1)
    bB, pb, d = x_ref.shape
    o_ref[...] = x_ref[...] + gt_ref[...] * acc.reshape(bB, pb, d)


def _moe_combine(ys, pos_tiles, w_tok, h2, x_all, mod_pb, gt_col, sgu, sd, tb):
    npb, pb, d = x_all.shape
    n_tb = pos_tiles.shape[0]
    bB = tb // pb
    last = n_tb - 1
    return pl.pallas_call(
        _combine_kernel,
        grid=(n_tb,),
        in_specs=[pl.BlockSpec((None, 1, TOP_K * tb), lambda i: (i, 0, 0), memory_space=pltpu.SMEM),
                  pl.BlockSpec((None, 1, TOP_K * tb), lambda i: (jnp.minimum(i + 1, last), 0, 0),
                               memory_space=pltpu.SMEM),
                  pl.BlockSpec(memory_space=pl.ANY),
                  pl.BlockSpec((tb, TOP_K), lambda i: (i, 0)),
                  pl.BlockSpec((tb, d), lambda i: (i, 0)),
                  pl.BlockSpec((bB, pb, d), lambda i: (i, 0, 0)),
                  pl.BlockSpec((bB, 1, d), lambda i: (i, 0, gt_col)),
                  pl.BlockSpec(sgu.shape, lambda i: (0, 0)),
                  pl.BlockSpec(sd.shape, lambda i: (0, 0))],
        out_specs=pl.BlockSpec((bB, pb, d), lambda i: (i, 0, 0)),
        out_shape=jax.ShapeDtypeStruct(x_all.shape, F32),
        scratch_shapes=[pltpu.VMEM((2, TOP_K * tb * SUBLANES, LANES), F32), pltpu.SemaphoreType.DMA((2,))],
        compiler_params=_cparams(("arbitrary",)),
        name="moe_combine",
    )(pos_tiles, pos_tiles, ys, w_tok, h2, x_all, mod_pb, sgu, sd)


def _moe(x_all, g, mod_pb, rw, rb, wg, wu, wd, sg, su, sd):
    npb, pb, d = x_all.shape
    n = npb * pb
    tm = 256
    tb = 128
    h2, e_t, w_t, r_t, cnt = _router(x_all, g, mod_pb, 4, 3, rw.T, rb)

    counts = cnt[:, 0]
    padded = (counts + tm - 1) // tm * tm
    ends = jnp.cumsum(padded)
    starts = ends - padded
    n_tiles = (n * TOP_K + N_EXPERTS * (tm - 1)) // tm
    onehot = e_t[:, :, None] == jnp.arange(N_EXPERTS, dtype=I32)
    pos = jnp.sum(jnp.where(onehot, starts, 0), axis=-1) + r_t
    n_used = (ends[-1] // tm).astype(I32)
    tile_idx = jnp.arange(n_tiles, dtype=I32)
    tile_start = jnp.minimum(tile_idx, n_used - 1) * tm
    tile_expert = jnp.sum(ends[None, :] <= tile_start[:, None], axis=1).astype(I32)
    is_seg_end = jnp.any((ends[None, :] == (tile_idx[:, None] + 1) * tm) & (padded[None, :] > 0), axis=1)
    zero_flag = jnp.logical_or(is_seg_end, tile_idx >= n_used).astype(I32)
    pos_tiles = pos.reshape(TOP_K, n // tb, tb).transpose(1, 0, 2).reshape(n // tb, 1, TOP_K * tb)

    xs = _moe_dispatch(h2, pos_tiles, zero_flag, n_tiles, tm, tb)
    ys = _routed_experts(xs, tile_expert, n_used.reshape(1), wg, wu, wd, tm)
    sgu = jnp.concatenate([sg, su], axis=-1).astype(BF16)
    return _moe_combine(ys, pos_tiles, w_t.T, h2, x_all, mod_pb, 5, sgu, sd.astype(BF16), tb)


def _final_kernel(x_ref, g_ref, o_ref):
    x = x_ref[...]
    ms = jnp.mean(x * x, axis=-1, keepdims=True)
    o_ref[...] = x * lax.rsqrt(ms + NORM_EPS) * g_ref[...]


def _final_norm(x_all, g, pb0, n_pb):
    npb, pb, d = x_all.shape
    bB = 32
    return pl.pallas_call(
        _final_kernel,
        grid=(n_pb // bB,),
        in_specs=[pl.BlockSpec((bB, pb, d), lambda i: (pb0 // bB + i, 0, 0)),
                  pl.BlockSpec((1, d), lambda i: (0, 0))],
        out_specs=pl.BlockSpec((bB, pb, d), lambda i: (i, 0, 0)),
        out_shape=jax.ShapeDtypeStruct((n_pb, pb, d), F32),
        compiler_params=_cparams(("parallel",)),
        name="final_norm",
    )(x_all, g.reshape(1, d))


def _block_diag(w):
    nb, di, do = w.shape
    return jnp.einsum("nde,nm->ndme", w, jnp.eye(nb, dtype=w.dtype)).reshape(nb * di, nb * do)


def kernel(x_prompt, x_sample, c_prompt, c_sample, cache_b_k, cache_b_v, state_a_conv, state_a_h, state_c_conv, state_c_S, norm1_g, norm2_g, final_g, w_mod, b_mod, w_in0, w_out0, a_conv_w, a_conv_b, a_w_r, a_b_r, a_w_i, a_b_i, a_lambda, b_rel_bias, w_in1, w_out1, c_conv_w, c_A_log, c_dt_bias, c_norm_g, router_w, router_bias, e_w_gate, e_w_up, e_w_down, s_w_gate, s_w_up, s_w_down):
    Bp, Tp, d = x_prompt.shape
    Bs, Ts, _ = x_sample.shape
    np_tok, ns_tok = Bp * Tp, Bs * Ts
    npb_p, npb_s = np_tok // PB, ns_tok // PB
    depth = w_mod.shape[0]

    x_all = jnp.concatenate([x_prompt.reshape(npb_p, PB, d), x_sample.reshape(npb_s, PB, d)], axis=0)
    mod = _modulation(jnp.concatenate([c_prompt, c_sample], axis=0), w_mod, b_mod)

    outs = {}
    for l in range(depth):
        mod_pb = jnp.concatenate([jnp.repeat(mod[l, :Bp], Tp // PB, axis=0),
                                  jnp.repeat(mod[l, Bp:], Ts // PB, axis=0)], axis=0)[:, None, :]
        j = l // 2
        if l % 2 == 0:
            proj = _in_projection(x_all, norm1_g[l], mod_pb, 1, 0, w_in0[j].astype(BF16), 1280)
            wr, wi = _block_diag(a_w_r[j]).astype(BF16), _block_diag(a_w_i[j]).astype(BF16)
            a_args = (a_conv_w[j], a_conv_b[j], wr, a_b_r[j], wi, a_b_i[j], a_lambda[j])
            ya_p, ac_p, ah_p = _rglru(proj, 0, Bp, Tp, 512, jnp.zeros((Bp, CONV_W - 1, DA), F32),
                                      jnp.zeros((Bp, DA), F32), *a_args)
            ya_s, ac_s, ah_s = _rglru(proj, np_tok, Bs, Ts, Ts, state_a_conv[j], state_a_h[j], *a_args)
            yb_p, bk_p, bv_p = _band_attention(proj, 0, Bp, Tp, CHUNK, _rel_bias(b_rel_bias[j], CHUNK))
            yb_s, bk_s, bv_s = _band_attention(proj, np_tok, Bs, Ts, Ts, _rel_bias(b_rel_bias[j], Ts),
                                               cache_b_k[j].reshape(Bs, BAND_PAST, DB),
                                               cache_b_v[j].reshape(Bs, BAND_PAST, DB))
            ya = jnp.concatenate([ya_p, ya_s], axis=0)
            yb = jnp.concatenate([yb_p, yb_s], axis=0)
            w_out = w_out0[j].astype(BF16)
            x_all = _out_projection([ya, yb], [w_out[:DA], w_out[DA:]], x_all, mod_pb, 2)
            outs.setdefault("bk_p", []).append(bk_p.reshape(Bp, -1, HB, DHB))
            outs.setdefault("bv_p", []).append(bv_p.reshape(Bp, -1, HB, DHB))
            outs.setdefault("bk_s", []).append(bk_s.reshape(Bs, -1, HB, DHB))
            outs.setdefault("bv_s", []).append(bv_s.reshape(Bs, -1, HB, DHB))
            outs.setdefault("ac_p", []).append(ac_p)
            outs.setdefault("ac_s", []).append(ac_s)
            outs.setdefault("ah_p", []).append(ah_p.reshape(Bp, DA))
            outs.setdefault("ah_s", []).append(ah_s.reshape(Bs, DA))
        else:
            w_in = jnp.pad(w_in1[j], ((0, 0), (0, IN1_PAD - IN1))).astype(BF16)
            proj = _in_projection(x_all, norm1_g[l], mod_pb, 1, 0, w_in, 1408)
            alog = jnp.pad(c_A_log[j], (0, LANES - HC)).reshape(1, LANES)
            dtb = jnp.pad(c_dt_bias[j], (0, LANES - HC)).reshape(1, LANES)
            cp, cs = min(DELTA_BLOCK, Tp), min(DELTA_BLOCK, Ts)
            q_p, k_p, v_p, gb_p, gt_p, cc_p = _gdn_pre(proj, 0, Bp, Tp, 256, cp,
                                                       jnp.zeros((Bp, CONV_W - 1, N_QKV), F32), c_conv_w[j], alog, dtb)
            q_s, k_s, v_s, gb_s, gt_s, cc_s = _gdn_pre(proj, np_tok, Bs, Ts, Ts, cs, state_c_conv[j], c_conv_w[j],
                                                       alog, dtb)
            o_p, cs_p = _delta_rule(q_p, k_p, v_p, gb_p, gt_p, proj, 0, Bp, Tp, cp,
                                    jnp.zeros((Bp, HC, DK, DV), F32), c_norm_g[j], 2)
            o_s, cs_s = _delta_rule(q_s, k_s, v_s, gb_s, gt_s, proj, np_tok, Bs, Ts, cs, state_c_S[j],
                                    c_norm_g[j], 4)
            o = jnp.concatenate([o_p, o_s], axis=0)
            x_all = _out_projection([o], [w_out1[j].astype(BF16)], x_all, mod_pb, 2)
            outs.setdefault("cc_p", []).append(cc_p)
            outs.setdefault("cc_s", []).append(cc_s)
            outs.setdefault("cs_p", []).append(cs_p)
            outs.setdefault("cs_s", []).append(cs_s)
        x_all = _moe(x_all, norm2_g[l], mod_pb, router_w[l], router_bias[l], e_w_gate[l], e_w_up[l],
                     e_w_down[l], s_w_gate[l], s_w_up[l], s_w_down[l])

    y_prompt = _final_norm(x_all, final_g, 0, npb_p).reshape(Bp, Tp, d)
    y_sample = _final_norm(x_all, final_g, npb_p, npb_s).reshape(Bs, Ts, d)
    st = {k: jnp.stack(v) for k, v in outs.items()}
    return (y_prompt, y_sample, st["bk_p"], st["bk_s"], st["bv_p"], st["bv_s"], st["ac_p"], st["ac_s"],
            st["ah_p"], st["ah_s"], st["cc_p"], st["cc_s"], st["cs_p"], st["cs_s"])
```

```python
import functools

import jax
import jax.numpy as jnp
from jax import lax
from jax.experimental import pallas as pl
from jax.experimental.pallas import tpu as pltpu

F32 = jnp.float32
BF16 = jnp.bfloat16
I32 = jnp.int32

D_MODEL = 1024
CHUNK = 64
NORM_EPS = 1e-6
CONV_W = 4
DA = 512
NB_A = 8
LRU_C = 8.0
HB = 8
DHB = 64
DB = HB * DHB
BAND_PAST = 8 * CHUNK
MAX_REL = 128
HC = 8
DK = 128
DV = 128
DCK = HC * DK
DCV = HC * DV
N_QKV = 2 * DCK + DCV
DELTA_BLOCK = 128
N_EXPERTS = 64
TOP_K = 8
N_GROUPS = 8
TOPK_GROUPS = 4
D_EXPERT = 256
D_SHARED = 256
ROUTE_SCALE = 2.5
IN0 = 2 * DA + 3 * DB
IN1 = N_QKV + DCV + 2 * HC

LANES = 128
SUBLANES = 8
PB = 32
IN1_PAD = N_QKV + DCV + LANES
VMEM_LIMIT = 48 * 1024 * 1024
HI = lax.Precision.HIGHEST


def _cparams(sem):
    return pltpu.CompilerParams(dimension_semantics=sem, vmem_limit_bytes=VMEM_LIMIT)


def _softplus(x):
    return jnp.maximum(x, 0.0) + jnp.log1p(jnp.exp(-jnp.abs(x)))


def _dot(a, b, precision=None):
    return jnp.dot(a, b, preferred_element_type=F32, precision=precision)


def _dot_nt(a, b, precision=None):
    return lax.dot_general(a, b, (((1,), (1,)), ((), ())), preferred_element_type=F32, precision=precision)


def _dot_tn(a, b, precision=None):
    return lax.dot_general(a, b, (((0,), (0,)), ((), ())), preferred_element_type=F32, precision=precision)


def _mod_kernel(c_ref, w_ref, b_ref, o_ref):
    c = c_ref[...]
    cond = c * jax.nn.sigmoid(c)
    o_ref[0] = _dot(cond.astype(BF16), w_ref[0].astype(BF16)) + b_ref[0]


def _modulation(c_all, w_mod, b_mod):
    depth, d, n6 = w_mod.shape
    nb = c_all.shape[0]
    tn = 1536
    return pl.pallas_call(
        _mod_kernel,
        grid=(depth, n6 // tn),
        in_specs=[pl.BlockSpec((nb, d), lambda l, j: (0, 0)),
                  pl.BlockSpec((1, d, tn), lambda l, j: (l, 0, j)),
                  pl.BlockSpec((1, 1, tn), lambda l, j: (l, 0, j))],
        out_specs=pl.BlockSpec((1, nb, tn), lambda l, j: (l, 0, j)),
        out_shape=jax.ShapeDtypeStruct((depth, nb, n6), F32),
        compiler_params=_cparams(("parallel", "parallel")),
        name="modulation",
    )(c_all, w_mod, b_mod.reshape(depth, 1, n6))


def _norm_mod(x3, g, sc, sh):
    ms = jnp.mean(x3 * x3, axis=-1, keepdims=True)
    y = x3 * lax.rsqrt(ms + NORM_EPS) * g
    h = y * (1.0 + sc) + sh
    return h.reshape(x3.shape[0] * x3.shape[1], x3.shape[2])


def _inproj_kernel(x_ref, g_ref, sc_ref, sh_ref, w_ref, o_ref, h_scr):
    @pl.when(pl.program_id(1) == 0)
    def _():
        h_scr[...] = _norm_mod(x_ref[...], g_ref[...], sc_ref[...], sh_ref[...]).astype(BF16)
    o_ref[...] = _dot(h_scr[...], w_ref[...])


def _in_projection(x_all, g, mod_pb, sc_col, sh_col, w_bf16, tn):
    npb, pb, d = x_all.shape
    nout = w_bf16.shape[1]
    bB = 32
    tm = bB * pb
    return pl.pallas_call(
        _inproj_kernel,
        grid=(npb // bB, nout // tn),
        in_specs=[pl.BlockSpec((bB, pb, d), lambda i, j: (i, 0, 0)),
                  pl.BlockSpec((1, d), lambda i, j: (0, 0)),
                  pl.BlockSpec((bB, 1, d), lambda i, j: (i, 0, sc_col)),
                  pl.BlockSpec((bB, 1, d), lambda i, j: (i, 0, sh_col)),
                  pl.BlockSpec((d, tn), lambda i, j: (0, j))],
        out_specs=pl.BlockSpec((tm, tn), lambda i, j: (i, j)),
        out_shape=jax.ShapeDtypeStruct((npb * pb, nout), F32),
        scratch_shapes=[pltpu.VMEM((tm, d), BF16)],
        compiler_params=_cparams(("parallel", "arbitrary")),
        name="in_projection",
    )(x_all, g.reshape(1, d), mod_pb, mod_pb, w_bf16)


def _rglru_kernel(xa_ref, ga_ref, cb_ref, h0_ref, cw_ref, cbias_ref, wr_ref, br_ref, wi_ref, bi_ref, lam_ref,
                  y_ref, cbo_ref, ho_ref, xbuf, hcar):
    tT = xa_ref.shape[0]

    @pl.when(pl.program_id(1) == 0)
    def _():
        xbuf[0:SUBLANES, :] = jnp.zeros((SUBLANES, DA), F32)
        xbuf[SUBLANES - (CONV_W - 1):SUBLANES, :] = cb_ref[...]
        hcar[...] = h0_ref[...]

    xa = xa_ref[...]
    xbuf[SUBLANES:SUBLANES + tT, :] = xa
    xc = cw_ref[3:4, :] * xa + cbias_ref[...]
    for j in range(CONV_W - 1):
        xc = xc + cw_ref[j:j + 1, :] * xbuf[SUBLANES - 3 + j:SUBLANES - 3 + j + tT, :]
    tail = xbuf[tT:tT + SUBLANES, :]
    xbuf[0:SUBLANES, :] = tail
    cbo_ref[...] = tail[SUBLANES - (CONV_W - 1):, :]

    xcb = xc.astype(BF16)
    r = jax.nn.sigmoid(_dot(xcb, wr_ref[...]) + br_ref[...])
    i = jax.nn.sigmoid(_dot(xcb, wi_ref[...]) + bi_ref[...])
    log_a = -LRU_C * r * _softplus(-lam_ref[...])
    a = jnp.exp(log_a)
    th = jnp.tanh(log_a)
    u = jnp.sqrt(-2.0 * th / (1.0 - th)) * (i * xc)

    rows = lax.broadcasted_iota(I32, (tT, DA), 0)
    s = 1
    while s < tT:
        a_sh = pltpu.roll(a, s, 0)
        u_sh = pltpu.roll(u, s, 0)
        m = rows >= s
        u = jnp.where(m, a * u_sh + u, u)
        a = jnp.where(m, a * a_sh, a)
        s *= 2
    h = a * hcar[...] + u
    hlast = h[tT - 1:tT, :]
    hcar[...] = hlast
    ho_ref[...] = hlast
    y_ref[...] = (h * jax.nn.gelu(ga_ref[...])).astype(BF16)


def _rglru(proj, row0, B, T, tT, conv_buf, h0, cw, cbias, wr, br, wi, bi, lam):
    nT = T // tT
    rb0 = row0 // tT
    vec = lambda: pl.BlockSpec((1, DA), lambda b, t: (0, 0))
    return pl.pallas_call(
        _rglru_kernel,
        grid=(B, nT),
        in_specs=[pl.BlockSpec((tT, DA), lambda b, t: (rb0 + b * nT + t, 0)),
                  pl.BlockSpec((tT, DA), lambda b, t: (rb0 + b * nT + t, 1)),
                  pl.BlockSpec((None, CONV_W - 1, DA), lambda b, t: (b, 0, 0)),
                  pl.BlockSpec((None, 1, DA), lambda b, t: (b, 0, 0)),
                  pl.BlockSpec((CONV_W, DA), lambda b, t: (0, 0)),
                  vec(),
                  pl.BlockSpec((DA, DA), lambda b, t: (0, 0)),
                  vec(),
                  pl.BlockSpec((DA, DA), lambda b, t: (0, 0)),
                  vec(), vec()],
        out_specs=[pl.BlockSpec((tT, DA), lambda b, t: (b * nT + t, 0)),
                   pl.BlockSpec((None, CONV_W - 1, DA), lambda b, t: (b, 0, 0)),
                   pl.BlockSpec((None, 1, DA), lambda b, t: (b, 0, 0))],
        out_shape=[jax.ShapeDtypeStruct((B * T, DA), BF16),
                   jax.ShapeDtypeStruct((B, CONV_W - 1, DA), F32),
                   jax.ShapeDtypeStruct((B, 1, DA), F32)],
        scratch_shapes=[pltpu.VMEM((tT + SUBLANES, DA), F32), pltpu.VMEM((1, DA), F32)],
        compiler_params=_cparams(("parallel", "arbitrary")),
        name="rglru",
    )(proj, proj, conv_buf, h0.reshape(B, 1, DA), cw, cbias.reshape(1, DA), wr, br.reshape(1, DA),
      wi, bi.reshape(1, DA), lam.reshape(1, DA))


def _attn_kernel(*refs, chq, has_hist):
    if has_hist:
        q_ref, k_ref, v_ref, hk_ref, hv_ref, bias_ref, o_ref, ko_ref, vo_ref, kbuf, vbuf = refs
    else:
        q_ref, k_ref, v_ref, bias_ref, o_ref, ko_ref, vo_ref, kbuf, vbuf = refs
    c = pl.program_id(1)
    T = k_ref.shape[0]
    W = BAND_PAST + chq
    keep = ko_ref.shape[0]

    heads = range(HB)

    @pl.when(c == 0)
    def _():
        ko_ref[...] = k_ref[T - keep:T, :]
        vo_ref[...] = v_ref[T - keep:T, :]
        for h in heads:
            sl = slice(h * DHB, (h + 1) * DHB)
            if has_hist:
                kbuf[h, 0:BAND_PAST, :] = hk_ref[:, sl].astype(BF16)
                vbuf[h, 0:BAND_PAST, :] = hv_ref[:, sl].astype(BF16)
            else:
                kbuf[h, 0:BAND_PAST, :] = jnp.zeros((BAND_PAST, DHB), BF16)
                vbuf[h, 0:BAND_PAST, :] = jnp.zeros((BAND_PAST, DHB), BF16)
            kbuf[h, BAND_PAST:BAND_PAST + T, :] = k_ref[:, sl].astype(BF16)
            vbuf[h, BAND_PAST:BAND_PAST + T, :] = v_ref[:, sl].astype(BF16)

    start = pl.multiple_of(c * chq, chq)
    win = pl.ds(start, W)
    q = q_ref[...].astype(BF16)
    s = [_dot_nt(q[:, h * DHB:(h + 1) * DHB], kbuf[h, win, :]) * (DHB ** -0.5) + bias_ref[h] for h in heads]
    if not has_hist:
        valid = start - BAND_PAST + lax.broadcasted_iota(I32, (chq, W), 1) >= 0
        s = [jnp.where(valid, x, -jnp.inf) for x in s]
    m = [jnp.max(x, axis=-1, keepdims=True) for x in s]
    e = [jnp.exp(x - mx) for x, mx in zip(s, m)]
    den = [jnp.sum(x, axis=-1, keepdims=True) for x in e]
    o = [_dot(e[h].astype(BF16), vbuf[h, win, :]) / den[h] for h in heads]
    o_ref[...] = jnp.concatenate(o, axis=-1).astype(BF16)


def _band_attention(proj, row0, B, T, chq, bias, hist_k=None, hist_v=None):
    nC = T // chq
    rbq = row0 // chq
    rbk = row0 // T
    has_hist = hist_k is not None
    keep = min(BAND_PAST, T)
    in_specs = [pl.BlockSpec((chq, DB), lambda b, c: (rbq + b * nC + c, 2)),
                pl.BlockSpec((T, DB), lambda b, c: (rbk + b, 3)),
                pl.BlockSpec((T, DB), lambda b, c: (rbk + b, 4))]
    args = [proj, proj, proj]
    if has_hist:
        in_specs += [pl.BlockSpec((None, BAND_PAST, DB), lambda b, c: (b, 0, 0))] * 2
        args += [hist_k, hist_v]
    in_specs.append(pl.BlockSpec((HB, chq, BAND_PAST + chq), lambda b, c: (0, 0, 0)))
    args.append(bias)
    return pl.pallas_call(
        functools.partial(_attn_kernel, chq=chq, has_hist=has_hist),
        grid=(B, nC),
        in_specs=in_specs,
        out_specs=[pl.BlockSpec((chq, DB), lambda b, c: (b * nC + c, 0)),
                   pl.BlockSpec((None, keep, DB), lambda b, c: (b, 0, 0)),
                   pl.BlockSpec((None, keep, DB), lambda b, c: (b, 0, 0))],
        out_shape=[jax.ShapeDtypeStruct((B * T, DB), BF16),
                   jax.ShapeDtypeStruct((B, keep, DB), F32),
                   jax.ShapeDtypeStruct((B, keep, DB), F32)],
        scratch_shapes=[pltpu.VMEM((HB, BAND_PAST + T, DHB), BF16), pltpu.VMEM((HB, BAND_PAST + T, DHB), BF16)],
        compiler_params=_cparams(("parallel", "arbitrary")),
        name="band_attention",
    )(*args)


def _rel_bias(table, chq):
    W = BAND_PAST + chq
    lw = W + chq - 1
    n_clipped = BAND_PAST - MAX_REL + chq
    w = jnp.concatenate([jnp.broadcast_to(table[:, 2 * MAX_REL:], (HB, n_clipped)),
                         table[:, MAX_REL + 1 - chq:2 * MAX_REL][:, ::-1]], axis=1)
    flat = jnp.tile(jnp.pad(w, ((0, 0), (0, 1))), (1, chq))[:, :chq * lw]
    return flat.reshape(HB, chq, lw)[:, :, chq - 1:chq - 1 + W]


def _outproj_kernel(*refs, n_in):
    ys, ws = refs[:n_in], refs[n_in:2 * n_in]
    x_ref, gt_ref, o_ref = refs[2 * n_in:]
    acc = _dot(ys[0][...], ws[0][...])
    for y_ref, w_ref in zip(ys[1:], ws[1:]):
        acc = acc + _dot(y_ref[...], w_ref[...])
    bB, pb, d = x_ref.shape
    o_ref[...] = x_ref[...] + gt_ref[...] * acc.reshape(bB, pb, d)


def _out_projection(ys, ws, x_all, mod_pb, gt_col):
    npb, pb, d = x_all.shape
    bB = 32
    tm = bB * pb
    n_in = len(ys)
    in_specs = [pl.BlockSpec((tm, y.shape[1]), lambda i: (i, 0)) for y in ys]
    in_specs += [pl.BlockSpec(w.shape, lambda i: (0, 0)) for w in ws]
    in_specs += [pl.BlockSpec((bB, pb, d), lambda i: (i, 0, 0)),
                 pl.BlockSpec((bB, 1, d), lambda i: (i, 0, gt_col))]
    return pl.pallas_call(
        functools.partial(_outproj_kernel, n_in=n_in),
        grid=(npb // bB,),
        in_specs=in_specs,
        out_specs=pl.BlockSpec((bB, pb, d), lambda i: (i, 0, 0)),
        out_shape=jax.ShapeDtypeStruct(x_all.shape, F32),
        compiler_params=_cparams(("parallel",)),
        name="out_projection",
    )(*ys, *ws, x_all, mod_pb)


def _gdnpre_kernel(qkv_ref, ab_ref, cb_ref, cw_ref, alog_ref, dtb_ref,
                   q_ref, k_ref, v_ref, gb_ref, gt_ref, cbo_ref, xbuf, *, C):
    tT = qkv_ref.shape[0]

    @pl.when(pl.program_id(1) == 0)
    def _():
        xbuf[0:SUBLANES, :] = jnp.zeros((SUBLANES, N_QKV), F32)
        xbuf[SUBLANES - (CONV_W - 1):SUBLANES, :] = cb_ref[...]

    x = qkv_ref[...]
    xbuf[SUBLANES:SUBLANES + tT, :] = x
    xc = cw_ref[3:4, :] * x
    for j in range(CONV_W - 1):
        xc = xc + cw_ref[j:j + 1, :] * xbuf[SUBLANES - 3 + j:SUBLANES - 3 + j + tT, :]
    tail = xbuf[tT:tT + SUBLANES, :]
    xbuf[0:SUBLANES, :] = tail
    cbo_ref[...] = tail[SUBLANES - (CONV_W - 1):, :]

    xs = xc * jax.nn.sigmoid(xc)
    for h in range(HC):
        qh = xs[:, h * DK:(h + 1) * DK]
        q_ref[:, h * DK:(h + 1) * DK] = (qh * lax.rsqrt(jnp.sum(qh * qh, axis=-1, keepdims=True) + NORM_EPS)
                                         * (DK ** -0.5))
        kh = xs[:, DCK + h * DK:DCK + (h + 1) * DK]
        k_ref[:, h * DK:(h + 1) * DK] = kh * lax.rsqrt(jnp.sum(kh * kh, axis=-1, keepdims=True) + NORM_EPS)
    v_ref[...] = xs[:, 2 * DCK:]

    ab = ab_ref[...]
    g = -jnp.exp(alog_ref[...]) * _softplus(ab + dtb_ref[...])
    beta = jax.nn.sigmoid(ab)
    lane = lax.broadcasted_iota(I32, ab.shape, 1)
    g = jnp.where(lane < HC, g, 0.0)
    ri = lax.broadcasted_iota(I32, (C, C), 0)
    ci = lax.broadcasted_iota(I32, (C, C), 1)
    tril = (ri >= ci).astype(F32)
    triu = (ri <= ci).astype(F32)
    for c in range(tT // C):
        gc = g[c * C:(c + 1) * C, :]
        G = _dot(tril, gc, HI)
        lane_c = lax.broadcasted_iota(I32, (C, LANES), 1)
        gb_ref[c * C:(c + 1) * C, :] = jnp.where(lane_c < HC, G, beta[c * C:(c + 1) * C, :])
        gt_ref[c * 2 * HC:(c + 1) * 2 * HC, :] = _dot_tn(gc, triu, HI)[:2 * HC, :]


def _gdn_pre(proj, row0, B, T, tT, C, conv_buf, cw, alog_pad, dtb_pad):
    nT = T // tT
    rb0 = row0 // tT
    cpt = tT // C
    return pl.pallas_call(
        functools.partial(_gdnpre_kernel, C=C),
        grid=(B, nT),
        in_specs=[pl.BlockSpec((tT, N_QKV), lambda b, t: (rb0 + b * nT + t, 0)),
                  pl.BlockSpec((tT, LANES), lambda b, t: (rb0 + b * nT + t, (N_QKV + DCV) // LANES)),
                  pl.BlockSpec((None, CONV_W - 1, N_QKV), lambda b, t: (b, 0, 0)),
                  pl.BlockSpec((CONV_W, N_QKV), lambda b, t: (0, 0)),
                  pl.BlockSpec((1, LANES), lambda b, t: (0, 0)),
                  pl.BlockSpec((1, LANES), lambda b, t: (0, 0))],
        out_specs=[pl.BlockSpec((tT, DCK), lambda b, t: (b * nT + t, 0)),
                   pl.BlockSpec((tT, DCK), lambda b, t: (b * nT + t, 0)),
                   pl.BlockSpec((tT, DCV), lambda b, t: (b * nT + t, 0)),
                   pl.BlockSpec((tT, LANES), lambda b, t: (b * nT + t, 0)),
                   pl.BlockSpec((cpt * 2 * HC, C), lambda b, t: (b * nT + t, 0)),
                   pl.BlockSpec((None, CONV_W - 1, N_QKV), lambda b, t: (b, 0, 0))],
        out_shape=[jax.ShapeDtypeStruct((B * T, DCK), F32),
                   jax.ShapeDtypeStruct((B * T, DCK), F32),
                   jax.ShapeDtypeStruct((B * T, DCV), F32),
                   jax.ShapeDtypeStruct((B * T, LANES), F32),
                   jax.ShapeDtypeStruct((B * T // C * 2 * HC, C), F32),
                   jax.ShapeDtypeStruct((B, CONV_W - 1, N_QKV), F32)],
        scratch_shapes=[pltpu.VMEM((tT + SUBLANES, N_QKV), F32)],
        compiler_params=_cparams(("parallel", "arbitrary")),
        name="gdn_pre",
    )(proj, proj, conv_buf, cw, alog_pad, dtb_pad)


def _split_bf16(a):
    hi = a.astype(BF16)
    return hi, (a - hi.astype(F32)).astype(BF16)


def _dot3(a, b):
    return _dot(a[0], b[0]) + (_dot(a[0], b[1]) + _dot(a[1], b[0]))


def _delta_kernel(q_ref, k_ref, v_ref, gb_ref, gt_ref, z_ref, s0_ref, ng_ref, o_ref, so_ref,
                  s_scr, u_scr, w_scr, qd_scr, kd_scr, a_scr, *, C, hps, unroll):
    T = q_ref.shape[0]
    n_chunks = T // C
    ri = lax.broadcasted_iota(I32, (C, C), 0)
    ci = lax.broadcasted_iota(I32, (C, C), 1)
    causal = ri >= ci
    strict = ri > ci
    eye = (ri == ci).astype(F32)
    head0 = pl.program_id(1) * hps
    lane = lax.broadcasted_iota(I32, (C, LANES), 1)

    def pick(x, col):
        lane_x = lax.broadcasted_iota(I32, x.shape, 1)
        return jnp.sum(jnp.where(lane_x == col, x, 0.0), axis=1, keepdims=True)

    def operands(n, j):
        rows = pl.ds(pl.multiple_of(n * C, C), C)
        cs = slice(j * DK, (j + 1) * DK)
        gbc = gb_ref[rows, :]
        G = pick(gbc, head0 + j)
        beta = pick(gbc, head0 + j + HC)
        return rows, cs, G, beta

    def prepare(streams):
        def lower_left(m):
            s = m.bit_length() - 1
            return jnp.logical_and(((ri ^ ci) >> s) == 1, ((ri >> s) & 1) == 1)

        Bs, Ps = [], []
        for n, j in streams:
            rows, cs, G, beta = operands(n, j)
            g_row = gt_ref[pl.ds(n * (2 * HC) + head0 + j, 1), :]
            k = k_ref[rows, cs]
            L = jnp.where(causal, jnp.exp(jnp.where(causal, G - g_row, 0.0)), 0.0)
            qk = _dot_nt(jnp.concatenate([k * beta, q_ref[rows, cs]], axis=0).astype(BF16), k.astype(BF16))
            a_scr[j, rows, :] = (qk[C:] * L).astype(BF16)
            Bm = jnp.where(strict, -(qk[:C] * L), 0.0)
            Bs.append(Bm)
            Ps.append(eye + jnp.where(lower_left(1), Bm, 0.0))
        m = 2
        while m < C:
            Ds = [_split_bf16(p) for p in Ps]
            Ws = [_split_bf16(_dot3(_split_bf16(jnp.where(lower_left(m), b, 0.0)), d)) for b, d in zip(Bs, Ds)]
            Ps = [p + _dot3(d, w) for p, d, w in zip(Ps, Ds, Ws)]
            m *= 2
        for (n, j), P in zip(streams, Ps):
            rows, cs, G, beta = operands(n, j)
            k = k_ref[rows, cs]
            eG = jnp.exp(G)
            rhs = jnp.concatenate([v_ref[rows, cs] * beta, (k * beta) * eG], axis=1)
            X = _dot3(_split_bf16(P), _split_bf16(rhs))
            u_scr[j, rows, :] = X[:, :DV]
            w_scr[j, rows, :] = X[:, DV:].astype(BF16)
            qd_scr[j, rows, :] = (q_ref[rows, cs] * eG).astype(BF16)
            kd_scr[j, rows, :] = (k * jnp.exp(G[C - 1:C, :] - G)).astype(BF16)

    def prepare_body(i, carry):
        prepare([(i * unroll + m, j) for m in range(unroll) for j in range(hps)])
        return carry

    lax.fori_loop(0, n_chunks // unroll, prepare_body, 0)

    s_scr[...] = s0_ref[...]

    def advance(n, carry):
        r0 = pl.multiple_of(n * C, C)
        rows = pl.ds(r0, C)
        g_end = gb_ref[pl.ds(r0 + C - 1, 1), :]
        heads = range(hps)
        S = [s_scr[j] for j in heads]
        Sb = [s.astype(BF16) for s in S]
        wS = [_dot(w_scr[j, rows, :], Sb[j]) for j in heads]
        qS = [_dot(qd_scr[j, rows, :], Sb[j]) for j in heads]
        vnb = [(u_scr[j, rows, :] - wS[j]).astype(BF16) for j in heads]
        kv = [_dot_tn(kd_scr[j, rows, :], vnb[j]) for j in heads]
        av = [_dot(a_scr[j, rows, :], vnb[j]) for j in heads]
        for j in heads:
            cs = slice(j * DK, (j + 1) * DK)
            s_scr[j] = S[j] * jnp.exp(pick(g_end, head0 + j)) + kv[j]
            o = qS[j] + av[j]
            on = o * lax.rsqrt(jnp.mean(o * o, axis=-1, keepdims=True) + NORM_EPS) * ng_ref[...]
            z = z_ref[rows, cs]
            o_ref[rows, cs] = (on * (z * jax.nn.sigmoid(z))).astype(BF16)
        return carry

    lax.fori_loop(0, n_chunks, advance, 0)
    so_ref[...] = s_scr[...]


def _delta_rule(q, k, v, gb, gt, proj, row0, B, T, C, S0, norm_g, hps):
    rb = row0 // T
    wb = hps * DK
    zcol0 = N_QKV // wb
    n_chunks = T // C
    qkv_spec = lambda: pl.BlockSpec((T, wb), lambda b, h: (b, h))
    return pl.pallas_call(
        functools.partial(_delta_kernel, C=C, hps=hps, unroll=2 if n_chunks % 2 == 0 else 1),
        grid=(B, HC // hps),
        in_specs=[qkv_spec(), qkv_spec(), qkv_spec(),
                  pl.BlockSpec((T, LANES), lambda b, h: (b, 0)),
                  pl.BlockSpec((n_chunks * 2 * HC, C), lambda b, h: (b, 0)),
                  pl.BlockSpec((T, wb), lambda b, h: (rb + b, zcol0 + h)),
                  pl.BlockSpec((None, hps, DK, DV), lambda b, h: (b, h, 0, 0)),
                  pl.BlockSpec((1, DV), lambda b, h: (0, 0))],
        out_specs=[pl.BlockSpec((T, wb), lambda b, h: (b, h)),
                   pl.BlockSpec((None, hps, DK, DV), lambda b, h: (b, h, 0, 0))],
        out_shape=[jax.ShapeDtypeStruct((B * T, DCV), BF16),
                   jax.ShapeDtypeStruct((B, HC, DK, DV), F32)],
        scratch_shapes=[pltpu.VMEM((hps, DK, DV), F32),
                        pltpu.VMEM((hps, T, DV), F32),
                        pltpu.VMEM((hps, T, DK), BF16),
                        pltpu.VMEM((hps, T, DK), BF16),
                        pltpu.VMEM((hps, T, DK), BF16),
                        pltpu.VMEM((hps, T, C), BF16)],
        compiler_params=_cparams(("parallel", "parallel")),
        name="delta_rule",
    )(q, k, v, gb, gt, proj, S0, norm_g.reshape(1, DV))


def _first_argmax(x, idx, size):
    m = jnp.max(x, axis=0, keepdims=True)
    am = jnp.min(jnp.where(x == m, idx, size), axis=0, keepdims=True)
    return m, am


def _router_kernel(x_ref, g_ref, sc_ref, sh_ref, rwt_ref, rb_ref,
                   h_ref, e_ref, w_ref, r_ref, cnt_ref, run_scr):
    tn = h_ref.shape[0]
    gsz = N_EXPERTS // N_GROUPS

    @pl.when(pl.program_id(0) == 0)
    def _():
        run_scr[...] = jnp.zeros_like(run_scr)

    h = _norm_mod(x_ref[...], g_ref[...], sc_ref[...], sh_ref[...])
    h_ref[...] = h
    scores = jax.nn.sigmoid(_dot_nt(rwt_ref[...], h, HI))
    sel = scores + rb_ref[...]

    gidx = lax.broadcasted_iota(I32, (gsz, tn), 0)
    gs_rows = []
    for gi in range(N_GROUPS):
        blk = sel[gi * gsz:(gi + 1) * gsz, :]
        m1, a1 = _first_argmax(blk, gidx, gsz)
        m2 = jnp.max(jnp.where(gidx == a1, -jnp.inf, blk), axis=0, keepdims=True)
        gs_rows.append(m1 + m2)
    gscore = jnp.concatenate(gs_rows, axis=0)
    gi8 = lax.broadcasted_iota(I32, (N_GROUPS, tn), 0)
    gmask = jnp.zeros((N_GROUPS, tn), jnp.bool_)
    gwork = gscore
    for _ in range(TOPK_GROUPS):
        _, a = _first_argmax(gwork, gi8, N_GROUPS)
        hit = gi8 == a
        gmask = jnp.logical_or(gmask, hit)
        gwork = jnp.where(hit, -jnp.inf, gwork)
    emask = jnp.concatenate([jnp.broadcast_to(gmask[gi:gi + 1, :], (gsz, tn)) for gi in range(N_GROUPS)], axis=0)

    eidx = lax.broadcasted_iota(I32, (N_EXPERTS, tn), 0)
    work = jnp.where(emask, sel, -jnp.inf)
    hits = []
    chosen = jnp.zeros((N_EXPERTS, tn), jnp.bool_)
    for _ in range(TOP_K):
        _, a = _first_argmax(work, eidx, N_EXPERTS)
        hit = eidx == a
        hits.append((a, hit))
        chosen = jnp.logical_or(chosen, hit)
        work = jnp.where(hit, -jnp.inf, work)
    chosen_f = chosen.astype(F32)
    denom = jnp.sum(scores * chosen_f, axis=0, keepdims=True)

    ti = lax.broadcasted_iota(I32, (tn, tn), 0)
    tj = lax.broadcasted_iota(I32, (tn, tn), 1)
    before = (ti < tj).astype(BF16)
    rank = _dot(chosen_f.astype(BF16), before) + run_scr[...]
    run_scr[...] = run_scr[...] + jnp.sum(chosen_f, axis=1, keepdims=True)
    cnt_ref[...] = jnp.broadcast_to(run_scr[...], cnt_ref.shape).astype(I32)

    e_rows, w_rows, r_rows = [], [], []
    for a, hit in hits:
        hf = hit.astype(F32)
        e_rows.append(a)
        w_rows.append(jnp.sum(scores * hf, axis=0, keepdims=True) / denom * ROUTE_SCALE)
        r_rows.append(jnp.sum(rank * hf, axis=0, keepdims=True))
    e_ref[...] = jnp.concatenate(e_rows, axis=0)
    w_ref[...] = jnp.concatenate(w_rows, axis=0)
    r_ref[...] = jnp.concatenate(r_rows, axis=0).astype(I32)


def _router(x_all, g, mod_pb, sc_col, sh_col, rwt, rb):
    npb, pb, d = x_all.shape
    n = npb * pb
    bB = 16
    tn = bB * pb
    return pl.pallas_call(
        _router_kernel,
        grid=(npb // bB,),
        in_specs=[pl.BlockSpec((bB, pb, d), lambda i: (i, 0, 0)),
                  pl.BlockSpec((1, d), lambda i: (0, 0)),
                  pl.BlockSpec((bB, 1, d), lambda i: (i, 0, sc_col)),
                  pl.BlockSpec((bB, 1, d), lambda i: (i, 0, sh_col)),
                  pl.BlockSpec((N_EXPERTS, d), lambda i: (0, 0)),
                  pl.BlockSpec((N_EXPERTS, 1), lambda i: (0, 0))],
        out_specs=[pl.BlockSpec((tn, d), lambda i: (i, 0)),
                   pl.BlockSpec((TOP_K, tn), lambda i: (0, i)),
                   pl.BlockSpec((TOP_K, tn), lambda i: (0, i)),
                   pl.BlockSpec((TOP_K, tn), lambda i: (0, i)),
                   pl.BlockSpec((N_EXPERTS, LANES), lambda i: (0, 0))],
        out_shape=[jax.ShapeDtypeStruct((n, d), F32),
                   jax.ShapeDtypeStruct((TOP_K, n), I32),
                   jax.ShapeDtypeStruct((TOP_K, n), F32),
                   jax.ShapeDtypeStruct((TOP_K, n), I32),
                   jax.ShapeDtypeStruct((N_EXPERTS, LANES), I32)],
        scratch_shapes=[pltpu.VMEM((N_EXPERTS, 1), F32)],
        compiler_params=_cparams(("arbitrary",)),
        name="moe_router",
    )(x_all, g.reshape(1, d), mod_pb, mod_pb, rwt, rb.reshape(N_EXPERTS, 1))


PACK_ROWS = D_MODEL // LANES


def _row_slab(ref, r):
    return ref.at[pl.ds(pl.multiple_of(r * PACK_ROWS, PACK_ROWS), PACK_ROWS), :]


def _to_row_tiles(dst_ref, words, row0=0):
    rows = words.shape[0]
    for c in range(PACK_ROWS):
        dst_ref[pl.ds(row0 * PACK_ROWS + c, rows, stride=PACK_ROWS), :] = words[:, c * LANES:(c + 1) * LANES]


def _lane_tile(src_ref, c, rows, row0=0):
    return src_ref[pl.ds(row0 * PACK_ROWS + c, rows, stride=PACK_ROWS), :]


def _from_row_tiles(src_ref, rows):
    return jnp.concatenate([_lane_tile(src_ref, c, rows) for c in range(PACK_ROWS)], axis=1)


def _dispatch_kernel(zf_ref, pos_ref, h_ref, xs_hbm, sbuf, zbuf, sem, zsem, *, tm):
    i = pl.program_id(0)
    n_steps = pl.num_programs(0)
    td = h_ref.shape[0]
    n_tiles = zf_ref.shape[0]
    slot = lax.rem(i, 2)

    @pl.when(i == 0)
    def _():
        zbuf[...] = jnp.zeros_like(zbuf)

        def zero_copy(t):
            r0 = pl.multiple_of(t * (tm * PACK_ROWS), tm * PACK_ROWS)
            return pltpu.make_async_copy(zbuf, xs_hbm.at[pl.ds(r0, tm * PACK_ROWS), :], zsem)

        def start(t, carry):
            @pl.when(zf_ref[t] != 0)
            def _():
                zero_copy(t).start()
            return carry

        def wait(t, carry):
            @pl.when(zf_ref[t] != 0)
            def _():
                zero_copy(t).wait()
            return carry

        lax.fori_loop(0, n_tiles, start, 0)
        lax.fori_loop(0, n_tiles, wait, 0)

    def drain(s):
        for _ in range(TOP_K):
            pltpu.make_async_copy(sbuf.at[s], xs_hbm.at[pl.ds(0, td * PACK_ROWS), :], sem.at[s]).wait()

    @pl.when(i >= 2)
    def _():
        drain(slot)

    _to_row_tiles(sbuf.at[slot], h_ref[...])

    def body(j, carry):
        t0 = pl.multiple_of(j * SUBLANES, SUBLANES)
        for u in range(SUBLANES):
            src = _row_slab(sbuf.at[slot], t0 + u)
            for k in range(TOP_K):
                p = pos_ref[0, k * td + t0 + u]
                pltpu.make_async_copy(src, _row_slab(xs_hbm, p), sem.at[slot]).start(priority=k % 2)
        return carry

    lax.fori_loop(0, td // SUBLANES, body, 0)

    @pl.when(i == n_steps - 1)
    def _():
        drain(slot)
        drain(1 - slot)


def _moe_dispatch(h2, pos_tiles, zero_flag, n_tiles, tm, td):
    n, d = h2.shape
    grid_spec = pltpu.PrefetchScalarGridSpec(
        num_scalar_prefetch=1,
        grid=(n // td,),
        in_specs=[pl.BlockSpec((None, 1, TOP_K * td), lambda i, zf: (i, 0, 0), memory_space=pltpu.SMEM),
                  pl.BlockSpec((td, d), lambda i, zf: (i, 0))],
        out_specs=pl.BlockSpec(memory_space=pl.ANY),
        scratch_shapes=[pltpu.VMEM((2, td * PACK_ROWS, LANES), F32), pltpu.VMEM((tm * PACK_ROWS, LANES), F32),
                        pltpu.SemaphoreType.DMA((2,)), pltpu.SemaphoreType.DMA(())],
    )
    return pl.pallas_call(
        functools.partial(_dispatch_kernel, tm=tm),
        grid_spec=grid_spec,
        out_shape=jax.ShapeDtypeStruct((n_tiles * tm * PACK_ROWS, LANES), F32),
        compiler_params=_cparams(("arbitrary",)),
        name="moe_dispatch",
    )(zero_flag, pos_tiles, h2)


def _experts_kernel(te_ref, nu_ref, xs_ref, wg_ref, wu_ref, wd_ref, ys_ref, wgu_s, wd_s):
    i = pl.program_id(0)
    nu = nu_ref[0]

    @pl.when(jnp.logical_or(i == 0, te_ref[i] != te_ref[jnp.maximum(i - 1, 0)]))
    def _():
        wgu_s[:, :D_EXPERT] = wg_ref[...].astype(BF16)
        wgu_s[:, D_EXPERT:] = wu_ref[...].astype(BF16)
        wd_s[...] = wd_ref[...].astype(BF16)

    @pl.when(i < nu)
    def _():
        x = _from_row_tiles(xs_ref, xs_ref.shape[0] // PACK_ROWS).astype(BF16)
        gu = _dot(x, wgu_s[...])
        hid = jax.nn.silu(gu[:, :D_EXPERT]) * gu[:, D_EXPERT:]
        _to_row_tiles(ys_ref, _dot(hid.astype(BF16), wd_s[...]))

    @pl.when(i >= nu)
    def _():
        ys_ref[...] = jnp.zeros_like(ys_ref)


def _routed_experts(xs, tile_expert, n_used, layer, wg, wu, wd, tm):
    n_tiles = xs.shape[0] // (tm * PACK_ROWS)
    d = wg.shape[2]
    grid_spec = pltpu.PrefetchScalarGridSpec(
        num_scalar_prefetch=2,
        grid=(n_tiles,),
        in_specs=[pl.BlockSpec((tm * PACK_ROWS, LANES), lambda i, te, nu: (jnp.minimum(i, nu[0] - 1), 0)),
                  pl.BlockSpec((None, None, d, D_EXPERT), lambda i, te, nu: (layer, te[i], 0, 0)),
                  pl.BlockSpec((None, None, d, D_EXPERT), lambda i, te, nu: (layer, te[i], 0, 0)),
                  pl.BlockSpec((None, None, D_EXPERT, d), lambda i, te, nu: (layer, te[i], 0, 0))],
        out_specs=pl.BlockSpec((tm * PACK_ROWS, LANES), lambda i, te, nu: (i, 0)),
        scratch_shapes=[pltpu.VMEM((d, 2 * D_EXPERT), BF16), pltpu.VMEM((D_EXPERT, d), BF16)],
    )
    return pl.pallas_call(
        _experts_kernel,
        grid_spec=grid_spec,
        out_shape=jax.ShapeDtypeStruct(xs.shape, F32),
        compiler_params=_cparams(("arbitrary",)),
        name="moe_experts",
    )(tile_expert, n_used, xs, wg, wu, wd)


def _gather_rows(idx_ref, n_rows, src_hbm, dst, sem):
    group = 4 * SUBLANES

    def body(j, carry):
        r0 = pl.multiple_of(j * group, group)
        for u in range(group):
            pltpu.make_async_copy(_row_slab(src_hbm, idx_ref[0, r0 + u]), _row_slab(dst, r0 + u),
                                  sem).start(priority=u % 2)
        return carry
    lax.fori_loop(0, n_rows // group, body, 0)


def _combine_kernel(pos_ref, nxt_ref, ys_hbm, w_ref, h_ref, x_ref, gt_ref, sgu_ref, sd_ref, o_ref, gbuf, sem):
    i = pl.program_id(0)
    tb = w_ref.shape[0]
    slot = lax.rem(i, 2)

    @pl.when(i == 0)
    def _():
        _gather_rows(pos_ref, TOP_K * tb, ys_hbm, gbuf.at[0], sem.at[0])

    @pl.when(i + 1 < pl.num_programs(0))
    def _():
        _gather_rows(nxt_ref, TOP_K * tb, ys_hbm, gbuf.at[1 - slot], sem.at[1 - slot])

    hb = h_ref[...].astype(BF16)
    gu = _dot(hb, sgu_ref[...])
    hid = jax.nn.silu(gu[:, :D_SHARED]) * gu[:, D_SHARED:]
    shared = _dot(hid.astype(BF16), sd_ref[...])
    pltpu.make_async_copy(ys_hbm.at[pl.ds(0, TOP_K * tb * PACK_ROWS), :], gbuf.at[slot], sem.at[slot]).wait()
    w = w_ref[...]
    wk = [jnp.broadcast_to(w[:, k:k + 1], (tb, LANES)) for k in range(TOP_K)]
    gs = gbuf.at[slot]
    cols = []
    for c in range(PACK_ROWS):
        a = wk[0] * _lane_tile(gs, c, tb)
        for k in range(1, TOP_K):
            a = a + wk[k] * _lane_tile(gs, c, tb, k * tb)
        cols.append(a)
    acc = shared + jnp.concatenate(cols, axis=1)
    bB, pb, d = x_ref.shape
    o_ref[...] = x_ref[...] + gt_ref[...] * acc.reshape(bB, pb, d)


def _moe_combine(ys, pos_tiles, w_tok, h2, x_all, mod_pb, gt_col, sgu, sd, tb):
    npb, pb, d = x_all.shape
    n_tb = pos_tiles.shape[0]
    bB = tb // pb
    last = n_tb - 1
    return pl.pallas_call(
        _combine_kernel,
        grid=(n_tb,),
        in_specs=[pl.BlockSpec((None, 1, TOP_K * tb), lambda i: (i, 0, 0), memory_space=pltpu.SMEM),
                  pl.BlockSpec((None, 1, TOP_K * tb), lambda i: (jnp.minimum(i + 1, last), 0, 0),
                               memory_space=pltpu.SMEM),
                  pl.BlockSpec(memory_space=pl.ANY),
                  pl.BlockSpec((tb, TOP_K), lambda i: (i, 0)),
                  pl.BlockSpec((tb, d), lambda i: (i, 0)),
                  pl.BlockSpec((bB, pb, d), lambda i: (i, 0, 0)),
                  pl.BlockSpec((bB, 1, d), lambda i: (i, 0, gt_col)),
                  pl.BlockSpec(sgu.shape, lambda i: (0, 0)),
                  pl.BlockSpec(sd.shape, lambda i: (0, 0))],
        out_specs=pl.BlockSpec((bB, pb, d), lambda i: (i, 0, 0)),
        out_shape=jax.ShapeDtypeStruct(x_all.shape, F32),
        scratch_shapes=[pltpu.VMEM((2, TOP_K * tb * PACK_ROWS, LANES), F32), pltpu.SemaphoreType.DMA((2,))],
        compiler_params=_cparams(("arbitrary",)),
        name="moe_combine",
    )(pos_tiles, pos_tiles, ys, w_tok, h2, x_all, mod_pb, sgu, sd)


def _moe(x_all, g, mod_pb, rw, rb, layer, wg, wu, wd, sg, su, sd):
    npb, pb, d = x_all.shape
    n = npb * pb
    tm = 256
    tb = 128
    h2, e_t, w_t, r_t, cnt = _router(x_all, g, mod_pb, 4, 3, rw.T, rb)

    counts = cnt[:, 0]
    padded = (counts + tm - 1) // tm * tm
    ends = jnp.cumsum(padded)
    starts = ends - padded
    n_tiles = (n * TOP_K + N_EXPERTS * (tm - 1)) // tm
    onehot = e_t[:, :, None] == jnp.arange(N_EXPERTS, dtype=I32)
    pos = jnp.sum(jnp.where(onehot, starts, 0), axis=-1) + r_t
    n_used = (ends[-1] // tm).astype(I32)
    tile_idx = jnp.arange(n_tiles, dtype=I32)
    tile_start = jnp.minimum(tile_idx, n_used - 1) * tm
    tile_expert = jnp.sum(ends[None, :] <= tile_start[:, None], axis=1).astype(I32)
    is_seg_end = jnp.any((ends[None, :] == (tile_idx[:, None] + 1) * tm) & (padded[None, :] > 0), axis=1)
    zero_flag = jnp.logical_or(is_seg_end, tile_idx >= n_used).astype(I32)
    pos_tiles = pos.reshape(TOP_K, n // tb, tb).transpose(1, 0, 2).reshape(n // tb, 1, TOP_K * tb)

    xs = _moe_dispatch(h2, pos_tiles, zero_flag, n_tiles, tm, tb)
    ys = _routed_experts(xs, tile_expert, n_used.reshape(1), layer, wg, wu, wd, tm)
    sgu = jnp.concatenate([sg, su], axis=-1).astype(BF16)
    return _moe_combine(ys, pos_tiles, w_t.T, h2, x_all, mod_pb, 5, sgu, sd.astype(BF16), tb)


def _final_kernel(x_ref, g_ref, o_ref):
    x = x_ref[...]
    ms = jnp.mean(x * x, axis=-1, keepdims=True)
    o_ref[...] = x * lax.rsqrt(ms + NORM_EPS) * g_ref[...]


def _final_norm(x_all, g, pb0, n_pb):
    npb, pb, d = x_all.shape
    bB = 32
    return pl.pallas_call(
        _final_kernel,
        grid=(n_pb // bB,),
        in_specs=[pl.BlockSpec((bB, pb, d), lambda i: (pb0 // bB + i, 0, 0)),
                  pl.BlockSpec((1, d), lambda i: (0, 0))],
        out_specs=pl.BlockSpec((bB, pb, d), lambda i: (i, 0, 0)),
        out_shape=jax.ShapeDtypeStruct((n_pb, pb, d), F32),
        compiler_params=_cparams(("parallel",)),
        name="final_norm",
    )(x_all, g.reshape(1, d))


def _block_diag(w):
    nb, di, do = w.shape
    return jnp.einsum("nde,nm->ndme", w, jnp.eye(nb, dtype=w.dtype)).reshape(nb * di, nb * do)


def kernel(x_prompt, x_sample, c_prompt, c_sample, cache_b_k, cache_b_v, state_a_conv, state_a_h, state_c_conv, state_c_S, norm1_g, norm2_g, final_g, w_mod, b_mod, w_in0, w_out0, a_conv_w, a_conv_b, a_w_r, a_b_r, a_w_i, a_b_i, a_lambda, b_rel_bias, w_in1, w_out1, c_conv_w, c_A_log, c_dt_bias, c_norm_g, router_w, router_bias, e_w_gate, e_w_up, e_w_down, s_w_gate, s_w_up, s_w_down):
    Bp, Tp, d = x_prompt.shape
    Bs, Ts, _ = x_sample.shape
    np_tok, ns_tok = Bp * Tp, Bs * Ts
    npb_p, npb_s = np_tok // PB, ns_tok // PB
    depth = w_mod.shape[0]

    x_all = jnp.concatenate([x_prompt.reshape(npb_p, PB, d), x_sample.reshape(npb_s, PB, d)], axis=0)
    mod = _modulation(jnp.concatenate([c_prompt, c_sample], axis=0), w_mod, b_mod)

    outs = {}
    for l in range(depth):
        mod_pb = jnp.concatenate([jnp.repeat(mod[l, :Bp], Tp // PB, axis=0),
                                  jnp.repeat(mod[l, Bp:], Ts // PB, axis=0)], axis=0)[:, None, :]
        j = l // 2
        if l % 2 == 0:
            proj = _in_projection(x_all, norm1_g[l], mod_pb, 1, 0, w_in0[j].astype(BF16), 1280)
            wr, wi = _block_diag(a_w_r[j]).astype(BF16), _block_diag(a_w_i[j]).astype(BF16)
            a_args = (a_conv_w[j], a_conv_b[j], wr, a_b_r[j], wi, a_b_i[j], a_lambda[j])
            ya_p, ac_p, ah_p = _rglru(proj, 0, Bp, Tp, 512, jnp.zeros((Bp, CONV_W - 1, DA), F32),
                                      jnp.zeros((Bp, DA), F32), *a_args)
            ya_s, ac_s, ah_s = _rglru(proj, np_tok, Bs, Ts, Ts, state_a_conv[j], state_a_h[j], *a_args)
            yb_p, bk_p, bv_p = _band_attention(proj, 0, Bp, Tp, CHUNK, _rel_bias(b_rel_bias[j], CHUNK))
            yb_s, bk_s, bv_s = _band_attention(proj, np_tok, Bs, Ts, Ts, _rel_bias(b_rel_bias[j], Ts),
                                               cache_b_k[j].reshape(Bs, BAND_PAST, DB),
                                               cache_b_v[j].reshape(Bs, BAND_PAST, DB))
            ya = jnp.concatenate([ya_p, ya_s], axis=0)
            yb = jnp.concatenate([yb_p, yb_s], axis=0)
            w_out = w_out0[j].astype(BF16)
            x_all = _out_projection([ya, yb], [w_out[:DA], w_out[DA:]], x_all, mod_pb, 2)
            outs.setdefault("bk_p", []).append(bk_p.reshape(Bp, -1, HB, DHB))
            outs.setdefault("bv_p", []).append(bv_p.reshape(Bp, -1, HB, DHB))
            outs.setdefault("bk_s", []).append(bk_s.reshape(Bs, -1, HB, DHB))
            outs.setdefault("bv_s", []).append(bv_s.reshape(Bs, -1, HB, DHB))
            outs.setdefault("ac_p", []).append(ac_p)
            outs.setdefault("ac_s", []).append(ac_s)
            outs.setdefault("ah_p", []).append(ah_p.reshape(Bp, DA))
            outs.setdefault("ah_s", []).append(ah_s.reshape(Bs, DA))
        else:
            w_in = jnp.pad(w_in1[j], ((0, 0), (0, IN1_PAD - IN1))).astype(BF16)
            proj = _in_projection(x_all, norm1_g[l], mod_pb, 1, 0, w_in, 1408)
            alog = jnp.pad(c_A_log[j], (0, LANES - HC)).reshape(1, LANES)
            dtb = jnp.pad(c_dt_bias[j], (0, LANES - HC)).reshape(1, LANES)
            cp, cs = min(DELTA_BLOCK, Tp), min(DELTA_BLOCK, Ts)
            q_p, k_p, v_p, gb_p, gt_p, cc_p = _gdn_pre(proj, 0, Bp, Tp, 256, cp,
                                                       jnp.zeros((Bp, CONV_W - 1, N_QKV), F32), c_conv_w[j], alog, dtb)
            q_s, k_s, v_s, gb_s, gt_s, cc_s = _gdn_pre(proj, np_tok, Bs, Ts, Ts, cs, state_c_conv[j], c_conv_w[j],
                                                       alog, dtb)
            o_p, cs_p = _delta_rule(q_p, k_p, v_p, gb_p, gt_p, proj, 0, Bp, Tp, cp,
                                    jnp.zeros((Bp, HC, DK, DV), F32), c_norm_g[j], 2)
            o_s, cs_s = _delta_rule(q_s, k_s, v_s, gb_s, gt_s, proj, np_tok, Bs, Ts, cs, state_c_S[j],
                                    c_norm_g[j], 4)
            o = jnp.concatenate([o_p, o_s], axis=0)
            x_all = _out_projection([o], [w_out1[j].astype(BF16)], x_all, mod_pb, 2)
            outs.setdefault("cc_p", []).append(cc_p)
            outs.setdefault("cc_s", []).append(cc_s)
            outs.setdefault("cs_p", []).append(cs_p)
            outs.setdefault("cs_s", []).append(cs_s)
        x_all = _moe(x_all, norm2_g[l], mod_pb, router_w[l], router_bias[l], l, e_w_gate, e_w_up, e_w_down,
                     s_w_gate[l], s_w_up[l], s_w_down[l])

    y_prompt = _final_norm(x_all, final_g, 0, npb_p).reshape(Bp, Tp, d)
    y_sample = _final_norm(x_all, final_g, npb_p, npb_s).reshape(Bs, Ts, d)
    st = {k: jnp.stack(v) for k, v in outs.items()}
    return (y_prompt, y_sample, st["bk_p"], st["bk_s"], st["bv_p"], st["bv_s"], st["ac_p"], st["ac_s"],
            st["ah_p"], st["ah_s"], st["cc_p"], st["cc_s"], st["cs_p"], st["cs_s"])
```

```python
import functools

import jax
import jax.numpy as jnp
from jax import lax
from jax.experimental import pallas as pl
from jax.experimental.pallas import tpu as pltpu

F32 = jnp.float32
BF16 = jnp.bfloat16
I32 = jnp.int32

D_MODEL = 1024
CHUNK = 64
NORM_EPS = 1e-6
CONV_W = 4
DA = 512
NB_A = 8
LRU_C = 8.0
HB = 8
DHB = 64
DB = HB * DHB
BAND_PAST = 8 * CHUNK
MAX_REL = 128
HC = 8
DK = 128
DV = 128
DCK = HC * DK
DCV = HC * DV
N_QKV = 2 * DCK + DCV
DELTA_BLOCK = 128
N_EXPERTS = 64
TOP_K = 8
N_GROUPS = 8
TOPK_GROUPS = 4
D_EXPERT = 256
D_SHARED = 256
ROUTE_SCALE = 2.5
IN0 = 2 * DA + 3 * DB
IN1 = N_QKV + DCV + 2 * HC

LANES = 128
SUBLANES = 8
PB = 32
IN1_PAD = N_QKV + DCV + LANES
VMEM_LIMIT = 48 * 1024 * 1024
HI = lax.Precision.HIGHEST


def _cparams(sem):
    return pltpu.CompilerParams(dimension_semantics=sem, vmem_limit_bytes=VMEM_LIMIT)


def _softplus(x):
    return jnp.maximum(x, 0.0) + jnp.log1p(jnp.exp(-jnp.abs(x)))


def _dot(a, b, precision=None):
    return jnp.dot(a, b, preferred_element_type=F32, precision=precision)


def _dot_nt(a, b, precision=None):
    return lax.dot_general(a, b, (((1,), (1,)), ((), ())), preferred_element_type=F32, precision=precision)


def _dot_tn(a, b, precision=None):
    return lax.dot_general(a, b, (((0,), (0,)), ((), ())), preferred_element_type=F32, precision=precision)


def _mod_kernel(c_ref, w_ref, b_ref, o_ref):
    c = c_ref[...]
    cond = c * jax.nn.sigmoid(c)
    o_ref[0] = _dot(cond.astype(BF16), w_ref[0].astype(BF16)) + b_ref[0]


def _modulation(c_all, w_mod, b_mod):
    depth, d, n6 = w_mod.shape
    nb = c_all.shape[0]
    tn = 1536
    return pl.pallas_call(
        _mod_kernel,
        grid=(depth, n6 // tn),
        in_specs=[pl.BlockSpec((nb, d), lambda l, j: (0, 0)),
                  pl.BlockSpec((1, d, tn), lambda l, j: (l, 0, j)),
                  pl.BlockSpec((1, 1, tn), lambda l, j: (l, 0, j))],
        out_specs=pl.BlockSpec((1, nb, tn), lambda l, j: (l, 0, j)),
        out_shape=jax.ShapeDtypeStruct((depth, nb, n6), F32),
        compiler_params=_cparams(("parallel", "parallel")),
        name="modulation",
    )(c_all, w_mod, b_mod.reshape(depth, 1, n6))


def _norm_mod(x3, g, sc, sh):
    ms = jnp.mean(x3 * x3, axis=-1, keepdims=True)
    y = x3 * lax.rsqrt(ms + NORM_EPS) * g
    h = y * (1.0 + sc) + sh
    return h.reshape(x3.shape[0] * x3.shape[1], x3.shape[2])


def _inproj_kernel(x_ref, g_ref, sc_ref, sh_ref, w_ref, o_ref, h_scr):
    @pl.when(pl.program_id(1) == 0)
    def _():
        h_scr[...] = _norm_mod(x_ref[...], g_ref[...], sc_ref[...], sh_ref[...]).astype(BF16)
    o_ref[...] = _dot(h_scr[...], w_ref[...])


def _in_projection(x_all, g, mod_pb, sc_col, sh_col, w_bf16, tn):
    npb, pb, d = x_all.shape
    nout = w_bf16.shape[1]
    bB = 32
    tm = bB * pb
    return pl.pallas_call(
        _inproj_kernel,
        grid=(npb // bB, nout // tn),
        in_specs=[pl.BlockSpec((bB, pb, d), lambda i, j: (i, 0, 0)),
                  pl.BlockSpec((1, d), lambda i, j: (0, 0)),
                  pl.BlockSpec((bB, 1, d), lambda i, j: (i, 0, sc_col)),
                  pl.BlockSpec((bB, 1, d), lambda i, j: (i, 0, sh_col)),
                  pl.BlockSpec((d, tn), lambda i, j: (0, j))],
        out_specs=pl.BlockSpec((tm, tn), lambda i, j: (i, j)),
        out_shape=jax.ShapeDtypeStruct((npb * pb, nout), F32),
        scratch_shapes=[pltpu.VMEM((tm, d), BF16)],
        compiler_params=_cparams(("parallel", "arbitrary")),
        name="in_projection",
    )(x_all, g.reshape(1, d), mod_pb, mod_pb, w_bf16)


def _rglru_kernel(xa_ref, ga_ref, cb_ref, h0_ref, cw_ref, cbias_ref, wr_ref, br_ref, wi_ref, bi_ref, lam_ref,
                  y_ref, cbo_ref, ho_ref, xbuf, hcar):
    tT = xa_ref.shape[0]

    @pl.when(pl.program_id(1) == 0)
    def _():
        xbuf[0:SUBLANES, :] = jnp.zeros((SUBLANES, DA), F32)
        xbuf[SUBLANES - (CONV_W - 1):SUBLANES, :] = cb_ref[...]
        hcar[...] = h0_ref[...]

    xa = xa_ref[...]
    xbuf[SUBLANES:SUBLANES + tT, :] = xa
    xc = cw_ref[3:4, :] * xa + cbias_ref[...]
    for j in range(CONV_W - 1):
        xc = xc + cw_ref[j:j + 1, :] * xbuf[SUBLANES - 3 + j:SUBLANES - 3 + j + tT, :]
    tail = xbuf[tT:tT + SUBLANES, :]
    xbuf[0:SUBLANES, :] = tail
    cbo_ref[...] = tail[SUBLANES - (CONV_W - 1):, :]

    xcb = xc.astype(BF16)
    r = jax.nn.sigmoid(_dot(xcb, wr_ref[...]) + br_ref[...])
    i = jax.nn.sigmoid(_dot(xcb, wi_ref[...]) + bi_ref[...])
    log_a = -LRU_C * r * _softplus(-lam_ref[...])
    a = jnp.exp(log_a)
    th = jnp.tanh(log_a)
    u = jnp.sqrt(-2.0 * th / (1.0 - th)) * (i * xc)

    rows = lax.broadcasted_iota(I32, (tT, DA), 0)
    s = 1
    while s < tT:
        a_sh = pltpu.roll(a, s, 0)
        u_sh = pltpu.roll(u, s, 0)
        m = rows >= s
        u = jnp.where(m, a * u_sh + u, u)
        a = jnp.where(m, a * a_sh, a)
        s *= 2
    h = a * hcar[...] + u
    hlast = h[tT - 1:tT, :]
    hcar[...] = hlast
    ho_ref[...] = hlast
    y_ref[...] = (h * jax.nn.gelu(ga_ref[...])).astype(BF16)


def _rglru(proj, row0, B, T, tT, conv_buf, h0, cw, cbias, wr, br, wi, bi, lam):
    nT = T // tT
    rb0 = row0 // tT
    vec = lambda: pl.BlockSpec((1, DA), lambda b, t: (0, 0))
    return pl.pallas_call(
        _rglru_kernel,
        grid=(B, nT),
        in_specs=[pl.BlockSpec((tT, DA), lambda b, t: (rb0 + b * nT + t, 0)),
                  pl.BlockSpec((tT, DA), lambda b, t: (rb0 + b * nT + t, 1)),
                  pl.BlockSpec((None, CONV_W - 1, DA), lambda b, t: (b, 0, 0)),
                  pl.BlockSpec((None, 1, DA), lambda b, t: (b, 0, 0)),
                  pl.BlockSpec((CONV_W, DA), lambda b, t: (0, 0)),
                  vec(),
                  pl.BlockSpec((DA, DA), lambda b, t: (0, 0)),
                  vec(),
                  pl.BlockSpec((DA, DA), lambda b, t: (0, 0)),
                  vec(), vec()],
        out_specs=[pl.BlockSpec((tT, DA), lambda b, t: (b * nT + t, 0)),
                   pl.BlockSpec((None, CONV_W - 1, DA), lambda b, t: (b, 0, 0)),
                   pl.BlockSpec((None, 1, DA), lambda b, t: (b, 0, 0))],
        out_shape=[jax.ShapeDtypeStruct((B * T, DA), BF16),
                   jax.ShapeDtypeStruct((B, CONV_W - 1, DA), F32),
                   jax.ShapeDtypeStruct((B, 1, DA), F32)],
        scratch_shapes=[pltpu.VMEM((tT + SUBLANES, DA), F32), pltpu.VMEM((1, DA), F32)],
        compiler_params=_cparams(("parallel", "arbitrary")),
        name="rglru",
    )(proj, proj, conv_buf, h0.reshape(B, 1, DA), cw, cbias.reshape(1, DA), wr, br.reshape(1, DA),
      wi, bi.reshape(1, DA), lam.reshape(1, DA))


def _attn_kernel(*refs, chq, has_hist):
    if has_hist:
        q_ref, k_ref, v_ref, hk_ref, hv_ref, bias_ref, o_ref, ko_ref, vo_ref, kbuf, vbuf = refs
    else:
        q_ref, k_ref, v_ref, bias_ref, o_ref, ko_ref, vo_ref, kbuf, vbuf = refs
    c = pl.program_id(1)
    T = k_ref.shape[0]
    W = BAND_PAST + chq
    keep = ko_ref.shape[0]

    heads = range(HB)

    @pl.when(c == 0)
    def _():
        ko_ref[...] = k_ref[T - keep:T, :]
        vo_ref[...] = v_ref[T - keep:T, :]
        for h in heads:
            sl = slice(h * DHB, (h + 1) * DHB)
            if has_hist:
                kbuf[h, 0:BAND_PAST, :] = hk_ref[:, sl].astype(BF16)
                vbuf[h, 0:BAND_PAST, :] = hv_ref[:, sl].astype(BF16)
            else:
                kbuf[h, 0:BAND_PAST, :] = jnp.zeros((BAND_PAST, DHB), BF16)
                vbuf[h, 0:BAND_PAST, :] = jnp.zeros((BAND_PAST, DHB), BF16)
            kbuf[h, BAND_PAST:BAND_PAST + T, :] = k_ref[:, sl].astype(BF16)
            vbuf[h, BAND_PAST:BAND_PAST + T, :] = v_ref[:, sl].astype(BF16)

    start = pl.multiple_of(c * chq, chq)
    win = pl.ds(start, W)
    q = q_ref[...].astype(BF16)
    s = [_dot_nt(q[:, h * DHB:(h + 1) * DHB], kbuf[h, win, :]) * (DHB ** -0.5) + bias_ref[h] for h in heads]
    if not has_hist:
        valid = start - BAND_PAST + lax.broadcasted_iota(I32, (chq, W), 1) >= 0
        s = [jnp.where(valid, x, -jnp.inf) for x in s]
    m = [jnp.max(x, axis=-1, keepdims=True) for x in s]
    e = [jnp.exp(x - mx) for x, mx in zip(s, m)]
    den = [jnp.sum(x, axis=-1, keepdims=True) for x in e]
    o = [_dot(e[h].astype(BF16), vbuf[h, win, :]) / den[h] for h in heads]
    o_ref[...] = jnp.concatenate(o, axis=-1).astype(BF16)


def _band_attention(proj, row0, B, T, chq, bias, hist_k=None, hist_v=None):
    nC = T // chq
    rbq = row0 // chq
    rbk = row0 // T
    has_hist = hist_k is not None
    keep = min(BAND_PAST, T)
    in_specs = [pl.BlockSpec((chq, DB), lambda b, c: (rbq + b * nC + c, 2)),
                pl.BlockSpec((T, DB), lambda b, c: (rbk + b, 3)),
                pl.BlockSpec((T, DB), lambda b, c: (rbk + b, 4))]
    args = [proj, proj, proj]
    if has_hist:
        in_specs += [pl.BlockSpec((None, BAND_PAST, DB), lambda b, c: (b, 0, 0))] * 2
        args += [hist_k, hist_v]
    in_specs.append(pl.BlockSpec((HB, chq, BAND_PAST + chq), lambda b, c: (0, 0, 0)))
    args.append(bias)
    return pl.pallas_call(
        functools.partial(_attn_kernel, chq=chq, has_hist=has_hist),
        grid=(B, nC),
        in_specs=in_specs,
        out_specs=[pl.BlockSpec((chq, DB), lambda b, c: (b * nC + c, 0)),
                   pl.BlockSpec((None, keep, DB), lambda b, c: (b, 0, 0)),
                   pl.BlockSpec((None, keep, DB), lambda b, c: (b, 0, 0))],
        out_shape=[jax.ShapeDtypeStruct((B * T, DB), BF16),
                   jax.ShapeDtypeStruct((B, keep, DB), F32),
                   jax.ShapeDtypeStruct((B, keep, DB), F32)],
        scratch_shapes=[pltpu.VMEM((HB, BAND_PAST + T, DHB), BF16), pltpu.VMEM((HB, BAND_PAST + T, DHB), BF16)],
        compiler_params=_cparams(("parallel", "arbitrary")),
        name="band_attention",
    )(*args)


def _rel_bias(table, chq):
    W = BAND_PAST + chq
    lw = W + chq - 1
    n_clipped = BAND_PAST - MAX_REL + chq
    w = jnp.concatenate([jnp.broadcast_to(table[:, 2 * MAX_REL:], (HB, n_clipped)),
                         table[:, MAX_REL + 1 - chq:2 * MAX_REL][:, ::-1]], axis=1)
    flat = jnp.tile(jnp.pad(w, ((0, 0), (0, 1))), (1, chq))[:, :chq * lw]
    return flat.reshape(HB, chq, lw)[:, :, chq - 1:chq - 1 + W]


def _outproj_kernel(*refs, n_in):
    ys, ws = refs[:n_in], refs[n_in:2 * n_in]
    x_ref, gt_ref, o_ref = refs[2 * n_in:]
    acc = _dot(ys[0][...], ws[0][...])
    for y_ref, w_ref in zip(ys[1:], ws[1:]):
        acc = acc + _dot(y_ref[...], w_ref[...])
    bB, pb, d = x_ref.shape
    o_ref[...] = x_ref[...] + gt_ref[...] * acc.reshape(bB, pb, d)


def _out_projection(ys, ws, x_all, mod_pb, gt_col):
    npb, pb, d = x_all.shape
    bB = 32
    tm = bB * pb
    n_in = len(ys)
    in_specs = [pl.BlockSpec((tm, y.shape[1]), lambda i: (i, 0)) for y in ys]
    in_specs += [pl.BlockSpec(w.shape, lambda i: (0, 0)) for w in ws]
    in_specs += [pl.BlockSpec((bB, pb, d), lambda i: (i, 0, 0)),
                 pl.BlockSpec((bB, 1, d), lambda i: (i, 0, gt_col))]
    return pl.pallas_call(
        functools.partial(_outproj_kernel, n_in=n_in),
        grid=(npb // bB,),
        in_specs=in_specs,
        out_specs=pl.BlockSpec((bB, pb, d), lambda i: (i, 0, 0)),
        out_shape=jax.ShapeDtypeStruct(x_all.shape, F32),
        compiler_params=_cparams(("parallel",)),
        name="out_projection",
    )(*ys, *ws, x_all, mod_pb)


def _gdnpre_kernel(qkv_ref, ab_ref, cb_ref, cw_ref, alog_ref, dtb_ref,
                   q_ref, k_ref, v_ref, gb_ref, gt_ref, cbo_ref, xbuf, *, C):
    tT = qkv_ref.shape[0]

    @pl.when(pl.program_id(1) == 0)
    def _():
        xbuf[0:SUBLANES, :] = jnp.zeros((SUBLANES, N_QKV), F32)
        xbuf[SUBLANES - (CONV_W - 1):SUBLANES, :] = cb_ref[...]

    for cb in range(N_QKV // DK):
        cols = slice(cb * DK, (cb + 1) * DK)
        x = qkv_ref[:, cols]
        xbuf[SUBLANES:SUBLANES + tT, cols] = x
        xc = cw_ref[3:4, cols] * x
        for j in range(CONV_W - 1):
            xc = xc + cw_ref[j:j + 1, cols] * xbuf[SUBLANES - 3 + j:SUBLANES - 3 + j + tT, cols]
        xs = xc * jax.nn.sigmoid(xc)
        if cb < 2 * HC:
            xn = xs * lax.rsqrt(jnp.sum(xs * xs, axis=-1, keepdims=True) + NORM_EPS)
            if cb < HC:
                q_ref[:, cols] = xn * (DK ** -0.5)
            else:
                k_ref[:, (cb - HC) * DK:(cb - HC + 1) * DK] = xn
        else:
            v_ref[:, (cb - 2 * HC) * DV:(cb - 2 * HC + 1) * DV] = xs
    tail = xbuf[tT:tT + SUBLANES, :]
    xbuf[0:SUBLANES, :] = tail
    cbo_ref[...] = tail[SUBLANES - (CONV_W - 1):, :]

    ab = ab_ref[...]
    g = -jnp.exp(alog_ref[...]) * _softplus(ab + dtb_ref[...])
    beta = jax.nn.sigmoid(ab)
    lane = lax.broadcasted_iota(I32, ab.shape, 1)
    g = jnp.where(lane < HC, g, 0.0)
    ri = lax.broadcasted_iota(I32, (C, C), 0)
    ci = lax.broadcasted_iota(I32, (C, C), 1)
    tril = (ri >= ci).astype(F32)
    triu = (ri <= ci).astype(F32)
    for c in range(tT // C):
        gc = g[c * C:(c + 1) * C, :]
        G = _dot(tril, gc, HI)
        lane_c = lax.broadcasted_iota(I32, (C, LANES), 1)
        gb_ref[c * C:(c + 1) * C, :] = jnp.where(lane_c < HC, G, beta[c * C:(c + 1) * C, :])
        gt_ref[c * 2 * HC:(c + 1) * 2 * HC, :] = _dot_tn(gc, triu, HI)[:2 * HC, :]


def _gdn_pre(proj, row0, B, T, tT, C, conv_buf, cw, alog_pad, dtb_pad):
    nT = T // tT
    rb0 = row0 // tT
    cpt = tT // C
    return pl.pallas_call(
        functools.partial(_gdnpre_kernel, C=C),
        grid=(B, nT),
        in_specs=[pl.BlockSpec((tT, N_QKV), lambda b, t: (rb0 + b * nT + t, 0)),
                  pl.BlockSpec((tT, LANES), lambda b, t: (rb0 + b * nT + t, (N_QKV + DCV) // LANES)),
                  pl.BlockSpec((None, CONV_W - 1, N_QKV), lambda b, t: (b, 0, 0)),
                  pl.BlockSpec((CONV_W, N_QKV), lambda b, t: (0, 0)),
                  pl.BlockSpec((1, LANES), lambda b, t: (0, 0)),
                  pl.BlockSpec((1, LANES), lambda b, t: (0, 0))],
        out_specs=[pl.BlockSpec((tT, DCK), lambda b, t: (b * nT + t, 0)),
                   pl.BlockSpec((tT, DCK), lambda b, t: (b * nT + t, 0)),
                   pl.BlockSpec((tT, DCV), lambda b, t: (b * nT + t, 0)),
                   pl.BlockSpec((tT, LANES), lambda b, t: (b * nT + t, 0)),
                   pl.BlockSpec((cpt * 2 * HC, C), lambda b, t: (b * nT + t, 0)),
                   pl.BlockSpec((None, CONV_W - 1, N_QKV), lambda b, t: (b, 0, 0))],
        out_shape=[jax.ShapeDtypeStruct((B * T, DCK), F32),
                   jax.ShapeDtypeStruct((B * T, DCK), F32),
                   jax.ShapeDtypeStruct((B * T, DCV), F32),
                   jax.ShapeDtypeStruct((B * T, LANES), F32),
                   jax.ShapeDtypeStruct((B * T // C * 2 * HC, C), F32),
                   jax.ShapeDtypeStruct((B, CONV_W - 1, N_QKV), F32)],
        scratch_shapes=[pltpu.VMEM((tT + SUBLANES, N_QKV), F32)],
        compiler_params=_cparams(("parallel", "arbitrary")),
        name="gdn_pre",
    )(proj, proj, conv_buf, cw, alog_pad, dtb_pad)


def _split_bf16(a):
    hi = a.astype(BF16)
    return hi, (a - hi.astype(F32)).astype(BF16)


def _dot3(a, b):
    return _dot(a[0], b[0]) + (_dot(a[0], b[1]) + _dot(a[1], b[0]))


def _delta_kernel(q_ref, k_ref, v_ref, gb_ref, gt_ref, z_ref, s0_ref, ng_ref, o_ref, so_ref,
                  s_scr, u_scr, w_scr, qd_scr, kd_scr, a_scr, *, C, hps, unroll):
    T = q_ref.shape[0]
    n_chunks = T // C
    ri = lax.broadcasted_iota(I32, (C, C), 0)
    ci = lax.broadcasted_iota(I32, (C, C), 1)
    causal = ri >= ci
    strict = ri > ci
    eye = (ri == ci).astype(F32)
    head0 = pl.program_id(1) * hps
    lane = lax.broadcasted_iota(I32, (C, LANES), 1)

    def pick(x, col):
        lane_x = lax.broadcasted_iota(I32, x.shape, 1)
        return jnp.sum(jnp.where(lane_x == col, x, 0.0), axis=1, keepdims=True)

    def operands(n, j):
        rows = pl.ds(pl.multiple_of(n * C, C), C)
        cs = slice(j * DK, (j + 1) * DK)
        gbc = gb_ref[rows, :]
        G = pick(gbc, head0 + j)
        beta = pick(gbc, head0 + j + HC)
        return rows, cs, G, beta

    def prepare(streams):
        def lower_left(m):
            s = m.bit_length() - 1
            return jnp.logical_and(((ri ^ ci) >> s) == 1, ((ri >> s) & 1) == 1)

        Bs, Ps = [], []
        for n, j in streams:
            rows, cs, G, beta = operands(n, j)
            g_row = gt_ref[pl.ds(n * (2 * HC) + head0 + j, 1), :]
            k = k_ref[rows, cs]
            L = jnp.where(causal, jnp.exp(jnp.where(causal, G - g_row, 0.0)), 0.0)
            qk = _dot_nt(jnp.concatenate([k * beta, q_ref[rows, cs]], axis=0).astype(BF16), k.astype(BF16))
            a_scr[j, rows, :] = (qk[C:] * L).astype(BF16)
            Bm = jnp.where(strict, -(qk[:C] * L), 0.0)
            Bs.append(_split_bf16(Bm))
            Ps.append(eye + jnp.where(lower_left(1), Bm, 0.0))
        zero = jnp.zeros((C, C), BF16)
        m = 2
        while m < C:
            pair = lower_left(m)
            Ds = [_split_bf16(p) for p in Ps]
            Es = [(jnp.where(pair, b[0], zero), jnp.where(pair, b[1], zero)) for b in Bs]
            Ws = [_split_bf16(_dot3(e, d)) for e, d in zip(Es, Ds)]
            Ps = [p + _dot3(d, w) for p, d, w in zip(Ps, Ds, Ws)]
            m *= 2
        for (n, j), P in zip(streams, Ps):
            rows, cs, G, beta = operands(n, j)
            k = k_ref[rows, cs]
            eG = jnp.exp(G)
            rhs = jnp.concatenate([v_ref[rows, cs] * beta, (k * beta) * eG], axis=1)
            X = _dot3(_split_bf16(P), _split_bf16(rhs))
            u_scr[j, rows, :] = X[:, :DV]
            w_scr[j, rows, :] = X[:, DV:].astype(BF16)
            qd_scr[j, rows, :] = (q_ref[rows, cs] * eG).astype(BF16)
            kd_scr[j, rows, :] = (k * jnp.exp(G[C - 1:C, :] - G)).astype(BF16)

    def prepare_body(i, carry):
        prepare([(i * unroll + m, j) for m in range(unroll) for j in range(hps)])
        return carry

    lax.fori_loop(0, n_chunks // unroll, prepare_body, 0)

    s_scr[...] = s0_ref[...]

    def advance(n, carry):
        r0 = pl.multiple_of(n * C, C)
        rows = pl.ds(r0, C)
        g_end = gb_ref[pl.ds(r0 + C - 1, 1), :]
        heads = range(hps)
        S = [s_scr[j] for j in heads]
        Sb = [s.astype(BF16) for s in S]
        wS = [_dot(w_scr[j, rows, :], Sb[j]) for j in heads]
        qS = [_dot(qd_scr[j, rows, :], Sb[j]) for j in heads]
        vnb = [(u_scr[j, rows, :] - wS[j]).astype(BF16) for j in heads]
        kv = [_dot_tn(kd_scr[j, rows, :], vnb[j]) for j in heads]
        av = [_dot(a_scr[j, rows, :], vnb[j]) for j in heads]
        for j in heads:
            cs = slice(j * DK, (j + 1) * DK)
            s_scr[j] = S[j] * jnp.exp(pick(g_end, head0 + j)) + kv[j]
            o = qS[j] + av[j]
            on = o * lax.rsqrt(jnp.mean(o * o, axis=-1, keepdims=True) + NORM_EPS) * ng_ref[...]
            z = z_ref[rows, cs]
            o_ref[rows, cs] = (on * (z * jax.nn.sigmoid(z))).astype(BF16)
        return carry

    lax.fori_loop(0, n_chunks, advance, 0)
    so_ref[...] = s_scr[...]


def _delta_rule(q, k, v, gb, gt, proj, row0, B, T, C, S0, norm_g, hps):
    rb = row0 // T
    wb = hps * DK
    zcol0 = N_QKV // wb
    n_chunks = T // C
    qkv_spec = lambda: pl.BlockSpec((T, wb), lambda b, h: (b, h))
    return pl.pallas_call(
        functools.partial(_delta_kernel, C=C, hps=hps, unroll=4 if n_chunks % 4 == 0 else 1),
        grid=(B, HC // hps),
        in_specs=[qkv_spec(), qkv_spec(), qkv_spec(),
                  pl.BlockSpec((T, LANES), lambda b, h: (b, 0)),
                  pl.BlockSpec((n_chunks * 2 * HC, C), lambda b, h: (b, 0)),
                  pl.BlockSpec((T, wb), lambda b, h: (rb + b, zcol0 + h)),
                  pl.BlockSpec((None, hps, DK, DV), lambda b, h: (b, h, 0, 0)),
                  pl.BlockSpec((1, DV), lambda b, h: (0, 0))],
        out_specs=[pl.BlockSpec((T, wb), lambda b, h: (b, h)),
                   pl.BlockSpec((None, hps, DK, DV), lambda b, h: (b, h, 0, 0))],
        out_shape=[jax.ShapeDtypeStruct((B * T, DCV), BF16),
                   jax.ShapeDtypeStruct((B, HC, DK, DV), F32)],
        scratch_shapes=[pltpu.VMEM((hps, DK, DV), F32),
                        pltpu.VMEM((hps, T, DV), F32),
                        pltpu.VMEM((hps, T, DK), BF16),
                        pltpu.VMEM((hps, T, DK), BF16),
                        pltpu.VMEM((hps, T, DK), BF16),
                        pltpu.VMEM((hps, T, C), BF16)],
        compiler_params=_cparams(("parallel", "parallel")),
        name="delta_rule",
    )(q, k, v, gb, gt, proj, S0, norm_g.reshape(1, DV))


def _first_argmax(x, idx, size):
    m = jnp.max(x, axis=0, keepdims=True)
    am = jnp.min(jnp.where(x == m, idx, size), axis=0, keepdims=True)
    return m, am


def _router_kernel(x_ref, g_ref, sc_ref, sh_ref, rwt_ref, rb_ref,
                   h_ref, e_ref, w_ref, r_ref, cnt_ref, run_scr):
    tn = h_ref.shape[0]
    gsz = N_EXPERTS // N_GROUPS

    @pl.when(pl.program_id(0) == 0)
    def _():
        run_scr[...] = jnp.zeros_like(run_scr)

    h = _norm_mod(x_ref[...], g_ref[...], sc_ref[...], sh_ref[...])
    h_ref[...] = h
    scores = jax.nn.sigmoid(_dot_nt(rwt_ref[...], h, HI))
    sel = scores + rb_ref[...]

    gidx = lax.broadcasted_iota(I32, (gsz, tn), 0)
    gs_rows = []
    for gi in range(N_GROUPS):
        blk = sel[gi * gsz:(gi + 1) * gsz, :]
        m1, a1 = _first_argmax(blk, gidx, gsz)
        m2 = jnp.max(jnp.where(gidx == a1, -jnp.inf, blk), axis=0, keepdims=True)
        gs_rows.append(m1 + m2)
    gscore = jnp.concatenate(gs_rows, axis=0)
    gi8 = lax.broadcasted_iota(I32, (N_GROUPS, tn), 0)
    gmask = jnp.zeros((N_GROUPS, tn), jnp.bool_)
    gwork = gscore
    for _ in range(TOPK_GROUPS):
        _, a = _first_argmax(gwork, gi8, N_GROUPS)
        hit = gi8 == a
        gmask = jnp.logical_or(gmask, hit)
        gwork = jnp.where(hit, -jnp.inf, gwork)
    emask = jnp.concatenate([jnp.broadcast_to(gmask[gi:gi + 1, :], (gsz, tn)) for gi in range(N_GROUPS)], axis=0)

    eidx = lax.broadcasted_iota(I32, (N_EXPERTS, tn), 0)
    work = jnp.where(emask, sel, -jnp.inf)
    hits = []
    chosen = jnp.zeros((N_EXPERTS, tn), jnp.bool_)
    for _ in range(TOP_K):
        _, a = _first_argmax(work, eidx, N_EXPERTS)
        hit = eidx == a
        hits.append((a, hit))
        chosen = jnp.logical_or(chosen, hit)
        work = jnp.where(hit, -jnp.inf, work)
    chosen_f = chosen.astype(F32)
    denom = jnp.sum(scores * chosen_f, axis=0, keepdims=True)

    ti = lax.broadcasted_iota(I32, (tn, tn), 0)
    tj = lax.broadcasted_iota(I32, (tn, tn), 1)
    before = (ti < tj).astype(BF16)
    rank = _dot(chosen_f.astype(BF16), before) + run_scr[...]
    run_scr[...] = run_scr[...] + jnp.sum(chosen_f, axis=1, keepdims=True)
    cnt_ref[...] = jnp.broadcast_to(run_scr[...], cnt_ref.shape).astype(I32)

    e_rows, w_rows, r_rows = [], [], []
    for a, hit in hits:
        hf = hit.astype(F32)
        e_rows.append(a)
        w_rows.append(jnp.sum(scores * hf, axis=0, keepdims=True) / denom * ROUTE_SCALE)
        r_rows.append(jnp.sum(rank * hf, axis=0, keepdims=True))
    e_ref[...] = jnp.concatenate(e_rows, axis=0)
    w_ref[...] = jnp.concatenate(w_rows, axis=0)
    r_ref[...] = jnp.concatenate(r_rows, axis=0).astype(I32)


def _router(x_all, g, mod_pb, sc_col, sh_col, rwt, rb):
    npb, pb, d = x_all.shape
    n = npb * pb
    bB = 16
    tn = bB * pb
    return pl.pallas_call(
        _router_kernel,
        grid=(npb // bB,),
        in_specs=[pl.BlockSpec((bB, pb, d), lambda i: (i, 0, 0)),
                  pl.BlockSpec((1, d), lambda i: (0, 0)),
                  pl.BlockSpec((bB, 1, d), lambda i: (i, 0, sc_col)),
                  pl.BlockSpec((bB, 1, d), lambda i: (i, 0, sh_col)),
                  pl.BlockSpec((N_EXPERTS, d), lambda i: (0, 0)),
                  pl.BlockSpec((N_EXPERTS, 1), lambda i: (0, 0))],
        out_specs=[pl.BlockSpec((tn, d), lambda i: (i, 0)),
                   pl.BlockSpec((TOP_K, tn), lambda i: (0, i)),
                   pl.BlockSpec((TOP_K, tn), lambda i: (0, i)),
                   pl.BlockSpec((TOP_K, tn), lambda i: (0, i)),
                   pl.BlockSpec((N_EXPERTS, LANES), lambda i: (0, 0))],
        out_shape=[jax.ShapeDtypeStruct((n, d), F32),
                   jax.ShapeDtypeStruct((TOP_K, n), I32),
                   jax.ShapeDtypeStruct((TOP_K, n), F32),
                   jax.ShapeDtypeStruct((TOP_K, n), I32),
                   jax.ShapeDtypeStruct((N_EXPERTS, LANES), I32)],
        scratch_shapes=[pltpu.VMEM((N_EXPERTS, 1), F32)],
        compiler_params=_cparams(("arbitrary",)),
        name="moe_router",
    )(x_all, g.reshape(1, d), mod_pb, mod_pb, rwt, rb.reshape(N_EXPERTS, 1))


PACK_ROWS = D_MODEL // LANES


def _row_slab(ref, r):
    return ref.at[pl.ds(pl.multiple_of(r * PACK_ROWS, PACK_ROWS), PACK_ROWS), :]


def _to_row_tiles(dst_ref, words, row0=0):
    rows = words.shape[0]
    for c in range(PACK_ROWS):
        dst_ref[pl.ds(row0 * PACK_ROWS + c, rows, stride=PACK_ROWS), :] = words[:, c * LANES:(c + 1) * LANES]


def _lane_tile(src_ref, c, rows, row0=0):
    return src_ref[pl.ds(row0 * PACK_ROWS + c, rows, stride=PACK_ROWS), :]


def _from_row_tiles(src_ref, rows):
    return jnp.concatenate([_lane_tile(src_ref, c, rows) for c in range(PACK_ROWS)], axis=1)


def _dispatch_kernel(zf_ref, pos_ref, h_ref, xs_hbm, sbuf, zbuf, sem, zsem, *, tm):
    i = pl.program_id(0)
    n_steps = pl.num_programs(0)
    td = h_ref.shape[0]
    n_tiles = zf_ref.shape[0]
    slot = lax.rem(i, 2)

    @pl.when(i == 0)
    def _():
        zbuf[...] = jnp.zeros_like(zbuf)

        def zero_copy(t):
            r0 = pl.multiple_of(t * (tm * PACK_ROWS), tm * PACK_ROWS)
            return pltpu.make_async_copy(zbuf, xs_hbm.at[pl.ds(r0, tm * PACK_ROWS), :], zsem)

        def start(t, carry):
            @pl.when(zf_ref[t] != 0)
            def _():
                zero_copy(t).start()
            return carry

        def wait(t, carry):
            @pl.when(zf_ref[t] != 0)
            def _():
                zero_copy(t).wait()
            return carry

        lax.fori_loop(0, n_tiles, start, 0)
        lax.fori_loop(0, n_tiles, wait, 0)

    def drain(s):
        for _ in range(TOP_K):
            pltpu.make_async_copy(sbuf.at[s], xs_hbm.at[pl.ds(0, td * PACK_ROWS), :], sem.at[s]).wait()

    @pl.when(i >= 2)
    def _():
        drain(slot)

    _to_row_tiles(sbuf.at[slot], h_ref[...])

    def body(j, carry):
        t0 = pl.multiple_of(j * SUBLANES, SUBLANES)
        for u in range(SUBLANES):
            src = _row_slab(sbuf.at[slot], t0 + u)
            for k in range(TOP_K):
                p = pos_ref[0, k * td + t0 + u]
                pltpu.make_async_copy(src, _row_slab(xs_hbm, p), sem.at[slot]).start(priority=k % 2)
        return carry

    lax.fori_loop(0, td // SUBLANES, body, 0)

    @pl.when(i == n_steps - 1)
    def _():
        drain(slot)
        drain(1 - slot)


def _moe_dispatch(h2, pos_tiles, zero_flag, n_tiles, tm, td):
    n, d = h2.shape
    grid_spec = pltpu.PrefetchScalarGridSpec(
        num_scalar_prefetch=1,
        grid=(n // td,),
        in_specs=[pl.BlockSpec((None, 1, TOP_K * td), lambda i, zf: (i, 0, 0), memory_space=pltpu.SMEM),
                  pl.BlockSpec((td, d), lambda i, zf: (i, 0))],
        out_specs=pl.BlockSpec(memory_space=pl.ANY),
        scratch_shapes=[pltpu.VMEM((2, td * PACK_ROWS, LANES), F32), pltpu.VMEM((tm * PACK_ROWS, LANES), F32),
                        pltpu.SemaphoreType.DMA((2,)), pltpu.SemaphoreType.DMA(())],
    )
    return pl.pallas_call(
        functools.partial(_dispatch_kernel, tm=tm),
        grid_spec=grid_spec,
        out_shape=jax.ShapeDtypeStruct((n_tiles * tm * PACK_ROWS, LANES), F32),
        compiler_params=_cparams(("arbitrary",)),
        name="moe_dispatch",
    )(zero_flag, pos_tiles, h2)


def _experts_kernel(te_ref, nu_ref, xs_ref, wg_ref, wu_ref, wd_ref, ys_ref, wgu_s, wd_s):
    i = pl.program_id(0)
    nu = nu_ref[0]

    @pl.when(jnp.logical_or(i == 0, te_ref[i] != te_ref[jnp.maximum(i - 1, 0)]))
    def _():
        wgu_s[:, :D_EXPERT] = wg_ref[...].astype(BF16)
        wgu_s[:, D_EXPERT:] = wu_ref[...].astype(BF16)
        wd_s[...] = wd_ref[...].astype(BF16)

    @pl.when(i < nu)
    def _():
        x = _from_row_tiles(xs_ref, xs_ref.shape[0] // PACK_ROWS).astype(BF16)
        gu = _dot(x, wgu_s[...])
        hid = jax.nn.silu(gu[:, :D_EXPERT]) * gu[:, D_EXPERT:]
        _to_row_tiles(ys_ref, _dot(hid.astype(BF16), wd_s[...]))

    @pl.when(i >= nu)
    def _():
        ys_ref[...] = jnp.zeros_like(ys_ref)


def _routed_experts(xs, tile_expert, n_used, layer, wg, wu, wd, tm):
    n_tiles = xs.shape[0] // (tm * PACK_ROWS)
    d = wg.shape[2]
    grid_spec = pltpu.PrefetchScalarGridSpec(
        num_scalar_prefetch=2,
        grid=(n_tiles,),
        in_specs=[pl.BlockSpec((tm * PACK_ROWS, LANES), lambda i, te, nu: (jnp.minimum(i, nu[0] - 1), 0)),
                  pl.BlockSpec((None, None, d, D_EXPERT), lambda i, te, nu: (layer, te[i], 0, 0)),
                  pl.BlockSpec((None, None, d, D_EXPERT), lambda i, te, nu: (layer, te[i], 0, 0)),
                  pl.BlockSpec((None, None, D_EXPERT, d), lambda i, te, nu: (layer, te[i], 0, 0))],
        out_specs=pl.BlockSpec((tm * PACK_ROWS, LANES), lambda i, te, nu: (i, 0)),
        scratch_shapes=[pltpu.VMEM((d, 2 * D_EXPERT), BF16), pltpu.VMEM((D_EXPERT, d), BF16)],
    )
    return pl.pallas_call(
        _experts_kernel,
        grid_spec=grid_spec,
        out_shape=jax.ShapeDtypeStruct(xs.shape, F32),
        compiler_params=_cparams(("arbitrary",)),
        name="moe_experts",
    )(tile_expert, n_used, xs, wg, wu, wd)


def _gather_rows(idx_ref, n_rows, src_hbm, dst, sem):
    group = 4 * SUBLANES

    def body(j, carry):
        r0 = pl.multiple_of(j * group, group)
        for u in range(group):
            pltpu.make_async_copy(_row_slab(src_hbm, idx_ref[0, r0 + u]), _row_slab(dst, r0 + u),
                                  sem).start(priority=u % 2)
        return carry
    lax.fori_loop(0, n_rows // group, body, 0)


def _combine_kernel(pos_ref, nxt_ref, ys_hbm, w_ref, h_ref, x_ref, gt_ref, sgu_ref, sd_ref, o_ref, gbuf, sem):
    i = pl.program_id(0)
    tb = w_ref.shape[0]
    slot = lax.rem(i, 2)

    @pl.when(i == 0)
    def _():
        _gather_rows(pos_ref, TOP_K * tb, ys_hbm, gbuf.at[0], sem.at[0])

    @pl.when(i + 1 < pl.num_programs(0))
    def _():
        _gather_rows(nxt_ref, TOP_K * tb, ys_hbm, gbuf.at[1 - slot], sem.at[1 - slot])

    hb = h_ref[...].astype(BF16)
    gu = _dot(hb, sgu_ref[...])
    hid = jax.nn.silu(gu[:, :D_SHARED]) * gu[:, D_SHARED:]
    shared = _dot(hid.astype(BF16), sd_ref[...])
    pltpu.make_async_copy(ys_hbm.at[pl.ds(0, TOP_K * tb * PACK_ROWS), :], gbuf.at[slot], sem.at[slot]).wait()
    w = w_ref[...]
    wk = [jnp.broadcast_to(w[:, k:k + 1], (tb, LANES)) for k in range(TOP_K)]
    gs = gbuf.at[slot]
    cols = []
    for c in range(PACK_ROWS):
        a = wk[0] * _lane_tile(gs, c, tb)
        for k in range(1, TOP_K):
            a = a + wk[k] * _lane_tile(gs, c, tb, k * tb)
        cols.append(a)
    acc = shared + jnp.concatenate(cols, axis=1)
    bB, pb, d = x_ref.shape
    o_ref[...] = x_ref[...] + gt_ref[...] * acc.reshape(bB, pb, d)


def _moe_combine(ys, pos_tiles, w_tok, h2, x_all, mod_pb, gt_col, sgu, sd, tb):
    npb, pb, d = x_all.shape
    n_tb = pos_tiles.shape[0]
    bB = tb // pb
    last = n_tb - 1
    return pl.pallas_call(
        _combine_kernel,
        grid=(n_tb,),
        in_specs=[pl.BlockSpec((None, 1, TOP_K * tb), lambda i: (i, 0, 0), memory_space=pltpu.SMEM),
                  pl.BlockSpec((None, 1, TOP_K * tb), lambda i: (jnp.minimum(i + 1, last), 0, 0),
                               memory_space=pltpu.SMEM),
                  pl.BlockSpec(memory_space=pl.ANY),
                  pl.BlockSpec((tb, TOP_K), lambda i: (i, 0)),
                  pl.BlockSpec((tb, d), lambda i: (i, 0)),
                  pl.BlockSpec((bB, pb, d), lambda i: (i, 0, 0)),
                  pl.BlockSpec((bB, 1, d), lambda i: (i, 0, gt_col)),
                  pl.BlockSpec(sgu.shape, lambda i: (0, 0)),
                  pl.BlockSpec(sd.shape, lambda i: (0, 0))],
        out_specs=pl.BlockSpec((bB, pb, d), lambda i: (i, 0, 0)),
        out_shape=jax.ShapeDtypeStruct(x_all.shape, F32),
        scratch_shapes=[pltpu.VMEM((2, TOP_K * tb * PACK_ROWS, LANES), F32), pltpu.SemaphoreType.DMA((2,))],
        compiler_params=_cparams(("arbitrary",)),
        name="moe_combine",
    )(pos_tiles, pos_tiles, ys, w_tok, h2, x_all, mod_pb, sgu, sd)


def _moe(x_all, g, mod_pb, rw, rb, layer, wg, wu, wd, sg, su, sd):
    npb, pb, d = x_all.shape
    n = npb * pb
    tm = 256
    tb = 128
    h2, e_t, w_t, r_t, cnt = _router(x_all, g, mod_pb, 4, 3, rw.T, rb)

    counts = cnt[:, 0]
    padded = (counts + tm - 1) // tm * tm
    ends = jnp.cumsum(padded)
    starts = ends - padded
    n_tiles = (n * TOP_K + N_EXPERTS * (tm - 1)) // tm
    onehot = e_t[:, :, None] == jnp.arange(N_EXPERTS, dtype=I32)
    pos = jnp.sum(jnp.where(onehot, starts, 0), axis=-1) + r_t
    n_used = (ends[-1] // tm).astype(I32)
    tile_idx = jnp.arange(n_tiles, dtype=I32)
    tile_start = jnp.minimum(tile_idx, n_used - 1) * tm
    tile_expert = jnp.sum(ends[None, :] <= tile_start[:, None], axis=1).astype(I32)
    is_seg_end = jnp.any((ends[None, :] == (tile_idx[:, None] + 1) * tm) & (padded[None, :] > 0), axis=1)
    zero_flag = jnp.logical_or(is_seg_end, tile_idx >= n_used).astype(I32)
    pos_tiles = pos.reshape(TOP_K, n // tb, tb).transpose(1, 0, 2).reshape(n // tb, 1, TOP_K * tb)

    xs = _moe_dispatch(h2, pos_tiles, zero_flag, n_tiles, tm, tb)
    ys = _routed_experts(xs, tile_expert, n_used.reshape(1), layer, wg, wu, wd, tm)
    sgu = jnp.concatenate([sg, su], axis=-1).astype(BF16)
    return _moe_combine(ys, pos_tiles, w_t.T, h2, x_all, mod_pb, 5, sgu, sd.astype(BF16), tb)


def _final_kernel(x_ref, g_ref, o_ref):
    x = x_ref[...]
    ms = jnp.mean(x * x, axis=-1, keepdims=True)
    o_ref[...] = x * lax.rsqrt(ms + NORM_EPS) * g_ref[...]


def _final_norm(x_all, g, pb0, n_pb):
    npb, pb, d = x_all.shape
    bB = 32
    return pl.pallas_call(
        _final_kernel,
        grid=(n_pb // bB,),
        in_specs=[pl.BlockSpec((bB, pb, d), lambda i: (pb0 // bB + i, 0, 0)),
                  pl.BlockSpec((1, d), lambda i: (0, 0))],
        out_specs=pl.BlockSpec((bB, pb, d), lambda i: (i, 0, 0)),
        out_shape=jax.ShapeDtypeStruct((n_pb, pb, d), F32),
        compiler_params=_cparams(("parallel",)),
        name="final_norm",
    )(x_all, g.reshape(1, d))


def _block_diag(w):
    nb, di, do = w.shape
    return jnp.einsum("nde,nm->ndme", w, jnp.eye(nb, dtype=w.dtype)).reshape(nb * di, nb * do)


def kernel(x_prompt, x_sample, c_prompt, c_sample, cache_b_k, cache_b_v, state_a_conv, state_a_h, state_c_conv, state_c_S, norm1_g, norm2_g, final_g, w_mod, b_mod, w_in0, w_out0, a_conv_w, a_conv_b, a_w_r, a_b_r, a_w_i, a_b_i, a_lambda, b_rel_bias, w_in1, w_out1, c_conv_w, c_A_log, c_dt_bias, c_norm_g, router_w, router_bias, e_w_gate, e_w_up, e_w_down, s_w_gate, s_w_up, s_w_down):
    Bp, Tp, d = x_prompt.shape
    Bs, Ts, _ = x_sample.shape
    np_tok, ns_tok = Bp * Tp, Bs * Ts
    npb_p, npb_s = np_tok // PB, ns_tok // PB
    depth = w_mod.shape[0]

    x_all = jnp.concatenate([x_prompt.reshape(npb_p, PB, d), x_sample.reshape(npb_s, PB, d)], axis=0)
    mod = _modulation(jnp.concatenate([c_prompt, c_sample], axis=0), w_mod, b_mod)

    outs = {}
    for l in range(depth):
        mod_pb = jnp.concatenate([jnp.repeat(mod[l, :Bp], Tp // PB, axis=0),
                                  jnp.repeat(mod[l, Bp:], Ts // PB, axis=0)], axis=0)[:, None, :]
        j = l // 2
        if l % 2 == 0:
            proj = _in_projection(x_all, norm1_g[l], mod_pb, 1, 0, w_in0[j].astype(BF16), 1280)
            wr, wi = _block_diag(a_w_r[j]).astype(BF16), _block_diag(a_w_i[j]).astype(BF16)
            a_args = (a_conv_w[j], a_conv_b[j], wr, a_b_r[j], wi, a_b_i[j], a_lambda[j])
            ya_p, ac_p, ah_p = _rglru(proj, 0, Bp, Tp, 512, jnp.zeros((Bp, CONV_W - 1, DA), F32),
                                      jnp.zeros((Bp, DA), F32), *a_args)
            ya_s, ac_s, ah_s = _rglru(proj, np_tok, Bs, Ts, Ts, state_a_conv[j], state_a_h[j], *a_args)
            yb_p, bk_p, bv_p = _band_attention(proj, 0, Bp, Tp, CHUNK, _rel_bias(b_rel_bias[j], CHUNK))
            yb_s, bk_s, bv_s = _band_attention(proj, np_tok, Bs, Ts, Ts, _rel_bias(b_rel_bias[j], Ts),
                                               cache_b_k[j].reshape(Bs, BAND_PAST, DB),
                                               cache_b_v[j].reshape(Bs, BAND_PAST, DB))
            ya = jnp.concatenate([ya_p, ya_s], axis=0)
            yb = jnp.concatenate([yb_p, yb_s], axis=0)
            w_out = w_out0[j].astype(BF16)
            x_all = _out_projection([ya, yb], [w_out[:DA], w_out[DA:]], x_all, mod_pb, 2)
            outs.setdefault("bk_p", []).append(bk_p.reshape(Bp, -1, HB, DHB))
            outs.setdefault("bv_p", []).append(bv_p.reshape(Bp, -1, HB, DHB))
            outs.setdefault("bk_s", []).append(bk_s.reshape(Bs, -1, HB, DHB))
            outs.setdefault("bv_s", []).append(bv_s.reshape(Bs, -1, HB, DHB))
            outs.setdefault("ac_p", []).append(ac_p)
            outs.setdefault("ac_s", []).append(ac_s)
            outs.setdefault("ah_p", []).append(ah_p.reshape(Bp, DA))
            outs.setdefault("ah_s", []).append(ah_s.reshape(Bs, DA))
        else:
            w_in = jnp.pad(w_in1[j], ((0, 0), (0, IN1_PAD - IN1))).astype(BF16)
            proj = _in_projection(x_all, norm1_g[l], mod_pb, 1, 0, w_in, 1408)
            alog = jnp.pad(c_A_log[j], (0, LANES - HC)).reshape(1, LANES)
            dtb = jnp.pad(c_dt_bias[j], (0, LANES - HC)).reshape(1, LANES)
            cp, cs = min(DELTA_BLOCK, Tp), min(DELTA_BLOCK, Ts)
            q_p, k_p, v_p, gb_p, gt_p, cc_p = _gdn_pre(proj, 0, Bp, Tp, 256, cp,
                                                       jnp.zeros((Bp, CONV_W - 1, N_QKV), F32), c_conv_w[j], alog, dtb)
            q_s, k_s, v_s, gb_s, gt_s, cc_s = _gdn_pre(proj, np_tok, Bs, Ts, Ts, cs, state_c_conv[j], c_conv_w[j],
                                                       alog, dtb)
            o_p, cs_p = _delta_rule(q_p, k_p, v_p, gb_p, gt_p, proj, 0, Bp, Tp, cp,
                                    jnp.zeros((Bp, HC, DK, DV), F32), c_norm_g[j], 2)
            o_s, cs_s = _delta_rule(q_s, k_s, v_s, gb_s, gt_s, proj, np_tok, Bs, Ts, cs, state_c_S[j],
                                    c_norm_g[j], 4)
            o = jnp.concatenate([o_p, o_s], axis=0)
            x_all = _out_projection([o], [w_out1[j].astype(BF16)], x_all, mod_pb, 2)
            outs.setdefault("cc_p", []).append(cc_p)
            outs.setdefault("cc_s", []).append(cc_s)
            outs.setdefault("cs_p", []).append(cs_p)
            outs.setdefault("cs_s", []).append(cs_s)
        x_all = _moe(x_all, norm2_g[l], mod_pb, router_w[l], router_bias[l], l, e_w_gate, e_w_up, e_w_down,
                     s_w_gate[l], s_w_up[l], s_w_down[l])

    y_prompt = _final_norm(x_all, final_g, 0, npb_p).reshape(Bp, Tp, d)
    y_sample = _final_norm(x_all, final_g, npb_p, npb_s).reshape(Bs, Ts, d)
    st = {k: jnp.stack(v) for k, v in outs.items()}
    return (y_prompt, y_sample, st["bk_p"], st["bk_s"], st["bv_p"], st["bv_s"], st["ac_p"], st["ac_s"],
            st["ah_p"], st["ah_s"], st["cc_p"], st["cc_s"], st["cs_p"], st["cs_s"])
```

```python
import functools

import jax
import jax.numpy as jnp
from jax import lax
from jax.experimental import pallas as pl
from jax.experimental.pallas import tpu as pltpu

F32 = jnp.float32
BF16 = jnp.bfloat16
I32 = jnp.int32

D_MODEL = 1024
CHUNK = 64
NORM_EPS = 1e-6
CONV_W = 4
DA = 512
NB_A = 8
LRU_C = 8.0
HB = 8
DHB = 64
DB = HB * DHB
BAND_PAST = 8 * CHUNK
MAX_REL = 128
HC = 8
DK = 128
DV = 128
DCK = HC * DK
DCV = HC * DV
N_QKV = 2 * DCK + DCV
DELTA_BLOCK = 128
N_EXPERTS = 64
TOP_K = 8
N_GROUPS = 8
TOPK_GROUPS = 4
D_EXPERT = 256
D_SHARED = 256
ROUTE_SCALE = 2.5
IN0 = 2 * DA + 3 * DB
IN1 = N_QKV + DCV + 2 * HC

LANES = 128
SUBLANES = 8
PB = 32
IN1_PAD = N_QKV + DCV + LANES
VMEM_LIMIT = 48 * 1024 * 1024
HI = lax.Precision.HIGHEST


def _cparams(sem):
    return pltpu.CompilerParams(dimension_semantics=sem, vmem_limit_bytes=VMEM_LIMIT)


def _softplus(x):
    return jnp.maximum(x, 0.0) + jnp.log1p(jnp.exp(-jnp.abs(x)))


def _dot(a, b, precision=None):
    return jnp.dot(a, b, preferred_element_type=F32, precision=precision)


def _dot_nt(a, b, precision=None):
    return lax.dot_general(a, b, (((1,), (1,)), ((), ())), preferred_element_type=F32, precision=precision)


def _dot_tn(a, b, precision=None):
    return lax.dot_general(a, b, (((0,), (0,)), ((), ())), preferred_element_type=F32, precision=precision)


def _mod_kernel(c_ref, w_ref, b_ref, o_ref):
    c = c_ref[...]
    cond = c * jax.nn.sigmoid(c)
    o_ref[0] = _dot(cond.astype(BF16), w_ref[0].astype(BF16)) + b_ref[0]


def _modulation(c_all, w_mod, b_mod):
    depth, d, n6 = w_mod.shape
    nb = c_all.shape[0]
    tn = 1536
    return pl.pallas_call(
        _mod_kernel,
        grid=(depth, n6 // tn),
        in_specs=[pl.BlockSpec((nb, d), lambda l, j: (0, 0)),
                  pl.BlockSpec((1, d, tn), lambda l, j: (l, 0, j)),
                  pl.BlockSpec((1, 1, tn), lambda l, j: (l, 0, j))],
        out_specs=pl.BlockSpec((1, nb, tn), lambda l, j: (l, 0, j)),
        out_shape=jax.ShapeDtypeStruct((depth, nb, n6), F32),
        compiler_params=_cparams(("parallel", "parallel")),
        name="modulation",
    )(c_all, w_mod, b_mod.reshape(depth, 1, n6))


def _norm_mod(x3, g, sc, sh):
    ms = jnp.mean(x3 * x3, axis=-1, keepdims=True)
    y = x3 * lax.rsqrt(ms + NORM_EPS) * g
    h = y * (1.0 + sc) + sh
    return h.reshape(x3.shape[0] * x3.shape[1], x3.shape[2])


def _inproj_kernel(x_ref, g_ref, sc_ref, sh_ref, w_ref, o_ref, h_scr):
    @pl.when(pl.program_id(1) == 0)
    def _():
        h_scr[...] = _norm_mod(x_ref[...], g_ref[...], sc_ref[...], sh_ref[...]).astype(BF16)
    o_ref[...] = _dot(h_scr[...], w_ref[...])


def _in_projection(x_all, g, mod_pb, sc_col, sh_col, w_bf16, tn):
    npb, pb, d = x_all.shape
    nout = w_bf16.shape[1]
    bB = 32
    tm = bB * pb
    return pl.pallas_call(
        _inproj_kernel,
        grid=(npb // bB, nout // tn),
        in_specs=[pl.BlockSpec((bB, pb, d), lambda i, j: (i, 0, 0)),
                  pl.BlockSpec((1, d), lambda i, j: (0, 0)),
                  pl.BlockSpec((bB, 1, d), lambda i, j: (i, 0, sc_col)),
                  pl.BlockSpec((bB, 1, d), lambda i, j: (i, 0, sh_col)),
                  pl.BlockSpec((d, tn), lambda i, j: (0, j))],
        out_specs=pl.BlockSpec((tm, tn), lambda i, j: (i, j)),
        out_shape=jax.ShapeDtypeStruct((npb * pb, nout), F32),
        scratch_shapes=[pltpu.VMEM((tm, d), BF16)],
        compiler_params=_cparams(("parallel", "arbitrary")),
        name="in_projection",
    )(x_all, g.reshape(1, d), mod_pb, mod_pb, w_bf16)


def _rglru_kernel(xa_ref, ga_ref, cb_ref, h0_ref, cw_ref, cbias_ref, wr_ref, br_ref, wi_ref, bi_ref, lam_ref,
                  y_ref, cbo_ref, ho_ref, xbuf, hcar):
    tT = xa_ref.shape[0]

    @pl.when(pl.program_id(1) == 0)
    def _():
        xbuf[0:SUBLANES, :] = jnp.zeros((SUBLANES, DA), F32)
        xbuf[SUBLANES - (CONV_W - 1):SUBLANES, :] = cb_ref[...]
        hcar[...] = h0_ref[...]

    xa = xa_ref[...]
    xbuf[SUBLANES:SUBLANES + tT, :] = xa
    xc = cw_ref[3:4, :] * xa + cbias_ref[...]
    for j in range(CONV_W - 1):
        xc = xc + cw_ref[j:j + 1, :] * xbuf[SUBLANES - 3 + j:SUBLANES - 3 + j + tT, :]
    tail = xbuf[tT:tT + SUBLANES, :]
    xbuf[0:SUBLANES, :] = tail
    cbo_ref[...] = tail[SUBLANES - (CONV_W - 1):, :]

    xcb = xc.astype(BF16)
    r = jax.nn.sigmoid(_dot(xcb, wr_ref[...]) + br_ref[...])
    i = jax.nn.sigmoid(_dot(xcb, wi_ref[...]) + bi_ref[...])
    log_a = -LRU_C * r * _softplus(-lam_ref[...])
    a = jnp.exp(log_a)
    th = jnp.tanh(log_a)
    u = jnp.sqrt(-2.0 * th / (1.0 - th)) * (i * xc)

    rows = lax.broadcasted_iota(I32, (tT, DA), 0)
    s = 1
    while s < tT:
        a_sh = pltpu.roll(a, s, 0)
        u_sh = pltpu.roll(u, s, 0)
        m = rows >= s
        u = jnp.where(m, a * u_sh + u, u)
        a = jnp.where(m, a * a_sh, a)
        s *= 2
    h = a * hcar[...] + u
    hlast = h[tT - 1:tT, :]
    hcar[...] = hlast
    ho_ref[...] = hlast
    y_ref[...] = (h * jax.nn.gelu(ga_ref[...])).astype(BF16)


def _rglru(proj, row0, B, T, tT, conv_buf, h0, cw, cbias, wr, br, wi, bi, lam):
    nT = T // tT
    rb0 = row0 // tT
    vec = lambda: pl.BlockSpec((1, DA), lambda b, t: (0, 0))
    return pl.pallas_call(
        _rglru_kernel,
        grid=(B, nT),
        in_specs=[pl.BlockSpec((tT, DA), lambda b, t: (rb0 + b * nT + t, 0)),
                  pl.BlockSpec((tT, DA), lambda b, t: (rb0 + b * nT + t, 1)),
                  pl.BlockSpec((None, CONV_W - 1, DA), lambda b, t: (b, 0, 0)),
                  pl.BlockSpec((None, 1, DA), lambda b, t: (b, 0, 0)),
                  pl.BlockSpec((CONV_W, DA), lambda b, t: (0, 0)),
                  vec(),
                  pl.BlockSpec((DA, DA), lambda b, t: (0, 0)),
                  vec(),
                  pl.BlockSpec((DA, DA), lambda b, t: (0, 0)),
                  vec(), vec()],
        out_specs=[pl.BlockSpec((tT, DA), lambda b, t: (b * nT + t, 0)),
                   pl.BlockSpec((None, CONV_W - 1, DA), lambda b, t: (b, 0, 0)),
                   pl.BlockSpec((None, 1, DA), lambda b, t: (b, 0, 0))],
        out_shape=[jax.ShapeDtypeStruct((B * T, DA), BF16),
                   jax.ShapeDtypeStruct((B, CONV_W - 1, DA), F32),
                   jax.ShapeDtypeStruct((B, 1, DA), F32)],
        scratch_shapes=[pltpu.VMEM((tT + SUBLANES, DA), F32), pltpu.VMEM((1, DA), F32)],
        compiler_params=_cparams(("parallel", "arbitrary")),
        name="rglru",
    )(proj, proj, conv_buf, h0.reshape(B, 1, DA), cw, cbias.reshape(1, DA), wr, br.reshape(1, DA),
      wi, bi.reshape(1, DA), lam.reshape(1, DA))


def _attn_kernel(*refs, chq, has_hist):
    if has_hist:
        q_ref, k_ref, v_ref, hk_ref, hv_ref, bias_ref, o_ref, ko_ref, vo_ref, kbuf, vbuf = refs
    else:
        q_ref, k_ref, v_ref, bias_ref, o_ref, ko_ref, vo_ref, kbuf, vbuf = refs
    c = pl.program_id(1)
    T = k_ref.shape[0]
    W = BAND_PAST + chq
    keep = ko_ref.shape[0]

    heads = range(HB)

    @pl.when(c == 0)
    def _():
        ko_ref[...] = k_ref[T - keep:T, :]
        vo_ref[...] = v_ref[T - keep:T, :]
        for h in heads:
            sl = slice(h * DHB, (h + 1) * DHB)
            if has_hist:
                kbuf[h, 0:BAND_PAST, :] = hk_ref[:, sl].astype(BF16)
                vbuf[h, 0:BAND_PAST, :] = hv_ref[:, sl].astype(BF16)
            else:
                kbuf[h, 0:BAND_PAST, :] = jnp.zeros((BAND_PAST, DHB), BF16)
                vbuf[h, 0:BAND_PAST, :] = jnp.zeros((BAND_PAST, DHB), BF16)
            kbuf[h, BAND_PAST:BAND_PAST + T, :] = k_ref[:, sl].astype(BF16)
            vbuf[h, BAND_PAST:BAND_PAST + T, :] = v_ref[:, sl].astype(BF16)

    start = pl.multiple_of(c * chq, chq)
    win = pl.ds(start, W)
    q = q_ref[...].astype(BF16)
    s = [_dot_nt(q[:, h * DHB:(h + 1) * DHB], kbuf[h, win, :]) * (DHB ** -0.5) + bias_ref[h] for h in heads]
    if not has_hist:
        valid = start - BAND_PAST + lax.broadcasted_iota(I32, (chq, W), 1) >= 0
        s = [jnp.where(valid, x, -jnp.inf) for x in s]
    m = [jnp.max(x, axis=-1, keepdims=True) for x in s]
    e = [jnp.exp(x - mx) for x, mx in zip(s, m)]
    den = [jnp.sum(x, axis=-1, keepdims=True) for x in e]
    o = [_dot(e[h].astype(BF16), vbuf[h, win, :]) / den[h] for h in heads]
    o_ref[...] = jnp.concatenate(o, axis=-1).astype(BF16)


def _band_attention(proj, row0, B, T, chq, bias, hist_k=None, hist_v=None):
    nC = T // chq
    rbq = row0 // chq
    rbk = row0 // T
    has_hist = hist_k is not None
    keep = min(BAND_PAST, T)
    in_specs = [pl.BlockSpec((chq, DB), lambda b, c: (rbq + b * nC + c, 2)),
                pl.BlockSpec((T, DB), lambda b, c: (rbk + b, 3)),
                pl.BlockSpec((T, DB), lambda b, c: (rbk + b, 4))]
    args = [proj, proj, proj]
    if has_hist:
        in_specs += [pl.BlockSpec((None, BAND_PAST, DB), lambda b, c: (b, 0, 0))] * 2
        args += [hist_k, hist_v]
    in_specs.append(pl.BlockSpec((HB, chq, BAND_PAST + chq), lambda b, c: (0, 0, 0)))
    args.append(bias)
    return pl.pallas_call(
        functools.partial(_attn_kernel, chq=chq, has_hist=has_hist),
        grid=(B, nC),
        in_specs=in_specs,
        out_specs=[pl.BlockSpec((chq, DB), lambda b, c: (b * nC + c, 0)),
                   pl.BlockSpec((None, keep, DB), lambda b, c: (b, 0, 0)),
                   pl.BlockSpec((None, keep, DB), lambda b, c: (b, 0, 0))],
        out_shape=[jax.ShapeDtypeStruct((B * T, DB), BF16),
                   jax.ShapeDtypeStruct((B, keep, DB), F32),
                   jax.ShapeDtypeStruct((B, keep, DB), F32)],
        scratch_shapes=[pltpu.VMEM((HB, BAND_PAST + T, DHB), BF16), pltpu.VMEM((HB, BAND_PAST + T, DHB), BF16)],
        compiler_params=_cparams(("parallel", "arbitrary")),
        name="band_attention",
    )(*args)


def _rel_bias(table, chq):
    W = BAND_PAST + chq
    lw = W + chq - 1
    n_clipped = BAND_PAST - MAX_REL + chq
    w = jnp.concatenate([jnp.broadcast_to(table[:, 2 * MAX_REL:], (HB, n_clipped)),
                         table[:, MAX_REL + 1 - chq:2 * MAX_REL][:, ::-1]], axis=1)
    flat = jnp.tile(jnp.pad(w, ((0, 0), (0, 1))), (1, chq))[:, :chq * lw]
    return flat.reshape(HB, chq, lw)[:, :, chq - 1:chq - 1 + W]


def _outproj_kernel(*refs, n_in):
    ys, ws = refs[:n_in], refs[n_in:2 * n_in]
    x_ref, gt_ref, o_ref = refs[2 * n_in:]
    acc = _dot(ys[0][...], ws[0][...])
    for y_ref, w_ref in zip(ys[1:], ws[1:]):
        acc = acc + _dot(y_ref[...], w_ref[...])
    bB, pb, d = x_ref.shape
    o_ref[...] = x_ref[...] + gt_ref[...] * acc.reshape(bB, pb, d)


def _out_projection(ys, ws, x_all, mod_pb, gt_col):
    npb, pb, d = x_all.shape
    bB = 32
    tm = bB * pb
    n_in = len(ys)
    in_specs = [pl.BlockSpec((tm, y.shape[1]), lambda i: (i, 0)) for y in ys]
    in_specs += [pl.BlockSpec(w.shape, lambda i: (0, 0)) for w in ws]
    in_specs += [pl.BlockSpec((bB, pb, d), lambda i: (i, 0, 0)),
                 pl.BlockSpec((bB, 1, d), lambda i: (i, 0, gt_col))]
    return pl.pallas_call(
        functools.partial(_outproj_kernel, n_in=n_in),
        grid=(npb // bB,),
        in_specs=in_specs,
        out_specs=pl.BlockSpec((bB, pb, d), lambda i: (i, 0, 0)),
        out_shape=jax.ShapeDtypeStruct(x_all.shape, F32),
        compiler_params=_cparams(("parallel",)),
        name="out_projection",
    )(*ys, *ws, x_all, mod_pb)


def _gdnpre_kernel(qkv_ref, ab_ref, cb_ref, cw_ref, alog_ref, dtb_ref,
                   q_ref, k_ref, v_ref, gb_ref, gt_ref, cbo_ref, xbuf, *, C):
    tT = qkv_ref.shape[0]

    @pl.when(pl.program_id(1) == 0)
    def _():
        xbuf[0:SUBLANES, :] = jnp.zeros((SUBLANES, N_QKV), F32)
        xbuf[SUBLANES - (CONV_W - 1):SUBLANES, :] = cb_ref[...]

    for cb in range(N_QKV // DK):
        cols = slice(cb * DK, (cb + 1) * DK)
        x = qkv_ref[:, cols]
        xbuf[SUBLANES:SUBLANES + tT, cols] = x
        xc = cw_ref[3:4, cols] * x
        for j in range(CONV_W - 1):
            xc = xc + cw_ref[j:j + 1, cols] * xbuf[SUBLANES - 3 + j:SUBLANES - 3 + j + tT, cols]
        xs = xc * jax.nn.sigmoid(xc)
        if cb < 2 * HC:
            xn = xs * lax.rsqrt(jnp.sum(xs * xs, axis=-1, keepdims=True) + NORM_EPS)
            if cb < HC:
                q_ref[:, cols] = xn * (DK ** -0.5)
            else:
                k_ref[:, (cb - HC) * DK:(cb - HC + 1) * DK] = xn
        else:
            v_ref[:, (cb - 2 * HC) * DV:(cb - 2 * HC + 1) * DV] = xs
    tail = xbuf[tT:tT + SUBLANES, :]
    xbuf[0:SUBLANES, :] = tail
    cbo_ref[...] = tail[SUBLANES - (CONV_W - 1):, :]

    ab = ab_ref[...]
    g = -jnp.exp(alog_ref[...]) * _softplus(ab + dtb_ref[...])
    beta = jax.nn.sigmoid(ab)
    lane = lax.broadcasted_iota(I32, ab.shape, 1)
    g = jnp.where(lane < HC, g, 0.0)
    ri = lax.broadcasted_iota(I32, (C, C), 0)
    ci = lax.broadcasted_iota(I32, (C, C), 1)
    tril = (ri >= ci).astype(F32)
    triu = (ri <= ci).astype(F32)
    for c in range(tT // C):
        gc = g[c * C:(c + 1) * C, :]
        G = _dot(tril, gc, HI)
        lane_c = lax.broadcasted_iota(I32, (C, LANES), 1)
        gb_ref[c * C:(c + 1) * C, :] = jnp.where(lane_c < HC, G, beta[c * C:(c + 1) * C, :])
        gt_ref[c * 2 * HC:(c + 1) * 2 * HC, :] = _dot_tn(gc, triu, HI)[:2 * HC, :]


def _gdn_pre(proj, row0, B, T, tT, C, conv_buf, cw, alog_pad, dtb_pad):
    nT = T // tT
    rb0 = row0 // tT
    cpt = tT // C
    return pl.pallas_call(
        functools.partial(_gdnpre_kernel, C=C),
        grid=(B, nT),
        in_specs=[pl.BlockSpec((tT, N_QKV), lambda b, t: (rb0 + b * nT + t, 0)),
                  pl.BlockSpec((tT, LANES), lambda b, t: (rb0 + b * nT + t, (N_QKV + DCV) // LANES)),
                  pl.BlockSpec((None, CONV_W - 1, N_QKV), lambda b, t: (b, 0, 0)),
                  pl.BlockSpec((CONV_W, N_QKV), lambda b, t: (0, 0)),
                  pl.BlockSpec((1, LANES), lambda b, t: (0, 0)),
                  pl.BlockSpec((1, LANES), lambda b, t: (0, 0))],
        out_specs=[pl.BlockSpec((tT, DCK), lambda b, t: (b * nT + t, 0)),
                   pl.BlockSpec((tT, DCK), lambda b, t: (b * nT + t, 0)),
                   pl.BlockSpec((tT, DCV), lambda b, t: (b * nT + t, 0)),
                   pl.BlockSpec((tT, LANES), lambda b, t: (b * nT + t, 0)),
                   pl.BlockSpec((cpt * 2 * HC, C), lambda b, t: (b * nT + t, 0)),
                   pl.BlockSpec((None, CONV_W - 1, N_QKV), lambda b, t: (b, 0, 0))],
        out_shape=[jax.ShapeDtypeStruct((B * T, DCK), F32),
                   jax.ShapeDtypeStruct((B * T, DCK), F32),
                   jax.ShapeDtypeStruct((B * T, DCV), F32),
                   jax.ShapeDtypeStruct((B * T, LANES), F32),
                   jax.ShapeDtypeStruct((B * T // C * 2 * HC, C), F32),
                   jax.ShapeDtypeStruct((B, CONV_W - 1, N_QKV), F32)],
        scratch_shapes=[pltpu.VMEM((tT + SUBLANES, N_QKV), F32)],
        compiler_params=_cparams(("parallel", "arbitrary")),
        name="gdn_pre",
    )(proj, proj, conv_buf, cw, alog_pad, dtb_pad)


def _split_bf16(a):
    hi = a.astype(BF16)
    return hi, (a - hi.astype(F32)).astype(BF16)


def _dot3(a, b):
    return _dot(a[0], b[0]) + (_dot(a[0], b[1]) + _dot(a[1], b[0]))


def _delta_kernel(q_ref, k_ref, v_ref, gb_ref, gt_ref, z_ref, s0_ref, ng_ref, o_ref, so_ref,
                  s_scr, u_scr, w_scr, qd_scr, kd_scr, a_scr, *, C, hps, unroll):
    T = q_ref.shape[0]
    n_chunks = T // C
    ri = lax.broadcasted_iota(I32, (C, C), 0)
    ci = lax.broadcasted_iota(I32, (C, C), 1)
    causal = ri >= ci
    strict = ri > ci
    eye = (ri == ci).astype(F32)
    head0 = pl.program_id(1) * hps
    lane = lax.broadcasted_iota(I32, (C, LANES), 1)

    def pick(x, col):
        lane_x = lax.broadcasted_iota(I32, x.shape, 1)
        return jnp.sum(jnp.where(lane_x == col, x, 0.0), axis=1, keepdims=True)

    def operands(n, j):
        rows = pl.ds(pl.multiple_of(n * C, C), C)
        cs = slice(j * DK, (j + 1) * DK)
        gbc = gb_ref[rows, :]
        G = pick(gbc, head0 + j)
        beta = pick(gbc, head0 + j + HC)
        return rows, cs, G, beta

    def prepare(streams):
        def lower_left(m):
            s = m.bit_length() - 1
            return jnp.logical_and(((ri ^ ci) >> s) == 1, ((ri >> s) & 1) == 1)

        Bs, Ps = [], []
        for n, j in streams:
            rows, cs, G, beta = operands(n, j)
            g_row = gt_ref[pl.ds(n * (2 * HC) + head0 + j, 1), :]
            k = k_ref[rows, cs]
            L = jnp.where(causal, jnp.exp(jnp.where(causal, G - g_row, 0.0)), 0.0)
            qk = _dot_nt(jnp.concatenate([k * beta, q_ref[rows, cs]], axis=0).astype(BF16), k.astype(BF16))
            a_scr[j, rows, :] = (qk[C:] * L).astype(BF16)
            Bm = jnp.where(strict, -(qk[:C] * L), 0.0)
            Bs.append(_split_bf16(Bm))
            Ps.append(eye + jnp.where(lower_left(1), Bm, 0.0))
        zero = jnp.zeros((C, C), BF16)
        m = 2
        while m < C:
            pair = lower_left(m)
            Ds = [_split_bf16(p) for p in Ps]
            Es = [(jnp.where(pair, b[0], zero), jnp.where(pair, b[1], zero)) for b in Bs]
            Ws = [_split_bf16(_dot3(e, d)) for e, d in zip(Es, Ds)]
            Ps = [p + _dot3(d, w) for p, d, w in zip(Ps, Ds, Ws)]
            m *= 2
        for (n, j), P in zip(streams, Ps):
            rows, cs, G, beta = operands(n, j)
            k = k_ref[rows, cs]
            eG = jnp.exp(G)
            rhs = jnp.concatenate([v_ref[rows, cs] * beta, (k * beta) * eG], axis=1)
            X = _dot3(_split_bf16(P), _split_bf16(rhs))
            u_scr[j, rows, :] = X[:, :DV]
            w_scr[j, rows, :] = X[:, DV:].astype(BF16)
            qd_scr[j, rows, :] = (q_ref[rows, cs] * eG).astype(BF16)
            kd_scr[j, rows, :] = (k * jnp.exp(G[C - 1:C, :] - G)).astype(BF16)

    def prepare_body(i, carry):
        prepare([(i * unroll + m, j) for m in range(unroll) for j in range(hps)])
        return carry

    lax.fori_loop(0, n_chunks // unroll, prepare_body, 0)

    s_scr[...] = s0_ref[...]

    def advance(n, carry):
        r0 = pl.multiple_of(n * C, C)
        rows = pl.ds(r0, C)
        g_end = gb_ref[pl.ds(r0 + C - 1, 1), :]
        heads = range(hps)
        S = [s_scr[j] for j in heads]
        Sb = [s.astype(BF16) for s in S]
        wS = [_dot(w_scr[j, rows, :], Sb[j]) for j in heads]
        qS = [_dot(qd_scr[j, rows, :], Sb[j]) for j in heads]
        vnb = [(u_scr[j, rows, :] - wS[j]).astype(BF16) for j in heads]
        kv = [_dot_tn(kd_scr[j, rows, :], vnb[j]) for j in heads]
        av = [_dot(a_scr[j, rows, :], vnb[j]) for j in heads]
        for j in heads:
            cs = slice(j * DK, (j + 1) * DK)
            s_scr[j] = S[j] * jnp.exp(pick(g_end, head0 + j)) + kv[j]
            o = qS[j] + av[j]
            on = o * lax.rsqrt(jnp.mean(o * o, axis=-1, keepdims=True) + NORM_EPS) * ng_ref[...]
            z = z_ref[rows, cs]
            o_ref[rows, cs] = (on * (z * jax.nn.sigmoid(z))).astype(BF16)
        return carry

    lax.fori_loop(0, n_chunks, advance, 0)
    so_ref[...] = s_scr[...]


def _delta_rule(q, k, v, gb, gt, proj, row0, B, T, C, S0, norm_g, hps):
    rb = row0 // T
    wb = hps * DK
    zcol0 = N_QKV // wb
    n_chunks = T // C
    qkv_spec = lambda: pl.BlockSpec((T, wb), lambda b, h: (b, h))
    return pl.pallas_call(
        functools.partial(_delta_kernel, C=C, hps=hps, unroll=4 if n_chunks % 4 == 0 else 1),
        grid=(B, HC // hps),
        in_specs=[qkv_spec(), qkv_spec(), qkv_spec(),
                  pl.BlockSpec((T, LANES), lambda b, h: (b, 0)),
                  pl.BlockSpec((n_chunks * 2 * HC, C), lambda b, h: (b, 0)),
                  pl.BlockSpec((T, wb), lambda b, h: (rb + b, zcol0 + h)),
                  pl.BlockSpec((None, hps, DK, DV), lambda b, h: (b, h, 0, 0)),
                  pl.BlockSpec((1, DV), lambda b, h: (0, 0))],
        out_specs=[pl.BlockSpec((T, wb), lambda b, h: (b, h)),
                   pl.BlockSpec((None, hps, DK, DV), lambda b, h: (b, h, 0, 0))],
        out_shape=[jax.ShapeDtypeStruct((B * T, DCV), BF16),
                   jax.ShapeDtypeStruct((B, HC, DK, DV), F32)],
        scratch_shapes=[pltpu.VMEM((hps, DK, DV), F32),
                        pltpu.VMEM((hps, T, DV), F32),
                        pltpu.VMEM((hps, T, DK), BF16),
                        pltpu.VMEM((hps, T, DK), BF16),
                        pltpu.VMEM((hps, T, DK), BF16),
                        pltpu.VMEM((hps, T, C), BF16)],
        compiler_params=_cparams(("parallel", "parallel")),
        name="delta_rule",
    )(q, k, v, gb, gt, proj, S0, norm_g.reshape(1, DV))


def _first_argmax(x, idx, size):
    m = jnp.max(x, axis=0, keepdims=True)
    am = jnp.min(jnp.where(x == m, idx, size), axis=0, keepdims=True)
    return m, am


def _router_kernel(x_ref, g_ref, sc_ref, sh_ref, rwt_ref, rb_ref,
                   h_ref, e_ref, w_ref, r_ref, cnt_ref, run_scr):
    tn = h_ref.shape[0]
    gsz = N_EXPERTS // N_GROUPS

    @pl.when(pl.program_id(0) == 0)
    def _():
        run_scr[...] = jnp.zeros_like(run_scr)

    h = _norm_mod(x_ref[...], g_ref[...], sc_ref[...], sh_ref[...])
    h_ref[...] = h
    scores = jax.nn.sigmoid(_dot_nt(rwt_ref[...], h, HI))
    sel = scores + rb_ref[...]

    gidx = lax.broadcasted_iota(I32, (gsz, tn), 0)
    gs_rows = []
    for gi in range(N_GROUPS):
        blk = sel[gi * gsz:(gi + 1) * gsz, :]
        m1, a1 = _first_argmax(blk, gidx, gsz)
        m2 = jnp.max(jnp.where(gidx == a1, -jnp.inf, blk), axis=0, keepdims=True)
        gs_rows.append(m1 + m2)
    gscore = jnp.concatenate(gs_rows, axis=0)
    gi8 = lax.broadcasted_iota(I32, (N_GROUPS, tn), 0)
    gmask = jnp.zeros((N_GROUPS, tn), jnp.bool_)
    gwork = gscore
    for _ in range(TOPK_GROUPS):
        _, a = _first_argmax(gwork, gi8, N_GROUPS)
        hit = gi8 == a
        gmask = jnp.logical_or(gmask, hit)
        gwork = jnp.where(hit, -jnp.inf, gwork)
    emask = jnp.concatenate([jnp.broadcast_to(gmask[gi:gi + 1, :], (gsz, tn)) for gi in range(N_GROUPS)], axis=0)

    eidx = lax.broadcasted_iota(I32, (N_EXPERTS, tn), 0)
    work = jnp.where(emask, sel, -jnp.inf)
    hits = []
    chosen = jnp.zeros((N_EXPERTS, tn), jnp.bool_)
    for _ in range(TOP_K):
        _, a = _first_argmax(work, eidx, N_EXPERTS)
        hit = eidx == a
        hits.append((a, hit))
        chosen = jnp.logical_or(chosen, hit)
        work = jnp.where(hit, -jnp.inf, work)
    chosen_f = chosen.astype(F32)
    denom = jnp.sum(scores * chosen_f, axis=0, keepdims=True)

    ti = lax.broadcasted_iota(I32, (tn, tn), 0)
    tj = lax.broadcasted_iota(I32, (tn, tn), 1)
    before = (ti < tj).astype(BF16)
    rank = _dot(chosen_f.astype(BF16), before) + run_scr[...]
    run_scr[...] = run_scr[...] + jnp.sum(chosen_f, axis=1, keepdims=True)
    cnt_ref[...] = jnp.broadcast_to(run_scr[...], cnt_ref.shape).astype(I32)

    e_rows, w_rows, r_rows = [], [], []
    for a, hit in hits:
        hf = hit.astype(F32)
        e_rows.append(a)
        w_rows.append(jnp.sum(scores * hf, axis=0, keepdims=True) / denom * ROUTE_SCALE)
        r_rows.append(jnp.sum(rank * hf, axis=0, keepdims=True))
    e_ref[...] = jnp.concatenate(e_rows, axis=0)
    w_ref[...] = jnp.concatenate(w_rows, axis=0)
    r_ref[...] = jnp.concatenate(r_rows, axis=0).astype(I32)


def _router(x_all, g, mod_pb, sc_col, sh_col, rwt, rb):
    npb, pb, d = x_all.shape
    n = npb * pb
    bB = 16
    tn = bB * pb
    return pl.pallas_call(
        _router_kernel,
        grid=(npb // bB,),
        in_specs=[pl.BlockSpec((bB, pb, d), lambda i: (i, 0, 0)),
                  pl.BlockSpec((1, d), lambda i: (0, 0)),
                  pl.BlockSpec((bB, 1, d), lambda i: (i, 0, sc_col)),
                  pl.BlockSpec((bB, 1, d), lambda i: (i, 0, sh_col)),
                  pl.BlockSpec((N_EXPERTS, d), lambda i: (0, 0)),
                  pl.BlockSpec((N_EXPERTS, 1), lambda i: (0, 0))],
        out_specs=[pl.BlockSpec((tn, d), lambda i: (i, 0)),
                   pl.BlockSpec((TOP_K, tn), lambda i: (0, i)),
                   pl.BlockSpec((TOP_K, tn), lambda i: (0, i)),
                   pl.BlockSpec((TOP_K, tn), lambda i: (0, i)),
                   pl.BlockSpec((N_EXPERTS, LANES), lambda i: (0, 0))],
        out_shape=[jax.ShapeDtypeStruct((n, d), F32),
                   jax.ShapeDtypeStruct((TOP_K, n), I32),
                   jax.ShapeDtypeStruct((TOP_K, n), F32),
                   jax.ShapeDtypeStruct((TOP_K, n), I32),
                   jax.ShapeDtypeStruct((N_EXPERTS, LANES), I32)],
        scratch_shapes=[pltpu.VMEM((N_EXPERTS, 1), F32)],
        compiler_params=_cparams(("arbitrary",)),
        name="moe_router",
    )(x_all, g.reshape(1, d), mod_pb, mod_pb, rwt, rb.reshape(N_EXPERTS, 1))


PACK_ROWS = D_MODEL // LANES


def _row_slab(ref, r):
    return ref.at[pl.ds(pl.multiple_of(r * PACK_ROWS, PACK_ROWS), PACK_ROWS), :]


def _to_row_tiles(dst_ref, words, row0=0):
    rows = words.shape[0]
    for c in range(PACK_ROWS):
        dst_ref[pl.ds(row0 * PACK_ROWS + c, rows, stride=PACK_ROWS), :] = words[:, c * LANES:(c + 1) * LANES]


def _lane_tile(src_ref, c, rows, row0=0):
    return src_ref[pl.ds(row0 * PACK_ROWS + c, rows, stride=PACK_ROWS), :]


def _from_row_tiles(src_ref, rows):
    return jnp.concatenate([_lane_tile(src_ref, c, rows) for c in range(PACK_ROWS)], axis=1)


def _dispatch_kernel(zf_ref, pos_ref, h_ref, xs_hbm, sbuf, zbuf, sem, zsem, *, tm):
    i = pl.program_id(0)
    n_steps = pl.num_programs(0)
    td = h_ref.shape[0]
    n_tiles = zf_ref.shape[0]
    slot = lax.rem(i, 2)

    @pl.when(i == 0)
    def _():
        zbuf[...] = jnp.zeros_like(zbuf)

        def zero_copy(t):
            r0 = pl.multiple_of(t * (tm * PACK_ROWS), tm * PACK_ROWS)
            return pltpu.make_async_copy(zbuf, xs_hbm.at[pl.ds(r0, tm * PACK_ROWS), :], zsem)

        def start(t, carry):
            @pl.when(zf_ref[t] != 0)
            def _():
                zero_copy(t).start()
            return carry

        def wait(t, carry):
            @pl.when(zf_ref[t] != 0)
            def _():
                zero_copy(t).wait()
            return carry

        lax.fori_loop(0, n_tiles, start, 0)
        lax.fori_loop(0, n_tiles, wait, 0)

    def drain(s):
        for _ in range(TOP_K):
            pltpu.make_async_copy(sbuf.at[s], xs_hbm.at[pl.ds(0, td * PACK_ROWS), :], sem.at[s]).wait()

    @pl.when(i >= 2)
    def _():
        drain(slot)

    _to_row_tiles(sbuf.at[slot], h_ref[...])

    def body(j, carry):
        t0 = pl.multiple_of(j * SUBLANES, SUBLANES)
        for u in range(SUBLANES):
            src = _row_slab(sbuf.at[slot], t0 + u)
            for k in range(TOP_K):
                p = pos_ref[0, k * td + t0 + u]
                pltpu.make_async_copy(src, _row_slab(xs_hbm, p), sem.at[slot]).start(priority=k % 2)
        return carry

    lax.fori_loop(0, td // SUBLANES, body, 0)

    @pl.when(i == n_steps - 1)
    def _():
        drain(slot)
        drain(1 - slot)


def _moe_dispatch(h2, pos_tiles, zero_flag, n_tiles, tm, td):
    n, d = h2.shape
    grid_spec = pltpu.PrefetchScalarGridSpec(
        num_scalar_prefetch=1,
        grid=(n // td,),
        in_specs=[pl.BlockSpec((None, 1, TOP_K * td), lambda i, zf: (i, 0, 0), memory_space=pltpu.SMEM),
                  pl.BlockSpec((td, d), lambda i, zf: (i, 0))],
        out_specs=pl.BlockSpec(memory_space=pl.ANY),
        scratch_shapes=[pltpu.VMEM((2, td * PACK_ROWS, LANES), F32), pltpu.VMEM((tm * PACK_ROWS, LANES), F32),
                        pltpu.SemaphoreType.DMA((2,)), pltpu.SemaphoreType.DMA(())],
    )
    return pl.pallas_call(
        functools.partial(_dispatch_kernel, tm=tm),
        grid_spec=grid_spec,
        out_shape=jax.ShapeDtypeStruct((n_tiles * tm * PACK_ROWS, LANES), F32),
        compiler_params=_cparams(("arbitrary",)),
        name="moe_dispatch",
    )(zero_flag, pos_tiles, h2)


def _experts_kernel(te_ref, nu_ref, xs_ref, wg_ref, wu_ref, wd_ref, ys_ref, wgu_s, wd_s):
    i = pl.program_id(0)
    nu = nu_ref[0]

    @pl.when(jnp.logical_or(i == 0, te_ref[i] != te_ref[jnp.maximum(i - 1, 0)]))
    def _():
        wgu_s[:, :D_EXPERT] = wg_ref[...].astype(BF16)
        wgu_s[:, D_EXPERT:] = wu_ref[...].astype(BF16)
        wd_s[...] = wd_ref[...].astype(BF16)

    @pl.when(i < nu)
    def _():
        x = _from_row_tiles(xs_ref, xs_ref.shape[0] // PACK_ROWS).astype(BF16)
        gu = _dot(x, wgu_s[...])
        hid = jax.nn.silu(gu[:, :D_EXPERT]) * gu[:, D_EXPERT:]
        _to_row_tiles(ys_ref, _dot(hid.astype(BF16), wd_s[...]))

    @pl.when(i >= nu)
    def _():
        ys_ref[...] = jnp.zeros_like(ys_ref)


def _routed_experts(xs, tile_expert, n_used, layer, wg, wu, wd, tm):
    n_tiles = xs.shape[0] // (tm * PACK_ROWS)
    d = wg.shape[2]
    grid_spec = pltpu.PrefetchScalarGridSpec(
        num_scalar_prefetch=2,
        grid=(n_tiles,),
        in_specs=[pl.BlockSpec((tm * PACK_ROWS, LANES), lambda i, te, nu: (jnp.minimum(i, nu[0] - 1), 0)),
                  pl.BlockSpec((None, None, d, D_EXPERT), lambda i, te, nu: (layer, te[i], 0, 0)),
                  pl.BlockSpec((None, None, d, D_EXPERT), lambda i, te, nu: (layer, te[i], 0, 0)),
                  pl.BlockSpec((None, None, D_EXPERT, d), lambda i, te, nu: (layer, te[i], 0, 0))],
        out_specs=pl.BlockSpec((tm * PACK_ROWS, LANES), lambda i, te, nu: (i, 0)),
        scratch_shapes=[pltpu.VMEM((d, 2 * D_EXPERT), BF16), pltpu.VMEM((D_EXPERT, d), BF16)],
    )
    return pl.pallas_call(
        _experts_kernel,
        grid_spec=grid_spec,
        out_shape=jax.ShapeDtypeStruct(xs.shape, F32),
        compiler_params=_cparams(("arbitrary",)),
        name="moe_experts",
    )(tile_expert, n_used, xs, wg, wu, wd)


def _gather_rows(idx_ref, n_rows, src_hbm, dst, sem):
    group = 4 * SUBLANES

    def body(j, carry):
        r0 = pl.multiple_of(j * group, group)
        for u in range(group):
            pltpu.make_async_copy(_row_slab(src_hbm, idx_ref[0, r0 + u]), _row_slab(dst, r0 + u),
                                  sem).start(priority=u % 2)
        return carry
    lax.fori_loop(0, n_rows // group, body, 0)


def _combine_kernel(pos_ref, nxt_ref, ys_hbm, w_ref, h_ref, x_ref, gt_ref, sgu_ref, sd_ref, o_ref,
                    gbuf0, gbuf1, acc_scr, sem):
    i = pl.program_id(0)
    last = pl.num_programs(0) - 1
    tb = w_ref.shape[0]
    n_rows = TOP_K * tb
    n_groups = tb // SUBLANES
    per_group = n_rows // n_groups

    def wait_tile(buf, s):
        pltpu.make_async_copy(ys_hbm.at[pl.ds(0, n_rows * PACK_ROWS), :], buf, sem.at[s]).wait()

    @pl.when(i == 0)
    def _():
        _gather_rows(pos_ref, n_rows, ys_hbm, gbuf0, sem.at[0])

    hb = h_ref[...].astype(BF16)
    gu = _dot(hb, sgu_ref[...])
    hid = jax.nn.silu(gu[:, :D_SHARED]) * gu[:, D_SHARED:]
    shared = _dot(hid.astype(BF16), sd_ref[...])

    def run(cur, cur_s, nxt, nxt_s):
        wait_tile(cur, cur_s)

        def body(g, carry):
            t0 = pl.multiple_of(g * SUBLANES, SUBLANES)
            r0 = pl.multiple_of(g * per_group, per_group)
            for u in range(per_group):
                pltpu.make_async_copy(_row_slab(ys_hbm, nxt_ref[0, r0 + u]), _row_slab(nxt, r0 + u),
                                      sem.at[nxt_s]).start(priority=u % 2)
            w8 = w_ref[pl.ds(t0, SUBLANES), :]
            wk = [jnp.broadcast_to(w8[:, k:k + 1], (SUBLANES, LANES)) for k in range(TOP_K)]
            for c in range(PACK_ROWS):
                a = wk[0] * _lane_tile(cur, c, SUBLANES, t0)
                for k in range(1, TOP_K):
                    a = a + wk[k] * _lane_tile(cur, c, SUBLANES, k * tb + t0)
                acc_scr[pl.ds(t0, SUBLANES), c * LANES:(c + 1) * LANES] = a
            return carry

        lax.fori_loop(0, n_groups, body, 0)

        @pl.when(i == last)
        def _():
            wait_tile(nxt, nxt_s)

    @pl.when(lax.rem(i, 2) == 0)
    def _():
        run(gbuf0, 0, gbuf1, 1)

    @pl.when(lax.rem(i, 2) == 1)
    def _():
        run(gbuf1, 1, gbuf0, 0)

    acc = shared + acc_scr[...]
    bB, pb, d = x_ref.shape
    o_ref[...] = x_ref[...] + gt_ref[...] * acc.reshape(bB, pb, d)


def _moe_combine(ys, pos_tiles, w_tok, h2, x_all, mod_pb, gt_col, sgu, sd, tb):
    npb, pb, d = x_all.shape
    n_tb = pos_tiles.shape[0]
    bB = tb // pb
    last = n_tb - 1
    return pl.pallas_call(
        _combine_kernel,
        grid=(n_tb,),
        in_specs=[pl.BlockSpec((None, 1, TOP_K * tb), lambda i: (i, 0, 0), memory_space=pltpu.SMEM),
                  pl.BlockSpec((None, 1, TOP_K * tb), lambda i: (jnp.minimum(i + 1, last), 0, 0),
                               memory_space=pltpu.SMEM),
                  pl.BlockSpec(memory_space=pl.ANY),
                  pl.BlockSpec((tb, TOP_K), lambda i: (i, 0)),
                  pl.BlockSpec((tb, d), lambda i: (i, 0)),
                  pl.BlockSpec((bB, pb, d), lambda i: (i, 0, 0)),
                  pl.BlockSpec((bB, 1, d), lambda i: (i, 0, gt_col)),
                  pl.BlockSpec(sgu.shape, lambda i: (0, 0)),
                  pl.BlockSpec(sd.shape, lambda i: (0, 0))],
        out_specs=pl.BlockSpec((bB, pb, d), lambda i: (i, 0, 0)),
        out_shape=jax.ShapeDtypeStruct(x_all.shape, F32),
        scratch_shapes=[pltpu.VMEM((TOP_K * tb * PACK_ROWS, LANES), F32),
                        pltpu.VMEM((TOP_K * tb * PACK_ROWS, LANES), F32),
                        pltpu.VMEM((tb, d), F32), pltpu.SemaphoreType.DMA((2,))],
        compiler_params=_cparams(("arbitrary",)),
        name="moe_combine",
    )(pos_tiles, pos_tiles, ys, w_tok, h2, x_all, mod_pb, sgu, sd)


def _moe(x_all, g, mod_pb, rw, rb, layer, wg, wu, wd, sg, su, sd):
    npb, pb, d = x_all.shape
    n = npb * pb
    tm = 512
    tb = 128
    h2, e_t, w_t, r_t, cnt = _router(x_all, g, mod_pb, 4, 3, rw.T, rb)

    counts = cnt[:, 0]
    padded = (counts + tm - 1) // tm * tm
    ends = jnp.cumsum(padded)
    starts = ends - padded
    n_tiles = (n * TOP_K + N_EXPERTS * (tm - 1)) // tm
    onehot = e_t[:, :, None] == jnp.arange(N_EXPERTS, dtype=I32)
    pos = jnp.sum(jnp.where(onehot, starts, 0), axis=-1) + r_t
    n_used = (ends[-1] // tm).astype(I32)
    tile_idx = jnp.arange(n_tiles, dtype=I32)
    tile_start = jnp.minimum(tile_idx, n_used - 1) * tm
    tile_expert = jnp.sum(ends[None, :] <= tile_start[:, None], axis=1).astype(I32)
    is_seg_end = jnp.any((ends[None, :] == (tile_idx[:, None] + 1) * tm) & (padded[None, :] > 0), axis=1)
    zero_flag = jnp.logical_or(is_seg_end, tile_idx >= n_used).astype(I32)
    pos_tiles = pos.reshape(TOP_K, n // tb, tb).transpose(1, 0, 2).reshape(n // tb, 1, TOP_K * tb)

    xs = _moe_dispatch(h2, pos_tiles, zero_flag, n_tiles, tm, tb)
    ys = _routed_experts(xs, tile_expert, n_used.reshape(1), layer, wg, wu, wd, tm)
    sgu = jnp.concatenate([sg, su], axis=-1).astype(BF16)
    return _moe_combine(ys, pos_tiles, w_t.T, h2, x_all, mod_pb, 5, sgu, sd.astype(BF16), tb)


def _final_kernel(x_ref, g_ref, o_ref):
    x = x_ref[...]
    ms = jnp.mean(x * x, axis=-1, keepdims=True)
    o_ref[...] = x * lax.rsqrt(ms + NORM_EPS) * g_ref[...]


def _final_norm(x_all, g, pb0, n_pb):
    npb, pb, d = x_all.shape
    bB = 32
    return pl.pallas_call(
        _final_kernel,
        grid=(n_pb // bB,),
        in_specs=[pl.BlockSpec((bB, pb, d), lambda i: (pb0 // bB + i, 0, 0)),
                  pl.BlockSpec((1, d), lambda i: (0, 0))],
        out_specs=pl.BlockSpec((bB, pb, d), lambda i: (i, 0, 0)),
        out_shape=jax.ShapeDtypeStruct((n_pb, pb, d), F32),
        compiler_params=_cparams(("parallel",)),
        name="final_norm",
    )(x_all, g.reshape(1, d))


def _block_diag(w):
    nb, di, do = w.shape
    return jnp.einsum("nde,nm->ndme", w, jnp.eye(nb, dtype=w.dtype)).reshape(nb * di, nb * do)


def kernel(x_prompt, x_sample, c_prompt, c_sample, cache_b_k, cache_b_v, state_a_conv, state_a_h, state_c_conv, state_c_S, norm1_g, norm2_g, final_g, w_mod, b_mod, w_in0, w_out0, a_conv_w, a_conv_b, a_w_r, a_b_r, a_w_i, a_b_i, a_lambda, b_rel_bias, w_in1, w_out1, c_conv_w, c_A_log, c_dt_bias, c_norm_g, router_w, router_bias, e_w_gate, e_w_up, e_w_down, s_w_gate, s_w_up, s_w_down):
    Bp, Tp, d = x_prompt.shape
    Bs, Ts, _ = x_sample.shape
    np_tok, ns_tok = Bp * Tp, Bs * Ts
    npb_p, npb_s = np_tok // PB, ns_tok // PB
    depth = w_mod.shape[0]

    x_all = jnp.concatenate([x_prompt.reshape(npb_p, PB, d), x_sample.reshape(npb_s, PB, d)], axis=0)
    mod = _modulation(jnp.concatenate([c_prompt, c_sample], axis=0), w_mod, b_mod)

    outs = {}
    for l in range(depth):
        mod_pb = jnp.concatenate([jnp.repeat(mod[l, :Bp], Tp // PB, axis=0),
                                  jnp.repeat(mod[l, Bp:], Ts // PB, axis=0)], axis=0)[:, None, :]
        j = l // 2
        if l % 2 == 0:
            proj = _in_projection(x_all, norm1_g[l], mod_pb, 1, 0, w_in0[j].astype(BF16), 1280)
            wr, wi = _block_diag(a_w_r[j]).astype(BF16), _block_diag(a_w_i[j]).astype(BF16)
            a_args = (a_conv_w[j], a_conv_b[j], wr, a_b_r[j], wi, a_b_i[j], a_lambda[j])
            ya_p, ac_p, ah_p = _rglru(proj, 0, Bp, Tp, 512, jnp.zeros((Bp, CONV_W - 1, DA), F32),
                                      jnp.zeros((Bp, DA), F32), *a_args)
            ya_s, ac_s, ah_s = _rglru(proj, np_tok, Bs, Ts, Ts, state_a_conv[j], state_a_h[j], *a_args)
            yb_p, bk_p, bv_p = _band_attention(proj, 0, Bp, Tp, CHUNK, _rel_bias(b_rel_bias[j], CHUNK))
            yb_s, bk_s, bv_s = _band_attention(proj, np_tok, Bs, Ts, Ts, _rel_bias(b_rel_bias[j], Ts),
                                               cache_b_k[j].reshape(Bs, BAND_PAST, DB),
                                               cache_b_v[j].reshape(Bs, BAND_PAST, DB))
            ya = jnp.concatenate([ya_p, ya_s], axis=0)
            yb = jnp.concatenate([yb_p, yb_s], axis=0)
            w_out = w_out0[j].astype(BF16)
            x_all = _out_projection([ya, yb], [w_out[:DA], w_out[DA:]], x_all, mod_pb, 2)
            outs.setdefault("bk_p", []).append(bk_p.reshape(Bp, -1, HB, DHB))
            outs.setdefault("bv_p", []).append(bv_p.reshape(Bp, -1, HB, DHB))
            outs.setdefault("bk_s", []).append(bk_s.reshape(Bs, -1, HB, DHB))
            outs.setdefault("bv_s", []).append(bv_s.reshape(Bs, -1, HB, DHB))
            outs.setdefault("ac_p", []).append(ac_p)
            outs.setdefault("ac_s", []).append(ac_s)
            outs.setdefault("ah_p", []).append(ah_p.reshape(Bp, DA))
            outs.setdefault("ah_s", []).append(ah_s.reshape(Bs, DA))
        else:
            w_in = jnp.pad(w_in1[j], ((0, 0), (0, IN1_PAD - IN1))).astype(BF16)
            proj = _in_projection(x_all, norm1_g[l], mod_pb, 1, 0, w_in, 1408)
            alog = jnp.pad(c_A_log[j], (0, LANES - HC)).reshape(1, LANES)
            dtb = jnp.pad(c_dt_bias[j], (0, LANES - HC)).reshape(1, LANES)
            cp, cs = min(DELTA_BLOCK, Tp), min(DELTA_BLOCK, Ts)
            q_p, k_p, v_p, gb_p, gt_p, cc_p = _gdn_pre(proj, 0, Bp, Tp, 256, cp,
                                                       jnp.zeros((Bp, CONV_W - 1, N_QKV), F32), c_conv_w[j], alog, dtb)
            q_s, k_s, v_s, gb_s, gt_s, cc_s = _gdn_pre(proj, np_tok, Bs, Ts, Ts, cs, state_c_conv[j], c_conv_w[j],
                                                       alog, dtb)
            o_p, cs_p = _delta_rule(q_p, k_p, v_p, gb_p, gt_p, proj, 0, Bp, Tp, cp,
                                    jnp.zeros((Bp, HC, DK, DV), F32), c_norm_g[j], 2)
            o_s, cs_s = _delta_rule(q_s, k_s, v_s, gb_s, gt_s, proj, np_tok, Bs, Ts, cs, state_c_S[j],
                                    c_norm_g[j], 4)
            o = jnp.concatenate([o_p, o_s], axis=0)
            x_all = _out_projection([o], [w_out1[j].astype(BF16)], x_all, mod_pb, 2)
            outs.setdefault("cc_p", []).append(cc_p)
            outs.setdefault("cc_s", []).append(cc_s)
            outs.setdefault("cs_p", []).append(cs_p)
            outs.setdefault("cs_s", []).append(cs_s)
        x_all = _moe(x_all, norm2_g[l], mod_pb, router_w[l], router_bias[l], l, e_w_gate, e_w_up, e_w_down,
                     s_w_gate[l], s_w_up[l], s_w_down[l])

    y_prompt = _final_norm(x_all, final_g, 0, npb_p).reshape(Bp, Tp, d)
    y_sample = _final_norm(x_all, final_g, npb_p, npb_s).reshape(Bs, Ts, d)
    st = {k: jnp.stack(v) for k, v in outs.items()}
    return (y_prompt, y_sample, st["bk_p"], st["bk_s"], st["bv_p"], st["bv_s"], st["ac_p"], st["ac_s"],
            st["ah_p"], st["ah_s"], st["cc_p"], st["cc_s"], st["cs_p"], st["cs_s"])
```

```python
import functools

import jax
import jax.numpy as jnp
from jax import lax
from jax.experimental import pallas as pl
from jax.experimental.pallas import tpu as pltpu

F32 = jnp.float32
BF16 = jnp.bfloat16
I32 = jnp.int32

D_MODEL = 1024
CHUNK = 64
NORM_EPS = 1e-6
CONV_W = 4
DA = 512
NB_A = 8
LRU_C = 8.0
HB = 8
DHB = 64
DB = HB * DHB
BAND_PAST = 8 * CHUNK
MAX_REL = 128
HC = 8
DK = 128
DV = 128
DCK = HC * DK
DCV = HC * DV
N_QKV = 2 * DCK + DCV
DELTA_BLOCK = 128
N_EXPERTS = 64
TOP_K = 8
N_GROUPS = 8
TOPK_GROUPS = 4
D_EXPERT = 256
D_SHARED = 256
ROUTE_SCALE = 2.5
IN0 = 2 * DA + 3 * DB
IN1 = N_QKV + DCV + 2 * HC

LANES = 128
SUBLANES = 8
PB = 32
IN1_PAD = N_QKV + DCV + LANES
VMEM_LIMIT = 48 * 1024 * 1024
HI = lax.Precision.HIGHEST


def _cparams(sem):
    return pltpu.CompilerParams(dimension_semantics=sem, vmem_limit_bytes=VMEM_LIMIT)


def _softplus(x):
    return jnp.maximum(x, 0.0) + jnp.log1p(jnp.exp(-jnp.abs(x)))


def _dot(a, b, precision=None):
    return jnp.dot(a, b, preferred_element_type=F32, precision=precision)


def _dot_nt(a, b, precision=None):
    return lax.dot_general(a, b, (((1,), (1,)), ((), ())), preferred_element_type=F32, precision=precision)


def _dot_tn(a, b, precision=None):
    return lax.dot_general(a, b, (((0,), (0,)), ((), ())), preferred_element_type=F32, precision=precision)


def _mod_kernel(c_ref, w_ref, b_ref, o_ref):
    c = c_ref[...]
    cond = c * jax.nn.sigmoid(c)
    o_ref[0] = _dot(cond.astype(BF16), w_ref[0].astype(BF16)) + b_ref[0]


def _modulation(c_all, w_mod, b_mod):
    depth, d, n6 = w_mod.shape
    nb = c_all.shape[0]
    tn = 1536
    return pl.pallas_call(
        _mod_kernel,
        grid=(depth, n6 // tn),
        in_specs=[pl.BlockSpec((nb, d), lambda l, j: (0, 0)),
                  pl.BlockSpec((1, d, tn), lambda l, j: (l, 0, j)),
                  pl.BlockSpec((1, 1, tn), lambda l, j: (l, 0, j))],
        out_specs=pl.BlockSpec((1, nb, tn), lambda l, j: (l, 0, j)),
        out_shape=jax.ShapeDtypeStruct((depth, nb, n6), F32),
        compiler_params=_cparams(("parallel", "parallel")),
        name="modulation",
    )(c_all, w_mod, b_mod.reshape(depth, 1, n6))


def _norm_mod(x3, g, sc, sh):
    ms = jnp.mean(x3 * x3, axis=-1, keepdims=True)
    y = x3 * lax.rsqrt(ms + NORM_EPS) * g
    h = y * (1.0 + sc) + sh
    return h.reshape(x3.shape[0] * x3.shape[1], x3.shape[2])


def _inproj_kernel(x_ref, g_ref, sc_ref, sh_ref, w_ref, o_ref, h_scr):
    @pl.when(pl.program_id(1) == 0)
    def _():
        h_scr[...] = _norm_mod(x_ref[...], g_ref[...], sc_ref[...], sh_ref[...]).astype(BF16)
    o_ref[...] = _dot(h_scr[...], w_ref[...])


def _in_projection(x_all, g, mod_pb, sc_col, sh_col, w_bf16, tn):
    npb, pb, d = x_all.shape
    nout = w_bf16.shape[1]
    bB = 32
    tm = bB * pb
    return pl.pallas_call(
        _inproj_kernel,
        grid=(npb // bB, nout // tn),
        in_specs=[pl.BlockSpec((bB, pb, d), lambda i, j: (i, 0, 0)),
                  pl.BlockSpec((1, d), lambda i, j: (0, 0)),
                  pl.BlockSpec((bB, 1, d), lambda i, j: (i, 0, sc_col)),
                  pl.BlockSpec((bB, 1, d), lambda i, j: (i, 0, sh_col)),
                  pl.BlockSpec((d, tn), lambda i, j: (0, j))],
        out_specs=pl.BlockSpec((tm, tn), lambda i, j: (i, j)),
        out_shape=jax.ShapeDtypeStruct((npb * pb, nout), F32),
        scratch_shapes=[pltpu.VMEM((tm, d), BF16)],
        compiler_params=_cparams(("parallel", "arbitrary")),
        name="in_projection",
    )(x_all, g.reshape(1, d), mod_pb, mod_pb, w_bf16)


def _rglru_kernel(xa_ref, ga_ref, cb_ref, h0_ref, cw_ref, cbias_ref, wr_ref, br_ref, wi_ref, bi_ref, lam_ref,
                  y_ref, cbo_ref, ho_ref, xbuf, hcar):
    tT = xa_ref.shape[0]

    @pl.when(pl.program_id(1) == 0)
    def _():
        xbuf[0:SUBLANES, :] = jnp.zeros((SUBLANES, DA), F32)
        xbuf[SUBLANES - (CONV_W - 1):SUBLANES, :] = cb_ref[...]
        hcar[...] = h0_ref[...]

    xa = xa_ref[...]
    xbuf[SUBLANES:SUBLANES + tT, :] = xa
    xc = cw_ref[3:4, :] * xa + cbias_ref[...]
    for j in range(CONV_W - 1):
        xc = xc + cw_ref[j:j + 1, :] * xbuf[SUBLANES - 3 + j:SUBLANES - 3 + j + tT, :]
    tail = xbuf[tT:tT + SUBLANES, :]
    xbuf[0:SUBLANES, :] = tail
    cbo_ref[...] = tail[SUBLANES - (CONV_W - 1):, :]

    xcb = xc.astype(BF16)
    r = jax.nn.sigmoid(_dot(xcb, wr_ref[...]) + br_ref[...])
    i = jax.nn.sigmoid(_dot(xcb, wi_ref[...]) + bi_ref[...])
    log_a = -LRU_C * r * _softplus(-lam_ref[...])
    a = jnp.exp(log_a)
    th = jnp.tanh(log_a)
    u = jnp.sqrt(-2.0 * th / (1.0 - th)) * (i * xc)

    rows = lax.broadcasted_iota(I32, (tT, DA), 0)
    s = 1
    while s < tT:
        a_sh = pltpu.roll(a, s, 0)
        u_sh = pltpu.roll(u, s, 0)
        m = rows >= s
        u = jnp.where(m, a * u_sh + u, u)
        a = jnp.where(m, a * a_sh, a)
        s *= 2
    h = a * hcar[...] + u
    hlast = h[tT - 1:tT, :]
    hcar[...] = hlast
    ho_ref[...] = hlast
    y_ref[...] = (h * jax.nn.gelu(ga_ref[...])).astype(BF16)


def _rglru(proj, row0, B, T, tT, conv_buf, h0, cw, cbias, wr, br, wi, bi, lam):
    nT = T // tT
    rb0 = row0 // tT
    vec = lambda: pl.BlockSpec((1, DA), lambda b, t: (0, 0))
    return pl.pallas_call(
        _rglru_kernel,
        grid=(B, nT),
        in_specs=[pl.BlockSpec((tT, DA), lambda b, t: (rb0 + b * nT + t, 0)),
                  pl.BlockSpec((tT, DA), lambda b, t: (rb0 + b * nT + t, 1)),
                  pl.BlockSpec((None, CONV_W - 1, DA), lambda b, t: (b, 0, 0)),
                  pl.BlockSpec((None, 1, DA), lambda b, t: (b, 0, 0)),
                  pl.BlockSpec((CONV_W, DA), lambda b, t: (0, 0)),
                  vec(),
                  pl.BlockSpec((DA, DA), lambda b, t: (0, 0)),
                  vec(),
                  pl.BlockSpec((DA, DA), lambda b, t: (0, 0)),
                  vec(), vec()],
        out_specs=[pl.BlockSpec((tT, DA), lambda b, t: (b * nT + t, 0)),
                   pl.BlockSpec((None, CONV_W - 1, DA), lambda b, t: (b, 0, 0)),
                   pl.BlockSpec((None, 1, DA), lambda b, t: (b, 0, 0))],
        out_shape=[jax.ShapeDtypeStruct((B * T, DA), BF16),
                   jax.ShapeDtypeStruct((B, CONV_W - 1, DA), F32),
                   jax.ShapeDtypeStruct((B, 1, DA), F32)],
        scratch_shapes=[pltpu.VMEM((tT + SUBLANES, DA), F32), pltpu.VMEM((1, DA), F32)],
        compiler_params=_cparams(("parallel", "arbitrary")),
        name="rglru",
    )(proj, proj, conv_buf, h0.reshape(B, 1, DA), cw, cbias.reshape(1, DA), wr, br.reshape(1, DA),
      wi, bi.reshape(1, DA), lam.reshape(1, DA))


def _attn_kernel(*refs, chq, has_hist):
    if has_hist:
        q_ref, k_ref, v_ref, hk_ref, hv_ref, bias_ref, o_ref, ko_ref, vo_ref, kbuf, vbuf = refs
    else:
        q_ref, k_ref, v_ref, bias_ref, o_ref, ko_ref, vo_ref, kbuf, vbuf = refs
    c = pl.program_id(1)
    T = k_ref.shape[0]
    W = BAND_PAST + chq
    keep = ko_ref.shape[0]

    heads = range(HB)

    @pl.when(c == 0)
    def _():
        ko_ref[...] = k_ref[T - keep:T, :]
        vo_ref[...] = v_ref[T - keep:T, :]
        for h in heads:
            sl = slice(h * DHB, (h + 1) * DHB)
            if has_hist:
                kbuf[h, 0:BAND_PAST, :] = hk_ref[:, sl].astype(BF16)
                vbuf[h, 0:BAND_PAST, :] = hv_ref[:, sl].astype(BF16)
            else:
                kbuf[h, 0:BAND_PAST, :] = jnp.zeros((BAND_PAST, DHB), BF16)
                vbuf[h, 0:BAND_PAST, :] = jnp.zeros((BAND_PAST, DHB), BF16)
            kbuf[h, BAND_PAST:BAND_PAST + T, :] = k_ref[:, sl].astype(BF16)
            vbuf[h, BAND_PAST:BAND_PAST + T, :] = v_ref[:, sl].astype(BF16)

    cps = q_ref.shape[0] // chq
    q = q_ref[...].astype(BF16)
    pairs = [(sub, h) for sub in range(cps) for h in heads]
    starts = [pl.multiple_of((c * cps + sub) * chq, chq) for sub in range(cps)]
    wins = [pl.ds(st, W) for st in starts]
    s = [_dot_nt(q[sub * chq:(sub + 1) * chq, h * DHB:(h + 1) * DHB], kbuf[h, wins[sub], :]) * (DHB ** -0.5)
         + bias_ref[h] for sub, h in pairs]
    if not has_hist:
        col = lax.broadcasted_iota(I32, (chq, W), 1)
        valid = [st - BAND_PAST + col >= 0 for st in starts]
        s = [jnp.where(valid[sub], x, -jnp.inf) for (sub, h), x in zip(pairs, s)]
    m = [jnp.max(x, axis=-1, keepdims=True) for x in s]
    e = [jnp.exp(x - mx) for x, mx in zip(s, m)]
    den = [jnp.sum(x, axis=-1, keepdims=True) for x in e]
    o = [_dot(ex.astype(BF16), vbuf[h, wins[sub], :]) / dn for (sub, h), ex, dn in zip(pairs, e, den)]
    for sub in range(cps):
        o_ref[sub * chq:(sub + 1) * chq, :] = jnp.concatenate(o[sub * HB:(sub + 1) * HB], axis=-1).astype(BF16)


def _band_attention(proj, row0, B, T, chq, bias, hist_k=None, hist_v=None):
    cps = 2 if (T // chq) % 2 == 0 else 1
    qrows = cps * chq
    nC = T // qrows
    rbq = row0 // qrows
    rbk = row0 // T
    has_hist = hist_k is not None
    keep = min(BAND_PAST, T)
    in_specs = [pl.BlockSpec((qrows, DB), lambda b, c: (rbq + b * nC + c, 2)),
                pl.BlockSpec((T, DB), lambda b, c: (rbk + b, 3)),
                pl.BlockSpec((T, DB), lambda b, c: (rbk + b, 4))]
    args = [proj, proj, proj]
    if has_hist:
        in_specs += [pl.BlockSpec((None, BAND_PAST, DB), lambda b, c: (b, 0, 0))] * 2
        args += [hist_k, hist_v]
    in_specs.append(pl.BlockSpec((HB, chq, BAND_PAST + chq), lambda b, c: (0, 0, 0)))
    args.append(bias)
    return pl.pallas_call(
        functools.partial(_attn_kernel, chq=chq, has_hist=has_hist),
        grid=(B, nC),
        in_specs=in_specs,
        out_specs=[pl.BlockSpec((qrows, DB), lambda b, c: (b * nC + c, 0)),
                   pl.BlockSpec((None, keep, DB), lambda b, c: (b, 0, 0)),
                   pl.BlockSpec((None, keep, DB), lambda b, c: (b, 0, 0))],
        out_shape=[jax.ShapeDtypeStruct((B * T, DB), BF16),
                   jax.ShapeDtypeStruct((B, keep, DB), F32),
                   jax.ShapeDtypeStruct((B, keep, DB), F32)],
        scratch_shapes=[pltpu.VMEM((HB, BAND_PAST + T, DHB), BF16), pltpu.VMEM((HB, BAND_PAST + T, DHB), BF16)],
        compiler_params=_cparams(("parallel", "arbitrary")),
        name="band_attention",
    )(*args)


def _rel_bias(table, chq):
    W = BAND_PAST + chq
    lw = W + chq - 1
    n_clipped = BAND_PAST - MAX_REL + chq
    w = jnp.concatenate([jnp.broadcast_to(table[:, 2 * MAX_REL:], (HB, n_clipped)),
                         table[:, MAX_REL + 1 - chq:2 * MAX_REL][:, ::-1]], axis=1)
    flat = jnp.tile(jnp.pad(w, ((0, 0), (0, 1))), (1, chq))[:, :chq * lw]
    return flat.reshape(HB, chq, lw)[:, :, chq - 1:chq - 1 + W]


def _outproj_kernel(*refs, n_in, n_first):
    ys, ws = refs[:2 * n_in], refs[2 * n_in:3 * n_in]
    x_ref, gt_ref, o_ref = refs[3 * n_in:]
    first = pl.program_id(0) < n_first
    acc = None
    for j, w_ref in enumerate(ws):
        y = jnp.where(first, ys[2 * j][...], ys[2 * j + 1][...])
        part = _dot(y, w_ref[...])
        acc = part if acc is None else acc + part
    bB, pb, d = x_ref.shape
    o_ref[...] = x_ref[...] + gt_ref[...] * acc.reshape(bB, pb, d)


def _out_projection(ys, ws, x_all, mod_pb, gt_col):
    npb, pb, d = x_all.shape
    bB = 32
    tm = bB * pb
    n_in = len(ys)
    n_first = ys[0][0].shape[0] // tm
    assert all(y0.shape[0] % tm == 0 and y1.shape[0] % tm == 0 for y0, y1 in ys)
    in_specs = []
    for y0, y1 in ys:
        in_specs += [pl.BlockSpec((tm, y0.shape[1]), lambda i: (jnp.minimum(i, n_first - 1), 0)),
                     pl.BlockSpec((tm, y1.shape[1]), lambda i: (jnp.maximum(i - n_first, 0), 0))]
    in_specs += [pl.BlockSpec(w.shape, lambda i: (0, 0)) for w in ws]
    in_specs += [pl.BlockSpec((bB, pb, d), lambda i: (i, 0, 0)),
                 pl.BlockSpec((bB, 1, d), lambda i: (i, 0, gt_col))]
    return pl.pallas_call(
        functools.partial(_outproj_kernel, n_in=n_in, n_first=n_first),
        grid=(npb // bB,),
        in_specs=in_specs,
        out_specs=pl.BlockSpec((bB, pb, d), lambda i: (i, 0, 0)),
        out_shape=jax.ShapeDtypeStruct(x_all.shape, F32),
        compiler_params=_cparams(("parallel",)),
        name="out_projection",
    )(*[y for pair in ys for y in pair], *ws, x_all, mod_pb)


def _gdnpre_kernel(qkv_ref, ab_ref, cb_ref, cw_ref, alog_ref, dtb_ref,
                   q_ref, k_ref, v_ref, gb_ref, gt_ref, cbo_ref, xbuf, *, C):
    tT = qkv_ref.shape[0]

    @pl.when(pl.program_id(1) == 0)
    def _():
        xbuf[0:SUBLANES, :] = jnp.zeros((SUBLANES, N_QKV), F32)
        xbuf[SUBLANES - (CONV_W - 1):SUBLANES, :] = cb_ref[...]

    for cb in range(N_QKV // DK):
        cols = slice(cb * DK, (cb + 1) * DK)
        x = qkv_ref[:, cols]
        xbuf[SUBLANES:SUBLANES + tT, cols] = x
        xc = cw_ref[3:4, cols] * x
        for j in range(CONV_W - 1):
            xc = xc + cw_ref[j:j + 1, cols] * xbuf[SUBLANES - 3 + j:SUBLANES - 3 + j + tT, cols]
        xs = xc * jax.nn.sigmoid(xc)
        if cb < 2 * HC:
            xn = xs * lax.rsqrt(jnp.sum(xs * xs, axis=-1, keepdims=True) + NORM_EPS)
            if cb < HC:
                q_ref[:, cols] = xn * (DK ** -0.5)
            else:
                k_ref[:, (cb - HC) * DK:(cb - HC + 1) * DK] = xn
        else:
            v_ref[:, (cb - 2 * HC) * DV:(cb - 2 * HC + 1) * DV] = xs
    tail = xbuf[tT:tT + SUBLANES, :]
    xbuf[0:SUBLANES, :] = tail
    cbo_ref[...] = tail[SUBLANES - (CONV_W - 1):, :]

    ab = ab_ref[...]
    g = -jnp.exp(alog_ref[...]) * _softplus(ab + dtb_ref[...])
    beta = jax.nn.sigmoid(ab)
    lane = lax.broadcasted_iota(I32, ab.shape, 1)
    g = jnp.where(lane < HC, g, 0.0)
    ri = lax.broadcasted_iota(I32, (C, C), 0)
    ci = lax.broadcasted_iota(I32, (C, C), 1)
    tril = (ri >= ci).astype(F32)
    triu = (ri <= ci).astype(F32)
    for c in range(tT // C):
        gc = g[c * C:(c + 1) * C, :]
        G = _dot(tril, gc, HI)
        lane_c = lax.broadcasted_iota(I32, (C, LANES), 1)
        gb_ref[c * C:(c + 1) * C, :] = jnp.where(lane_c < HC, G, beta[c * C:(c + 1) * C, :])
        gt_ref[c * 2 * HC:(c + 1) * 2 * HC, :] = _dot_tn(gc, triu, HI)[:2 * HC, :]


def _gdn_pre(proj, row0, B, T, tT, C, conv_buf, cw, alog_pad, dtb_pad):
    nT = T // tT
    rb0 = row0 // tT
    cpt = tT // C
    return pl.pallas_call(
        functools.partial(_gdnpre_kernel, C=C),
        grid=(B, nT),
        in_specs=[pl.BlockSpec((tT, N_QKV), lambda b, t: (rb0 + b * nT + t, 0)),
                  pl.BlockSpec((tT, LANES), lambda b, t: (rb0 + b * nT + t, (N_QKV + DCV) // LANES)),
                  pl.BlockSpec((None, CONV_W - 1, N_QKV), lambda b, t: (b, 0, 0)),
                  pl.BlockSpec((CONV_W, N_QKV), lambda b, t: (0, 0)),
                  pl.BlockSpec((1, LANES), lambda b, t: (0, 0)),
                  pl.BlockSpec((1, LANES), lambda b, t: (0, 0))],
        out_specs=[pl.BlockSpec((tT, DCK), lambda b, t: (b * nT + t, 0)),
                   pl.BlockSpec((tT, DCK), lambda b, t: (b * nT + t, 0)),
                   pl.BlockSpec((tT, DCV), lambda b, t: (b * nT + t, 0)),
                   pl.BlockSpec((tT, LANES), lambda b, t: (b * nT + t, 0)),
                   pl.BlockSpec((cpt * 2 * HC, C), lambda b, t: (b * nT + t, 0)),
                   pl.BlockSpec((None, CONV_W - 1, N_QKV), lambda b, t: (b, 0, 0))],
        out_shape=[jax.ShapeDtypeStruct((B * T, DCK), F32),
                   jax.ShapeDtypeStruct((B * T, DCK), F32),
                   jax.ShapeDtypeStruct((B * T, DCV), F32),
                   jax.ShapeDtypeStruct((B * T, LANES), F32),
                   jax.ShapeDtypeStruct((B * T // C * 2 * HC, C), F32),
                   jax.ShapeDtypeStruct((B, CONV_W - 1, N_QKV), F32)],
        scratch_shapes=[pltpu.VMEM((tT + SUBLANES, N_QKV), F32)],
        compiler_params=_cparams(("parallel", "arbitrary")),
        name="gdn_pre",
    )(proj, proj, conv_buf, cw, alog_pad, dtb_pad)


def _split_bf16(a):
    hi = a.astype(BF16)
    return hi, (a - hi.astype(F32)).astype(BF16)


def _dot3(a, b):
    return _dot(a[0], b[0]) + (_dot(a[0], b[1]) + _dot(a[1], b[0]))


def _delta_kernel(q_ref, k_ref, v_ref, gb_ref, gt_ref, z_ref, s0_ref, ng_ref, o_ref, so_ref,
                  s_scr, u_scr, w_scr, qd_scr, kd_scr, a_scr, *, C, hps, unroll):
    T = q_ref.shape[0]
    n_chunks = T // C
    ri = lax.broadcasted_iota(I32, (C, C), 0)
    ci = lax.broadcasted_iota(I32, (C, C), 1)
    causal = ri >= ci
    strict = ri > ci
    eye = (ri == ci).astype(F32)
    head0 = pl.program_id(1) * hps
    lane = lax.broadcasted_iota(I32, (C, LANES), 1)

    def pick(x, col):
        lane_x = lax.broadcasted_iota(I32, x.shape, 1)
        return jnp.sum(jnp.where(lane_x == col, x, 0.0), axis=1, keepdims=True)

    def operands(n, j):
        rows = pl.ds(pl.multiple_of(n * C, C), C)
        cs = slice(j * DK, (j + 1) * DK)
        gbc = gb_ref[rows, :]
        G = pick(gbc, head0 + j)
        beta = pick(gbc, head0 + j + HC)
        return rows, cs, G, beta

    def prepare(streams):
        def lower_left(m):
            s = m.bit_length() - 1
            return jnp.logical_and(((ri ^ ci) >> s) == 1, ((ri >> s) & 1) == 1)

        Bs, Ps = [], []
        for n, j in streams:
            rows, cs, G, beta = operands(n, j)
            g_row = gt_ref[pl.ds(n * (2 * HC) + head0 + j, 1), :]
            k = k_ref[rows, cs]
            L = jnp.where(causal, jnp.exp(jnp.where(causal, G - g_row, 0.0)), 0.0)
            qk = _dot_nt(jnp.concatenate([k * beta, q_ref[rows, cs]], axis=0).astype(BF16), k.astype(BF16))
            a_scr[j, rows, :] = (qk[C:] * L).astype(BF16)
            Bm = jnp.where(strict, -(qk[:C] * L), 0.0)
            Bs.append(_split_bf16(Bm))
            Ps.append(eye + jnp.where(lower_left(1), Bm, 0.0))
        zero = jnp.zeros((C, C), BF16)
        m = 2
        while m < C:
            pair = lower_left(m)
            Ds = [_split_bf16(p) for p in Ps]
            Es = [(jnp.where(pair, b[0], zero), jnp.where(pair, b[1], zero)) for b in Bs]
            Ws = [_split_bf16(_dot3(e, d)) for e, d in zip(Es, Ds)]
            Ps = [p + _dot3(d, w) for p, d, w in zip(Ps, Ds, Ws)]
            m *= 2
        for (n, j), P in zip(streams, Ps):
            rows, cs, G, beta = operands(n, j)
            k = k_ref[rows, cs]
            eG = jnp.exp(G)
            rhs = jnp.concatenate([v_ref[rows, cs] * beta, (k * beta) * eG], axis=1)
            X = _dot3(_split_bf16(P), _split_bf16(rhs))
            u_scr[j, rows, :] = X[:, :DV]
            w_scr[j, rows, :] = X[:, DV:].astype(BF16)
            qd_scr[j, rows, :] = (q_ref[rows, cs] * eG).astype(BF16)
            kd_scr[j, rows, :] = (k * jnp.exp(G[C - 1:C, :] - G)).astype(BF16)

    def prepare_body(i, carry):
        prepare([(i * unroll + m, j) for m in range(unroll) for j in range(hps)])
        return carry

    lax.fori_loop(0, n_chunks // unroll, prepare_body, 0)

    s_scr[...] = s0_ref[...]

    def advance(n, carry):
        r0 = pl.multiple_of(n * C, C)
        rows = pl.ds(r0, C)
        g_end = gb_ref[pl.ds(r0 + C - 1, 1), :]
        heads = range(hps)
        S = [s_scr[j] for j in heads]
        Sb = [s.astype(BF16) for s in S]
        wS = [_dot(w_scr[j, rows, :], Sb[j]) for j in heads]
        qS = [_dot(qd_scr[j, rows, :], Sb[j]) for j in heads]
        vnb = [(u_scr[j, rows, :] - wS[j]).astype(BF16) for j in heads]
        kv = [_dot_tn(kd_scr[j, rows, :], vnb[j]) for j in heads]
        av = [_dot(a_scr[j, rows, :], vnb[j]) for j in heads]
        for j in heads:
            cs = slice(j * DK, (j + 1) * DK)
            s_scr[j] = S[j] * jnp.exp(pick(g_end, head0 + j)) + kv[j]
            o = qS[j] + av[j]
            on = o * lax.rsqrt(jnp.mean(o * o, axis=-1, keepdims=True) + NORM_EPS) * ng_ref[...]
            z = z_ref[rows, cs]
            o_ref[rows, cs] = (on * (z * jax.nn.sigmoid(z))).astype(BF16)
        return carry

    lax.fori_loop(0, n_chunks, advance, 0)
    so_ref[...] = s_scr[...]


def _delta_rule(q, k, v, gb, gt, proj, row0, B, T, C, S0, norm_g, hps):
    rb = row0 // T
    wb = hps * DK
    zcol0 = N_QKV // wb
    n_chunks = T // C
    qkv_spec = lambda: pl.BlockSpec((T, wb), lambda b, h: (b, h))
    return pl.pallas_call(
        functools.partial(_delta_kernel, C=C, hps=hps, unroll=4 if n_chunks % 4 == 0 else 1),
        grid=(B, HC // hps),
        in_specs=[qkv_spec(), qkv_spec(), qkv_spec(),
                  pl.BlockSpec((T, LANES), lambda b, h: (b, 0)),
                  pl.BlockSpec((n_chunks * 2 * HC, C), lambda b, h: (b, 0)),
                  pl.BlockSpec((T, wb), lambda b, h: (rb + b, zcol0 + h)),
                  pl.BlockSpec((None, hps, DK, DV), lambda b, h: (b, h, 0, 0)),
                  pl.BlockSpec((1, DV), lambda b, h: (0, 0))],
        out_specs=[pl.BlockSpec((T, wb), lambda b, h: (b, h)),
                   pl.BlockSpec((None, hps, DK, DV), lambda b, h: (b, h, 0, 0))],
        out_shape=[jax.ShapeDtypeStruct((B * T, DCV), BF16),
                   jax.ShapeDtypeStruct((B, HC, DK, DV), F32)],
        scratch_shapes=[pltpu.VMEM((hps, DK, DV), F32),
                        pltpu.VMEM((hps, T, DV), F32),
                        pltpu.VMEM((hps, T, DK), BF16),
                        pltpu.VMEM((hps, T, DK), BF16),
                        pltpu.VMEM((hps, T, DK), BF16),
                        pltpu.VMEM((hps, T, C), BF16)],
        compiler_params=_cparams(("parallel", "parallel")),
        name="delta_rule",
    )(q, k, v, gb, gt, proj, S0, norm_g.reshape(1, DV))


def _first_argmax(x, idx, size):
    m = jnp.max(x, axis=0, keepdims=True)
    am = jnp.min(jnp.where(x == m, idx, size), axis=0, keepdims=True)
    return m, am


def _router_kernel(x_ref, g_ref, sc_ref, sh_ref, rwt_ref, rb_ref,
                   h_ref, e_ref, w_ref, r_ref, cnt_ref, run_scr):
    tn = h_ref.shape[0]
    gsz = N_EXPERTS // N_GROUPS

    @pl.when(pl.program_id(0) == 0)
    def _():
        run_scr[...] = jnp.zeros_like(run_scr)

    h = _norm_mod(x_ref[...], g_ref[...], sc_ref[...], sh_ref[...])
    h_ref[...] = h
    scores = jax.nn.sigmoid(_dot_nt(rwt_ref[...], h, HI))
    sel = scores + rb_ref[...]

    gidx = lax.broadcasted_iota(I32, (gsz, tn), 0)
    gs_rows = []
    for gi in range(N_GROUPS):
        blk = sel[gi * gsz:(gi + 1) * gsz, :]
        m1, a1 = _first_argmax(blk, gidx, gsz)
        m2 = jnp.max(jnp.where(gidx == a1, -jnp.inf, blk), axis=0, keepdims=True)
        gs_rows.append(m1 + m2)
    gscore = jnp.concatenate(gs_rows, axis=0)
    gi8 = lax.broadcasted_iota(I32, (N_GROUPS, tn), 0)
    gmask = jnp.zeros((N_GROUPS, tn), jnp.bool_)
    gwork = gscore
    for _ in range(TOPK_GROUPS):
        _, a = _first_argmax(gwork, gi8, N_GROUPS)
        hit = gi8 == a
        gmask = jnp.logical_or(gmask, hit)
        gwork = jnp.where(hit, -jnp.inf, gwork)
    emask = jnp.concatenate([jnp.broadcast_to(gmask[gi:gi + 1, :], (gsz, tn)) for gi in range(N_GROUPS)], axis=0)

    eidx = lax.broadcasted_iota(I32, (N_EXPERTS, tn), 0)
    work = jnp.where(emask, sel, -jnp.inf)
    hits = []
    chosen = jnp.zeros((N_EXPERTS, tn), jnp.bool_)
    for _ in range(TOP_K):
        _, a = _first_argmax(work, eidx, N_EXPERTS)
        hit = eidx == a
        hits.append((a, hit))
        chosen = jnp.logical_or(chosen, hit)
        work = jnp.where(hit, -jnp.inf, work)
    chosen_f = chosen.astype(F32)
    denom = jnp.sum(scores * chosen_f, axis=0, keepdims=True)

    ti = lax.broadcasted_iota(I32, (tn, tn), 0)
    tj = lax.broadcasted_iota(I32, (tn, tn), 1)
    before = (ti < tj).astype(BF16)
    rank = _dot(chosen_f.astype(BF16), before) + run_scr[...]
    run_scr[...] = run_scr[...] + jnp.sum(chosen_f, axis=1, keepdims=True)
    cnt_ref[...] = jnp.broadcast_to(run_scr[...], cnt_ref.shape).astype(I32)

    e_rows, w_rows, r_rows = [], [], []
    for a, hit in hits:
        hf = hit.astype(F32)
        e_rows.append(a)
        w_rows.append(jnp.sum(scores * hf, axis=0, keepdims=True) / denom * ROUTE_SCALE)
        r_rows.append(jnp.sum(rank * hf, axis=0, keepdims=True))
    e_ref[...] = jnp.concatenate(e_rows, axis=0)
    w_ref[...] = jnp.concatenate(w_rows, axis=0)
    r_ref[...] = jnp.concatenate(r_rows, axis=0).astype(I32)


def _router(x_all, g, mod_pb, sc_col, sh_col, rwt, rb):
    npb, pb, d = x_all.shape
    n = npb * pb
    bB = 16
    tn = bB * pb
    return pl.pallas_call(
        _router_kernel,
        grid=(npb // bB,),
        in_specs=[pl.BlockSpec((bB, pb, d), lambda i: (i, 0, 0)),
                  pl.BlockSpec((1, d), lambda i: (0, 0)),
                  pl.BlockSpec((bB, 1, d), lambda i: (i, 0, sc_col)),
                  pl.BlockSpec((bB, 1, d), lambda i: (i, 0, sh_col)),
                  pl.BlockSpec((N_EXPERTS, d), lambda i: (0, 0)),
                  pl.BlockSpec((N_EXPERTS, 1), lambda i: (0, 0))],
        out_specs=[pl.BlockSpec((tn, d), lambda i: (i, 0)),
                   pl.BlockSpec((TOP_K, tn), lambda i: (0, i)),
                   pl.BlockSpec((TOP_K, tn), lambda i: (0, i)),
                   pl.BlockSpec((TOP_K, tn), lambda i: (0, i)),
                   pl.BlockSpec((N_EXPERTS, LANES), lambda i: (0, 0))],
        out_shape=[jax.ShapeDtypeStruct((n, d), F32),
                   jax.ShapeDtypeStruct((TOP_K, n), I32),
                   jax.ShapeDtypeStruct((TOP_K, n), F32),
                   jax.ShapeDtypeStruct((TOP_K, n), I32),
                   jax.ShapeDtypeStruct((N_EXPERTS, LANES), I32)],
        scratch_shapes=[pltpu.VMEM((N_EXPERTS, 1), F32)],
        compiler_params=_cparams(("arbitrary",)),
        name="moe_router",
    )(x_all, g.reshape(1, d), mod_pb, mod_pb, rwt, rb.reshape(N_EXPERTS, 1))


PACK_ROWS = D_MODEL // LANES


def _row_slab(ref, r):
    return ref.at[pl.ds(pl.multiple_of(r * PACK_ROWS, PACK_ROWS), PACK_ROWS), :]


def _to_row_tiles(dst_ref, words, row0=0):
    rows = words.shape[0]
    for c in range(PACK_ROWS):
        dst_ref[pl.ds(row0 * PACK_ROWS + c, rows, stride=PACK_ROWS), :] = words[:, c * LANES:(c + 1) * LANES]


def _lane_tile(src_ref, c, rows, row0=0):
    return src_ref[pl.ds(row0 * PACK_ROWS + c, rows, stride=PACK_ROWS), :]


def _from_row_tiles(src_ref, rows):
    return jnp.concatenate([_lane_tile(src_ref, c, rows) for c in range(PACK_ROWS)], axis=1)


def _dispatch_kernel(zf_ref, pos_ref, h_ref, xs_hbm, sbuf, zbuf, sem, zsem, *, tm):
    i = pl.program_id(0)
    n_steps = pl.num_programs(0)
    td = h_ref.shape[0]
    n_tiles = zf_ref.shape[0]
    slot = lax.rem(i, 2)

    @pl.when(i == 0)
    def _():
        zbuf[...] = jnp.zeros_like(zbuf)

        def zero_copy(t):
            r0 = pl.multiple_of(t * (tm * PACK_ROWS), tm * PACK_ROWS)
            return pltpu.make_async_copy(zbuf, xs_hbm.at[pl.ds(r0, tm * PACK_ROWS), :], zsem)

        def start(t, carry):
            @pl.when(zf_ref[t] != 0)
            def _():
                zero_copy(t).start()
            return carry

        def wait(t, carry):
            @pl.when(zf_ref[t] != 0)
            def _():
                zero_copy(t).wait()
            return carry

        lax.fori_loop(0, n_tiles, start, 0)
        lax.fori_loop(0, n_tiles, wait, 0)

    def drain(s):
        for _ in range(TOP_K):
            pltpu.make_async_copy(sbuf.at[s], xs_hbm.at[pl.ds(0, td * PACK_ROWS), :], sem.at[s]).wait()

    @pl.when(i >= 2)
    def _():
        drain(slot)

    _to_row_tiles(sbuf.at[slot], h_ref[...])

    def body(j, carry):
        t0 = pl.multiple_of(j * SUBLANES, SUBLANES)
        for u in range(SUBLANES):
            src = _row_slab(sbuf.at[slot], t0 + u)
            for k in range(TOP_K):
                p = pos_ref[0, k * td + t0 + u]
                pltpu.make_async_copy(src, _row_slab(xs_hbm, p), sem.at[slot]).start(priority=k % 2)
        return carry

    lax.fori_loop(0, td // SUBLANES, body, 0)

    @pl.when(i == n_steps - 1)
    def _():
        drain(slot)
        drain(1 - slot)


def _moe_dispatch(h2, pos_tiles, zero_flag, n_tiles, tm, td):
    n, d = h2.shape
    grid_spec = pltpu.PrefetchScalarGridSpec(
        num_scalar_prefetch=1,
        grid=(n // td,),
        in_specs=[pl.BlockSpec((None, 1, TOP_K * td), lambda i, zf: (i, 0, 0), memory_space=pltpu.SMEM),
                  pl.BlockSpec((td, d), lambda i, zf: (i, 0))],
        out_specs=pl.BlockSpec(memory_space=pl.ANY),
        scratch_shapes=[pltpu.VMEM((2, td * PACK_ROWS, LANES), F32), pltpu.VMEM((tm * PACK_ROWS, LANES), F32),
                        pltpu.SemaphoreType.DMA((2,)), pltpu.SemaphoreType.DMA(())],
    )
    return pl.pallas_call(
        functools.partial(_dispatch_kernel, tm=tm),
        grid_spec=grid_spec,
        out_shape=jax.ShapeDtypeStruct((n_tiles * tm * PACK_ROWS, LANES), F32),
        compiler_params=_cparams(("arbitrary",)),
        name="moe_dispatch",
    )(zero_flag, pos_tiles, h2)


def _experts_kernel(te_ref, nu_ref, xs_ref, wg_ref, wu_ref, wd_ref, ys_ref, wgu_s, wd_s):
    i = pl.program_id(0)
    nu = nu_ref[0]

    @pl.when(jnp.logical_or(i == 0, te_ref[i] != te_ref[jnp.maximum(i - 1, 0)]))
    def _():
        wgu_s[:, :D_EXPERT] = wg_ref[...].astype(BF16)
        wgu_s[:, D_EXPERT:] = wu_ref[...].astype(BF16)
        wd_s[...] = wd_ref[...].astype(BF16)

    @pl.when(i < nu)
    def _():
        x = _from_row_tiles(xs_ref, xs_ref.shape[0] // PACK_ROWS).astype(BF16)
        gu = _dot(x, wgu_s[...])
        hid = jax.nn.silu(gu[:, :D_EXPERT]) * gu[:, D_EXPERT:]
        _to_row_tiles(ys_ref, _dot(hid.astype(BF16), wd_s[...]))

    @pl.when(i >= nu)
    def _():
        ys_ref[...] = jnp.zeros_like(ys_ref)


def _routed_experts(xs, tile_expert, n_used, layer, wg, wu, wd, tm):
    n_tiles = xs.shape[0] // (tm * PACK_ROWS)
    d = wg.shape[2]
    grid_spec = pltpu.PrefetchScalarGridSpec(
        num_scalar_prefetch=2,
        grid=(n_tiles,),
        in_specs=[pl.BlockSpec((tm * PACK_ROWS, LANES), lambda i, te, nu: (jnp.minimum(i, nu[0] - 1), 0)),
                  pl.BlockSpec((None, None, d, D_EXPERT), lambda i, te, nu: (layer, te[i], 0, 0)),
                  pl.BlockSpec((None, None, d, D_EXPERT), lambda i, te, nu: (layer, te[i], 0, 0)),
                  pl.BlockSpec((None, None, D_EXPERT, d), lambda i, te, nu: (layer, te[i], 0, 0))],
        out_specs=pl.BlockSpec((tm * PACK_ROWS, LANES), lambda i, te, nu: (i, 0)),
        scratch_shapes=[pltpu.VMEM((d, 2 * D_EXPERT), BF16), pltpu.VMEM((D_EXPERT, d), BF16)],
    )
    return pl.pallas_call(
        _experts_kernel,
        grid_spec=grid_spec,
        out_shape=jax.ShapeDtypeStruct(xs.shape, F32),
        compiler_params=_cparams(("arbitrary",)),
        name="moe_experts",
    )(tile_expert, n_used, xs, wg, wu, wd)


def _gather_rows(idx_ref, n_rows, src_hbm, dst, sem):
    group = 4 * SUBLANES

    def body(j, carry):
        r0 = pl.multiple_of(j * group, group)
        for u in range(group):
            pltpu.make_async_copy(_row_slab(src_hbm, idx_ref[0, r0 + u]), _row_slab(dst, r0 + u),
                                  sem).start(priority=u % 2)
        return carry
    lax.fori_loop(0, n_rows // group, body, 0)


def _combine_kernel(pos_ref, nxt_ref, ys_hbm, w_ref, h_ref, x_ref, gt_ref, sgu_ref, sd_ref, o_ref,
                    gbuf0, gbuf1, acc_scr, sem):
    i = pl.program_id(0)
    last = pl.num_programs(0) - 1
    tb = w_ref.shape[0]
    n_rows = TOP_K * tb
    n_groups = tb // SUBLANES
    per_group = n_rows // n_groups

    def wait_tile(buf, s):
        pltpu.make_async_copy(ys_hbm.at[pl.ds(0, n_rows * PACK_ROWS), :], buf, sem.at[s]).wait()

    @pl.when(i == 0)
    def _():
        _gather_rows(pos_ref, n_rows, ys_hbm, gbuf0, sem.at[0])

    hb = h_ref[...].astype(BF16)
    gu = _dot(hb, sgu_ref[...])
    hid = jax.nn.silu(gu[:, :D_SHARED]) * gu[:, D_SHARED:]
    shared = _dot(hid.astype(BF16), sd_ref[...])

    def run(cur, cur_s, nxt, nxt_s):
        wait_tile(cur, cur_s)

        for g in range(n_groups):
            t0 = g * SUBLANES
            for u in range(per_group):
                r = g * per_group + u
                pltpu.make_async_copy(_row_slab(ys_hbm, nxt_ref[0, r]), nxt.at[r * PACK_ROWS:(r + 1) * PACK_ROWS, :],
                                      sem.at[nxt_s]).start(priority=u % 2)
            w8 = w_ref[t0:t0 + SUBLANES, :]
            wk = [jnp.broadcast_to(w8[:, k:k + 1], (SUBLANES, LANES)) for k in range(TOP_K)]
            for c in range(PACK_ROWS):
                a = wk[0] * _lane_tile(cur, c, SUBLANES, t0)
                for k in range(1, TOP_K):
                    a = a + wk[k] * _lane_tile(cur, c, SUBLANES, k * tb + t0)
                acc_scr[t0:t0 + SUBLANES, c * LANES:(c + 1) * LANES] = a

        @pl.when(i == last)
        def _():
            wait_tile(nxt, nxt_s)

    @pl.when(lax.rem(i, 2) == 0)
    def _():
        run(gbuf0, 0, gbuf1, 1)

    @pl.when(lax.rem(i, 2) == 1)
    def _():
        run(gbuf1, 1, gbuf0, 0)

    acc = shared + acc_scr[...]
    bB, pb, d = x_ref.shape
    o_ref[...] = x_ref[...] + gt_ref[...] * acc.reshape(bB, pb, d)


def _moe_combine(ys, pos_tiles, w_tok, h2, x_all, mod_pb, gt_col, sgu, sd, tb):
    npb, pb, d = x_all.shape
    n_tb = pos_tiles.shape[0]
    bB = tb // pb
    last = n_tb - 1
    return pl.pallas_call(
        _combine_kernel,
        grid=(n_tb,),
        in_specs=[pl.BlockSpec((None, 1, TOP_K * tb), lambda i: (i, 0, 0), memory_space=pltpu.SMEM),
                  pl.BlockSpec((None, 1, TOP_K * tb), lambda i: (jnp.minimum(i + 1, last), 0, 0),
                               memory_space=pltpu.SMEM),
                  pl.BlockSpec(memory_space=pl.ANY),
                  pl.BlockSpec((tb, TOP_K), lambda i: (i, 0)),
                  pl.BlockSpec((tb, d), lambda i: (i, 0)),
                  pl.BlockSpec((bB, pb, d), lambda i: (i, 0, 0)),
                  pl.BlockSpec((bB, 1, d), lambda i: (i, 0, gt_col)),
                  pl.BlockSpec(sgu.shape, lambda i: (0, 0)),
                  pl.BlockSpec(sd.shape, lambda i: (0, 0))],
        out_specs=pl.BlockSpec((bB, pb, d), lambda i: (i, 0, 0)),
        out_shape=jax.ShapeDtypeStruct(x_all.shape, F32),
        scratch_shapes=[pltpu.VMEM((TOP_K * tb * PACK_ROWS, LANES), F32),
                        pltpu.VMEM((TOP_K * tb * PACK_ROWS, LANES), F32),
                        pltpu.VMEM((tb, d), F32), pltpu.SemaphoreType.DMA((2,))],
        compiler_params=_cparams(("arbitrary",)),
        name="moe_combine",
    )(pos_tiles, pos_tiles, ys, w_tok, h2, x_all, mod_pb, sgu, sd)


def _moe(x_all, g, mod_pb, rw, rb, layer, wg, wu, wd, sg, su, sd):
    npb, pb, d = x_all.shape
    n = npb * pb
    tm = 512
    tb = 128
    h2, e_t, w_t, r_t, cnt = _router(x_all, g, mod_pb, 4, 3, rw.T, rb)

    counts = cnt[:, 0]
    padded = (counts + tm - 1) // tm * tm
    ends = jnp.cumsum(padded)
    starts = ends - padded
    n_tiles = (n * TOP_K + N_EXPERTS * (tm - 1)) // tm
    onehot = e_t[:, :, None] == jnp.arange(N_EXPERTS, dtype=I32)
    pos = jnp.sum(jnp.where(onehot, starts, 0), axis=-1) + r_t
    n_used = (ends[-1] // tm).astype(I32)
    tile_idx = jnp.arange(n_tiles, dtype=I32)
    tile_start = jnp.minimum(tile_idx, n_used - 1) * tm
    tile_expert = jnp.sum(ends[None, :] <= tile_start[:, None], axis=1).astype(I32)
    is_seg_end = jnp.any((ends[None, :] == (tile_idx[:, None] + 1) * tm) & (padded[None, :] > 0), axis=1)
    zero_flag = jnp.logical_or(is_seg_end, tile_idx >= n_used).astype(I32)
    pos_tiles = pos.reshape(TOP_K, n // tb, tb).transpose(1, 0, 2).reshape(n // tb, 1, TOP_K * tb)

    xs = _moe_dispatch(h2, pos_tiles, zero_flag, n_tiles, tm, tb)
    ys = _routed_experts(xs, tile_expert, n_used.reshape(1), layer, wg, wu, wd, tm)
    sgu = jnp.concatenate([sg, su], axis=-1).astype(BF16)
    return _moe_combine(ys, pos_tiles, w_t.T, h2, x_all, mod_pb, 5, sgu, sd.astype(BF16), tb)


def _final_kernel(x_ref, g_ref, o_ref):
    x = x_ref[...]
    ms = jnp.mean(x * x, axis=-1, keepdims=True)
    o_ref[...] = x * lax.rsqrt(ms + NORM_EPS) * g_ref[...]


def _final_norm(x_all, g, pb0, n_pb):
    npb, pb, d = x_all.shape
    bB = 32
    return pl.pallas_call(
        _final_kernel,
        grid=(n_pb // bB,),
        in_specs=[pl.BlockSpec((bB, pb, d), lambda i: (pb0 // bB + i, 0, 0)),
                  pl.BlockSpec((1, d), lambda i: (0, 0))],
        out_specs=pl.BlockSpec((bB, pb, d), lambda i: (i, 0, 0)),
        out_shape=jax.ShapeDtypeStruct((n_pb, pb, d), F32),
        compiler_params=_cparams(("parallel",)),
        name="final_norm",
    )(x_all, g.reshape(1, d))


def _block_diag(w):
    nb, di, do = w.shape
    return jnp.einsum("nde,nm->ndme", w, jnp.eye(nb, dtype=w.dtype)).reshape(nb * di, nb * do)


def kernel(x_prompt, x_sample, c_prompt, c_sample, cache_b_k, cache_b_v, state_a_conv, state_a_h, state_c_conv, state_c_S, norm1_g, norm2_g, final_g, w_mod, b_mod, w_in0, w_out0, a_conv_w, a_conv_b, a_w_r, a_b_r, a_w_i, a_b_i, a_lambda, b_rel_bias, w_in1, w_out1, c_conv_w, c_A_log, c_dt_bias, c_norm_g, router_w, router_bias, e_w_gate, e_w_up, e_w_down, s_w_gate, s_w_up, s_w_down):
    Bp, Tp, d = x_prompt.shape
    Bs, Ts, _ = x_sample.shape
    np_tok, ns_tok = Bp * Tp, Bs * Ts
    npb_p, npb_s = np_tok // PB, ns_tok // PB
    depth = w_mod.shape[0]

    x_all = jnp.concatenate([x_prompt.reshape(npb_p, PB, d), x_sample.reshape(npb_s, PB, d)], axis=0)
    mod = _modulation(jnp.concatenate([c_prompt, c_sample], axis=0), w_mod, b_mod)

    outs = {}
    for l in range(depth):
        mod_pb = jnp.concatenate([jnp.repeat(mod[l, :Bp], Tp // PB, axis=0),
                                  jnp.repeat(mod[l, Bp:], Ts // PB, axis=0)], axis=0)[:, None, :]
        j = l // 2
        if l % 2 == 0:
            proj = _in_projection(x_all, norm1_g[l], mod_pb, 1, 0, w_in0[j].astype(BF16), 1280)
            wr, wi = _block_diag(a_w_r[j]).astype(BF16), _block_diag(a_w_i[j]).astype(BF16)
            a_args = (a_conv_w[j], a_conv_b[j], wr, a_b_r[j], wi, a_b_i[j], a_lambda[j])
            ya_p, ac_p, ah_p = _rglru(proj, 0, Bp, Tp, 512, jnp.zeros((Bp, CONV_W - 1, DA), F32),
                                      jnp.zeros((Bp, DA), F32), *a_args)
            ya_s, ac_s, ah_s = _rglru(proj, np_tok, Bs, Ts, Ts, state_a_conv[j], state_a_h[j], *a_args)
            yb_p, bk_p, bv_p = _band_attention(proj, 0, Bp, Tp, CHUNK, _rel_bias(b_rel_bias[j], CHUNK))
            yb_s, bk_s, bv_s = _band_attention(proj, np_tok, Bs, Ts, Ts, _rel_bias(b_rel_bias[j], Ts),
                                               cache_b_k[j].reshape(Bs, BAND_PAST, DB),
                                               cache_b_v[j].reshape(Bs, BAND_PAST, DB))
            w_out = w_out0[j].astype(BF16)
            x_all = _out_projection([(ya_p, ya_s), (yb_p, yb_s)], [w_out[:DA], w_out[DA:]], x_all, mod_pb, 2)
            outs.setdefault("bk_p", []).append(bk_p.reshape(Bp, -1, HB, DHB))
            outs.setdefault("bv_p", []).append(bv_p.reshape(Bp, -1, HB, DHB))
            outs.setdefault("bk_s", []).append(bk_s.reshape(Bs, -1, HB, DHB))
            outs.setdefault("bv_s", []).append(bv_s.reshape(Bs, -1, HB, DHB))
            outs.setdefault("ac_p", []).append(ac_p)
            outs.setdefault("ac_s", []).append(ac_s)
            outs.setdefault("ah_p", []).append(ah_p.reshape(Bp, DA))
            outs.setdefault("ah_s", []).append(ah_s.reshape(Bs, DA))
        else:
            w_in = jnp.pad(w_in1[j], ((0, 0), (0, IN1_PAD - IN1))).astype(BF16)
            proj = _in_projection(x_all, norm1_g[l], mod_pb, 1, 0, w_in, 1408)
            alog = jnp.pad(c_A_log[j], (0, LANES - HC)).reshape(1, LANES)
            dtb = jnp.pad(c_dt_bias[j], (0, LANES - HC)).reshape(1, LANES)
            cp, cs = min(DELTA_BLOCK, Tp), min(DELTA_BLOCK, Ts)
            q_p, k_p, v_p, gb_p, gt_p, cc_p = _gdn_pre(proj, 0, Bp, Tp, 256, cp,
                                                       jnp.zeros((Bp, CONV_W - 1, N_QKV), F32), c_conv_w[j], alog, dtb)
            q_s, k_s, v_s, gb_s, gt_s, cc_s = _gdn_pre(proj, np_tok, Bs, Ts, Ts, cs, state_c_conv[j], c_conv_w[j],
                                                       alog, dtb)
            o_p, cs_p = _delta_rule(q_p, k_p, v_p, gb_p, gt_p, proj, 0, Bp, Tp, cp,
                                    jnp.zeros((Bp, HC, DK, DV), F32), c_norm_g[j], 2)
            o_s, cs_s = _delta_rule(q_s, k_s, v_s, gb_s, gt_s, proj, np_tok, Bs, Ts, cs, state_c_S[j],
                                    c_norm_g[j], 4)
            x_all = _out_projection([(o_p, o_s)], [w_out1[j].astype(BF16)], x_all, mod_pb, 2)
            outs.setdefault("cc_p", []).append(cc_p)
            outs.setdefault("cc_s", []).append(cc_s)
            outs.setdefault("cs_p", []).append(cs_p)
            outs.setdefault("cs_s", []).append(cs_s)
        x_all = _moe(x_all, norm2_g[l], mod_pb, router_w[l], router_bias[l], l, e_w_gate, e_w_up, e_w_down,
                     s_w_gate[l], s_w_up[l], s_w_down[l])

    y_prompt = _final_norm(x_all, final_g, 0, npb_p).reshape(Bp, Tp, d)
    y_sample = _final_norm(x_all, final_g, npb_p, npb_s).reshape(Bs, Ts, d)
    st = {k: jnp.stack(v) for k, v in outs.items()}
    return (y_prompt, y_sample, st["bk_p"], st["bk_s"], st["bv_p"], st["bv_s"], st["ac_p"], st["ac_s"],
            st["ah_p"], st["ah_s"], st["cc_p"], st["cc_s"], st["cs_p"], st["cs_s"])
```

```python
import functools

import jax
import jax.numpy as jnp
from jax import lax
from jax.experimental import pallas as pl
from jax.experimental.pallas import tpu as pltpu

F32 = jnp.float32
BF16 = jnp.bfloat16
I32 = jnp.int32

D_MODEL = 1024
CHUNK = 64
NORM_EPS = 1e-6
CONV_W = 4
DA = 512
NB_A = 8
LRU_C = 8.0
HB = 8
DHB = 64
DB = HB * DHB
BAND_PAST = 8 * CHUNK
MAX_REL = 128
HC = 8
DK = 128
DV = 128
DCK = HC * DK
DCV = HC * DV
N_QKV = 2 * DCK + DCV
DELTA_BLOCK = 128
N_EXPERTS = 64
TOP_K = 8
N_GROUPS = 8
TOPK_GROUPS = 4
D_EXPERT = 256
D_SHARED = 256
ROUTE_SCALE = 2.5
IN0 = 2 * DA + 3 * DB
IN1 = N_QKV + DCV + 2 * HC

LANES = 128
SUBLANES = 8
PB = 32
IN1_PAD = N_QKV + DCV + LANES
VMEM_LIMIT = 48 * 1024 * 1024
HI = lax.Precision.HIGHEST


def _cparams(sem):
    return pltpu.CompilerParams(dimension_semantics=sem, vmem_limit_bytes=VMEM_LIMIT)


def _softplus(x):
    return jnp.maximum(x, 0.0) + jnp.log1p(jnp.exp(-jnp.abs(x)))


def _dot(a, b, precision=None):
    return jnp.dot(a, b, preferred_element_type=F32, precision=precision)


def _dot_nt(a, b, precision=None):
    return lax.dot_general(a, b, (((1,), (1,)), ((), ())), preferred_element_type=F32, precision=precision)


def _dot_tn(a, b, precision=None):
    return lax.dot_general(a, b, (((0,), (0,)), ((), ())), preferred_element_type=F32, precision=precision)


def _mod_kernel(c_ref, w_ref, b_ref, o_ref):
    c = c_ref[...]
    cond = c * jax.nn.sigmoid(c)
    o_ref[0] = _dot(cond.astype(BF16), w_ref[0].astype(BF16)) + b_ref[0]


def _modulation(c_all, w_mod, b_mod):
    depth, d, n6 = w_mod.shape
    nb = c_all.shape[0]
    tn = 1536
    return pl.pallas_call(
        _mod_kernel,
        grid=(depth, n6 // tn),
        in_specs=[pl.BlockSpec((nb, d), lambda l, j: (0, 0)),
                  pl.BlockSpec((1, d, tn), lambda l, j: (l, 0, j)),
                  pl.BlockSpec((1, 1, tn), lambda l, j: (l, 0, j))],
        out_specs=pl.BlockSpec((1, nb, tn), lambda l, j: (l, 0, j)),
        out_shape=jax.ShapeDtypeStruct((depth, nb, n6), F32),
        compiler_params=_cparams(("parallel", "parallel")),
        name="modulation",
    )(c_all, w_mod, b_mod.reshape(depth, 1, n6))


def _norm_mod(x3, g, sc, sh):
    ms = jnp.mean(x3 * x3, axis=-1, keepdims=True)
    y = x3 * lax.rsqrt(ms + NORM_EPS) * g
    h = y * (1.0 + sc) + sh
    return h.reshape(x3.shape[0] * x3.shape[1], x3.shape[2])


def _stream_operands(x, bB):
    if not isinstance(x, tuple):
        return [x], [pl.BlockSpec((bB,) + x.shape[1:], lambda i, *_: (i, 0, 0))], None
    x0, x1 = x
    assert x0.shape[0] % bB == 0 and x1.shape[0] % bB == 0
    n_first = x0.shape[0] // bB
    return [x0, x1], [pl.BlockSpec((bB,) + x0.shape[1:], lambda i, *_: (jnp.minimum(i, n_first - 1), 0, 0)),
                      pl.BlockSpec((bB,) + x1.shape[1:], lambda i, *_: (jnp.maximum(i - n_first, 0), 0, 0))], n_first


def _stream_block(x_refs, n_first):
    if n_first is None:
        return x_refs[0][...]
    return jnp.where(pl.program_id(0) < n_first, x_refs[0][...], x_refs[1][...])


def _inproj_kernel(*refs, n_first):
    x_refs, (g_ref, sc_ref, sh_ref, w_ref, o_ref, h_scr) = refs[:-6], refs[-6:]

    @pl.when(pl.program_id(1) == 0)
    def _():
        h_scr[...] = _norm_mod(_stream_block(x_refs, n_first), g_ref[...], sc_ref[...], sh_ref[...]).astype(BF16)
    o_ref[...] = _dot(h_scr[...], w_ref[...])


def _in_projection(x, g, mod_pb, sc_col, sh_col, w_bf16, tn):
    npb, _, d6 = mod_pb.shape
    d, nout = w_bf16.shape
    bB = 32
    tm = bB * PB
    x_args, x_specs, n_first = _stream_operands(x, bB)
    return pl.pallas_call(
        functools.partial(_inproj_kernel, n_first=n_first),
        grid=(npb // bB, nout // tn),
        in_specs=x_specs + [pl.BlockSpec((1, d), lambda i, j: (0, 0)),
                            pl.BlockSpec((bB, 1, d), lambda i, j: (i, 0, sc_col)),
                            pl.BlockSpec((bB, 1, d), lambda i, j: (i, 0, sh_col)),
                            pl.BlockSpec((d, tn), lambda i, j: (0, j))],
        out_specs=pl.BlockSpec((tm, tn), lambda i, j: (i, j)),
        out_shape=jax.ShapeDtypeStruct((npb * PB, nout), F32),
        scratch_shapes=[pltpu.VMEM((tm, d), BF16)],
        compiler_params=_cparams(("parallel", "arbitrary")),
        name="in_projection",
    )(*x_args, g.reshape(1, d), mod_pb, mod_pb, w_bf16)


def _rglru_kernel(xa_ref, ga_ref, cb_ref, h0_ref, cw_ref, cbias_ref, wr_ref, br_ref, wi_ref, bi_ref, lam_ref,
                  y_ref, cbo_ref, ho_ref, xbuf, hcar):
    tT = xa_ref.shape[0]

    @pl.when(pl.program_id(1) == 0)
    def _():
        xbuf[0:SUBLANES, :] = jnp.zeros((SUBLANES, DA), F32)
        xbuf[SUBLANES - (CONV_W - 1):SUBLANES, :] = cb_ref[...]
        hcar[...] = h0_ref[...]

    xa = xa_ref[...]
    xbuf[SUBLANES:SUBLANES + tT, :] = xa
    xc = cw_ref[3:4, :] * xa + cbias_ref[...]
    for j in range(CONV_W - 1):
        xc = xc + cw_ref[j:j + 1, :] * xbuf[SUBLANES - 3 + j:SUBLANES - 3 + j + tT, :]
    tail = xbuf[tT:tT + SUBLANES, :]
    xbuf[0:SUBLANES, :] = tail
    cbo_ref[...] = tail[SUBLANES - (CONV_W - 1):, :]

    xcb = xc.astype(BF16)
    r = jax.nn.sigmoid(_dot(xcb, wr_ref[...]) + br_ref[...])
    i = jax.nn.sigmoid(_dot(xcb, wi_ref[...]) + bi_ref[...])
    log_a = -LRU_C * r * _softplus(-lam_ref[...])
    a = jnp.exp(log_a)
    th = jnp.tanh(log_a)
    u = jnp.sqrt(-2.0 * th / (1.0 - th)) * (i * xc)

    rows = lax.broadcasted_iota(I32, (tT, DA), 0)
    s = 1
    while s < tT:
        a_sh = pltpu.roll(a, s, 0)
        u_sh = pltpu.roll(u, s, 0)
        m = rows >= s
        u = jnp.where(m, a * u_sh + u, u)
        a = jnp.where(m, a * a_sh, a)
        s *= 2
    h = a * hcar[...] + u
    hlast = h[tT - 1:tT, :]
    hcar[...] = hlast
    ho_ref[...] = hlast
    y_ref[...] = (h * jax.nn.gelu(ga_ref[...])).astype(BF16)


def _rglru(proj, row0, B, T, tT, conv_buf, h0, cw, cbias, wr, br, wi, bi, lam):
    nT = T // tT
    rb0 = row0 // tT
    vec = lambda: pl.BlockSpec((1, DA), lambda b, t: (0, 0))
    return pl.pallas_call(
        _rglru_kernel,
        grid=(B, nT),
        in_specs=[pl.BlockSpec((tT, DA), lambda b, t: (rb0 + b * nT + t, 0)),
                  pl.BlockSpec((tT, DA), lambda b, t: (rb0 + b * nT + t, 1)),
                  pl.BlockSpec((None, CONV_W - 1, DA), lambda b, t: (b, 0, 0)),
                  pl.BlockSpec((None, 1, DA), lambda b, t: (b, 0, 0)),
                  pl.BlockSpec((CONV_W, DA), lambda b, t: (0, 0)),
                  vec(),
                  pl.BlockSpec((DA, DA), lambda b, t: (0, 0)),
                  vec(),
                  pl.BlockSpec((DA, DA), lambda b, t: (0, 0)),
                  vec(), vec()],
        out_specs=[pl.BlockSpec((tT, DA), lambda b, t: (b * nT + t, 0)),
                   pl.BlockSpec((None, CONV_W - 1, DA), lambda b, t: (b, 0, 0)),
                   pl.BlockSpec((None, 1, DA), lambda b, t: (b, 0, 0))],
        out_shape=[jax.ShapeDtypeStruct((B * T, DA), BF16),
                   jax.ShapeDtypeStruct((B, CONV_W - 1, DA), F32),
                   jax.ShapeDtypeStruct((B, 1, DA), F32)],
        scratch_shapes=[pltpu.VMEM((tT + SUBLANES, DA), F32), pltpu.VMEM((1, DA), F32)],
        compiler_params=_cparams(("parallel", "arbitrary")),
        name="rglru",
    )(proj, proj, conv_buf, h0.reshape(B, 1, DA), cw, cbias.reshape(1, DA), wr, br.reshape(1, DA),
      wi, bi.reshape(1, DA), lam.reshape(1, DA))


def _attn_kernel(*refs, chq, has_hist):
    if has_hist:
        q_ref, k_ref, v_ref, hk_ref, hv_ref, bias_ref, o_ref, ko_ref, vo_ref, kbuf, vbuf = refs
    else:
        q_ref, k_ref, v_ref, bias_ref, o_ref, ko_ref, vo_ref, kbuf, vbuf = refs
    c = pl.program_id(1)
    T = k_ref.shape[0]
    W = BAND_PAST + chq
    keep = ko_ref.shape[0]

    heads = range(HB)

    @pl.when(c == 0)
    def _():
        ko_ref[...] = k_ref[T - keep:T, :]
        vo_ref[...] = v_ref[T - keep:T, :]
        for h in heads:
            sl = slice(h * DHB, (h + 1) * DHB)
            if has_hist:
                kbuf[h, 0:BAND_PAST, :] = hk_ref[:, sl].astype(BF16)
                vbuf[h, 0:BAND_PAST, :] = hv_ref[:, sl].astype(BF16)
            else:
                kbuf[h, 0:BAND_PAST, :] = jnp.zeros((BAND_PAST, DHB), BF16)
                vbuf[h, 0:BAND_PAST, :] = jnp.zeros((BAND_PAST, DHB), BF16)
            kbuf[h, BAND_PAST:BAND_PAST + T, :] = k_ref[:, sl].astype(BF16)
            vbuf[h, BAND_PAST:BAND_PAST + T, :] = v_ref[:, sl].astype(BF16)

    cps = q_ref.shape[0] // chq
    q = q_ref[...].astype(BF16)
    pairs = [(sub, h) for sub in range(cps) for h in heads]
    starts = [pl.multiple_of((c * cps + sub) * chq, chq) for sub in range(cps)]
    wins = [pl.ds(st, W) for st in starts]
    s = [_dot_nt(q[sub * chq:(sub + 1) * chq, h * DHB:(h + 1) * DHB], kbuf[h, wins[sub], :]) * (DHB ** -0.5)
         + bias_ref[h] for sub, h in pairs]
    if not has_hist:
        col = lax.broadcasted_iota(I32, (chq, W), 1)
        valid = [st - BAND_PAST + col >= 0 for st in starts]
        s = [jnp.where(valid[sub], x, -jnp.inf) for (sub, h), x in zip(pairs, s)]
    m = [jnp.max(x, axis=-1, keepdims=True) for x in s]
    e = [jnp.exp(x - mx) for x, mx in zip(s, m)]
    den = [jnp.sum(x, axis=-1, keepdims=True) for x in e]
    o = [_dot(ex.astype(BF16), vbuf[h, wins[sub], :]) / dn for (sub, h), ex, dn in zip(pairs, e, den)]
    for sub in range(cps):
        o_ref[sub * chq:(sub + 1) * chq, :] = jnp.concatenate(o[sub * HB:(sub + 1) * HB], axis=-1).astype(BF16)


def _band_attention(proj, row0, B, T, chq, bias, hist_k=None, hist_v=None):
    cps = 2 if (T // chq) % 2 == 0 else 1
    qrows = cps * chq
    nC = T // qrows
    rbq = row0 // qrows
    rbk = row0 // T
    has_hist = hist_k is not None
    keep = min(BAND_PAST, T)
    in_specs = [pl.BlockSpec((qrows, DB), lambda b, c: (rbq + b * nC + c, 2)),
                pl.BlockSpec((T, DB), lambda b, c: (rbk + b, 3)),
                pl.BlockSpec((T, DB), lambda b, c: (rbk + b, 4))]
    args = [proj, proj, proj]
    if has_hist:
        in_specs += [pl.BlockSpec((None, BAND_PAST, DB), lambda b, c: (b, 0, 0))] * 2
        args += [hist_k, hist_v]
    in_specs.append(pl.BlockSpec((HB, chq, BAND_PAST + chq), lambda b, c: (0, 0, 0)))
    args.append(bias)
    return pl.pallas_call(
        functools.partial(_attn_kernel, chq=chq, has_hist=has_hist),
        grid=(B, nC),
        in_specs=in_specs,
        out_specs=[pl.BlockSpec((qrows, DB), lambda b, c: (b * nC + c, 0)),
                   pl.BlockSpec((None, keep, DB), lambda b, c: (b, 0, 0)),
                   pl.BlockSpec((None, keep, DB), lambda b, c: (b, 0, 0))],
        out_shape=[jax.ShapeDtypeStruct((B * T, DB), BF16),
                   jax.ShapeDtypeStruct((B, keep, DB), F32),
                   jax.ShapeDtypeStruct((B, keep, DB), F32)],
        scratch_shapes=[pltpu.VMEM((HB, BAND_PAST + T, DHB), BF16), pltpu.VMEM((HB, BAND_PAST + T, DHB), BF16)],
        compiler_params=_cparams(("parallel", "arbitrary")),
        name="band_attention",
    )(*args)


def _rel_bias(table, chq):
    W = BAND_PAST + chq
    lw = W + chq - 1
    n_clipped = BAND_PAST - MAX_REL + chq
    w = jnp.concatenate([jnp.broadcast_to(table[:, 2 * MAX_REL:], (HB, n_clipped)),
                         table[:, MAX_REL + 1 - chq:2 * MAX_REL][:, ::-1]], axis=1)
    flat = jnp.tile(jnp.pad(w, ((0, 0), (0, 1))), (1, chq))[:, :chq * lw]
    return flat.reshape(HB, chq, lw)[:, :, chq - 1:chq - 1 + W]


def _outproj_kernel(*refs, n_in, n_first):
    ys, ws = refs[:2 * n_in], refs[2 * n_in:3 * n_in]
    x_refs, (gt_ref, o_ref) = refs[3 * n_in:-2], refs[-2:]
    first = pl.program_id(0) < n_first
    acc = None
    for j, w_ref in enumerate(ws):
        y = jnp.where(first, ys[2 * j][...], ys[2 * j + 1][...])
        part = _dot(y, w_ref[...])
        acc = part if acc is None else acc + part
    x = _stream_block(x_refs, n_first if len(x_refs) == 2 else None)
    o_ref[...] = x + gt_ref[...] * acc.reshape(x.shape)


def _out_projection(ys, ws, x, mod_pb, gt_col):
    npb, pb, d = mod_pb.shape[0], PB, ws[0].shape[1]
    bB = 32
    tm = bB * pb
    n_in = len(ys)
    x_args, x_specs, x_first = _stream_operands(x, bB)
    n_first = ys[0][0].shape[0] // tm
    assert all(y0.shape[0] % tm == 0 and y1.shape[0] % tm == 0 for y0, y1 in ys)
    in_specs = []
    for y0, y1 in ys:
        in_specs += [pl.BlockSpec((tm, y0.shape[1]), lambda i: (jnp.minimum(i, n_first - 1), 0)),
                     pl.BlockSpec((tm, y1.shape[1]), lambda i: (jnp.maximum(i - n_first, 0), 0))]
    assert x_first is None or x_first == n_first
    in_specs += [pl.BlockSpec(w.shape, lambda i: (0, 0)) for w in ws]
    in_specs += x_specs + [pl.BlockSpec((bB, 1, d), lambda i: (i, 0, gt_col))]
    return pl.pallas_call(
        functools.partial(_outproj_kernel, n_in=n_in, n_first=n_first),
        grid=(npb // bB,),
        in_specs=in_specs,
        out_specs=pl.BlockSpec((bB, pb, d), lambda i: (i, 0, 0)),
        out_shape=jax.ShapeDtypeStruct((npb, pb, d), F32),
        compiler_params=_cparams(("parallel",)),
        name="out_projection",
    )(*[y for pair in ys for y in pair], *ws, *x_args, mod_pb)


def _gdnpre_kernel(qkv_ref, ab_ref, cb_ref, cw_ref, alog_ref, dtb_ref,
                   q_ref, k_ref, v_ref, gb_ref, gt_ref, cbo_ref, xbuf, *, C):
    tT = qkv_ref.shape[0]

    @pl.when(pl.program_id(1) == 0)
    def _():
        xbuf[0:SUBLANES, :] = jnp.zeros((SUBLANES, N_QKV), F32)
        xbuf[SUBLANES - (CONV_W - 1):SUBLANES, :] = cb_ref[...]

    for cb in range(N_QKV // DK):
        cols = slice(cb * DK, (cb + 1) * DK)
        x = qkv_ref[:, cols]
        xbuf[SUBLANES:SUBLANES + tT, cols] = x
        xc = cw_ref[3:4, cols] * x
        for j in range(CONV_W - 1):
            xc = xc + cw_ref[j:j + 1, cols] * xbuf[SUBLANES - 3 + j:SUBLANES - 3 + j + tT, cols]
        xs = xc * jax.nn.sigmoid(xc)
        if cb < 2 * HC:
            xn = xs * lax.rsqrt(jnp.sum(xs * xs, axis=-1, keepdims=True) + NORM_EPS)
            if cb < HC:
                q_ref[:, cols] = xn * (DK ** -0.5)
            else:
                k_ref[:, (cb - HC) * DK:(cb - HC + 1) * DK] = xn
        else:
            v_ref[:, (cb - 2 * HC) * DV:(cb - 2 * HC + 1) * DV] = xs
    tail = xbuf[tT:tT + SUBLANES, :]
    xbuf[0:SUBLANES, :] = tail
    cbo_ref[...] = tail[SUBLANES - (CONV_W - 1):, :]

    ab = ab_ref[...]
    g = -jnp.exp(alog_ref[...]) * _softplus(ab + dtb_ref[...])
    beta = jax.nn.sigmoid(ab)
    lane = lax.broadcasted_iota(I32, ab.shape, 1)
    g = jnp.where(lane < HC, g, 0.0)
    ri = lax.broadcasted_iota(I32, (C, C), 0)
    ci = lax.broadcasted_iota(I32, (C, C), 1)
    tril = (ri >= ci).astype(F32)
    triu = (ri <= ci).astype(F32)
    for c in range(tT // C):
        gc = g[c * C:(c + 1) * C, :]
        G = _dot(tril, gc, HI)
        lane_c = lax.broadcasted_iota(I32, (C, LANES), 1)
        gb_ref[c * C:(c + 1) * C, :] = jnp.where(lane_c < HC, G, beta[c * C:(c + 1) * C, :])
        gt_ref[c * 2 * HC:(c + 1) * 2 * HC, :] = _dot_tn(gc, triu, HI)[:2 * HC, :]


def _gdn_pre(proj, row0, B, T, tT, C, conv_buf, cw, alog_pad, dtb_pad):
    nT = T // tT
    rb0 = row0 // tT
    cpt = tT // C
    return pl.pallas_call(
        functools.partial(_gdnpre_kernel, C=C),
        grid=(B, nT),
        in_specs=[pl.BlockSpec((tT, N_QKV), lambda b, t: (rb0 + b * nT + t, 0)),
                  pl.BlockSpec((tT, LANES), lambda b, t: (rb0 + b * nT + t, (N_QKV + DCV) // LANES)),
                  pl.BlockSpec((None, CONV_W - 1, N_QKV), lambda b, t: (b, 0, 0)),
                  pl.BlockSpec((CONV_W, N_QKV), lambda b, t: (0, 0)),
                  pl.BlockSpec((1, LANES), lambda b, t: (0, 0)),
                  pl.BlockSpec((1, LANES), lambda b, t: (0, 0))],
        out_specs=[pl.BlockSpec((tT, DCK), lambda b, t: (b * nT + t, 0)),
                   pl.BlockSpec((tT, DCK), lambda b, t: (b * nT + t, 0)),
                   pl.BlockSpec((tT, DCV), lambda b, t: (b * nT + t, 0)),
                   pl.BlockSpec((tT, LANES), lambda b, t: (b * nT + t, 0)),
                   pl.BlockSpec((cpt * 2 * HC, C), lambda b, t: (b * nT + t, 0)),
                   pl.BlockSpec((None, CONV_W - 1, N_QKV), lambda b, t: (b, 0, 0))],
        out_shape=[jax.ShapeDtypeStruct((B * T, DCK), F32),
                   jax.ShapeDtypeStruct((B * T, DCK), F32),
                   jax.ShapeDtypeStruct((B * T, DCV), F32),
                   jax.ShapeDtypeStruct((B * T, LANES), F32),
                   jax.ShapeDtypeStruct((B * T // C * 2 * HC, C), F32),
                   jax.ShapeDtypeStruct((B, CONV_W - 1, N_QKV), F32)],
        scratch_shapes=[pltpu.VMEM((tT + SUBLANES, N_QKV), F32)],
        compiler_params=_cparams(("parallel", "arbitrary")),
        name="gdn_pre",
    )(proj, proj, conv_buf, cw, alog_pad, dtb_pad)


def _split_bf16(a):
    hi = a.astype(BF16)
    return hi, (a - hi.astype(F32)).astype(BF16)


def _dot3(a, b):
    return _dot(a[0], b[0]) + (_dot(a[0], b[1]) + _dot(a[1], b[0]))


def _delta_kernel(q_ref, k_ref, v_ref, gb_ref, gt_ref, z_ref, s0_ref, ng_ref, o_ref, so_ref,
                  s_scr, u_scr, w_scr, qd_scr, kd_scr, a_scr, *, C, hps, unroll):
    T = q_ref.shape[0]
    n_chunks = T // C
    ri = lax.broadcasted_iota(I32, (C, C), 0)
    ci = lax.broadcasted_iota(I32, (C, C), 1)
    causal = ri >= ci
    strict = ri > ci
    eye = (ri == ci).astype(F32)
    head0 = pl.program_id(1) * hps
    lane = lax.broadcasted_iota(I32, (C, LANES), 1)

    def pick(x, col):
        lane_x = lax.broadcasted_iota(I32, x.shape, 1)
        return jnp.sum(jnp.where(lane_x == col, x, 0.0), axis=1, keepdims=True)

    def operands(n, j):
        rows = pl.ds(pl.multiple_of(n * C, C), C)
        cs = slice(j * DK, (j + 1) * DK)
        gbc = gb_ref[rows, :]
        G = pick(gbc, head0 + j)
        beta = pick(gbc, head0 + j + HC)
        return rows, cs, G, beta

    def prepare(streams):
        def lower_left(m):
            s = m.bit_length() - 1
            return jnp.logical_and(((ri ^ ci) >> s) == 1, ((ri >> s) & 1) == 1)

        Bs, Ps = [], []
        for n, j in streams:
            rows, cs, G, beta = operands(n, j)
            g_row = gt_ref[pl.ds(n * (2 * HC) + head0 + j, 1), :]
            k = k_ref[rows, cs]
            L = jnp.where(causal, jnp.exp(jnp.where(causal, G - g_row, 0.0)), 0.0)
            qk = _dot_nt(jnp.concatenate([k * beta, q_ref[rows, cs]], axis=0).astype(BF16), k.astype(BF16))
            a_scr[j, rows, :] = (qk[C:] * L).astype(BF16)
            Bm = jnp.where(strict, -(qk[:C] * L), 0.0)
            Bs.append(_split_bf16(Bm))
            Ps.append(eye + jnp.where(lower_left(1), Bm, 0.0))
        zero = jnp.zeros((C, C), BF16)
        m = 2
        while m < C:
            pair = lower_left(m)
            Ds = [_split_bf16(p) for p in Ps]
            Es = [(jnp.where(pair, b[0], zero), jnp.where(pair, b[1], zero)) for b in Bs]
            Ws = [_split_bf16(_dot3(e, d)) for e, d in zip(Es, Ds)]
            Ps = [p + _dot3(d, w) for p, d, w in zip(Ps, Ds, Ws)]
            m *= 2
        for (n, j), P in zip(streams, Ps):
            rows, cs, G, beta = operands(n, j)
            k = k_ref[rows, cs]
            eG = jnp.exp(G)
            rhs = jnp.concatenate([v_ref[rows, cs] * beta, (k * beta) * eG], axis=1)
            X = _dot3(_split_bf16(P), _split_bf16(rhs))
            u_scr[j, rows, :] = X[:, :DV]
            w_scr[j, rows, :] = X[:, DV:].astype(BF16)
            qd_scr[j, rows, :] = (q_ref[rows, cs] * eG).astype(BF16)
            kd_scr[j, rows, :] = (k * jnp.exp(G[C - 1:C, :] - G)).astype(BF16)

    def prepare_body(i, carry):
        prepare([(i * unroll + m, j) for m in range(unroll) for j in range(hps)])
        return carry

    lax.fori_loop(0, n_chunks // unroll, prepare_body, 0)

    s_scr[...] = s0_ref[...]

    def advance(n, carry):
        r0 = pl.multiple_of(n * C, C)
        rows = pl.ds(r0, C)
        g_end = gb_ref[pl.ds(r0 + C - 1, 1), :]
        heads = range(hps)
        S = [s_scr[j] for j in heads]
        Sb = [s.astype(BF16) for s in S]
        wqS = [_dot(jnp.concatenate([w_scr[j, rows, :], qd_scr[j, rows, :]], axis=0), Sb[j]) for j in heads]
        wS = [x[:C] for x in wqS]
        qS = [x[C:] for x in wqS]
        vnb = [(u_scr[j, rows, :] - wS[j]).astype(BF16) for j in heads]
        kv = [_dot_tn(kd_scr[j, rows, :], vnb[j]) for j in heads]
        av = [_dot(a_scr[j, rows, :], vnb[j]) for j in heads]
        for j in heads:
            cs = slice(j * DK, (j + 1) * DK)
            s_scr[j] = S[j] * jnp.exp(pick(g_end, head0 + j)) + kv[j]
            o = qS[j] + av[j]
            on = o * lax.rsqrt(jnp.mean(o * o, axis=-1, keepdims=True) + NORM_EPS) * ng_ref[...]
            z = z_ref[rows, cs]
            o_ref[rows, cs] = (on * (z * jax.nn.sigmoid(z))).astype(BF16)
        return carry

    lax.fori_loop(0, n_chunks, advance, 0)
    so_ref[...] = s_scr[...]


def _delta_rule(q, k, v, gb, gt, proj, row0, B, T, C, S0, norm_g, hps):
    rb = row0 // T
    wb = hps * DK
    zcol0 = N_QKV // wb
    n_chunks = T // C
    qkv_spec = lambda: pl.BlockSpec((T, wb), lambda b, h: (b, h))
    return pl.pallas_call(
        functools.partial(_delta_kernel, C=C, hps=hps, unroll=4 if n_chunks % 4 == 0 else 1),
        grid=(B, HC // hps),
        in_specs=[qkv_spec(), qkv_spec(), qkv_spec(),
                  pl.BlockSpec((T, LANES), lambda b, h: (b, 0)),
                  pl.BlockSpec((n_chunks * 2 * HC, C), lambda b, h: (b, 0)),
                  pl.BlockSpec((T, wb), lambda b, h: (rb + b, zcol0 + h)),
                  pl.BlockSpec((None, hps, DK, DV), lambda b, h: (b, h, 0, 0)),
                  pl.BlockSpec((1, DV), lambda b, h: (0, 0))],
        out_specs=[pl.BlockSpec((T, wb), lambda b, h: (b, h)),
                   pl.BlockSpec((None, hps, DK, DV), lambda b, h: (b, h, 0, 0))],
        out_shape=[jax.ShapeDtypeStruct((B * T, DCV), BF16),
                   jax.ShapeDtypeStruct((B, HC, DK, DV), F32)],
        scratch_shapes=[pltpu.VMEM((hps, DK, DV), F32),
                        pltpu.VMEM((hps, T, DV), F32),
                        pltpu.VMEM((hps, T, DK), BF16),
                        pltpu.VMEM((hps, T, DK), BF16),
                        pltpu.VMEM((hps, T, DK), BF16),
                        pltpu.VMEM((hps, T, C), BF16)],
        compiler_params=_cparams(("parallel", "parallel")),
        name="delta_rule",
    )(q, k, v, gb, gt, proj, S0, norm_g.reshape(1, DV))


def _first_argmax(x, idx, size):
    m = jnp.max(x, axis=0, keepdims=True)
    am = jnp.min(jnp.where(x == m, idx, size), axis=0, keepdims=True)
    return m, am


def _router_kernel(x_ref, g_ref, sc_ref, sh_ref, rwt_ref, rb_ref,
                   h_ref, e_ref, w_ref, r_ref, cnt_ref, run_scr):
    tn = h_ref.shape[0]
    gsz = N_EXPERTS // N_GROUPS

    @pl.when(pl.program_id(0) == 0)
    def _():
        run_scr[...] = jnp.zeros_like(run_scr)

    h = _norm_mod(x_ref[...], g_ref[...], sc_ref[...], sh_ref[...])
    h_ref[...] = h
    scores = jax.nn.sigmoid(_dot_nt(rwt_ref[...], h, HI))
    sel = scores + rb_ref[...]

    gidx = lax.broadcasted_iota(I32, (gsz, tn), 0)
    gs_rows = []
    for gi in range(N_GROUPS):
        blk = sel[gi * gsz:(gi + 1) * gsz, :]
        m1, a1 = _first_argmax(blk, gidx, gsz)
        m2 = jnp.max(jnp.where(gidx == a1, -jnp.inf, blk), axis=0, keepdims=True)
        gs_rows.append(m1 + m2)
    gscore = jnp.concatenate(gs_rows, axis=0)
    gi8 = lax.broadcasted_iota(I32, (N_GROUPS, tn), 0)
    gmask = jnp.zeros((N_GROUPS, tn), jnp.bool_)
    gwork = gscore
    for _ in range(TOPK_GROUPS):
        _, a = _first_argmax(gwork, gi8, N_GROUPS)
        hit = gi8 == a
        gmask = jnp.logical_or(gmask, hit)
        gwork = jnp.where(hit, -jnp.inf, gwork)
    emask = jnp.concatenate([jnp.broadcast_to(gmask[gi:gi + 1, :], (gsz, tn)) for gi in range(N_GROUPS)], axis=0)

    eidx = lax.broadcasted_iota(I32, (N_EXPERTS, tn), 0)
    work = jnp.where(emask, sel, -jnp.inf)
    hits = []
    chosen = jnp.zeros((N_EXPERTS, tn), jnp.bool_)
    for _ in range(TOP_K):
        _, a = _first_argmax(work, eidx, N_EXPERTS)
        hit = eidx == a
        hits.append((a, hit))
        chosen = jnp.logical_or(chosen, hit)
        work = jnp.where(hit, -jnp.inf, work)
    chosen_f = chosen.astype(F32)
    denom = jnp.sum(scores * chosen_f, axis=0, keepdims=True)

    ti = lax.broadcasted_iota(I32, (tn, tn), 0)
    tj = lax.broadcasted_iota(I32, (tn, tn), 1)
    before = (ti < tj).astype(BF16)
    rank = _dot(chosen_f.astype(BF16), before) + run_scr[...]
    run_scr[...] = run_scr[...] + jnp.sum(chosen_f, axis=1, keepdims=True)
    cnt_ref[...] = jnp.broadcast_to(run_scr[...], cnt_ref.shape).astype(I32)

    e_rows, w_rows, r_rows = [], [], []
    for a, hit in hits:
        hf = hit.astype(F32)
        e_rows.append(a)
        w_rows.append(jnp.sum(scores * hf, axis=0, keepdims=True) / denom * ROUTE_SCALE)
        r_rows.append(jnp.sum(rank * hf, axis=0, keepdims=True))
    e_ref[...] = jnp.concatenate(e_rows, axis=0)
    w_ref[...] = jnp.concatenate(w_rows, axis=0)
    r_ref[...] = jnp.concatenate(r_rows, axis=0).astype(I32)


def _router(x_all, g, mod_pb, sc_col, sh_col, rwt, rb):
    npb, pb, d = x_all.shape
    n = npb * pb
    bB = 16
    tn = bB * pb
    return pl.pallas_call(
        _router_kernel,
        grid=(npb // bB,),
        in_specs=[pl.BlockSpec((bB, pb, d), lambda i: (i, 0, 0)),
                  pl.BlockSpec((1, d), lambda i: (0, 0)),
                  pl.BlockSpec((bB, 1, d), lambda i: (i, 0, sc_col)),
                  pl.BlockSpec((bB, 1, d), lambda i: (i, 0, sh_col)),
                  pl.BlockSpec((N_EXPERTS, d), lambda i: (0, 0)),
                  pl.BlockSpec((N_EXPERTS, 1), lambda i: (0, 0))],
        out_specs=[pl.BlockSpec((tn, d), lambda i: (i, 0)),
                   pl.BlockSpec((TOP_K, tn), lambda i: (0, i)),
                   pl.BlockSpec((TOP_K, tn), lambda i: (0, i)),
                   pl.BlockSpec((TOP_K, tn), lambda i: (0, i)),
                   pl.BlockSpec((N_EXPERTS, LANES), lambda i: (0, 0))],
        out_shape=[jax.ShapeDtypeStruct((n, d), F32),
                   jax.ShapeDtypeStruct((TOP_K, n), I32),
                   jax.ShapeDtypeStruct((TOP_K, n), F32),
                   jax.ShapeDtypeStruct((TOP_K, n), I32),
                   jax.ShapeDtypeStruct((N_EXPERTS, LANES), I32)],
        scratch_shapes=[pltpu.VMEM((N_EXPERTS, 1), F32)],
        compiler_params=_cparams(("arbitrary",)),
        name="moe_router",
    )(x_all, g.reshape(1, d), mod_pb, mod_pb, rwt, rb.reshape(N_EXPERTS, 1))


PACK_ROWS = D_MODEL // LANES


def _row_slab(ref, r):
    return ref.at[pl.ds(pl.multiple_of(r * PACK_ROWS, PACK_ROWS), PACK_ROWS), :]


def _to_row_tiles(dst_ref, words, row0=0):
    rows = words.shape[0]
    for c in range(PACK_ROWS):
        dst_ref[pl.ds(row0 * PACK_ROWS + c, rows, stride=PACK_ROWS), :] = words[:, c * LANES:(c + 1) * LANES]


def _lane_tile(src_ref, c, rows, row0=0):
    return src_ref[pl.ds(row0 * PACK_ROWS + c, rows, stride=PACK_ROWS), :]


def _from_row_tiles(src_ref, rows):
    return jnp.concatenate([_lane_tile(src_ref, c, rows) for c in range(PACK_ROWS)], axis=1)


def _dispatch_kernel(zf_ref, pos_ref, h_ref, sgu_ref, sd_ref, xs_hbm, shared_ref, sbuf, zbuf, sem, zsem, *, tm):
    i = pl.program_id(0)
    n_steps = pl.num_programs(0)
    td = h_ref.shape[0]
    n_tiles = zf_ref.shape[0]
    slot = lax.rem(i, 2)

    @pl.when(i == 0)
    def _():
        zbuf[...] = jnp.zeros_like(zbuf)

        def zero_copy(t):
            r0 = pl.multiple_of(t * (tm * PACK_ROWS), tm * PACK_ROWS)
            return pltpu.make_async_copy(zbuf, xs_hbm.at[pl.ds(r0, tm * PACK_ROWS), :], zsem)

        def start(t, carry):
            @pl.when(zf_ref[t] != 0)
            def _():
                zero_copy(t).start()
            return carry

        def wait(t, carry):
            @pl.when(zf_ref[t] != 0)
            def _():
                zero_copy(t).wait()
            return carry

        lax.fori_loop(0, n_tiles, start, 0)
        lax.fori_loop(0, n_tiles, wait, 0)

    def drain(s):
        for _ in range(TOP_K):
            pltpu.make_async_copy(sbuf.at[s], xs_hbm.at[pl.ds(0, td * PACK_ROWS), :], sem.at[s]).wait()

    @pl.when(i >= 2)
    def _():
        drain(slot)

    h = h_ref[...]
    _to_row_tiles(sbuf.at[slot], h)
    gu = _dot(h.astype(BF16), sgu_ref[...])
    hid = jax.nn.silu(gu[:, :D_SHARED]) * gu[:, D_SHARED:]
    shared_ref[...] = _dot(hid.astype(BF16), sd_ref[...])

    def body(j, carry):
        t0 = pl.multiple_of(j * SUBLANES, SUBLANES)
        for u in range(SUBLANES):
            src = _row_slab(sbuf.at[slot], t0 + u)
            for k in range(TOP_K):
                p = pos_ref[0, k * td + t0 + u]
                pltpu.make_async_copy(src, _row_slab(xs_hbm, p), sem.at[slot]).start(priority=k % 2)
        return carry

    lax.fori_loop(0, td // SUBLANES, body, 0)

    @pl.when(i == n_steps - 1)
    def _():
        drain(slot)
        drain(1 - slot)


def _moe_dispatch(h2, pos_tiles, zero_flag, sgu, sd, n_tiles, tm, td):
    n, d = h2.shape
    grid_spec = pltpu.PrefetchScalarGridSpec(
        num_scalar_prefetch=1,
        grid=(n // td,),
        in_specs=[pl.BlockSpec((None, 1, TOP_K * td), lambda i, zf: (i, 0, 0), memory_space=pltpu.SMEM),
                  pl.BlockSpec((td, d), lambda i, zf: (i, 0)),
                  pl.BlockSpec(sgu.shape, lambda i, zf: (0, 0)),
                  pl.BlockSpec(sd.shape, lambda i, zf: (0, 0))],
        out_specs=[pl.BlockSpec(memory_space=pl.ANY), pl.BlockSpec((td, d), lambda i, zf: (i, 0))],
        scratch_shapes=[pltpu.VMEM((2, td * PACK_ROWS, LANES), F32), pltpu.VMEM((tm * PACK_ROWS, LANES), F32),
                        pltpu.SemaphoreType.DMA((2,)), pltpu.SemaphoreType.DMA(())],
    )
    return pl.pallas_call(
        functools.partial(_dispatch_kernel, tm=tm),
        grid_spec=grid_spec,
        out_shape=[jax.ShapeDtypeStruct((n_tiles * tm * PACK_ROWS, LANES), F32),
                   jax.ShapeDtypeStruct((n, d), F32)],
        compiler_params=_cparams(("arbitrary",)),
        name="moe_dispatch",
    )(zero_flag, pos_tiles, h2, sgu, sd)


def _experts_kernel(te_ref, nu_ref, xs_ref, wg_ref, wu_ref, wd_ref, ys_ref, wgu_s, wd_s):
    i = pl.program_id(0)
    nu = nu_ref[0]

    @pl.when(jnp.logical_or(i == 0, te_ref[i] != te_ref[jnp.maximum(i - 1, 0)]))
    def _():
        wgu_s[:, :D_EXPERT] = wg_ref[...].astype(BF16)
        wgu_s[:, D_EXPERT:] = wu_ref[...].astype(BF16)
        wd_s[...] = wd_ref[...].astype(BF16)

    @pl.when(i < nu)
    def _():
        x = _from_row_tiles(xs_ref, xs_ref.shape[0] // PACK_ROWS).astype(BF16)
        gu = _dot(x, wgu_s[...])
        hid = jax.nn.silu(gu[:, :D_EXPERT]) * gu[:, D_EXPERT:]
        _to_row_tiles(ys_ref, _dot(hid.astype(BF16), wd_s[...]))

    @pl.when(i >= nu)
    def _():
        ys_ref[...] = jnp.zeros_like(ys_ref)


def _routed_experts(xs, tile_expert, n_used, layer, wg, wu, wd, tm):
    n_tiles = xs.shape[0] // (tm * PACK_ROWS)
    d = wg.shape[2]
    grid_spec = pltpu.PrefetchScalarGridSpec(
        num_scalar_prefetch=2,
        grid=(n_tiles,),
        in_specs=[pl.BlockSpec((tm * PACK_ROWS, LANES), lambda i, te, nu: (jnp.minimum(i, nu[0] - 1), 0)),
                  pl.BlockSpec((None, None, d, D_EXPERT), lambda i, te, nu: (layer, te[i], 0, 0)),
                  pl.BlockSpec((None, None, d, D_EXPERT), lambda i, te, nu: (layer, te[i], 0, 0)),
                  pl.BlockSpec((None, None, D_EXPERT, d), lambda i, te, nu: (layer, te[i], 0, 0))],
        out_specs=pl.BlockSpec((tm * PACK_ROWS, LANES), lambda i, te, nu: (i, 0)),
        scratch_shapes=[pltpu.VMEM((d, 2 * D_EXPERT), BF16), pltpu.VMEM((D_EXPERT, d), BF16)],
    )
    return pl.pallas_call(
        _experts_kernel,
        grid_spec=grid_spec,
        out_shape=jax.ShapeDtypeStruct(xs.shape, F32),
        compiler_params=_cparams(("arbitrary",)),
        name="moe_experts",
    )(tile_expert, n_used, xs, wg, wu, wd)


def _gather_rows(idx_ref, n_rows, src_hbm, dst, sem):
    group = 4 * SUBLANES

    def body(j, carry):
        r0 = pl.multiple_of(j * group, group)
        for u in range(group):
            pltpu.make_async_copy(_row_slab(src_hbm, idx_ref[0, r0 + u]), _row_slab(dst, r0 + u),
                                  sem).start(priority=u % 2)
        return carry
    lax.fori_loop(0, n_rows // group, body, 0)


def _combine_kernel(pos_ref, nxt_ref, ys_hbm, w_ref, shared_ref, x_ref, gt_ref, o_ref,
                    gbuf0, gbuf1, acc_scr, sem):
    i = pl.program_id(0)
    last = pl.num_programs(0) - 1
    tb = w_ref.shape[0]
    n_rows = TOP_K * tb
    n_groups = tb // SUBLANES
    per_group = n_rows // n_groups

    def wait_tile(buf, s):
        pltpu.make_async_copy(ys_hbm.at[pl.ds(0, n_rows * PACK_ROWS), :], buf, sem.at[s]).wait()

    @pl.when(i == 0)
    def _():
        _gather_rows(pos_ref, n_rows, ys_hbm, gbuf0, sem.at[0])

    def run(cur, cur_s, nxt, nxt_s):
        wait_tile(cur, cur_s)

        for g in range(n_groups):
            t0 = g * SUBLANES
            for u in range(per_group):
                r = g * per_group + u
                pltpu.make_async_copy(_row_slab(ys_hbm, nxt_ref[0, r]), nxt.at[r * PACK_ROWS:(r + 1) * PACK_ROWS, :],
                                      sem.at[nxt_s]).start(priority=u % 2)
            w8 = w_ref[t0:t0 + SUBLANES, :]
            wk = [jnp.broadcast_to(w8[:, k:k + 1], (SUBLANES, LANES)) for k in range(TOP_K)]
            for c in range(PACK_ROWS):
                a = wk[0] * _lane_tile(cur, c, SUBLANES, t0)
                for k in range(1, TOP_K):
                    a = a + wk[k] * _lane_tile(cur, c, SUBLANES, k * tb + t0)
                acc_scr[t0:t0 + SUBLANES, c * LANES:(c + 1) * LANES] = a

        @pl.when(i == last)
        def _():
            wait_tile(nxt, nxt_s)

    @pl.when(lax.rem(i, 2) == 0)
    def _():
        run(gbuf0, 0, gbuf1, 1)

    @pl.when(lax.rem(i, 2) == 1)
    def _():
        run(gbuf1, 1, gbuf0, 0)

    acc = shared_ref[...] + acc_scr[...]
    bB, pb, d = x_ref.shape
    o_ref[...] = x_ref[...] + gt_ref[...] * acc.reshape(bB, pb, d)


def _moe_combine(ys, pos_tiles, w_tok, shared, x_all, mod_pb, gt_col, tb):
    npb, pb, d = x_all.shape
    n_tb = pos_tiles.shape[0]
    bB = tb // pb
    last = n_tb - 1
    return pl.pallas_call(
        _combine_kernel,
        grid=(n_tb,),
        in_specs=[pl.BlockSpec((None, 1, TOP_K * tb), lambda i: (i, 0, 0), memory_space=pltpu.SMEM),
                  pl.BlockSpec((None, 1, TOP_K * tb), lambda i: (jnp.minimum(i + 1, last), 0, 0),
                               memory_space=pltpu.SMEM),
                  pl.BlockSpec(memory_space=pl.ANY),
                  pl.BlockSpec((tb, TOP_K), lambda i: (i, 0)),
                  pl.BlockSpec((tb, d), lambda i: (i, 0)),
                  pl.BlockSpec((bB, pb, d), lambda i: (i, 0, 0)),
                  pl.BlockSpec((bB, 1, d), lambda i: (i, 0, gt_col))],
        out_specs=pl.BlockSpec((bB, pb, d), lambda i: (i, 0, 0)),
        out_shape=jax.ShapeDtypeStruct(x_all.shape, F32),
        scratch_shapes=[pltpu.VMEM((TOP_K * tb * PACK_ROWS, LANES), F32),
                        pltpu.VMEM((TOP_K * tb * PACK_ROWS, LANES), F32),
                        pltpu.VMEM((tb, d), F32), pltpu.SemaphoreType.DMA((2,))],
        compiler_params=_cparams(("arbitrary",)),
        name="moe_combine",
    )(pos_tiles, pos_tiles, ys, w_tok, shared, x_all, mod_pb)


def _moe(x_all, g, mod_pb, rw, rb, layer, wg, wu, wd, sg, su, sd):
    npb, pb, d = x_all.shape
    n = npb * pb
    tm = 512
    tb = 128
    h2, e_t, w_t, r_t, cnt = _router(x_all, g, mod_pb, 4, 3, rw.T, rb)

    counts = cnt[:, 0]
    padded = (counts + tm - 1) // tm * tm
    ends = jnp.cumsum(padded)
    starts = ends - padded
    n_tiles = (n * TOP_K + N_EXPERTS * (tm - 1)) // tm
    onehot = e_t[:, :, None] == jnp.arange(N_EXPERTS, dtype=I32)
    pos = jnp.sum(jnp.where(onehot, starts, 0), axis=-1) + r_t
    n_used = (ends[-1] // tm).astype(I32)
    tile_idx = jnp.arange(n_tiles, dtype=I32)
    tile_start = jnp.minimum(tile_idx, n_used - 1) * tm
    tile_expert = jnp.sum(ends[None, :] <= tile_start[:, None], axis=1).astype(I32)
    is_seg_end = jnp.any((ends[None, :] == (tile_idx[:, None] + 1) * tm) & (padded[None, :] > 0), axis=1)
    zero_flag = jnp.logical_or(is_seg_end, tile_idx >= n_used).astype(I32)
    pos_tiles = pos.reshape(TOP_K, n // tb, tb).transpose(1, 0, 2).reshape(n // tb, 1, TOP_K * tb)

    sgu = jnp.concatenate([sg, su], axis=-1).astype(BF16)
    xs, shared = _moe_dispatch(h2, pos_tiles, zero_flag, sgu, sd.astype(BF16), n_tiles, tm, tb)
    ys = _routed_experts(xs, tile_expert, n_used.reshape(1), layer, wg, wu, wd, tm)
    return _moe_combine(ys, pos_tiles, w_t.T, shared, x_all, mod_pb, 5, tb)


def _final_kernel(x_ref, g_ref, o_ref):
    x = x_ref[...]
    ms = jnp.mean(x * x, axis=-1, keepdims=True)
    o_ref[...] = x * lax.rsqrt(ms + NORM_EPS) * g_ref[...]


def _final_norm(x_all, g, pb0, n_pb):
    npb, pb, d = x_all.shape
    bB = 32
    return pl.pallas_call(
        _final_kernel,
        grid=(n_pb // bB,),
        in_specs=[pl.BlockSpec((bB, pb, d), lambda i: (pb0 // bB + i, 0, 0)),
                  pl.BlockSpec((1, d), lambda i: (0, 0))],
        out_specs=pl.BlockSpec((bB, pb, d), lambda i: (i, 0, 0)),
        out_shape=jax.ShapeDtypeStruct((n_pb, pb, d), F32),
        compiler_params=_cparams(("parallel",)),
        name="final_norm",
    )(x_all, g.reshape(1, d))


def _block_diag(w):
    nb, di, do = w.shape
    return jnp.einsum("nde,nm->ndme", w, jnp.eye(nb, dtype=w.dtype)).reshape(nb * di, nb * do)


def kernel(x_prompt, x_sample, c_prompt, c_sample, cache_b_k, cache_b_v, state_a_conv, state_a_h, state_c_conv, state_c_S, norm1_g, norm2_g, final_g, w_mod, b_mod, w_in0, w_out0, a_conv_w, a_conv_b, a_w_r, a_b_r, a_w_i, a_b_i, a_lambda, b_rel_bias, w_in1, w_out1, c_conv_w, c_A_log, c_dt_bias, c_norm_g, router_w, router_bias, e_w_gate, e_w_up, e_w_down, s_w_gate, s_w_up, s_w_down):
    Bp, Tp, d = x_prompt.shape
    Bs, Ts, _ = x_sample.shape
    np_tok, ns_tok = Bp * Tp, Bs * Ts
    npb_p, npb_s = np_tok // PB, ns_tok // PB
    depth = w_mod.shape[0]

    x_all = (x_prompt.reshape(npb_p, PB, d), x_sample.reshape(npb_s, PB, d))
    mod = _modulation(jnp.concatenate([c_prompt, c_sample], axis=0), w_mod, b_mod)

    outs = {}
    for l in range(depth):
        mod_pb = jnp.concatenate([jnp.repeat(mod[l, :Bp], Tp // PB, axis=0),
                                  jnp.repeat(mod[l, Bp:], Ts // PB, axis=0)], axis=0)[:, None, :]
        j = l // 2
        if l % 2 == 0:
            proj = _in_projection(x_all, norm1_g[l], mod_pb, 1, 0, w_in0[j].astype(BF16), 1280)
            wr, wi = _block_diag(a_w_r[j]).astype(BF16), _block_diag(a_w_i[j]).astype(BF16)
            a_args = (a_conv_w[j], a_conv_b[j], wr, a_b_r[j], wi, a_b_i[j], a_lambda[j])
            ya_p, ac_p, ah_p = _rglru(proj, 0, Bp, Tp, 512, jnp.zeros((Bp, CONV_W - 1, DA), F32),
                                      jnp.zeros((Bp, DA), F32), *a_args)
            ya_s, ac_s, ah_s = _rglru(proj, np_tok, Bs, Ts, Ts, state_a_conv[j], state_a_h[j], *a_args)
            yb_p, bk_p, bv_p = _band_attention(proj, 0, Bp, Tp, CHUNK, _rel_bias(b_rel_bias[j], CHUNK))
            yb_s, bk_s, bv_s = _band_attention(proj, np_tok, Bs, Ts, Ts, _rel_bias(b_rel_bias[j], Ts),
                                               cache_b_k[j].reshape(Bs, BAND_PAST, DB),
                                               cache_b_v[j].reshape(Bs, BAND_PAST, DB))
            w_out = w_out0[j].astype(BF16)
            x_all = _out_projection([(ya_p, ya_s), (yb_p, yb_s)], [w_out[:DA], w_out[DA:]], x_all, mod_pb, 2)
            outs.setdefault("bk_p", []).append(bk_p.reshape(Bp, -1, HB, DHB))
            outs.setdefault("bv_p", []).append(bv_p.reshape(Bp, -1, HB, DHB))
            outs.setdefault("bk_s", []).append(bk_s.reshape(Bs, -1, HB, DHB))
            outs.setdefault("bv_s", []).append(bv_s.reshape(Bs, -1, HB, DHB))
            outs.setdefault("ac_p", []).append(ac_p)
            outs.setdefault("ac_s", []).append(ac_s)
            outs.setdefault("ah_p", []).append(ah_p.reshape(Bp, DA))
            outs.setdefault("ah_s", []).append(ah_s.reshape(Bs, DA))
        else:
            w_in = jnp.pad(w_in1[j], ((0, 0), (0, IN1_PAD - IN1))).astype(BF16)
            proj = _in_projection(x_all, norm1_g[l], mod_pb, 1, 0, w_in, 1408)
            alog = jnp.pad(c_A_log[j], (0, LANES - HC)).reshape(1, LANES)
            dtb = jnp.pad(c_dt_bias[j], (0, LANES - HC)).reshape(1, LANES)
            cp, cs = min(DELTA_BLOCK, Tp), min(DELTA_BLOCK, Ts)
            q_p, k_p, v_p, gb_p, gt_p, cc_p = _gdn_pre(proj, 0, Bp, Tp, 256, cp,
                                                       jnp.zeros((Bp, CONV_W - 1, N_QKV), F32), c_conv_w[j], alog, dtb)
            q_s, k_s, v_s, gb_s, gt_s, cc_s = _gdn_pre(proj, np_tok, Bs, Ts, Ts, cs, state_c_conv[j], c_conv_w[j],
                                                       alog, dtb)
            o_p, cs_p = _delta_rule(q_p, k_p, v_p, gb_p, gt_p, proj, 0, Bp, Tp, cp,
                                    jnp.zeros((Bp, HC, DK, DV), F32), c_norm_g[j], 2)
            o_s, cs_s = _delta_rule(q_s, k_s, v_s, gb_s, gt_s, proj, np_tok, Bs, Ts, cs, state_c_S[j],
                                    c_norm_g[j], 4)
            x_all = _out_projection([(o_p, o_s)], [w_out1[j].astype(BF16)], x_all, mod_pb, 2)
            outs.setdefault("cc_p", []).append(cc_p)
            outs.setdefault("cc_s", []).append(cc_s)
            outs.setdefault("cs_p", []).append(cs_p)
            outs.setdefault("cs_s", []).append(cs_s)
        x_all = _moe(x_all, norm2_g[l], mod_pb, router_w[l], router_bias[l], l, e_w_gate, e_w_up, e_w_down,
                     s_w_gate[l], s_w_up[l], s_w_down[l])

    y_prompt = _final_norm(x_all, final_g, 0, npb_p).reshape(Bp, Tp, d)
    y_sample = _final_norm(x_all, final_g, npb_p, npb_s).reshape(Bs, Ts, d)
    st = {k: jnp.stack(v) for k, v in outs.items()}
    return (y_prompt, y_sample, st["bk_p"], st["bk_s"], st["bv_p"], st["bv_s"], st["ac_p"], st["ac_s"],
            st["ah_p"], st["ah_s"], st["cc_p"], st["cc_s"], st["cs_p"], st["cs_s"])
```

```python
import functools

import jax
import jax.numpy as jnp
from jax import lax
from jax.experimental import pallas as pl
from jax.experimental.pallas import tpu as pltpu

F32 = jnp.float32
BF16 = jnp.bfloat16
I32 = jnp.int32

D_MODEL = 1024
CHUNK = 64
NORM_EPS = 1e-6
CONV_W = 4
DA = 512
NB_A = 8
LRU_C = 8.0
HB = 8
DHB = 64
DB = HB * DHB
BAND_PAST = 8 * CHUNK
MAX_REL = 128
HC = 8
DK = 128
DV = 128
DCK = HC * DK
DCV = HC * DV
N_QKV = 2 * DCK + DCV
DELTA_BLOCK = 128
N_EXPERTS = 64
TOP_K = 8
N_GROUPS = 8
TOPK_GROUPS = 4
D_EXPERT = 256
D_SHARED = 256
ROUTE_SCALE = 2.5
IN0 = 2 * DA + 3 * DB
IN1 = N_QKV + DCV + 2 * HC

LANES = 128
SUBLANES = 8
PB = 32
IN1_PAD = N_QKV + DCV + LANES
VMEM_LIMIT = 48 * 1024 * 1024
HI = lax.Precision.HIGHEST


def _cparams(sem):
    return pltpu.CompilerParams(dimension_semantics=sem, vmem_limit_bytes=VMEM_LIMIT)


def _softplus(x):
    return jnp.maximum(x, 0.0) + jnp.log1p(jnp.exp(-jnp.abs(x)))


def _dot(a, b, precision=None):
    return jnp.dot(a, b, preferred_element_type=F32, precision=precision)


def _dot_nt(a, b, precision=None):
    return lax.dot_general(a, b, (((1,), (1,)), ((), ())), preferred_element_type=F32, precision=precision)


def _dot_tn(a, b, precision=None):
    return lax.dot_general(a, b, (((0,), (0,)), ((), ())), preferred_element_type=F32, precision=precision)


def _mod_kernel(c_ref, w_ref, b_ref, o_ref):
    c = c_ref[...]
    cond = c * jax.nn.sigmoid(c)
    o_ref[0] = _dot(cond.astype(BF16), w_ref[0].astype(BF16)) + b_ref[0]


def _modulation(c_all, w_mod, b_mod):
    depth, d, n6 = w_mod.shape
    nb = c_all.shape[0]
    tn = 1536
    return pl.pallas_call(
        _mod_kernel,
        grid=(depth, n6 // tn),
        in_specs=[pl.BlockSpec((nb, d), lambda l, j: (0, 0)),
                  pl.BlockSpec((1, d, tn), lambda l, j: (l, 0, j)),
                  pl.BlockSpec((1, 1, tn), lambda l, j: (l, 0, j))],
        out_specs=pl.BlockSpec((1, nb, tn), lambda l, j: (l, 0, j)),
        out_shape=jax.ShapeDtypeStruct((depth, nb, n6), F32),
        compiler_params=_cparams(("parallel", "parallel")),
        name="modulation",
    )(c_all, w_mod, b_mod.reshape(depth, 1, n6))


def _norm_mod(x3, g, sc, sh):
    ms = jnp.mean(x3 * x3, axis=-1, keepdims=True)
    y = x3 * lax.rsqrt(ms + NORM_EPS) * g
    h = y * (1.0 + sc) + sh
    return h.reshape(x3.shape[0] * x3.shape[1], x3.shape[2])


def _stream_operands(x, bB):
    if not isinstance(x, tuple):
        return [x], [pl.BlockSpec((bB,) + x.shape[1:], lambda i, *_: (i, 0, 0))], None
    x0, x1 = x
    assert x0.shape[0] % bB == 0 and x1.shape[0] % bB == 0
    n_first = x0.shape[0] // bB
    return [x0, x1], [pl.BlockSpec((bB,) + x0.shape[1:], lambda i, *_: (jnp.minimum(i, n_first - 1), 0, 0)),
                      pl.BlockSpec((bB,) + x1.shape[1:], lambda i, *_: (jnp.maximum(i - n_first, 0), 0, 0))], n_first


def _stream_block(x_refs, n_first):
    if n_first is None:
        return x_refs[0][...]
    return jnp.where(pl.program_id(0) < n_first, x_refs[0][...], x_refs[1][...])


def _inproj_kernel(*refs, n_first):
    x_refs, (g_ref, sc_ref, sh_ref, w_ref, o_ref, h_scr) = refs[:-6], refs[-6:]

    def normalise(x_ref):
        h_scr[...] = _norm_mod(x_ref[...], g_ref[...], sc_ref[...], sh_ref[...]).astype(BF16)

    new_rows = pl.program_id(1) == 0
    if n_first is None:
        pl.when(new_rows)(lambda: normalise(x_refs[0]))
    else:
        in_first = pl.program_id(0) < n_first
        pl.when(jnp.logical_and(new_rows, in_first))(lambda: normalise(x_refs[0]))
        pl.when(jnp.logical_and(new_rows, jnp.logical_not(in_first)))(lambda: normalise(x_refs[1]))
    o_ref[...] = _dot(h_scr[...], w_ref[...])


def _in_projection(x, g, mod_pb, sc_col, sh_col, w_bf16, tn):
    npb, _, d6 = mod_pb.shape
    d, nout = w_bf16.shape
    bB = 32
    tm = bB * PB
    x_args, x_specs, n_first = _stream_operands(x, bB)
    return pl.pallas_call(
        functools.partial(_inproj_kernel, n_first=n_first),
        grid=(npb // bB, nout // tn),
        in_specs=x_specs + [pl.BlockSpec((1, d), lambda i, j: (0, 0)),
                            pl.BlockSpec((bB, 1, d), lambda i, j: (i, 0, sc_col)),
                            pl.BlockSpec((bB, 1, d), lambda i, j: (i, 0, sh_col)),
                            pl.BlockSpec((d, tn), lambda i, j: (0, j))],
        out_specs=pl.BlockSpec((tm, tn), lambda i, j: (i, j)),
        out_shape=jax.ShapeDtypeStruct((npb * PB, nout), F32),
        scratch_shapes=[pltpu.VMEM((tm, d), BF16)],
        compiler_params=_cparams(("parallel", "arbitrary")),
        name="in_projection",
    )(*x_args, g.reshape(1, d), mod_pb, mod_pb, w_bf16)


def _rglru_kernel(xa_ref, ga_ref, cb_ref, h0_ref, cw_ref, cbias_ref, wr_ref, br_ref, wi_ref, bi_ref, lam_ref,
                  y_ref, cbo_ref, ho_ref, xbuf, hcar):
    tT = xa_ref.shape[0]

    @pl.when(pl.program_id(1) == 0)
    def _():
        xbuf[0:SUBLANES, :] = jnp.zeros((SUBLANES, DA), F32)
        xbuf[SUBLANES - (CONV_W - 1):SUBLANES, :] = cb_ref[...]
        hcar[...] = h0_ref[...]

    xa = xa_ref[...]
    xbuf[SUBLANES:SUBLANES + tT, :] = xa
    xc = cw_ref[3:4, :] * xa + cbias_ref[...]
    for j in range(CONV_W - 1):
        xc = xc + cw_ref[j:j + 1, :] * xbuf[SUBLANES - 3 + j:SUBLANES - 3 + j + tT, :]
    tail = xbuf[tT:tT + SUBLANES, :]
    xbuf[0:SUBLANES, :] = tail
    cbo_ref[...] = tail[SUBLANES - (CONV_W - 1):, :]

    xcb = xc.astype(BF16)
    r = jax.nn.sigmoid(_dot(xcb, wr_ref[...]) + br_ref[...])
    i = jax.nn.sigmoid(_dot(xcb, wi_ref[...]) + bi_ref[...])
    log_a = -LRU_C * r * _softplus(-lam_ref[...])
    a = jnp.exp(log_a)
    th = jnp.tanh(log_a)
    u = jnp.sqrt(-2.0 * th / (1.0 - th)) * (i * xc)

    rows = lax.broadcasted_iota(I32, (tT, DA), 0)
    s = 1
    while s < tT:
        a_sh = pltpu.roll(a, s, 0)
        u_sh = pltpu.roll(u, s, 0)
        m = rows >= s
        u = jnp.where(m, a * u_sh + u, u)
        a = jnp.where(m, a * a_sh, a)
        s *= 2
    h = a * hcar[...] + u
    hlast = h[tT - 1:tT, :]
    hcar[...] = hlast
    ho_ref[...] = hlast
    y_ref[...] = (h * jax.nn.gelu(ga_ref[...])).astype(BF16)


def _rglru(proj, row0, B, T, tT, conv_buf, h0, cw, cbias, wr, br, wi, bi, lam):
    nT = T // tT
    rb0 = row0 // tT
    vec = lambda: pl.BlockSpec((1, DA), lambda b, t: (0, 0))
    return pl.pallas_call(
        _rglru_kernel,
        grid=(B, nT),
        in_specs=[pl.BlockSpec((tT, DA), lambda b, t: (rb0 + b * nT + t, 0)),
                  pl.BlockSpec((tT, DA), lambda b, t: (rb0 + b * nT + t, 1)),
                  pl.BlockSpec((None, CONV_W - 1, DA), lambda b, t: (b, 0, 0)),
                  pl.BlockSpec((None, 1, DA), lambda b, t: (b, 0, 0)),
                  pl.BlockSpec((CONV_W, DA), lambda b, t: (0, 0)),
                  vec(),
                  pl.BlockSpec((DA, DA), lambda b, t: (0, 0)),
                  vec(),
                  pl.BlockSpec((DA, DA), lambda b, t: (0, 0)),
                  vec(), vec()],
        out_specs=[pl.BlockSpec((tT, DA), lambda b, t: (b * nT + t, 0)),
                   pl.BlockSpec((None, CONV_W - 1, DA), lambda b, t: (b, 0, 0)),
                   pl.BlockSpec((None, 1, DA), lambda b, t: (b, 0, 0))],
        out_shape=[jax.ShapeDtypeStruct((B * T, DA), BF16),
                   jax.ShapeDtypeStruct((B, CONV_W - 1, DA), F32),
                   jax.ShapeDtypeStruct((B, 1, DA), F32)],
        scratch_shapes=[pltpu.VMEM((tT + SUBLANES, DA), F32), pltpu.VMEM((1, DA), F32)],
        compiler_params=_cparams(("parallel", "arbitrary")),
        name="rglru",
    )(proj, proj, conv_buf, h0.reshape(B, 1, DA), cw, cbias.reshape(1, DA), wr, br.reshape(1, DA),
      wi, bi.reshape(1, DA), lam.reshape(1, DA))


def _attn_kernel(*refs, chq, has_hist):
    if has_hist:
        q_ref, k_ref, v_ref, hk_ref, hv_ref, bias_ref, o_ref, ko_ref, vo_ref, kbuf, vbuf = refs
    else:
        q_ref, k_ref, v_ref, bias_ref, o_ref, ko_ref, vo_ref, kbuf, vbuf = refs
    c = pl.program_id(1)
    T = k_ref.shape[0]
    W = BAND_PAST + chq
    keep = ko_ref.shape[0]

    heads = range(HB)

    @pl.when(c == 0)
    def _():
        ko_ref[...] = k_ref[T - keep:T, :]
        vo_ref[...] = v_ref[T - keep:T, :]
        for h in heads:
            sl = slice(h * DHB, (h + 1) * DHB)
            if has_hist:
                kbuf[h, 0:BAND_PAST, :] = hk_ref[:, sl].astype(BF16)
                vbuf[h, 0:BAND_PAST, :] = hv_ref[:, sl].astype(BF16)
            else:
                kbuf[h, 0:BAND_PAST, :] = jnp.zeros((BAND_PAST, DHB), BF16)
                vbuf[h, 0:BAND_PAST, :] = jnp.zeros((BAND_PAST, DHB), BF16)
            kbuf[h, BAND_PAST:BAND_PAST + T, :] = k_ref[:, sl].astype(BF16)
            vbuf[h, BAND_PAST:BAND_PAST + T, :] = v_ref[:, sl].astype(BF16)

    cps = q_ref.shape[0] // chq
    q = q_ref[...].astype(BF16)
    pairs = [(sub, h) for sub in range(cps) for h in heads]
    starts = [pl.multiple_of((c * cps + sub) * chq, chq) for sub in range(cps)]
    wins = [pl.ds(st, W) for st in starts]
    s = [_dot_nt(q[sub * chq:(sub + 1) * chq, h * DHB:(h + 1) * DHB], kbuf[h, wins[sub], :]) * (DHB ** -0.5)
         + bias_ref[h] for sub, h in pairs]
    if not has_hist:
        col = lax.broadcasted_iota(I32, (chq, W), 1)
        valid = [st - BAND_PAST + col >= 0 for st in starts]
        s = [jnp.where(valid[sub], x, -jnp.inf) for (sub, h), x in zip(pairs, s)]
    m = [jnp.max(x, axis=-1, keepdims=True) for x in s]
    e = [jnp.exp(x - mx) for x, mx in zip(s, m)]
    den = [jnp.sum(x, axis=-1, keepdims=True) for x in e]
    o = [_dot(ex.astype(BF16), vbuf[h, wins[sub], :]) / dn for (sub, h), ex, dn in zip(pairs, e, den)]
    for sub in range(cps):
        o_ref[sub * chq:(sub + 1) * chq, :] = jnp.concatenate(o[sub * HB:(sub + 1) * HB], axis=-1).astype(BF16)


def _band_attention(proj, row0, B, T, chq, bias, hist_k=None, hist_v=None):
    cps = 2 if (T // chq) % 2 == 0 else 1
    qrows = cps * chq
    nC = T // qrows
    rbq = row0 // qrows
    rbk = row0 // T
    has_hist = hist_k is not None
    keep = min(BAND_PAST, T)
    in_specs = [pl.BlockSpec((qrows, DB), lambda b, c: (rbq + b * nC + c, 2)),
                pl.BlockSpec((T, DB), lambda b, c: (rbk + b, 3)),
                pl.BlockSpec((T, DB), lambda b, c: (rbk + b, 4))]
    args = [proj, proj, proj]
    if has_hist:
        in_specs += [pl.BlockSpec((None, BAND_PAST, DB), lambda b, c: (b, 0, 0))] * 2
        args += [hist_k, hist_v]
    in_specs.append(pl.BlockSpec((HB, chq, BAND_PAST + chq), lambda b, c: (0, 0, 0)))
    args.append(bias)
    return pl.pallas_call(
        functools.partial(_attn_kernel, chq=chq, has_hist=has_hist),
        grid=(B, nC),
        in_specs=in_specs,
        out_specs=[pl.BlockSpec((qrows, DB), lambda b, c: (b * nC + c, 0)),
                   pl.BlockSpec((None, keep, DB), lambda b, c: (b, 0, 0)),
                   pl.BlockSpec((None, keep, DB), lambda b, c: (b, 0, 0))],
        out_shape=[jax.ShapeDtypeStruct((B * T, DB), BF16),
                   jax.ShapeDtypeStruct((B, keep, DB), F32),
                   jax.ShapeDtypeStruct((B, keep, DB), F32)],
        scratch_shapes=[pltpu.VMEM((HB, BAND_PAST + T, DHB), BF16), pltpu.VMEM((HB, BAND_PAST + T, DHB), BF16)],
        compiler_params=_cparams(("parallel", "arbitrary")),
        name="band_attention",
    )(*args)


def _rel_bias(table, chq):
    W = BAND_PAST + chq
    lw = W + chq - 1
    n_clipped = BAND_PAST - MAX_REL + chq
    w = jnp.concatenate([jnp.broadcast_to(table[:, 2 * MAX_REL:], (HB, n_clipped)),
                         table[:, MAX_REL + 1 - chq:2 * MAX_REL][:, ::-1]], axis=1)
    flat = jnp.tile(jnp.pad(w, ((0, 0), (0, 1))), (1, chq))[:, :chq * lw]
    return flat.reshape(HB, chq, lw)[:, :, chq - 1:chq - 1 + W]


def _outproj_kernel(*refs, n_in, n_first):
    ys, ws = refs[:2 * n_in], refs[2 * n_in:3 * n_in]
    x_refs, (gt_ref, o_ref) = refs[3 * n_in:-2], refs[-2:]
    first = pl.program_id(0) < n_first
    acc = None
    for j, w_ref in enumerate(ws):
        y = jnp.where(first, ys[2 * j][...], ys[2 * j + 1][...])
        part = _dot(y, w_ref[...])
        acc = part if acc is None else acc + part
    x = _stream_block(x_refs, n_first if len(x_refs) == 2 else None)
    o_ref[...] = x + gt_ref[...] * acc.reshape(x.shape)


def _out_projection(ys, ws, x, mod_pb, gt_col):
    npb, pb, d = mod_pb.shape[0], PB, ws[0].shape[1]
    bB = 32
    tm = bB * pb
    n_in = len(ys)
    x_args, x_specs, x_first = _stream_operands(x, bB)
    n_first = ys[0][0].shape[0] // tm
    assert all(y0.shape[0] % tm == 0 and y1.shape[0] % tm == 0 for y0, y1 in ys)
    in_specs = []
    for y0, y1 in ys:
        in_specs += [pl.BlockSpec((tm, y0.shape[1]), lambda i: (jnp.minimum(i, n_first - 1), 0)),
                     pl.BlockSpec((tm, y1.shape[1]), lambda i: (jnp.maximum(i - n_first, 0), 0))]
    assert x_first is None or x_first == n_first
    in_specs += [pl.BlockSpec(w.shape, lambda i: (0, 0)) for w in ws]
    in_specs += x_specs + [pl.BlockSpec((bB, 1, d), lambda i: (i, 0, gt_col))]
    return pl.pallas_call(
        functools.partial(_outproj_kernel, n_in=n_in, n_first=n_first),
        grid=(npb // bB,),
        in_specs=in_specs,
        out_specs=pl.BlockSpec((bB, pb, d), lambda i: (i, 0, 0)),
        out_shape=jax.ShapeDtypeStruct((npb, pb, d), F32),
        compiler_params=_cparams(("parallel",)),
        name="out_projection",
    )(*[y for pair in ys for y in pair], *ws, *x_args, mod_pb)


def _gdnpre_kernel(qkv_ref, ab_ref, cb_ref, cw_ref, alog_ref, dtb_ref,
                   q_ref, k_ref, v_ref, gb_ref, gt_ref, cbo_ref, xbuf, *, C):
    tT = qkv_ref.shape[0]

    @pl.when(pl.program_id(1) == 0)
    def _():
        xbuf[0:SUBLANES, :] = jnp.zeros((SUBLANES, N_QKV), F32)
        xbuf[SUBLANES - (CONV_W - 1):SUBLANES, :] = cb_ref[...]

    for cb in range(N_QKV // DK):
        cols = slice(cb * DK, (cb + 1) * DK)
        x = qkv_ref[:, cols]
        xbuf[SUBLANES:SUBLANES + tT, cols] = x
        xc = cw_ref[3:4, cols] * x
        for j in range(CONV_W - 1):
            xc = xc + cw_ref[j:j + 1, cols] * xbuf[SUBLANES - 3 + j:SUBLANES - 3 + j + tT, cols]
        xs = xc * jax.nn.sigmoid(xc)
        if cb < 2 * HC:
            xn = xs * lax.rsqrt(jnp.sum(xs * xs, axis=-1, keepdims=True) + NORM_EPS)
            if cb < HC:
                q_ref[:, cols] = xn * (DK ** -0.5)
            else:
                k_ref[:, (cb - HC) * DK:(cb - HC + 1) * DK] = xn
        else:
            v_ref[:, (cb - 2 * HC) * DV:(cb - 2 * HC + 1) * DV] = xs
    tail = xbuf[tT:tT + SUBLANES, :]
    xbuf[0:SUBLANES, :] = tail
    cbo_ref[...] = tail[SUBLANES - (CONV_W - 1):, :]

    ab = ab_ref[...]
    g = -jnp.exp(alog_ref[...]) * _softplus(ab + dtb_ref[...])
    beta = jax.nn.sigmoid(ab)
    lane = lax.broadcasted_iota(I32, ab.shape, 1)
    g = jnp.where(lane < HC, g, 0.0)
    ri = lax.broadcasted_iota(I32, (C, C), 0)
    ci = lax.broadcasted_iota(I32, (C, C), 1)
    tril = (ri >= ci).astype(F32)
    triu = (ri <= ci).astype(F32)
    for c in range(tT // C):
        gc = g[c * C:(c + 1) * C, :]
        G = _dot(tril, gc, HI)
        lane_c = lax.broadcasted_iota(I32, (C, LANES), 1)
        gb_ref[c * C:(c + 1) * C, :] = jnp.where(lane_c < HC, G, beta[c * C:(c + 1) * C, :])
        gt_ref[c * 2 * HC:(c + 1) * 2 * HC, :] = _dot_tn(gc, triu, HI)[:2 * HC, :]


def _gdn_pre(proj, row0, B, T, tT, C, conv_buf, cw, alog_pad, dtb_pad):
    nT = T // tT
    rb0 = row0 // tT
    cpt = tT // C
    return pl.pallas_call(
        functools.partial(_gdnpre_kernel, C=C),
        grid=(B, nT),
        in_specs=[pl.BlockSpec((tT, N_QKV), lambda b, t: (rb0 + b * nT + t, 0)),
                  pl.BlockSpec((tT, LANES), lambda b, t: (rb0 + b * nT + t, (N_QKV + DCV) // LANES)),
                  pl.BlockSpec((None, CONV_W - 1, N_QKV), lambda b, t: (b, 0, 0)),
                  pl.BlockSpec((CONV_W, N_QKV), lambda b, t: (0, 0)),
                  pl.BlockSpec((1, LANES), lambda b, t: (0, 0)),
                  pl.BlockSpec((1, LANES), lambda b, t: (0, 0))],
        out_specs=[pl.BlockSpec((tT, DCK), lambda b, t: (b * nT + t, 0)),
                   pl.BlockSpec((tT, DCK), lambda b, t: (b * nT + t, 0)),
                   pl.BlockSpec((tT, DCV), lambda b, t: (b * nT + t, 0)),
                   pl.BlockSpec((tT, LANES), lambda b, t: (b * nT + t, 0)),
                   pl.BlockSpec((cpt * 2 * HC, C), lambda b, t: (b * nT + t, 0)),
                   pl.BlockSpec((None, CONV_W - 1, N_QKV), lambda b, t: (b, 0, 0))],
        out_shape=[jax.ShapeDtypeStruct((B * T, DCK), F32),
                   jax.ShapeDtypeStruct((B * T, DCK), F32),
                   jax.ShapeDtypeStruct((B * T, DCV), F32),
                   jax.ShapeDtypeStruct((B * T, LANES), F32),
                   jax.ShapeDtypeStruct((B * T // C * 2 * HC, C), F32),
                   jax.ShapeDtypeStruct((B, CONV_W - 1, N_QKV), F32)],
        scratch_shapes=[pltpu.VMEM((tT + SUBLANES, N_QKV), F32)],
        compiler_params=_cparams(("parallel", "arbitrary")),
        name="gdn_pre",
    )(proj, proj, conv_buf, cw, alog_pad, dtb_pad)


def _split_bf16(a):
    hi = a.astype(BF16)
    return hi, (a - hi.astype(F32)).astype(BF16)


def _dot3(a, b):
    return _dot(a[0], b[0]) + (_dot(a[0], b[1]) + _dot(a[1], b[0]))


def _delta_kernel(q_ref, k_ref, v_ref, gb_ref, gt_ref, z_ref, s0_ref, ng_ref, o_ref, so_ref,
                  s_scr, u_scr, w_scr, qd_scr, kd_scr, a_scr, *, C, hps, unroll):
    T = q_ref.shape[0]
    n_chunks = T // C
    ri = lax.broadcasted_iota(I32, (C, C), 0)
    ci = lax.broadcasted_iota(I32, (C, C), 1)
    causal = ri >= ci
    strict = ri > ci
    eye = (ri == ci).astype(F32)
    head0 = pl.program_id(1) * hps
    lane = lax.broadcasted_iota(I32, (C, LANES), 1)

    def pick(x, col):
        lane_x = lax.broadcasted_iota(I32, x.shape, 1)
        return jnp.sum(jnp.where(lane_x == col, x, 0.0), axis=1, keepdims=True)

    def operands(n, j):
        rows = pl.ds(pl.multiple_of(n * C, C), C)
        cs = slice(j * DK, (j + 1) * DK)
        gbc = gb_ref[rows, :]
        G = pick(gbc, head0 + j)
        beta = pick(gbc, head0 + j + HC)
        return rows, cs, G, beta

    def prepare(streams):
        def lower_left(m):
            s = m.bit_length() - 1
            return jnp.logical_and(((ri ^ ci) >> s) == 1, ((ri >> s) & 1) == 1)

        Bs, Ps = [], []
        for n, j in streams:
            rows, cs, G, beta = operands(n, j)
            g_row = gt_ref[pl.ds(n * (2 * HC) + head0 + j, 1), :]
            k = k_ref[rows, cs]
            L = jnp.where(causal, jnp.exp(jnp.where(causal, G - g_row, 0.0)), 0.0)
            qk = _dot_nt(jnp.concatenate([k * beta, q_ref[rows, cs]], axis=0).astype(BF16), k.astype(BF16))
            a_scr[j, rows, :] = (qk[C:] * L).astype(BF16)
            Bm = jnp.where(strict, -(qk[:C] * L), 0.0)
            Bs.append(_split_bf16(Bm))
            Ps.append(eye + jnp.where(lower_left(1), Bm, 0.0))
        zero = jnp.zeros((C, C), BF16)
        m = 2
        while m < C:
            pair = lower_left(m)
            Ds = [_split_bf16(p) for p in Ps]
            Es = [(jnp.where(pair, b[0], zero), jnp.where(pair, b[1], zero)) for b in Bs]
            Ws = [_split_bf16(_dot3(e, d)) for e, d in zip(Es, Ds)]
            Ps = [p + _dot3(d, w) for p, d, w in zip(Ps, Ds, Ws)]
            m *= 2
        for (n, j), P in zip(streams, Ps):
            rows, cs, G, beta = operands(n, j)
            k = k_ref[rows, cs]
            eG = jnp.exp(G)
            rhs = jnp.concatenate([v_ref[rows, cs] * beta, (k * beta) * eG], axis=1)
            X = _dot3(_split_bf16(P), _split_bf16(rhs))
            u_scr[j, rows, :] = X[:, :DV]
            w_scr[j, rows, :] = X[:, DV:].astype(BF16)
            qd_scr[j, rows, :] = (q_ref[rows, cs] * eG).astype(BF16)
            kd_scr[j, rows, :] = (k * jnp.exp(G[C - 1:C, :] - G)).astype(BF16)

    def prepare_body(i, carry):
        prepare([(i * unroll + m, j) for m in range(unroll) for j in range(hps)])
        return carry

    lax.fori_loop(0, n_chunks // unroll, prepare_body, 0)

    s_scr[...] = s0_ref[...]

    def advance(n, carry):
        r0 = pl.multiple_of(n * C, C)
        rows = pl.ds(r0, C)
        g_end = gb_ref[pl.ds(r0 + C - 1, 1), :]
        heads = range(hps)
        S = [s_scr[j] for j in heads]
        Sb = [s.astype(BF16) for s in S]
        wqS = [_dot(jnp.concatenate([w_scr[j, rows, :], qd_scr[j, rows, :]], axis=0), Sb[j]) for j in heads]
        wS = [x[:C] for x in wqS]
        qS = [x[C:] for x in wqS]
        vnb = [(u_scr[j, rows, :] - wS[j]).astype(BF16) for j in heads]
        kv = [_dot_tn(kd_scr[j, rows, :], vnb[j]) for j in heads]
        av = [_dot(a_scr[j, rows, :], vnb[j]) for j in heads]
        for j in heads:
            cs = slice(j * DK, (j + 1) * DK)
            s_scr[j] = S[j] * jnp.exp(pick(g_end, head0 + j)) + kv[j]
            o = qS[j] + av[j]
            on = o * lax.rsqrt(jnp.mean(o * o, axis=-1, keepdims=True) + NORM_EPS) * ng_ref[...]
            z = z_ref[rows, cs]
            o_ref[rows, cs] = (on * (z * jax.nn.sigmoid(z))).astype(BF16)
        return carry

    lax.fori_loop(0, n_chunks, advance, 0)
    so_ref[...] = s_scr[...]


def _delta_rule(q, k, v, gb, gt, proj, row0, B, T, C, S0, norm_g, hps):
    rb = row0 // T
    wb = hps * DK
    zcol0 = N_QKV // wb
    n_chunks = T // C
    qkv_spec = lambda: pl.BlockSpec((T, wb), lambda b, h: (b, h))
    return pl.pallas_call(
        functools.partial(_delta_kernel, C=C, hps=hps, unroll=4 if n_chunks % 4 == 0 else 1),
        grid=(B, HC // hps),
        in_specs=[qkv_spec(), qkv_spec(), qkv_spec(),
                  pl.BlockSpec((T, LANES), lambda b, h: (b, 0)),
                  pl.BlockSpec((n_chunks * 2 * HC, C), lambda b, h: (b, 0)),
                  pl.BlockSpec((T, wb), lambda b, h: (rb + b, zcol0 + h)),
                  pl.BlockSpec((None, hps, DK, DV), lambda b, h: (b, h, 0, 0)),
                  pl.BlockSpec((1, DV), lambda b, h: (0, 0))],
        out_specs=[pl.BlockSpec((T, wb), lambda b, h: (b, h)),
                   pl.BlockSpec((None, hps, DK, DV), lambda b, h: (b, h, 0, 0))],
        out_shape=[jax.ShapeDtypeStruct((B * T, DCV), BF16),
                   jax.ShapeDtypeStruct((B, HC, DK, DV), F32)],
        scratch_shapes=[pltpu.VMEM((hps, DK, DV), F32),
                        pltpu.VMEM((hps, T, DV), F32),
                        pltpu.VMEM((hps, T, DK), BF16),
                        pltpu.VMEM((hps, T, DK), BF16),
                        pltpu.VMEM((hps, T, DK), BF16),
                        pltpu.VMEM((hps, T, C), BF16)],
        compiler_params=_cparams(("parallel", "parallel")),
        name="delta_rule",
    )(q, k, v, gb, gt, proj, S0, norm_g.reshape(1, DV))


def _first_argmax(x, idx, size):
    m = jnp.max(x, axis=0, keepdims=True)
    am = jnp.min(jnp.where(x == m, idx, size), axis=0, keepdims=True)
    return m, am


def _router_kernel(x_ref, g_ref, sc_ref, sh_ref, rwt_ref, rb_ref,
                   h_ref, e_ref, w_ref, r_ref, cnt_ref, run_scr):
    tn = h_ref.shape[0]
    gsz = N_EXPERTS // N_GROUPS

    @pl.when(pl.program_id(0) == 0)
    def _():
        run_scr[...] = jnp.zeros_like(run_scr)

    h = _norm_mod(x_ref[...], g_ref[...], sc_ref[...], sh_ref[...])
    h_ref[...] = h
    scores = jax.nn.sigmoid(_dot_nt(rwt_ref[...], h, HI))
    sel = scores + rb_ref[...]

    gidx = lax.broadcasted_iota(I32, (gsz, tn), 0)
    gs_rows = []
    for gi in range(N_GROUPS):
        blk = sel[gi * gsz:(gi + 1) * gsz, :]
        m1, a1 = _first_argmax(blk, gidx, gsz)
        m2 = jnp.max(jnp.where(gidx == a1, -jnp.inf, blk), axis=0, keepdims=True)
        gs_rows.append(m1 + m2)
    gscore = jnp.concatenate(gs_rows, axis=0)
    gi8 = lax.broadcasted_iota(I32, (N_GROUPS, tn), 0)
    gmask = jnp.zeros((N_GROUPS, tn), jnp.bool_)
    gwork = gscore
    for _ in range(TOPK_GROUPS):
        _, a = _first_argmax(gwork, gi8, N_GROUPS)
        hit = gi8 == a
        gmask = jnp.logical_or(gmask, hit)
        gwork = jnp.where(hit, -jnp.inf, gwork)
    emask = jnp.concatenate([jnp.broadcast_to(gmask[gi:gi + 1, :], (gsz, tn)) for gi in range(N_GROUPS)], axis=0)

    eidx = lax.broadcasted_iota(I32, (N_EXPERTS, tn), 0)
    work = jnp.where(emask, sel, -jnp.inf)
    hits = []
    chosen = jnp.zeros((N_EXPERTS, tn), jnp.bool_)
    for _ in range(TOP_K):
        _, a = _first_argmax(work, eidx, N_EXPERTS)
        hit = eidx == a
        hits.append((a, hit))
        chosen = jnp.logical_or(chosen, hit)
        work = jnp.where(hit, -jnp.inf, work)
    chosen_f = chosen.astype(F32)
    denom = jnp.sum(scores * chosen_f, axis=0, keepdims=True)

    ti = lax.broadcasted_iota(I32, (tn, tn), 0)
    tj = lax.broadcasted_iota(I32, (tn, tn), 1)
    before = (ti < tj).astype(BF16)
    rank = _dot(chosen_f.astype(BF16), before) + run_scr[...]
    run_scr[...] = run_scr[...] + jnp.sum(chosen_f, axis=1, keepdims=True)
    cnt_ref[...] = jnp.broadcast_to(run_scr[...], cnt_ref.shape).astype(I32)

    e_rows, w_rows, r_rows = [], [], []
    for a, hit in hits:
        hf = hit.astype(F32)
        e_rows.append(a)
        w_rows.append(jnp.sum(scores * hf, axis=0, keepdims=True) / denom * ROUTE_SCALE)
        r_rows.append(jnp.sum(rank * hf, axis=0, keepdims=True))
    e_ref[...] = jnp.concatenate(e_rows, axis=0)
    w_ref[...] = jnp.concatenate(w_rows, axis=0)
    r_ref[...] = jnp.concatenate(r_rows, axis=0).astype(I32)


def _router(x_all, g, mod_pb, sc_col, sh_col, rwt, rb):
    npb, pb, d = x_all.shape
    n = npb * pb
    bB = 16
    tn = bB * pb
    return pl.pallas_call(
        _router_kernel,
        grid=(npb // bB,),
        in_specs=[pl.BlockSpec((bB, pb, d), lambda i: (i, 0, 0)),
                  pl.BlockSpec((1, d), lambda i: (0, 0)),
                  pl.BlockSpec((bB, 1, d), lambda i: (i, 0, sc_col)),
                  pl.BlockSpec((bB, 1, d), lambda i: (i, 0, sh_col)),
                  pl.BlockSpec((N_EXPERTS, d), lambda i: (0, 0)),
                  pl.BlockSpec((N_EXPERTS, 1), lambda i: (0, 0))],
        out_specs=[pl.BlockSpec((tn, d), lambda i: (i, 0)),
                   pl.BlockSpec((TOP_K, tn), lambda i: (0, i)),
                   pl.BlockSpec((TOP_K, tn), lambda i: (0, i)),
                   pl.BlockSpec((TOP_K, tn), lambda i: (0, i)),
                   pl.BlockSpec((N_EXPERTS, LANES), lambda i: (0, 0))],
        out_shape=[jax.ShapeDtypeStruct((n, d), F32),
                   jax.ShapeDtypeStruct((TOP_K, n), I32),
                   jax.ShapeDtypeStruct((TOP_K, n), F32),
                   jax.ShapeDtypeStruct((TOP_K, n), I32),
                   jax.ShapeDtypeStruct((N_EXPERTS, LANES), I32)],
        scratch_shapes=[pltpu.VMEM((N_EXPERTS, 1), F32)],
        compiler_params=_cparams(("arbitrary",)),
        name="moe_router",
    )(x_all, g.reshape(1, d), mod_pb, mod_pb, rwt, rb.reshape(N_EXPERTS, 1))


PACK_ROWS = D_MODEL // LANES


def _row_slab(ref, r):
    return ref.at[pl.ds(pl.multiple_of(r * PACK_ROWS, PACK_ROWS), PACK_ROWS), :]


def _to_row_tiles(dst_ref, words, row0=0):
    rows = words.shape[0]
    for c in range(PACK_ROWS):
        dst_ref[pl.ds(row0 * PACK_ROWS + c, rows, stride=PACK_ROWS), :] = words[:, c * LANES:(c + 1) * LANES]


def _lane_tile(src_ref, c, rows, row0=0):
    return src_ref[pl.ds(row0 * PACK_ROWS + c, rows, stride=PACK_ROWS), :]


def _from_row_tiles(src_ref, rows):
    return jnp.concatenate([_lane_tile(src_ref, c, rows) for c in range(PACK_ROWS)], axis=1)


def _dispatch_kernel(zf_ref, pos_ref, h_ref, sgu_ref, sd_ref, xs_hbm, shared_ref, sbuf, zbuf, sem, zsem, *, tm):
    i = pl.program_id(0)
    n_steps = pl.num_programs(0)
    td = h_ref.shape[0]
    n_tiles = zf_ref.shape[0]
    slot = lax.rem(i, 2)

    @pl.when(i == 0)
    def _():
        zbuf[...] = jnp.zeros_like(zbuf)

        def zero_copy(t):
            r0 = pl.multiple_of(t * (tm * PACK_ROWS), tm * PACK_ROWS)
            return pltpu.make_async_copy(zbuf, xs_hbm.at[pl.ds(r0, tm * PACK_ROWS), :], zsem)

        def start(t, carry):
            @pl.when(zf_ref[t] != 0)
            def _():
                zero_copy(t).start()
            return carry

        def wait(t, carry):
            @pl.when(zf_ref[t] != 0)
            def _():
                zero_copy(t).wait()
            return carry

        lax.fori_loop(0, n_tiles, start, 0)
        lax.fori_loop(0, n_tiles, wait, 0)

    def drain(s):
        for _ in range(TOP_K):
            pltpu.make_async_copy(sbuf.at[s], xs_hbm.at[pl.ds(0, td * PACK_ROWS), :], sem.at[s]).wait()

    @pl.when(i >= 2)
    def _():
        drain(slot)

    h = h_ref[...]
    _to_row_tiles(sbuf.at[slot], h)
    gu = _dot(h.astype(BF16), sgu_ref[...])
    hid = jax.nn.silu(gu[:, :D_SHARED]) * gu[:, D_SHARED:]
    shared_ref[...] = _dot(hid.astype(BF16), sd_ref[...])

    def body(j, carry):
        t0 = pl.multiple_of(j * SUBLANES, SUBLANES)
        for u in range(SUBLANES):
            src = _row_slab(sbuf.at[slot], t0 + u)
            for k in range(TOP_K):
                p = pos_ref[0, k * td + t0 + u]
                pltpu.make_async_copy(src, _row_slab(xs_hbm, p), sem.at[slot]).start(priority=k % 2)
        return carry

    lax.fori_loop(0, td // SUBLANES, body, 0)

    @pl.when(i == n_steps - 1)
    def _():
        drain(slot)
        drain(1 - slot)


def _moe_dispatch(h2, pos_tiles, zero_flag, sgu, sd, n_tiles, tm, td):
    n, d = h2.shape
    grid_spec = pltpu.PrefetchScalarGridSpec(
        num_scalar_prefetch=1,
        grid=(n // td,),
        in_specs=[pl.BlockSpec((None, 1, TOP_K * td), lambda i, zf: (i, 0, 0), memory_space=pltpu.SMEM),
                  pl.BlockSpec((td, d), lambda i, zf: (i, 0)),
                  pl.BlockSpec(sgu.shape, lambda i, zf: (0, 0)),
                  pl.BlockSpec(sd.shape, lambda i, zf: (0, 0))],
        out_specs=[pl.BlockSpec(memory_space=pl.ANY), pl.BlockSpec((td, d), lambda i, zf: (i, 0))],
        scratch_shapes=[pltpu.VMEM((2, td * PACK_ROWS, LANES), F32), pltpu.VMEM((tm * PACK_ROWS, LANES), F32),
                        pltpu.SemaphoreType.DMA((2,)), pltpu.SemaphoreType.DMA(())],
    )
    return pl.pallas_call(
        functools.partial(_dispatch_kernel, tm=tm),
        grid_spec=grid_spec,
        out_shape=[jax.ShapeDtypeStruct((n_tiles * tm * PACK_ROWS, LANES), F32),
                   jax.ShapeDtypeStruct((n, d), F32)],
        compiler_params=_cparams(("arbitrary",)),
        name="moe_dispatch",
    )(zero_flag, pos_tiles, h2, sgu, sd)


N_IN_BUF = 3
N_OUT_BUF = 2


def _experts_kernel(te_ref, nu_ref, xs_hbm, wg_ref, wu_ref, wd_ref, ys_hbm, wgu_s, wd_s, xin, yout, sem_in, sem_out):
    i = pl.program_id(0)
    nu = nu_ref[0]
    rows = xin.shape[1]
    tm = rows // PACK_ROWS

    def tile(ref, t):
        return ref.at[pl.ds(pl.multiple_of(t * rows, rows), rows), :]

    def load(t):
        s = lax.rem(t, N_IN_BUF)
        return pltpu.make_async_copy(tile(xs_hbm, t), xin.at[s], sem_in.at[s])

    def store(t):
        s = lax.rem(t, N_OUT_BUF)
        return pltpu.make_async_copy(yout.at[s], tile(ys_hbm, t), sem_out.at[s])

    @pl.when(i == 0)
    def _():
        load(0).start()

        @pl.when(nu > 1)
        def _():
            load(1).start()

    @pl.when(i + 2 < nu)
    def _():
        load(i + 2).start()

    @pl.when(jnp.logical_or(i == 0, te_ref[i] != te_ref[jnp.maximum(i - 1, 0)]))
    def _():
        wgu_s[:, :D_EXPERT] = wg_ref[...].astype(BF16)
        wgu_s[:, D_EXPERT:] = wu_ref[...].astype(BF16)
        wd_s[...] = wd_ref[...].astype(BF16)

    @pl.when(i < nu)
    def _():
        load(i).wait()

        @pl.when(i >= N_OUT_BUF)
        def _():
            store(i - N_OUT_BUF).wait()

        x = _from_row_tiles(xin.at[lax.rem(i, N_IN_BUF)], tm).astype(BF16)
        gu = _dot(x, wgu_s[...])
        hid = jax.nn.silu(gu[:, :D_EXPERT]) * gu[:, D_EXPERT:]
        _to_row_tiles(yout.at[lax.rem(i, N_OUT_BUF)], _dot(hid.astype(BF16), wd_s[...]))
        store(i).start()

    @pl.when(i == nu - 1)
    def _():
        @pl.when(i >= 1)
        def _():
            store(i - 1).wait()

        store(i).wait()

    @pl.when(i >= nu)
    def _():
        @pl.when(i == nu)
        def _():
            yout[0] = jnp.zeros(yout.shape[1:], F32)

        zero_store = pltpu.make_async_copy(yout.at[0], tile(ys_hbm, i), sem_out.at[0])
        zero_store.start()
        zero_store.wait()


def _routed_experts(xs, tile_expert, n_used, layer, wg, wu, wd, tm):
    n_tiles = xs.shape[0] // (tm * PACK_ROWS)
    d = wg.shape[2]
    grid_spec = pltpu.PrefetchScalarGridSpec(
        num_scalar_prefetch=2,
        grid=(n_tiles,),
        in_specs=[pl.BlockSpec(memory_space=pl.ANY),
                  pl.BlockSpec((None, None, d, D_EXPERT), lambda i, te, nu: (layer, te[i], 0, 0)),
                  pl.BlockSpec((None, None, d, D_EXPERT), lambda i, te, nu: (layer, te[i], 0, 0)),
                  pl.BlockSpec((None, None, D_EXPERT, d), lambda i, te, nu: (layer, te[i], 0, 0))],
        out_specs=pl.BlockSpec(memory_space=pl.ANY),
        scratch_shapes=[pltpu.VMEM((d, 2 * D_EXPERT), BF16), pltpu.VMEM((D_EXPERT, d), BF16),
                        pltpu.VMEM((N_IN_BUF, tm * PACK_ROWS, LANES), F32),
                        pltpu.VMEM((N_OUT_BUF, tm * PACK_ROWS, LANES), F32),
                        pltpu.SemaphoreType.DMA((N_IN_BUF,)), pltpu.SemaphoreType.DMA((N_OUT_BUF,))],
    )
    return pl.pallas_call(
        _experts_kernel,
        grid_spec=grid_spec,
        out_shape=jax.ShapeDtypeStruct(xs.shape, F32),
        compiler_params=_cparams(("arbitrary",)),
        name="moe_experts",
    )(tile_expert, n_used, xs, wg, wu, wd)


def _gather_rows(idx_ref, n_rows, src_hbm, dst, sem):
    group = 4 * SUBLANES

    def body(j, carry):
        r0 = pl.multiple_of(j * group, group)
        for u in range(group):
            pltpu.make_async_copy(_row_slab(src_hbm, idx_ref[0, r0 + u]), _row_slab(dst, r0 + u),
                                  sem).start(priority=u % 2)
        return carry
    lax.fori_loop(0, n_rows // group, body, 0)


def _combine_kernel(pos_ref, nxt_ref, ys_hbm, w_ref, shared_ref, x_ref, gt_ref, o_ref,
                    gbuf0, gbuf1, acc_scr, sem):
    i = pl.program_id(0)
    last = pl.num_programs(0) - 1
    tb = w_ref.shape[0]
    n_rows = TOP_K * tb
    n_groups = tb // SUBLANES
    per_group = n_rows // n_groups

    def wait_tile(buf, s):
        pltpu.make_async_copy(ys_hbm.at[pl.ds(0, n_rows * PACK_ROWS), :], buf, sem.at[s]).wait()

    @pl.when(i == 0)
    def _():
        _gather_rows(pos_ref, n_rows, ys_hbm, gbuf0, sem.at[0])

    def run(cur, cur_s, nxt, nxt_s):
        wait_tile(cur, cur_s)

        for g in range(n_groups):
            t0 = g * SUBLANES
            for u in range(per_group):
                r = g * per_group + u
                pltpu.make_async_copy(_row_slab(ys_hbm, nxt_ref[0, r]), nxt.at[r * PACK_ROWS:(r + 1) * PACK_ROWS, :],
                                      sem.at[nxt_s]).start(priority=u % 2)
            w8 = w_ref[t0:t0 + SUBLANES, :]
            wk = [jnp.broadcast_to(w8[:, k:k + 1], (SUBLANES, LANES)) for k in range(TOP_K)]
            for c in range(PACK_ROWS):
                a = wk[0] * _lane_tile(cur, c, SUBLANES, t0)
                for k in range(1, TOP_K):
                    a = a + wk[k] * _lane_tile(cur, c, SUBLANES, k * tb + t0)
                acc_scr[t0:t0 + SUBLANES, c * LANES:(c + 1) * LANES] = a

        @pl.when(i == last)
        def _():
            wait_tile(nxt, nxt_s)

    @pl.when(lax.rem(i, 2) == 0)
    def _():
        run(gbuf0, 0, gbuf1, 1)

    @pl.when(lax.rem(i, 2) == 1)
    def _():
        run(gbuf1, 1, gbuf0, 0)

    acc = shared_ref[...] + acc_scr[...]
    bB, pb, d = x_ref.shape
    o_ref[...] = x_ref[...] + gt_ref[...] * acc.reshape(bB, pb, d)


def _moe_combine(ys, pos_tiles, w_tok, shared, x_all, mod_pb, gt_col, tb):
    npb, pb, d = x_all.shape
    n_tb = pos_tiles.shape[0]
    bB = tb // pb
    last = n_tb - 1
    return pl.pallas_call(
        _combine_kernel,
        grid=(n_tb,),
        in_specs=[pl.BlockSpec((None, 1, TOP_K * tb), lambda i: (i, 0, 0), memory_space=pltpu.SMEM),
                  pl.BlockSpec((None, 1, TOP_K * tb), lambda i: (jnp.minimum(i + 1, last), 0, 0),
                               memory_space=pltpu.SMEM),
                  pl.BlockSpec(memory_space=pl.ANY),
                  pl.BlockSpec((tb, TOP_K), lambda i: (i, 0)),
                  pl.BlockSpec((tb, d), lambda i: (i, 0)),
                  pl.BlockSpec((bB, pb, d), lambda i: (i, 0, 0)),
                  pl.BlockSpec((bB, 1, d), lambda i: (i, 0, gt_col))],
        out_specs=pl.BlockSpec((bB, pb, d), lambda i: (i, 0, 0)),
        out_shape=jax.ShapeDtypeStruct(x_all.shape, F32),
        scratch_shapes=[pltpu.VMEM((TOP_K * tb * PACK_ROWS, LANES), F32),
                        pltpu.VMEM((TOP_K * tb * PACK_ROWS, LANES), F32),
                        pltpu.VMEM((tb, d), F32), pltpu.SemaphoreType.DMA((2,))],
        compiler_params=_cparams(("arbitrary",)),
        name="moe_combine",
    )(pos_tiles, pos_tiles, ys, w_tok, shared, x_all, mod_pb)


def _moe(x_all, g, mod_pb, rw, rb, layer, wg, wu, wd, sg, su, sd):
    npb, pb, d = x_all.shape
    n = npb * pb
    tm = 512
    tb = 128
    h2, e_t, w_t, r_t, cnt = _router(x_all, g, mod_pb, 4, 3, rw.T, rb)

    counts = cnt[:, 0]
    padded = (counts + tm - 1) // tm * tm
    ends = jnp.cumsum(padded)
    starts = ends - padded
    n_tiles = (n * TOP_K + N_EXPERTS * (tm - 1)) // tm
    onehot = e_t[:, :, None] == jnp.arange(N_EXPERTS, dtype=I32)
    pos = jnp.sum(jnp.where(onehot, starts, 0), axis=-1) + r_t
    n_used = (ends[-1] // tm).astype(I32)
    tile_idx = jnp.arange(n_tiles, dtype=I32)
    tile_start = jnp.minimum(tile_idx, n_used - 1) * tm
    tile_expert = jnp.sum(ends[None, :] <= tile_start[:, None], axis=1).astype(I32)
    is_seg_end = jnp.any((ends[None, :] == (tile_idx[:, None] + 1) * tm) & (padded[None, :] > 0), axis=1)
    zero_flag = jnp.logical_or(is_seg_end, tile_idx >= n_used).astype(I32)
    pos_tiles = pos.reshape(TOP_K, n // tb, tb).transpose(1, 0, 2).reshape(n // tb, 1, TOP_K * tb)

    sgu = jnp.concatenate([sg, su], axis=-1).astype(BF16)
    xs, shared = _moe_dispatch(h2, pos_tiles, zero_flag, sgu, sd.astype(BF16), n_tiles, tm, tb)
    ys = _routed_experts(xs, tile_expert, n_used.reshape(1), layer, wg, wu, wd, tm)
    return _moe_combine(ys, pos_tiles, w_t.T, shared, x_all, mod_pb, 5, tb)


def _final_kernel(x_ref, g_ref, o_ref):
    x = x_ref[...]
    ms = jnp.mean(x * x, axis=-1, keepdims=True)
    o_ref[...] = x * lax.rsqrt(ms + NORM_EPS) * g_ref[...]


def _final_norm(x_all, g, pb0, n_pb):
    npb, pb, d = x_all.shape
    bB = 32
    return pl.pallas_call(
        _final_kernel,
        grid=(n_pb // bB,),
        in_specs=[pl.BlockSpec((bB, pb, d), lambda i: (pb0 // bB + i, 0, 0)),
                  pl.BlockSpec((1, d), lambda i: (0, 0))],
        out_specs=pl.BlockSpec((bB, pb, d), lambda i: (i, 0, 0)),
        out_shape=jax.ShapeDtypeStruct((n_pb, pb, d), F32),
        compiler_params=_cparams(("parallel",)),
        name="final_norm",
    )(x_all, g.reshape(1, d))


def _block_diag(w):
    nb, di, do = w.shape
    return jnp.einsum("nde,nm->ndme", w, jnp.eye(nb, dtype=w.dtype)).reshape(nb * di, nb * do)


def kernel(x_prompt, x_sample, c_prompt, c_sample, cache_b_k, cache_b_v, state_a_conv, state_a_h, state_c_conv, state_c_S, norm1_g, norm2_g, final_g, w_mod, b_mod, w_in0, w_out0, a_conv_w, a_conv_b, a_w_r, a_b_r, a_w_i, a_b_i, a_lambda, b_rel_bias, w_in1, w_out1, c_conv_w, c_A_log, c_dt_bias, c_norm_g, router_w, router_bias, e_w_gate, e_w_up, e_w_down, s_w_gate, s_w_up, s_w_down):
    Bp, Tp, d = x_prompt.shape
    Bs, Ts, _ = x_sample.shape
    np_tok, ns_tok = Bp * Tp, Bs * Ts
    npb_p, npb_s = np_tok // PB, ns_tok // PB
    depth = w_mod.shape[0]

    x_all = (x_prompt.reshape(npb_p, PB, d), x_sample.reshape(npb_s, PB, d))
    mod = _modulation(jnp.concatenate([c_prompt, c_sample], axis=0), w_mod, b_mod)

    outs = {}
    for l in range(depth):
        mod_pb = jnp.concatenate([jnp.repeat(mod[l, :Bp], Tp // PB, axis=0),
                                  jnp.repeat(mod[l, Bp:], Ts // PB, axis=0)], axis=0)[:, None, :]
        j = l // 2
        if l % 2 == 0:
            proj = _in_projection(x_all, norm1_g[l], mod_pb, 1, 0, w_in0[j].astype(BF16), 1280)
            wr, wi = _block_diag(a_w_r[j]).astype(BF16), _block_diag(a_w_i[j]).astype(BF16)
            a_args = (a_conv_w[j], a_conv_b[j], wr, a_b_r[j], wi, a_b_i[j], a_lambda[j])
            ya_p, ac_p, ah_p = _rglru(proj, 0, Bp, Tp, 512, jnp.zeros((Bp, CONV_W - 1, DA), F32),
                                      jnp.zeros((Bp, DA), F32), *a_args)
            ya_s, ac_s, ah_s = _rglru(proj, np_tok, Bs, Ts, Ts, state_a_conv[j], state_a_h[j], *a_args)
            yb_p, bk_p, bv_p = _band_attention(proj, 0, Bp, Tp, CHUNK, _rel_bias(b_rel_bias[j], CHUNK))
            yb_s, bk_s, bv_s = _band_attention(proj, np_tok, Bs, Ts, Ts, _rel_bias(b_rel_bias[j], Ts),
                                               cache_b_k[j].reshape(Bs, BAND_PAST, DB),
                                               cache_b_v[j].reshape(Bs, BAND_PAST, DB))
            w_out = w_out0[j].astype(BF16)
            x_all = _out_projection([(ya_p, ya_s), (yb_p, yb_s)], [w_out[:DA], w_out[DA:]], x_all, mod_pb, 2)
            outs.setdefault("bk_p", []).append(bk_p.reshape(Bp, -1, HB, DHB))
            outs.setdefault("bv_p", []).append(bv_p.reshape(Bp, -1, HB, DHB))
            outs.setdefault("bk_s", []).append(bk_s.reshape(Bs, -1, HB, DHB))
            outs.setdefault("bv_s", []).append(bv_s.reshape(Bs, -1, HB, DHB))
            outs.setdefault("ac_p", []).append(ac_p)
            outs.setdefault("ac_s", []).append(ac_s)
            outs.setdefault("ah_p", []).append(ah_p.reshape(Bp, DA))
            outs.setdefault("ah_s", []).append(ah_s.reshape(Bs, DA))
        else:
            w_in = jnp.pad(w_in1[j], ((0, 0), (0, IN1_PAD - IN1))).astype(BF16)
            proj = _in_projection(x_all, norm1_g[l], mod_pb, 1, 0, w_in, 1408)
            alog = jnp.pad(c_A_log[j], (0, LANES - HC)).reshape(1, LANES)
            dtb = jnp.pad(c_dt_bias[j], (0, LANES - HC)).reshape(1, LANES)
            cp, cs = min(DELTA_BLOCK, Tp), min(DELTA_BLOCK, Ts)
            q_p, k_p, v_p, gb_p, gt_p, cc_p = _gdn_pre(proj, 0, Bp, Tp, 256, cp,
                                                       jnp.zeros((Bp, CONV_W - 1, N_QKV), F32), c_conv_w[j], alog, dtb)
            q_s, k_s, v_s, gb_s, gt_s, cc_s = _gdn_pre(proj, np_tok, Bs, Ts, Ts, cs, state_c_conv[j], c_conv_w[j],
                                                       alog, dtb)
            o_p, cs_p = _delta_rule(q_p, k_p, v_p, gb_p, gt_p, proj, 0, Bp, Tp, cp,
                                    jnp.zeros((Bp, HC, DK, DV), F32), c_norm_g[j], 2)
            o_s, cs_s = _delta_rule(q_s, k_s, v_s, gb_s, gt_s, proj, np_tok, Bs, Ts, cs, state_c_S[j],
                                    c_norm_g[j], 4)
            x_all = _out_projection([(o_p, o_s)], [w_out1[j].astype(BF16)], x_all, mod_pb, 2)
            outs.setdefault("cc_p", []).append(cc_p)
            outs.setdefault("cc_s", []).append(cc_s)
            outs.setdefault("cs_p", []).append(cs_p)
            outs.setdefault("cs_s", []).append(cs_s)
        x_all = _moe(x_all, norm2_g[l], mod_pb, router_w[l], router_bias[l], l, e_w_gate, e_w_up, e_w_down,
                     s_w_gate[l], s_w_up[l], s_w_down[l])

    y_prompt = _final_norm(x_all, final_g, 0, npb_p).reshape(Bp, Tp, d)
    y_sample = _final_norm(x_all, final_g, npb_p, npb_s).reshape(Bs, Ts, d)
    st = {k: jnp.stack(v) for k, v in outs.items()}
    return (y_prompt, y_sample, st["bk_p"], st["bk_s"], st["bv_p"], st["bv_s"], st["ac_p"], st["ac_s"],
            st["ah_p"], st["ah_s"], st["cc_p"], st["cc_s"], st["cs_p"], st["cs_s"])
```

```python
import functools

import jax
import jax.numpy as jnp
from jax import lax
from jax.experimental import pallas as pl
from jax.experimental.pallas import tpu as pltpu

F32 = jnp.float32
BF16 = jnp.bfloat16
I32 = jnp.int32

D_MODEL = 1024
CHUNK = 64
NORM_EPS = 1e-6
CONV_W = 4
DA = 512
NB_A = 8
LRU_C = 8.0
HB = 8
DHB = 64
DB = HB * DHB
BAND_PAST = 8 * CHUNK
MAX_REL = 128
HC = 8
DK = 128
DV = 128
DCK = HC * DK
DCV = HC * DV
N_QKV = 2 * DCK + DCV
DELTA_BLOCK = 128
N_EXPERTS = 64
TOP_K = 8
N_GROUPS = 8
TOPK_GROUPS = 4
D_EXPERT = 256
D_SHARED = 256
ROUTE_SCALE = 2.5
IN0 = 2 * DA + 3 * DB
IN1 = N_QKV + DCV + 2 * HC

LANES = 128
SUBLANES = 8
PB = 32
IN1_PAD = N_QKV + DCV + LANES
VMEM_LIMIT = 48 * 1024 * 1024
HI = lax.Precision.HIGHEST


def _cparams(sem):
    return pltpu.CompilerParams(dimension_semantics=sem, vmem_limit_bytes=VMEM_LIMIT)


def _softplus(x):
    return jnp.maximum(x, 0.0) + jnp.log1p(jnp.exp(-jnp.abs(x)))


def _dot(a, b, precision=None):
    return jnp.dot(a, b, preferred_element_type=F32, precision=precision)


def _dot_nt(a, b, precision=None):
    return lax.dot_general(a, b, (((1,), (1,)), ((), ())), preferred_element_type=F32, precision=precision)


def _dot_tn(a, b, precision=None):
    return lax.dot_general(a, b, (((0,), (0,)), ((), ())), preferred_element_type=F32, precision=precision)


def _mod_kernel(c_ref, w_ref, b_ref, o_ref):
    c = c_ref[...]
    cond = c * jax.nn.sigmoid(c)
    o_ref[0] = _dot(cond.astype(BF16), w_ref[0].astype(BF16)) + b_ref[0]


def _modulation(c_all, w_mod, b_mod):
    depth, d, n6 = w_mod.shape
    nb = c_all.shape[0]
    tn = 1536
    return pl.pallas_call(
        _mod_kernel,
        grid=(depth, n6 // tn),
        in_specs=[pl.BlockSpec((nb, d), lambda l, j: (0, 0)),
                  pl.BlockSpec((1, d, tn), lambda l, j: (l, 0, j)),
                  pl.BlockSpec((1, 1, tn), lambda l, j: (l, 0, j))],
        out_specs=pl.BlockSpec((1, nb, tn), lambda l, j: (l, 0, j)),
        out_shape=jax.ShapeDtypeStruct((depth, nb, n6), F32),
        compiler_params=_cparams(("parallel", "parallel")),
        name="modulation",
    )(c_all, w_mod, b_mod.reshape(depth, 1, n6))


def _norm_mod(x3, g, sc, sh):
    ms = jnp.mean(x3 * x3, axis=-1, keepdims=True)
    y = x3 * lax.rsqrt(ms + NORM_EPS) * g
    h = y * (1.0 + sc) + sh
    return h.reshape(x3.shape[0] * x3.shape[1], x3.shape[2])


def _stream_operands(x, bB):
    if not isinstance(x, tuple):
        return [x], [pl.BlockSpec((bB,) + x.shape[1:], lambda i, *_: (i, 0, 0))], None
    x0, x1 = x
    assert x0.shape[0] % bB == 0 and x1.shape[0] % bB == 0
    n_first = x0.shape[0] // bB
    return [x0, x1], [pl.BlockSpec((bB,) + x0.shape[1:], lambda i, *_: (jnp.minimum(i, n_first - 1), 0, 0)),
                      pl.BlockSpec((bB,) + x1.shape[1:], lambda i, *_: (jnp.maximum(i - n_first, 0), 0, 0))], n_first


def _stream_block(x_refs, n_first):
    if n_first is None:
        return x_refs[0][...]
    return jnp.where(pl.program_id(0) < n_first, x_refs[0][...], x_refs[1][...])


def _inproj_kernel(*refs, n_first):
    x_refs, (g_ref, sc_ref, sh_ref, w_ref, o_ref, h_scr) = refs[:-6], refs[-6:]

    def normalise(x_ref):
        h_scr[...] = _norm_mod(x_ref[...], g_ref[...], sc_ref[...], sh_ref[...]).astype(BF16)

    new_rows = pl.program_id(1) == 0
    if n_first is None:
        pl.when(new_rows)(lambda: normalise(x_refs[0]))
    else:
        in_first = pl.program_id(0) < n_first
        pl.when(jnp.logical_and(new_rows, in_first))(lambda: normalise(x_refs[0]))
        pl.when(jnp.logical_and(new_rows, jnp.logical_not(in_first)))(lambda: normalise(x_refs[1]))
    o_ref[...] = _dot(h_scr[...], w_ref[...])


def _in_projection(x, g, mod_pb, sc_col, sh_col, w_bf16, tn):
    npb, _, d6 = mod_pb.shape
    d, nout = w_bf16.shape
    bB = 32
    tm = bB * PB
    x_args, x_specs, n_first = _stream_operands(x, bB)
    return pl.pallas_call(
        functools.partial(_inproj_kernel, n_first=n_first),
        grid=(npb // bB, nout // tn),
        in_specs=x_specs + [pl.BlockSpec((1, d), lambda i, j: (0, 0)),
                            pl.BlockSpec((bB, 1, d), lambda i, j: (i, 0, sc_col)),
                            pl.BlockSpec((bB, 1, d), lambda i, j: (i, 0, sh_col)),
                            pl.BlockSpec((d, tn), lambda i, j: (0, j))],
        out_specs=pl.BlockSpec((tm, tn), lambda i, j: (i, j)),
        out_shape=jax.ShapeDtypeStruct((npb * PB, nout), F32),
        scratch_shapes=[pltpu.VMEM((tm, d), BF16)],
        compiler_params=_cparams(("parallel", "arbitrary")),
        name="in_projection",
    )(*x_args, g.reshape(1, d), mod_pb, mod_pb, w_bf16)


def _rglru_kernel(xa_ref, ga_ref, cb_ref, h0_ref, cw_ref, cbias_ref, wr_ref, br_ref, wi_ref, bi_ref, lam_ref,
                  y_ref, cbo_ref, ho_ref, xbuf, hcar):
    tT = xa_ref.shape[0]

    @pl.when(pl.program_id(1) == 0)
    def _():
        xbuf[0:SUBLANES, :] = jnp.zeros((SUBLANES, DA), F32)
        xbuf[SUBLANES - (CONV_W - 1):SUBLANES, :] = cb_ref[...]
        hcar[...] = h0_ref[...]

    xa = xa_ref[...]
    xbuf[SUBLANES:SUBLANES + tT, :] = xa
    xc = cw_ref[3:4, :] * xa + cbias_ref[...]
    for j in range(CONV_W - 1):
        xc = xc + cw_ref[j:j + 1, :] * xbuf[SUBLANES - 3 + j:SUBLANES - 3 + j + tT, :]
    tail = xbuf[tT:tT + SUBLANES, :]
    xbuf[0:SUBLANES, :] = tail
    cbo_ref[...] = tail[SUBLANES - (CONV_W - 1):, :]

    xcb = xc.astype(BF16)
    r = jax.nn.sigmoid(_dot(xcb, wr_ref[...]) + br_ref[...])
    i = jax.nn.sigmoid(_dot(xcb, wi_ref[...]) + bi_ref[...])
    log_a = -LRU_C * r * _softplus(-lam_ref[...])
    a = jnp.exp(log_a)
    th = jnp.tanh(log_a)
    u = jnp.sqrt(-2.0 * th / (1.0 - th)) * (i * xc)

    rows = lax.broadcasted_iota(I32, (tT, DA), 0)
    s = 1
    while s < tT:
        a_sh = pltpu.roll(a, s, 0)
        u_sh = pltpu.roll(u, s, 0)
        m = rows >= s
        u = jnp.where(m, a * u_sh + u, u)
        a = jnp.where(m, a * a_sh, a)
        s *= 2
    h = a * hcar[...] + u
    hlast = h[tT - 1:tT, :]
    hcar[...] = hlast
    ho_ref[...] = hlast
    y_ref[...] = (h * jax.nn.gelu(ga_ref[...])).astype(BF16)


def _rglru(proj, row0, B, T, tT, conv_buf, h0, cw, cbias, wr, br, wi, bi, lam):
    nT = T // tT
    rb0 = row0 // tT
    vec = lambda: pl.BlockSpec((1, DA), lambda b, t: (0, 0))
    return pl.pallas_call(
        _rglru_kernel,
        grid=(B, nT),
        in_specs=[pl.BlockSpec((tT, DA), lambda b, t: (rb0 + b * nT + t, 0)),
                  pl.BlockSpec((tT, DA), lambda b, t: (rb0 + b * nT + t, 1)),
                  pl.BlockSpec((None, CONV_W - 1, DA), lambda b, t: (b, 0, 0)),
                  pl.BlockSpec((None, 1, DA), lambda b, t: (b, 0, 0)),
                  pl.BlockSpec((CONV_W, DA), lambda b, t: (0, 0)),
                  vec(),
                  pl.BlockSpec((DA, DA), lambda b, t: (0, 0)),
                  vec(),
                  pl.BlockSpec((DA, DA), lambda b, t: (0, 0)),
                  vec(), vec()],
        out_specs=[pl.BlockSpec((tT, DA), lambda b, t: (b * nT + t, 0)),
                   pl.BlockSpec((None, CONV_W - 1, DA), lambda b, t: (b, 0, 0)),
                   pl.BlockSpec((None, 1, DA), lambda b, t: (b, 0, 0))],
        out_shape=[jax.ShapeDtypeStruct((B * T, DA), BF16),
                   jax.ShapeDtypeStruct((B, CONV_W - 1, DA), F32),
                   jax.ShapeDtypeStruct((B, 1, DA), F32)],
        scratch_shapes=[pltpu.VMEM((tT + SUBLANES, DA), F32), pltpu.VMEM((1, DA), F32)],
        compiler_params=_cparams(("parallel", "arbitrary")),
        name="rglru",
    )(proj, proj, conv_buf, h0.reshape(B, 1, DA), cw, cbias.reshape(1, DA), wr, br.reshape(1, DA),
      wi, bi.reshape(1, DA), lam.reshape(1, DA))


def _attn_kernel(*refs, chq, has_hist):
    if has_hist:
        q_ref, k_ref, v_ref, hk_ref, hv_ref, bias_ref, o_ref, ko_ref, vo_ref, kbuf, vbuf = refs
    else:
        q_ref, k_ref, v_ref, bias_ref, o_ref, ko_ref, vo_ref, kbuf, vbuf = refs
    c = pl.program_id(1)
    T = k_ref.shape[0]
    W = BAND_PAST + chq
    keep = ko_ref.shape[0]

    heads = range(HB)

    @pl.when(c == 0)
    def _():
        ko_ref[...] = k_ref[T - keep:T, :]
        vo_ref[...] = v_ref[T - keep:T, :]
        for h in heads:
            sl = slice(h * DHB, (h + 1) * DHB)
            if has_hist:
                kbuf[h, 0:BAND_PAST, :] = hk_ref[:, sl].astype(BF16)
                vbuf[h, 0:BAND_PAST, :] = hv_ref[:, sl].astype(BF16)
            else:
                kbuf[h, 0:BAND_PAST, :] = jnp.zeros((BAND_PAST, DHB), BF16)
                vbuf[h, 0:BAND_PAST, :] = jnp.zeros((BAND_PAST, DHB), BF16)
            kbuf[h, BAND_PAST:BAND_PAST + T, :] = k_ref[:, sl].astype(BF16)
            vbuf[h, BAND_PAST:BAND_PAST + T, :] = v_ref[:, sl].astype(BF16)

    cps = q_ref.shape[0] // chq
    q = q_ref[...].astype(BF16)
    pairs = [(sub, h) for sub in range(cps) for h in heads]
    starts = [pl.multiple_of((c * cps + sub) * chq, chq) for sub in range(cps)]
    wins = [pl.ds(st, W) for st in starts]
    s = [_dot_nt(q[sub * chq:(sub + 1) * chq, h * DHB:(h + 1) * DHB], kbuf[h, wins[sub], :]) * (DHB ** -0.5)
         + bias_ref[h] for sub, h in pairs]
    if not has_hist:
        col = lax.broadcasted_iota(I32, (chq, W), 1)
        valid = [st - BAND_PAST + col >= 0 for st in starts]
        s = [jnp.where(valid[sub], x, -jnp.inf) for (sub, h), x in zip(pairs, s)]
    m = [jnp.max(x, axis=-1, keepdims=True) for x in s]
    e = [jnp.exp(x - mx) for x, mx in zip(s, m)]
    den = [jnp.sum(x, axis=-1, keepdims=True) for x in e]
    o = [_dot(ex.astype(BF16), vbuf[h, wins[sub], :]) / dn for (sub, h), ex, dn in zip(pairs, e, den)]
    for sub in range(cps):
        o_ref[sub * chq:(sub + 1) * chq, :] = jnp.concatenate(o[sub * HB:(sub + 1) * HB], axis=-1).astype(BF16)


def _band_attention(proj, row0, B, T, chq, bias, hist_k=None, hist_v=None):
    cps = 2 if (T // chq) % 2 == 0 else 1
    qrows = cps * chq
    nC = T // qrows
    rbq = row0 // qrows
    rbk = row0 // T
    has_hist = hist_k is not None
    keep = min(BAND_PAST, T)
    in_specs = [pl.BlockSpec((qrows, DB), lambda b, c: (rbq + b * nC + c, 2)),
                pl.BlockSpec((T, DB), lambda b, c: (rbk + b, 3)),
                pl.BlockSpec((T, DB), lambda b, c: (rbk + b, 4))]
    args = [proj, proj, proj]
    if has_hist:
        in_specs += [pl.BlockSpec((None, BAND_PAST, DB), lambda b, c: (b, 0, 0))] * 2
        args += [hist_k, hist_v]
    in_specs.append(pl.BlockSpec((HB, chq, BAND_PAST + chq), lambda b, c: (0, 0, 0)))
    args.append(bias)
    return pl.pallas_call(
        functools.partial(_attn_kernel, chq=chq, has_hist=has_hist),
        grid=(B, nC),
        in_specs=in_specs,
        out_specs=[pl.BlockSpec((qrows, DB), lambda b, c: (b * nC + c, 0)),
                   pl.BlockSpec((None, keep, DB), lambda b, c: (b, 0, 0)),
                   pl.BlockSpec((None, keep, DB), lambda b, c: (b, 0, 0))],
        out_shape=[jax.ShapeDtypeStruct((B * T, DB), BF16),
                   jax.ShapeDtypeStruct((B, keep, DB), F32),
                   jax.ShapeDtypeStruct((B, keep, DB), F32)],
        scratch_shapes=[pltpu.VMEM((HB, BAND_PAST + T, DHB), BF16), pltpu.VMEM((HB, BAND_PAST + T, DHB), BF16)],
        compiler_params=_cparams(("parallel", "arbitrary")),
        name="band_attention",
    )(*args)


def _rel_bias(table, chq):
    W = BAND_PAST + chq
    lw = W + chq - 1
    n_clipped = BAND_PAST - MAX_REL + chq
    w = jnp.concatenate([jnp.broadcast_to(table[:, 2 * MAX_REL:], (HB, n_clipped)),
                         table[:, MAX_REL + 1 - chq:2 * MAX_REL][:, ::-1]], axis=1)
    flat = jnp.tile(jnp.pad(w, ((0, 0), (0, 1))), (1, chq))[:, :chq * lw]
    return flat.reshape(HB, chq, lw)[:, :, chq - 1:chq - 1 + W]


def _outproj_kernel(*refs, n_in, n_first):
    ys, ws = refs[:2 * n_in], refs[2 * n_in:3 * n_in]
    x_refs, (gt_ref, o_ref) = refs[3 * n_in:-2], refs[-2:]
    first = pl.program_id(0) < n_first
    acc = None
    for j, w_ref in enumerate(ws):
        y = jnp.where(first, ys[2 * j][...], ys[2 * j + 1][...])
        part = _dot(y, w_ref[...])
        acc = part if acc is None else acc + part
    x = _stream_block(x_refs, n_first if len(x_refs) == 2 else None)
    o_ref[...] = x + gt_ref[...] * acc.reshape(x.shape)


def _out_projection(ys, ws, x, mod_pb, gt_col):
    npb, pb, d = mod_pb.shape[0], PB, ws[0].shape[1]
    bB = 32
    tm = bB * pb
    n_in = len(ys)
    x_args, x_specs, x_first = _stream_operands(x, bB)
    n_first = ys[0][0].shape[0] // tm
    assert all(y0.shape[0] % tm == 0 and y1.shape[0] % tm == 0 for y0, y1 in ys)
    in_specs = []
    for y0, y1 in ys:
        in_specs += [pl.BlockSpec((tm, y0.shape[1]), lambda i: (jnp.minimum(i, n_first - 1), 0)),
                     pl.BlockSpec((tm, y1.shape[1]), lambda i: (jnp.maximum(i - n_first, 0), 0))]
    assert x_first is None or x_first == n_first
    in_specs += [pl.BlockSpec(w.shape, lambda i: (0, 0)) for w in ws]
    in_specs += x_specs + [pl.BlockSpec((bB, 1, d), lambda i: (i, 0, gt_col))]
    return pl.pallas_call(
        functools.partial(_outproj_kernel, n_in=n_in, n_first=n_first),
        grid=(npb // bB,),
        in_specs=in_specs,
        out_specs=pl.BlockSpec((bB, pb, d), lambda i: (i, 0, 0)),
        out_shape=jax.ShapeDtypeStruct((npb, pb, d), F32),
        compiler_params=_cparams(("parallel",)),
        name="out_projection",
    )(*[y for pair in ys for y in pair], *ws, *x_args, mod_pb)


def _gdnpre_kernel(qkv_ref, ab_ref, cb_ref, cw_ref, alog_ref, dtb_ref,
                   q_ref, k_ref, v_ref, gb_ref, gt_ref, cbo_ref, xbuf, *, C):
    tT = qkv_ref.shape[0]

    @pl.when(pl.program_id(1) == 0)
    def _():
        xbuf[0:SUBLANES, :] = jnp.zeros((SUBLANES, N_QKV), F32)
        xbuf[SUBLANES - (CONV_W - 1):SUBLANES, :] = cb_ref[...]

    for cb in range(N_QKV // DK):
        cols = slice(cb * DK, (cb + 1) * DK)
        x = qkv_ref[:, cols]
        xbuf[SUBLANES:SUBLANES + tT, cols] = x
        xc = cw_ref[3:4, cols] * x
        for j in range(CONV_W - 1):
            xc = xc + cw_ref[j:j + 1, cols] * xbuf[SUBLANES - 3 + j:SUBLANES - 3 + j + tT, cols]
        xs = xc * jax.nn.sigmoid(xc)
        if cb < 2 * HC:
            xn = xs * lax.rsqrt(jnp.sum(xs * xs, axis=-1, keepdims=True) + NORM_EPS)
            if cb < HC:
                q_ref[:, cols] = xn * (DK ** -0.5)
            else:
                k_ref[:, (cb - HC) * DK:(cb - HC + 1) * DK] = xn
        else:
            v_ref[:, (cb - 2 * HC) * DV:(cb - 2 * HC + 1) * DV] = xs
    tail = xbuf[tT:tT + SUBLANES, :]
    xbuf[0:SUBLANES, :] = tail
    cbo_ref[...] = tail[SUBLANES - (CONV_W - 1):, :]

    ab = ab_ref[...]
    g = -jnp.exp(alog_ref[...]) * _softplus(ab + dtb_ref[...])
    beta = jax.nn.sigmoid(ab)
    lane = lax.broadcasted_iota(I32, ab.shape, 1)
    g = jnp.where(lane < HC, g, 0.0)
    ri = lax.broadcasted_iota(I32, (C, C), 0)
    ci = lax.broadcasted_iota(I32, (C, C), 1)
    tril = (ri >= ci).astype(F32)
    triu = (ri <= ci).astype(F32)
    for c in range(tT // C):
        gc = g[c * C:(c + 1) * C, :]
        G = _dot(tril, gc, HI)
        lane_c = lax.broadcasted_iota(I32, (C, LANES), 1)
        gb_ref[c * C:(c + 1) * C, :] = jnp.where(lane_c < HC, G, beta[c * C:(c + 1) * C, :])
        gt_ref[c * 2 * HC:(c + 1) * 2 * HC, :] = _dot_tn(gc, triu, HI)[:2 * HC, :]


def _gdn_pre(proj, row0, B, T, tT, C, conv_buf, cw, alog_pad, dtb_pad):
    nT = T // tT
    rb0 = row0 // tT
    cpt = tT // C
    return pl.pallas_call(
        functools.partial(_gdnpre_kernel, C=C),
        grid=(B, nT),
        in_specs=[pl.BlockSpec((tT, N_QKV), lambda b, t: (rb0 + b * nT + t, 0)),
                  pl.BlockSpec((tT, LANES), lambda b, t: (rb0 + b * nT + t, (N_QKV + DCV) // LANES)),
                  pl.BlockSpec((None, CONV_W - 1, N_QKV), lambda b, t: (b, 0, 0)),
                  pl.BlockSpec((CONV_W, N_QKV), lambda b, t: (0, 0)),
                  pl.BlockSpec((1, LANES), lambda b, t: (0, 0)),
                  pl.BlockSpec((1, LANES), lambda b, t: (0, 0))],
        out_specs=[pl.BlockSpec((tT, DCK), lambda b, t: (b * nT + t, 0)),
                   pl.BlockSpec((tT, DCK), lambda b, t: (b * nT + t, 0)),
                   pl.BlockSpec((tT, DCV), lambda b, t: (b * nT + t, 0)),
                   pl.BlockSpec((tT, LANES), lambda b, t: (b * nT + t, 0)),
                   pl.BlockSpec((cpt * 2 * HC, C), lambda b, t: (b * nT + t, 0)),
                   pl.BlockSpec((None, CONV_W - 1, N_QKV), lambda b, t: (b, 0, 0))],
        out_shape=[jax.ShapeDtypeStruct((B * T, DCK), F32),
                   jax.ShapeDtypeStruct((B * T, DCK), F32),
                   jax.ShapeDtypeStruct((B * T, DCV), F32),
                   jax.ShapeDtypeStruct((B * T, LANES), F32),
                   jax.ShapeDtypeStruct((B * T // C * 2 * HC, C), F32),
                   jax.ShapeDtypeStruct((B, CONV_W - 1, N_QKV), F32)],
        scratch_shapes=[pltpu.VMEM((tT + SUBLANES, N_QKV), F32)],
        compiler_params=_cparams(("parallel", "arbitrary")),
        name="gdn_pre",
    )(proj, proj, conv_buf, cw, alog_pad, dtb_pad)


def _split_bf16(a):
    hi = a.astype(BF16)
    return hi, (a - hi.astype(F32)).astype(BF16)


def _dot3(a, b):
    return _dot(a[0], b[0]) + (_dot(a[0], b[1]) + _dot(a[1], b[0]))


def _delta_kernel(q_ref, k_ref, v_ref, gb_ref, gt_ref, z_ref, s0_ref, ng_ref, o_ref, so_ref,
                  s_scr, u_scr, w_scr, qd_scr, kd_scr, a_scr, *, C, hps, unroll):
    T = q_ref.shape[0]
    n_chunks = T // C
    ri = lax.broadcasted_iota(I32, (C, C), 0)
    ci = lax.broadcasted_iota(I32, (C, C), 1)
    causal = ri >= ci
    strict = ri > ci
    eye = (ri == ci).astype(F32)
    head0 = pl.program_id(1) * hps
    lane = lax.broadcasted_iota(I32, (C, LANES), 1)

    def pick(x, col):
        lane_x = lax.broadcasted_iota(I32, x.shape, 1)
        return jnp.sum(jnp.where(lane_x == col, x, 0.0), axis=1, keepdims=True)

    def operands(n, j):
        rows = pl.ds(pl.multiple_of(n * C, C), C)
        cs = slice(j * DK, (j + 1) * DK)
        gbc = gb_ref[rows, :]
        G = pick(gbc, head0 + j)
        beta = pick(gbc, head0 + j + HC)
        return rows, cs, G, beta

    def prepare(streams):
        def lower_left(m):
            s = m.bit_length() - 1
            return jnp.logical_and(((ri ^ ci) >> s) == 1, ((ri >> s) & 1) == 1)

        Bs, Ps = [], []
        for n, j in streams:
            rows, cs, G, beta = operands(n, j)
            g_row = gt_ref[pl.ds(n * (2 * HC) + head0 + j, 1), :]
            k = k_ref[rows, cs]
            L = jnp.where(causal, jnp.exp(jnp.where(causal, G - g_row, 0.0)), 0.0)
            qk = _dot_nt(jnp.concatenate([k * beta, q_ref[rows, cs]], axis=0).astype(BF16), k.astype(BF16))
            a_scr[j, rows, :] = (qk[C:] * L).astype(BF16)
            Bm = jnp.where(strict, -(qk[:C] * L), 0.0)
            Bs.append(_split_bf16(Bm))
            Ps.append(eye + jnp.where(lower_left(1), Bm, 0.0))
        zero = jnp.zeros((C, C), BF16)
        m = 2
        while m < C:
            pair = lower_left(m)
            Ds = [_split_bf16(p) for p in Ps]
            Es = [(jnp.where(pair, b[0], zero), jnp.where(pair, b[1], zero)) for b in Bs]
            Ws = [_split_bf16(_dot3(e, d)) for e, d in zip(Es, Ds)]
            Ps = [p + _dot3(d, w) for p, d, w in zip(Ps, Ds, Ws)]
            m *= 2
        for (n, j), P in zip(streams, Ps):
            rows, cs, G, beta = operands(n, j)
            k = k_ref[rows, cs]
            eG = jnp.exp(G)
            rhs = jnp.concatenate([v_ref[rows, cs] * beta, (k * beta) * eG], axis=1)
            X = _dot3(_split_bf16(P), _split_bf16(rhs))
            u_scr[j, rows, :] = X[:, :DV]
            w_scr[j, rows, :] = X[:, DV:].astype(BF16)
            qd_scr[j, rows, :] = (q_ref[rows, cs] * eG).astype(BF16)
            kd_scr[j, rows, :] = (k * jnp.exp(G[C - 1:C, :] - G)).astype(BF16)

    def prepare_body(i, carry):
        prepare([(i * unroll + m, j) for m in range(unroll) for j in range(hps)])
        return carry

    lax.fori_loop(0, n_chunks // unroll, prepare_body, 0)

    s_scr[...] = s0_ref[...]

    def advance(n, carry):
        r0 = pl.multiple_of(n * C, C)
        rows = pl.ds(r0, C)
        g_end = gb_ref[pl.ds(r0 + C - 1, 1), :]
        heads = range(hps)
        S = [s_scr[j] for j in heads]
        Sb = [s.astype(BF16) for s in S]
        wqS = [_dot(jnp.concatenate([w_scr[j, rows, :], qd_scr[j, rows, :]], axis=0), Sb[j]) for j in heads]
        wS = [x[:C] for x in wqS]
        qS = [x[C:] for x in wqS]
        vnb = [(u_scr[j, rows, :] - wS[j]).astype(BF16) for j in heads]
        kv = [_dot_tn(kd_scr[j, rows, :], vnb[j]) for j in heads]
        av = [_dot(a_scr[j, rows, :], vnb[j]) for j in heads]
        for j in heads:
            cs = slice(j * DK, (j + 1) * DK)
            s_scr[j] = S[j] * jnp.exp(pick(g_end, head0 + j)) + kv[j]
            o = qS[j] + av[j]
            on = o * lax.rsqrt(jnp.mean(o * o, axis=-1, keepdims=True) + NORM_EPS) * ng_ref[...]
            z = z_ref[rows, cs]
            o_ref[rows, cs] = (on * (z * jax.nn.sigmoid(z))).astype(BF16)
        return carry

    lax.fori_loop(0, n_chunks, advance, 0)
    so_ref[...] = s_scr[...]


def _delta_rule(q, k, v, gb, gt, proj, row0, B, T, C, S0, norm_g, hps):
    rb = row0 // T
    wb = hps * DK
    zcol0 = N_QKV // wb
    n_chunks = T // C
    qkv_spec = lambda: pl.BlockSpec((T, wb), lambda b, h: (b, h))
    return pl.pallas_call(
        functools.partial(_delta_kernel, C=C, hps=hps, unroll=4 if n_chunks % 4 == 0 else 1),
        grid=(B, HC // hps),
        in_specs=[qkv_spec(), qkv_spec(), qkv_spec(),
                  pl.BlockSpec((T, LANES), lambda b, h: (b, 0)),
                  pl.BlockSpec((n_chunks * 2 * HC, C), lambda b, h: (b, 0)),
                  pl.BlockSpec((T, wb), lambda b, h: (rb + b, zcol0 + h)),
                  pl.BlockSpec((None, hps, DK, DV), lambda b, h: (b, h, 0, 0)),
                  pl.BlockSpec((1, DV), lambda b, h: (0, 0))],
        out_specs=[pl.BlockSpec((T, wb), lambda b, h: (b, h)),
                   pl.BlockSpec((None, hps, DK, DV), lambda b, h: (b, h, 0, 0))],
        out_shape=[jax.ShapeDtypeStruct((B * T, DCV), BF16),
                   jax.ShapeDtypeStruct((B, HC, DK, DV), F32)],
        scratch_shapes=[pltpu.VMEM((hps, DK, DV), F32),
                        pltpu.VMEM((hps, T, DV), F32),
                        pltpu.VMEM((hps, T, DK), BF16),
                        pltpu.VMEM((hps, T, DK), BF16),
                        pltpu.VMEM((hps, T, DK), BF16),
                        pltpu.VMEM((hps, T, C), BF16)],
        compiler_params=_cparams(("parallel", "parallel")),
        name="delta_rule",
    )(q, k, v, gb, gt, proj, S0, norm_g.reshape(1, DV))


def _first_argmax(x, idx, size):
    m = jnp.max(x, axis=0, keepdims=True)
    am = jnp.min(jnp.where(x == m, idx, size), axis=0, keepdims=True)
    return m, am


def _router_kernel(x_ref, g_ref, sc_ref, sh_ref, rwt_ref, rb_ref,
                   h_ref, e_ref, w_ref, r_ref, cnt_ref, run_scr):
    tn = h_ref.shape[0]
    gsz = N_EXPERTS // N_GROUPS

    @pl.when(pl.program_id(0) == 0)
    def _():
        run_scr[...] = jnp.zeros_like(run_scr)

    h = _norm_mod(x_ref[...], g_ref[...], sc_ref[...], sh_ref[...])
    h_ref[...] = h
    rw_hi, rw_lo = _split_bf16(rwt_ref[...])
    h_hi, h_lo = _split_bf16(h)
    scores = jax.nn.sigmoid(_dot_nt(rw_hi, h_hi) + (_dot_nt(rw_hi, h_lo) + _dot_nt(rw_lo, h_hi)))
    sel = scores + rb_ref[...]

    gidx = lax.broadcasted_iota(I32, (gsz, tn), 0)
    gs_rows = []
    for gi in range(N_GROUPS):
        blk = sel[gi * gsz:(gi + 1) * gsz, :]
        m1, a1 = _first_argmax(blk, gidx, gsz)
        m2 = jnp.max(jnp.where(gidx == a1, -jnp.inf, blk), axis=0, keepdims=True)
        gs_rows.append(m1 + m2)
    gscore = jnp.concatenate(gs_rows, axis=0)
    gi8 = lax.broadcasted_iota(I32, (N_GROUPS, tn), 0)
    gmask = jnp.zeros((N_GROUPS, tn), jnp.bool_)
    gwork = gscore
    for _ in range(TOPK_GROUPS):
        _, a = _first_argmax(gwork, gi8, N_GROUPS)
        hit = gi8 == a
        gmask = jnp.logical_or(gmask, hit)
        gwork = jnp.where(hit, -jnp.inf, gwork)
    emask = jnp.concatenate([jnp.broadcast_to(gmask[gi:gi + 1, :], (gsz, tn)) for gi in range(N_GROUPS)], axis=0)

    eidx = lax.broadcasted_iota(I32, (N_EXPERTS, tn), 0)
    work = jnp.where(emask, sel, -jnp.inf)
    hits = []
    chosen = jnp.zeros((N_EXPERTS, tn), jnp.bool_)
    for _ in range(TOP_K):
        _, a = _first_argmax(work, eidx, N_EXPERTS)
        hit = eidx == a
        hits.append((a, hit))
        chosen = jnp.logical_or(chosen, hit)
        work = jnp.where(hit, -jnp.inf, work)
    chosen_f = chosen.astype(F32)
    denom = jnp.sum(scores * chosen_f, axis=0, keepdims=True)

    ti = lax.broadcasted_iota(I32, (tn, tn), 0)
    tj = lax.broadcasted_iota(I32, (tn, tn), 1)
    before = (ti < tj).astype(BF16)
    rank = _dot(chosen_f.astype(BF16), before) + run_scr[...]
    run_scr[...] = run_scr[...] + jnp.sum(chosen_f, axis=1, keepdims=True)
    cnt_ref[...] = jnp.broadcast_to(run_scr[...], cnt_ref.shape).astype(I32)

    e_rows, w_rows, r_rows = [], [], []
    for a, hit in hits:
        hf = hit.astype(F32)
        e_rows.append(a)
        w_rows.append(jnp.sum(scores * hf, axis=0, keepdims=True) / denom * ROUTE_SCALE)
        r_rows.append(jnp.sum(rank * hf, axis=0, keepdims=True))
    e_ref[...] = jnp.concatenate(e_rows, axis=0)
    w_ref[...] = jnp.concatenate(w_rows, axis=0)
    r_ref[...] = jnp.concatenate(r_rows, axis=0).astype(I32)


def _router(x_all, g, mod_pb, sc_col, sh_col, rwt, rb):
    npb, pb, d = x_all.shape
    n = npb * pb
    bB = 16
    tn = bB * pb
    return pl.pallas_call(
        _router_kernel,
        grid=(npb // bB,),
        in_specs=[pl.BlockSpec((bB, pb, d), lambda i: (i, 0, 0)),
                  pl.BlockSpec((1, d), lambda i: (0, 0)),
                  pl.BlockSpec((bB, 1, d), lambda i: (i, 0, sc_col)),
                  pl.BlockSpec((bB, 1, d), lambda i: (i, 0, sh_col)),
                  pl.BlockSpec((N_EXPERTS, d), lambda i: (0, 0)),
                  pl.BlockSpec((N_EXPERTS, 1), lambda i: (0, 0))],
        out_specs=[pl.BlockSpec((tn, d), lambda i: (i, 0)),
                   pl.BlockSpec((TOP_K, tn), lambda i: (0, i)),
                   pl.BlockSpec((TOP_K, tn), lambda i: (0, i)),
                   pl.BlockSpec((TOP_K, tn), lambda i: (0, i)),
                   pl.BlockSpec((N_EXPERTS, LANES), lambda i: (0, 0))],
        out_shape=[jax.ShapeDtypeStruct((n, d), F32),
                   jax.ShapeDtypeStruct((TOP_K, n), I32),
                   jax.ShapeDtypeStruct((TOP_K, n), F32),
                   jax.ShapeDtypeStruct((TOP_K, n), I32),
                   jax.ShapeDtypeStruct((N_EXPERTS, LANES), I32)],
        scratch_shapes=[pltpu.VMEM((N_EXPERTS, 1), F32)],
        compiler_params=_cparams(("arbitrary",)),
        name="moe_router",
    )(x_all, g.reshape(1, d), mod_pb, mod_pb, rwt, rb.reshape(N_EXPERTS, 1))


PACK_ROWS = D_MODEL // LANES


def _row_slab(ref, r):
    return ref.at[pl.ds(pl.multiple_of(r * PACK_ROWS, PACK_ROWS), PACK_ROWS), :]


def _to_row_tiles(dst_ref, words, row0=0):
    rows = words.shape[0]
    for c in range(PACK_ROWS):
        dst_ref[pl.ds(row0 * PACK_ROWS + c, rows, stride=PACK_ROWS), :] = words[:, c * LANES:(c + 1) * LANES]


def _lane_tile(src_ref, c, rows, row0=0):
    return src_ref[pl.ds(row0 * PACK_ROWS + c, rows, stride=PACK_ROWS), :]


def _from_row_tiles(src_ref, rows):
    return jnp.concatenate([_lane_tile(src_ref, c, rows) for c in range(PACK_ROWS)], axis=1)


def _dispatch_kernel(zf_ref, pos_ref, h_ref, sgu_ref, sd_ref, xs_hbm, shared_ref, sbuf, zbuf, sem, zsem, *, tm):
    i = pl.program_id(0)
    n_steps = pl.num_programs(0)
    td = h_ref.shape[0]
    n_tiles = zf_ref.shape[0]
    slot = lax.rem(i, 2)

    @pl.when(i == 0)
    def _():
        zbuf[...] = jnp.zeros_like(zbuf)

        def zero_copy(t):
            r0 = pl.multiple_of(t * (tm * PACK_ROWS), tm * PACK_ROWS)
            return pltpu.make_async_copy(zbuf, xs_hbm.at[pl.ds(r0, tm * PACK_ROWS), :], zsem)

        def start(t, carry):
            @pl.when(zf_ref[t] != 0)
            def _():
                zero_copy(t).start()
            return carry

        def wait(t, carry):
            @pl.when(zf_ref[t] != 0)
            def _():
                zero_copy(t).wait()
            return carry

        lax.fori_loop(0, n_tiles, start, 0)
        lax.fori_loop(0, n_tiles, wait, 0)

    def drain(s):
        for _ in range(TOP_K):
            pltpu.make_async_copy(sbuf.at[s], xs_hbm.at[pl.ds(0, td * PACK_ROWS), :], sem.at[s]).wait()

    @pl.when(i >= 2)
    def _():
        drain(slot)

    h = h_ref[...]
    _to_row_tiles(sbuf.at[slot], h)
    gu = _dot(h.astype(BF16), sgu_ref[...])
    hid = jax.nn.silu(gu[:, :D_SHARED]) * gu[:, D_SHARED:]
    shared_ref[...] = _dot(hid.astype(BF16), sd_ref[...])

    def body(j, carry):
        t0 = pl.multiple_of(j * SUBLANES, SUBLANES)
        for u in range(SUBLANES):
            src = _row_slab(sbuf.at[slot], t0 + u)
            for k in range(TOP_K):
                p = pos_ref[0, k * td + t0 + u]
                pltpu.make_async_copy(src, _row_slab(xs_hbm, p), sem.at[slot]).start(priority=k % 2)
        return carry

    lax.fori_loop(0, td // SUBLANES, body, 0)

    @pl.when(i == n_steps - 1)
    def _():
        drain(slot)
        drain(1 - slot)


def _moe_dispatch(h2, pos_tiles, zero_flag, sgu, sd, n_tiles, tm, td):
    n, d = h2.shape
    grid_spec = pltpu.PrefetchScalarGridSpec(
        num_scalar_prefetch=1,
        grid=(n // td,),
        in_specs=[pl.BlockSpec((None, 1, TOP_K * td), lambda i, zf: (i, 0, 0), memory_space=pltpu.SMEM),
                  pl.BlockSpec((td, d), lambda i, zf: (i, 0)),
                  pl.BlockSpec(sgu.shape, lambda i, zf: (0, 0)),
                  pl.BlockSpec(sd.shape, lambda i, zf: (0, 0))],
        out_specs=[pl.BlockSpec(memory_space=pl.ANY), pl.BlockSpec((td, d), lambda i, zf: (i, 0))],
        scratch_shapes=[pltpu.VMEM((2, td * PACK_ROWS, LANES), F32), pltpu.VMEM((tm * PACK_ROWS, LANES), F32),
                        pltpu.SemaphoreType.DMA((2,)), pltpu.SemaphoreType.DMA(())],
    )
    return pl.pallas_call(
        functools.partial(_dispatch_kernel, tm=tm),
        grid_spec=grid_spec,
        out_shape=[jax.ShapeDtypeStruct((n_tiles * tm * PACK_ROWS, LANES), F32),
                   jax.ShapeDtypeStruct((n, d), F32)],
        compiler_params=_cparams(("arbitrary",)),
        name="moe_dispatch",
    )(zero_flag, pos_tiles, h2, sgu, sd)


N_IN_BUF = 3
N_OUT_BUF = 2


def _experts_kernel(te_ref, nu_ref, xs_hbm, wg_ref, wu_ref, wd_ref, ys_hbm, wgu_s, wd_s, xin, yout, sem_in, sem_out):
    i = pl.program_id(0)
    nu = nu_ref[0]
    rows = xin.shape[1]
    tm = rows // PACK_ROWS

    def tile(ref, t):
        return ref.at[pl.ds(pl.multiple_of(t * rows, rows), rows), :]

    def load(t):
        s = lax.rem(t, N_IN_BUF)
        return pltpu.make_async_copy(tile(xs_hbm, t), xin.at[s], sem_in.at[s])

    def store(t):
        s = lax.rem(t, N_OUT_BUF)
        return pltpu.make_async_copy(yout.at[s], tile(ys_hbm, t), sem_out.at[s])

    @pl.when(i == 0)
    def _():
        load(0).start()

        @pl.when(nu > 1)
        def _():
            load(1).start()

    @pl.when(i + 2 < nu)
    def _():
        load(i + 2).start()

    @pl.when(jnp.logical_or(i == 0, te_ref[i] != te_ref[jnp.maximum(i - 1, 0)]))
    def _():
        wgu_s[:, :D_EXPERT] = wg_ref[...].astype(BF16)
        wgu_s[:, D_EXPERT:] = wu_ref[...].astype(BF16)
        wd_s[...] = wd_ref[...].astype(BF16)

    @pl.when(i < nu)
    def _():
        load(i).wait()

        @pl.when(i >= N_OUT_BUF)
        def _():
            store(i - N_OUT_BUF).wait()

        x = _from_row_tiles(xin.at[lax.rem(i, N_IN_BUF)], tm).astype(BF16)
        gu = _dot(x, wgu_s[...])
        hid = jax.nn.silu(gu[:, :D_EXPERT]) * gu[:, D_EXPERT:]
        _to_row_tiles(yout.at[lax.rem(i, N_OUT_BUF)], _dot(hid.astype(BF16), wd_s[...]))
        store(i).start()

    @pl.when(i == nu - 1)
    def _():
        @pl.when(i >= 1)
        def _():
            store(i - 1).wait()

        store(i).wait()

    @pl.when(i >= nu)
    def _():
        @pl.when(i == nu)
        def _():
            yout[0] = jnp.zeros(yout.shape[1:], F32)

        zero_store = pltpu.make_async_copy(yout.at[0], tile(ys_hbm, i), sem_out.at[0])
        zero_store.start()
        zero_store.wait()


def _routed_experts(xs, tile_expert, n_used, layer, wg, wu, wd, tm):
    n_tiles = xs.shape[0] // (tm * PACK_ROWS)
    d = wg.shape[2]
    grid_spec = pltpu.PrefetchScalarGridSpec(
        num_scalar_prefetch=2,
        grid=(n_tiles,),
        in_specs=[pl.BlockSpec(memory_space=pl.ANY),
                  pl.BlockSpec((None, None, d, D_EXPERT), lambda i, te, nu: (layer, te[i], 0, 0)),
                  pl.BlockSpec((None, None, d, D_EXPERT), lambda i, te, nu: (layer, te[i], 0, 0)),
                  pl.BlockSpec((None, None, D_EXPERT, d), lambda i, te, nu: (layer, te[i], 0, 0))],
        out_specs=pl.BlockSpec(memory_space=pl.ANY),
        scratch_shapes=[pltpu.VMEM((d, 2 * D_EXPERT), BF16), pltpu.VMEM((D_EXPERT, d), BF16),
                        pltpu.VMEM((N_IN_BUF, tm * PACK_ROWS, LANES), F32),
                        pltpu.VMEM((N_OUT_BUF, tm * PACK_ROWS, LANES), F32),
                        pltpu.SemaphoreType.DMA((N_IN_BUF,)), pltpu.SemaphoreType.DMA((N_OUT_BUF,))],
    )
    return pl.pallas_call(
        _experts_kernel,
        grid_spec=grid_spec,
        out_shape=jax.ShapeDtypeStruct(xs.shape, F32),
        compiler_params=_cparams(("arbitrary",)),
        name="moe_experts",
    )(tile_expert, n_used, xs, wg, wu, wd)


def _gather_rows(idx_ref, n_rows, src_hbm, dst, sem):
    group = 4 * SUBLANES

    def body(j, carry):
        r0 = pl.multiple_of(j * group, group)
        for u in range(group):
            pltpu.make_async_copy(_row_slab(src_hbm, idx_ref[0, r0 + u]), _row_slab(dst, r0 + u),
                                  sem).start(priority=u % 2)
        return carry
    lax.fori_loop(0, n_rows // group, body, 0)


def _combine_kernel(pos_ref, nxt_ref, ys_hbm, w_ref, shared_ref, x_ref, gt_ref, o_ref,
                    gbuf0, gbuf1, acc_scr, sem):
    i = pl.program_id(0)
    last = pl.num_programs(0) - 1
    tb = w_ref.shape[0]
    n_rows = TOP_K * tb
    n_groups = tb // SUBLANES
    per_group = n_rows // n_groups

    def wait_tile(buf, s):
        pltpu.make_async_copy(ys_hbm.at[pl.ds(0, n_rows * PACK_ROWS), :], buf, sem.at[s]).wait()

    @pl.when(i == 0)
    def _():
        _gather_rows(pos_ref, n_rows, ys_hbm, gbuf0, sem.at[0])

    def run(cur, cur_s, nxt, nxt_s):
        wait_tile(cur, cur_s)

        for g in range(n_groups):
            t0 = g * SUBLANES
            for u in range(per_group):
                r = g * per_group + u
                pltpu.make_async_copy(_row_slab(ys_hbm, nxt_ref[0, r]), nxt.at[r * PACK_ROWS:(r + 1) * PACK_ROWS, :],
                                      sem.at[nxt_s]).start(priority=u % 2)
            w8 = w_ref[t0:t0 + SUBLANES, :]
            wk = [jnp.broadcast_to(w8[:, k:k + 1], (SUBLANES, LANES)) for k in range(TOP_K)]
            for c in range(PACK_ROWS):
                a = wk[0] * _lane_tile(cur, c, SUBLANES, t0)
                for k in range(1, TOP_K):
                    a = a + wk[k] * _lane_tile(cur, c, SUBLANES, k * tb + t0)
                acc_scr[t0:t0 + SUBLANES, c * LANES:(c + 1) * LANES] = a

        @pl.when(i == last)
        def _():
            wait_tile(nxt, nxt_s)

    @pl.when(lax.rem(i, 2) == 0)
    def _():
        run(gbuf0, 0, gbuf1, 1)

    @pl.when(lax.rem(i, 2) == 1)
    def _():
        run(gbuf1, 1, gbuf0, 0)

    acc = shared_ref[...] + acc_scr[...]
    bB, pb, d = x_ref.shape
    o_ref[...] = x_ref[...] + gt_ref[...] * acc.reshape(bB, pb, d)


def _moe_combine(ys, pos_tiles, w_tok, shared, x_all, mod_pb, gt_col, tb):
    npb, pb, d = x_all.shape
    n_tb = pos_tiles.shape[0]
    bB = tb // pb
    last = n_tb - 1
    return pl.pallas_call(
        _combine_kernel,
        grid=(n_tb,),
        in_specs=[pl.BlockSpec((None, 1, TOP_K * tb), lambda i: (i, 0, 0), memory_space=pltpu.SMEM),
                  pl.BlockSpec((None, 1, TOP_K * tb), lambda i: (jnp.minimum(i + 1, last), 0, 0),
                               memory_space=pltpu.SMEM),
                  pl.BlockSpec(memory_space=pl.ANY),
                  pl.BlockSpec((tb, TOP_K), lambda i: (i, 0)),
                  pl.BlockSpec((tb, d), lambda i: (i, 0)),
                  pl.BlockSpec((bB, pb, d), lambda i: (i, 0, 0)),
                  pl.BlockSpec((bB, 1, d), lambda i: (i, 0, gt_col))],
        out_specs=pl.BlockSpec((bB, pb, d), lambda i: (i, 0, 0)),
        out_shape=jax.ShapeDtypeStruct(x_all.shape, F32),
        scratch_shapes=[pltpu.VMEM((TOP_K * tb * PACK_ROWS, LANES), F32),
                        pltpu.VMEM((TOP_K * tb * PACK_ROWS, LANES), F32),
                        pltpu.VMEM((tb, d), F32), pltpu.SemaphoreType.DMA((2,))],
        compiler_params=_cparams(("arbitrary",)),
        name="moe_combine",
    )(pos_tiles, pos_tiles, ys, w_tok, shared, x_all, mod_pb)


def _moe(x_all, g, mod_pb, rw, rb, layer, wg, wu, wd, sg, su, sd):
    npb, pb, d = x_all.shape
    n = npb * pb
    tm = 512
    tb = 128
    h2, e_t, w_t, r_t, cnt = _router(x_all, g, mod_pb, 4, 3, rw.T, rb)

    counts = cnt[:, 0]
    padded = (counts + tm - 1) // tm * tm
    ends = jnp.cumsum(padded)
    starts = ends - padded
    n_tiles = (n * TOP_K + N_EXPERTS * (tm - 1)) // tm
    onehot = e_t[:, :, None] == jnp.arange(N_EXPERTS, dtype=I32)
    pos = jnp.sum(jnp.where(onehot, starts, 0), axis=-1) + r_t
    n_used = (ends[-1] // tm).astype(I32)
    tile_idx = jnp.arange(n_tiles, dtype=I32)
    tile_start = jnp.minimum(tile_idx, n_used - 1) * tm
    tile_expert = jnp.sum(ends[None, :] <= tile_start[:, None], axis=1).astype(I32)
    is_seg_end = jnp.any((ends[None, :] == (tile_idx[:, None] + 1) * tm) & (padded[None, :] > 0), axis=1)
    zero_flag = jnp.logical_or(is_seg_end, tile_idx >= n_used).astype(I32)
    pos_tiles = pos.reshape(TOP_K, n // tb, tb).transpose(1, 0, 2).reshape(n // tb, 1, TOP_K * tb)

    sgu = jnp.concatenate([sg, su], axis=-1).astype(BF16)
    xs, shared = _moe_dispatch(h2, pos_tiles, zero_flag, sgu, sd.astype(BF16), n_tiles, tm, tb)
    ys = _routed_experts(xs, tile_expert, n_used.reshape(1), layer, wg, wu, wd, tm)
    return _moe_combine(ys, pos_tiles, w_t.T, shared, x_all, mod_pb, 5, tb)


def _final_kernel(x_ref, g_ref, o_ref):
    x = x_ref[...]
    ms = jnp.mean(x * x, axis=-1, keepdims=True)
    o_ref[...] = x * lax.rsqrt(ms + NORM_EPS) * g_ref[...]


def _final_norm(x_all, g, pb0, n_pb):
    npb, pb, d = x_all.shape
    bB = 32
    return pl.pallas_call(
        _final_kernel,
        grid=(n_pb // bB,),
        in_specs=[pl.BlockSpec((bB, pb, d), lambda i: (pb0 // bB + i, 0, 0)),
                  pl.BlockSpec((1, d), lambda i: (0, 0))],
        out_specs=pl.BlockSpec((bB, pb, d), lambda i: (i, 0, 0)),
        out_shape=jax.ShapeDtypeStruct((n_pb, pb, d), F32),
        compiler_params=_cparams(("parallel",)),
        name="final_norm",
    )(x_all, g.reshape(1, d))


def _block_diag(w):
    nb, di, do = w.shape
    return jnp.einsum("nde,nm->ndme", w, jnp.eye(nb, dtype=w.dtype)).reshape(nb * di, nb * do)


def kernel(x_prompt, x_sample, c_prompt, c_sample, cache_b_k, cache_b_v, state_a_conv, state_a_h, state_c_conv, state_c_S, norm1_g, norm2_g, final_g, w_mod, b_mod, w_in0, w_out0, a_conv_w, a_conv_b, a_w_r, a_b_r, a_w_i, a_b_i, a_lambda, b_rel_bias, w_in1, w_out1, c_conv_w, c_A_log, c_dt_bias, c_norm_g, router_w, router_bias, e_w_gate, e_w_up, e_w_down, s_w_gate, s_w_up, s_w_down):
    Bp, Tp, d = x_prompt.shape
    Bs, Ts, _ = x_sample.shape
    np_tok, ns_tok = Bp * Tp, Bs * Ts
    npb_p, npb_s = np_tok // PB, ns_tok // PB
    depth = w_mod.shape[0]

    x_all = (x_prompt.reshape(npb_p, PB, d), x_sample.reshape(npb_s, PB, d))
    mod = _modulation(jnp.concatenate([c_prompt, c_sample], axis=0), w_mod, b_mod)

    outs = {}
    for l in range(depth):
        mod_pb = jnp.concatenate([jnp.repeat(mod[l, :Bp], Tp // PB, axis=0),
                                  jnp.repeat(mod[l, Bp:], Ts // PB, axis=0)], axis=0)[:, None, :]
        j = l // 2
        if l % 2 == 0:
            proj = _in_projection(x_all, norm1_g[l], mod_pb, 1, 0, w_in0[j].astype(BF16), 1280)
            wr, wi = _block_diag(a_w_r[j]).astype(BF16), _block_diag(a_w_i[j]).astype(BF16)
            a_args = (a_conv_w[j], a_conv_b[j], wr, a_b_r[j], wi, a_b_i[j], a_lambda[j])
            ya_p, ac_p, ah_p = _rglru(proj, 0, Bp, Tp, 512, jnp.zeros((Bp, CONV_W - 1, DA), F32),
                                      jnp.zeros((Bp, DA), F32), *a_args)
            ya_s, ac_s, ah_s = _rglru(proj, np_tok, Bs, Ts, Ts, state_a_conv[j], state_a_h[j], *a_args)
            yb_p, bk_p, bv_p = _band_attention(proj, 0, Bp, Tp, CHUNK, _rel_bias(b_rel_bias[j], CHUNK))
            yb_s, bk_s, bv_s = _band_attention(proj, np_tok, Bs, Ts, Ts, _rel_bias(b_rel_bias[j], Ts),
                                               cache_b_k[j].reshape(Bs, BAND_PAST, DB),
                                               cache_b_v[j].reshape(Bs, BAND_PAST, DB))
            w_out = w_out0[j].astype(BF16)
            x_all = _out_projection([(ya_p, ya_s), (yb_p, yb_s)], [w_out[:DA], w_out[DA:]], x_all, mod_pb, 2)
            outs.setdefault("bk_p", []).append(bk_p.reshape(Bp, -1, HB, DHB))
            outs.setdefault("bv_p", []).append(bv_p.reshape(Bp, -1, HB, DHB))
            outs.setdefault("bk_s", []).append(bk_s.reshape(Bs, -1, HB, DHB))
            outs.setdefault("bv_s", []).append(bv_s.reshape(Bs, -1, HB, DHB))
            outs.setdefault("ac_p", []).append(ac_p)
            outs.setdefault("ac_s", []).append(ac_s)
            outs.setdefault("ah_p", []).append(ah_p.reshape(Bp, DA))
            outs.setdefault("ah_s", []).append(ah_s.reshape(Bs, DA))
        else:
            w_in = jnp.pad(w_in1[j], ((0, 0), (0, IN1_PAD - IN1))).astype(BF16)
            proj = _in_projection(x_all, norm1_g[l], mod_pb, 1, 0, w_in, 1408)
            alog = jnp.pad(c_A_log[j], (0, LANES - HC)).reshape(1, LANES)
            dtb = jnp.pad(c_dt_bias[j], (0, LANES - HC)).reshape(1, LANES)
            cp, cs = min(DELTA_BLOCK, Tp), min(DELTA_BLOCK, Ts)
            q_p, k_p, v_p, gb_p, gt_p, cc_p = _gdn_pre(proj, 0, Bp, Tp, 256, cp,
                                                       jnp.zeros((Bp, CONV_W - 1, N_QKV), F32), c_conv_w[j], alog, dtb)
            q_s, k_s, v_s, gb_s, gt_s, cc_s = _gdn_pre(proj, np_tok, Bs, Ts, Ts, cs, state_c_conv[j], c_conv_w[j],
                                                       alog, dtb)
            o_p, cs_p = _delta_rule(q_p, k_p, v_p, gb_p, gt_p, proj, 0, Bp, Tp, cp,
                                    jnp.zeros((Bp, HC, DK, DV), F32), c_norm_g[j], 2)
            o_s, cs_s = _delta_rule(q_s, k_s, v_s, gb_s, gt_s, proj, np_tok, Bs, Ts, cs, state_c_S[j],
                                    c_norm_g[j], 4)
            x_all = _out_projection([(o_p, o_s)], [w_out1[j].astype(BF16)], x_all, mod_pb, 2)
            outs.setdefault("cc_p", []).append(cc_p)
            outs.setdefault("cc_s", []).append(cc_s)
            outs.setdefault("cs_p", []).append(cs_p)
            outs.setdefault("cs_s", []).append(cs_s)
        x_all = _moe(x_all, norm2_g[l], mod_pb, router_w[l], router_bias[l], l, e_w_gate, e_w_up, e_w_down,
                     s_w_gate[l], s_w_up[l], s_w_down[l])

    y_prompt = _final_norm(x_all, final_g, 0, npb_p).reshape(Bp, Tp, d)
    y_sample = _final_norm(x_all, final_g, npb_p, npb_s).reshape(Bs, Ts, d)
    st = {k: jnp.stack(v) for k, v in outs.items()}
    return (y_prompt, y_sample, st["bk_p"], st["bk_s"], st["bv_p"], st["bv_s"], st["ac_p"], st["ac_s"],
            st["ah_p"], st["ah_s"], st["cc_p"], st["cc_s"], st["cs_p"], st["cs_s"])
```
